```python
import math
import functools
import jax
import jax.numpy as jnp
from jax import lax
import numpy as np

D_MODEL = 1024
BATCH = 16
SEQ = 256
DEPTH = 2
DEC_BATCH = 2
DEC_SEQ = 2048
PAST_LEN = 512

GRID_W = 64
HEAD_DIM = 64
NA_HEADS = 4
NA_WIN_ROWS = 8
NA_WIN_COLS = 16
NA_COL_BLOCK = 16
NA_BAND = NA_COL_BLOCK + NA_WIN_COLS
GQA_HEADS = 4
GQA_KV_HEADS = 2
ROPE_THETA = 10000.0
Q_BLOCK = 128
SSM_HEADS = 8
SSM_HEAD_DIM = 64
SSM_STATE = 64
SSM_GROUPS = 2
SSM_INNER = SSM_HEADS * SSM_HEAD_DIM
SSM_BC_DIM = SSM_GROUPS * SSM_STATE
CONV_DIM = SSM_INNER + 2 * SSM_BC_DIM
CONV_W = 5
CHUNK = 128
NA_DIM = NA_HEADS * HEAD_DIM
GQA_Q_DIM = GQA_HEADS * HEAD_DIM
GQA_KV_DIM = GQA_KV_HEADS * HEAD_DIM
D_MIX = NA_DIM + GQA_Q_DIM + SSM_INNER
IN_SIZES = (NA_DIM, NA_DIM, NA_DIM, GQA_Q_DIM, GQA_KV_DIM, GQA_KV_DIM, SSM_INNER, CONV_DIM, SSM_HEADS)
IN_DIM = sum(IN_SIZES)
N_EXPERTS = 32
TOP_K = 4
D_FF = D_MODEL
MOE_BLOCK = 128
SWIGLU_LIMIT = 7.0
SWIGLU_ALPHA = 1.702
EPS = 1e-6
NEG_INF = -1e30

kernel_name = 'hybrid_diffusion_trunk_step'


def rms_norm(x, g):
    xf = x.astype(jnp.float32)
    xf = xf * lax.rsqrt(jnp.mean(xf * xf, axis=-1, keepdims=True) + EPS)
    return (xf * g.astype(jnp.float32)).astype(x.dtype)


def split_heads(t, n):
    return t.reshape(t.shape[:-1] + (n, HEAD_DIM))


def axial_rope(length):
    t = jnp.arange(length)
    row = (t // GRID_W).astype(jnp.float32)
    col = (t % GRID_W).astype(jnp.float32)
    axis_dim = HEAD_DIM // 2
    inv_freq = ROPE_THETA ** (-jnp.arange(0, axis_dim, 2, dtype=jnp.float32) / axis_dim)
    ang = jnp.concatenate([row[:, None] * inv_freq, col[:, None] * inv_freq], axis=-1)
    return jnp.cos(ang), jnp.sin(ang)


def apply_rope(x, cos, sin):
    xf = x.astype(jnp.float32).reshape(x.shape[:-1] + (HEAD_DIM // 2, 2))
    x1, x2 = xf[..., 0], xf[..., 1]
    c, s = cos[None, :, None, :], sin[None, :, None, :]
    out = jnp.stack([x1 * c - x2 * s, x1 * s + x2 * c], axis=-1)
    return out.reshape(x.shape).astype(x.dtype)


def block_attention(q, k, v):
    b, lq, h, d = q.shape
    kvh = k.shape[2]
    rep = h // kvh
    scale = d ** -0.5
    qb = jnp.moveaxis(q.reshape(b, lq // Q_BLOCK, Q_BLOCK, kvh, rep, d), 1, 0)

    def one_block(qblk):
        s = jnp.einsum('bqgrd,bkgd->bgrqk', qblk, k).astype(jnp.float32) * scale
        p = jax.nn.softmax(s, axis=-1).astype(v.dtype)
        return jnp.einsum('bgrqk,bkgd->bqgrd', p, v)

    out = lax.map(one_block, qb)
    return jnp.moveaxis(out, 0, 1).reshape(b, lq, h, d)


def neighbourhood_attention(q, k, v, k_ctx, v_ctx, rpb):
    b, length, h, d = q.shape
    rows = length // GRID_W
    wr = min(NA_WIN_ROWS, rows)
    ncb = GRID_W // NA_COL_BLOCK
    r = jnp.arange(rows)
    key_rows = jnp.clip(r - wr // 2, 0, rows - wr)[:, None] + jnp.arange(wr)
    j = jnp.arange(ncb)
    band = jnp.clip(j * NA_COL_BLOCK - NA_WIN_COLS // 2, 0, GRID_W - NA_BAND)[:, None] + jnp.arange(NA_BAND)
    q_col = j[:, None] * NA_COL_BLOCK + jnp.arange(NA_COL_BLOCK)
    col0 = jnp.clip(q_col - NA_WIN_COLS // 2, 0, GRID_W - NA_WIN_COLS)
    bc = band[:, None, :]
    in_win = (bc >= col0[..., None]) & (bc < col0[..., None] + NA_WIN_COLS)
    kg = k.reshape(b, rows, GRID_W, h, d)
    vg = v.reshape(b, rows, GRID_W, h, d)
    ir = key_rows[:, None, :, None]
    ic = band[None, :, None, :]
    n_nb = wr * NA_BAND
    k_nb = kg[:, ir, ic].reshape(b, rows, ncb, n_nb, h, d)
    v_nb = vg[:, ir, ic].reshape(b, rows, ncb, n_nb, h, d)
    qg = q.reshape(b, rows, ncb, NA_COL_BLOCK, h, d)
    dr = key_rows - r[:, None] + NA_WIN_ROWS - 1
    dc = jnp.clip(bc - q_col[..., None] + NA_WIN_COLS - 1, 0, 2 * NA_WIN_COLS - 2)
    bias = rpb.astype(jnp.float32)[:, dr[:, None, None, :, None], dc[None, :, :, None, :]]
    bias = jnp.where(in_win[None, None, :, :, None, :], bias, NEG_INF)
    bias = jnp.moveaxis(bias, 0, 2).reshape(rows, ncb, h, NA_COL_BLOCK, n_nb)
    scale = d ** -0.5
    s_nb = jnp.einsum('brjqhd,brjkhd->brjhqk', qg, k_nb).astype(jnp.float32) * scale + bias
    s_ctx = jnp.einsum('brjqhd,bchd->brjhqc', qg, k_ctx).astype(jnp.float32) * scale
    p = jax.nn.softmax(jnp.concatenate([s_nb, s_ctx], axis=-1), axis=-1).astype(v.dtype)
    o = (jnp.einsum('brjhqk,brjkhd->brjqhd', p[..., :n_nb], v_nb)
         + jnp.einsum('brjhqc,bchd->brjqhd', p[..., n_nb:], v_ctx))
    return o.reshape(b, length, h, d)


def depthwise_conv(x, w, bias):
    out = lax.conv_general_dilated(x, w[:, None, :], window_strides=(1,),
                                   padding=[(CONV_W // 2, CONV_W // 2)],
                                   dimension_numbers=('NWC', 'WIO', 'NWC'),
                                   feature_group_count=x.shape[-1])
    return out + bias


def ssd_scan(x, dt, a, bm, cm, h0):
    b, length, h, p = x.shape
    nc = length // CHUNK
    rep = h // SSM_GROUPS
    xf = x.astype(jnp.float32).reshape(b, nc, CHUNK, h, p)
    bh = jnp.repeat(bm.astype(jnp.float32), rep, axis=2).reshape(b, nc, CHUNK, h, SSM_STATE)
    ch = jnp.repeat(cm.astype(jnp.float32), rep, axis=2).reshape(b, nc, CHUNK, h, SSM_STATE)
    dtc = dt.reshape(b, nc, CHUNK, h)
    cum = jnp.cumsum(dtc * a, axis=2)
    tri = jnp.tril(jnp.ones((CHUNK, CHUNK), dtype=bool))[None, None, :, :, None]
    seg = cum[:, :, :, None, :] - cum[:, :, None, :, :]
    decay = jnp.exp(jnp.where(tri, seg, -jnp.inf))
    xdt = xf * dtc[..., None]
    scores = jnp.einsum('bcihn,bcjhn->bcijh', ch, bh) * decay
    y_intra = jnp.einsum('bcijh,bcjhp->bcihp', scores, xdt)
    to_end = jnp.exp(cum[:, :, -1:, :] - cum)
    chunk_states = jnp.einsum('bcjhn,bcjh,bcjhp->bchpn', bh, to_end, xdt)
    chunk_decay = jnp.exp(cum[:, :, -1, :])

    def step(state, inp):
        st, dec = inp
        return state * dec[:, :, None, None] + st, state

    h_final, h_start = lax.scan(step, h0.astype(jnp.float32),
                                (jnp.moveaxis(chunk_states, 1, 0), jnp.moveaxis(chunk_decay, 1, 0)))
    h_start = jnp.moveaxis(h_start, 0, 1)
    y_inter = jnp.einsum('bcihn,bchpn,bcih->bcihp', ch, h_start, jnp.exp(cum))
    return (y_intra + y_inter).reshape(b, length, h, p), h_final


def ssm_mixer(z, xbc, dt_raw, lp, h0_fwd, h0_bwd):
    b, length, _ = z.shape
    xbc = jax.nn.silu(depthwise_conv(xbc, lp['conv_w'], lp['conv_b']))
    xs, bm, cm = jnp.split(xbc, [SSM_INNER, SSM_INNER + SSM_BC_DIM], axis=-1)
    xs = xs.reshape(b, length, SSM_HEADS, SSM_HEAD_DIM)
    bm = bm.reshape(b, length, SSM_GROUPS, SSM_STATE)
    cm = cm.reshape(b, length, SSM_GROUPS, SSM_STATE)
    a = -jnp.exp(lp['a_log'].astype(jnp.float32))
    dtf = dt_raw.astype(jnp.float32)
    dt_bias = lp['dt_bias'].astype(jnp.float32)
    dt_f = jax.nn.softplus(dtf + dt_bias[0])
    dt_b = jax.nn.softplus(dtf + dt_bias[1])
    rev = lambda t: jnp.flip(t, axis=1)
    y_f, st_f = ssd_scan(xs, dt_f, a[0], bm, cm, h0_fwd)
    y_b, st_b = ssd_scan(rev(xs), rev(dt_b), a[1], rev(bm), rev(cm), h0_bwd)
    y = y_f + rev(y_b) + xs.astype(jnp.float32) * lp['d_skip'].astype(jnp.float32)[:, None]
    y = y.reshape(b, length, SSM_INNER) * jax.nn.silu(z.astype(jnp.float32))
    y = rms_norm(y, lp['ssm_norm']).astype(z.dtype)
    return y, st_f.astype(z.dtype), st_b.astype(z.dtype)


def moe(h, lp):
    shp = h.shape
    t = h.reshape(-1, shp[-1])
    n = t.shape[0]
    n_slots = n * TOP_K
    logits = (t @ lp['w_router']).astype(jnp.float32) + lp['b_router'].astype(jnp.float32)
    top_val, top_idx = lax.top_k(logits, TOP_K)
    gates = jax.nn.softmax(top_val, axis=-1)
    flat_e = top_idx.reshape(-1)
    order = jnp.argsort(flat_e)
    e_sorted = flat_e[order]
    tok = order // TOP_K
    sizes = jnp.bincount(flat_e, length=N_EXPERTS)
    padded = (sizes + MOE_BLOCK - 1) // MOE_BLOCK * MOE_BLOCK
    starts = jnp.cumsum(sizes) - sizes
    pad_ends = jnp.cumsum(padded)
    pad_starts = pad_ends - padded
    dest = pad_starts[e_sorted] + jnp.arange(n_slots) - starts[e_sorted]
    n_blocks = -(-n_slots // MOE_BLOCK) + N_EXPERTS
    buf = jnp.zeros((n_blocks * MOE_BLOCK, shp[-1]), h.dtype).at[dest].set(t[tok])
    blk_expert = jnp.minimum(jnp.searchsorted(pad_ends, jnp.arange(n_blocks) * MOE_BLOCK, side='right'),
                             N_EXPERTS - 1)

    def expert_block(args):
        xb, e = args
        up = xb @ lp['w_up'][e] + lp['b_up'][e]
        g = jnp.minimum(up[:, 0::2], SWIGLU_LIMIT)
        lin = jnp.clip(up[:, 1::2], -SWIGLU_LIMIT, SWIGLU_LIMIT)
        act = g * jax.nn.sigmoid(SWIGLU_ALPHA * g) * (lin + 1)
        return act @ lp['w_down'][e] + lp['b_down'][e]

    ybuf = lax.map(expert_block, (buf.reshape(n_blocks, MOE_BLOCK, shp[-1]), blk_expert))
    w_slot = gates.reshape(-1)[order].astype(h.dtype)
    y_slots = ybuf.reshape(-1, shp[-1])[dest] * w_slot[:, None]
    return jax.ops.segment_sum(y_slots, tok, num_segments=n).reshape(shp)


def project_in(h, w_in):
    bounds = [int(v) for v in np.cumsum(IN_SIZES)[:-1]]
    return jnp.split(h @ w_in, bounds, axis=-1)


def merge_groups(na_o, g_o, s_o):
    b, length = s_o.shape[:2]
    return jnp.concatenate([na_o.reshape(b, length, NA_DIM), g_o.reshape(b, length, GQA_Q_DIM), s_o], axis=-1)


def context_mixer(h, lp):
    na_q, na_k, na_v, g_q, g_k, g_v, z, xbc, dt = project_in(h, lp['w_in'])
    na_k = split_heads(na_k, NA_HEADS)
    na_v = split_heads(na_v, NA_HEADS)
    na_o = block_attention(split_heads(na_q, NA_HEADS), na_k, na_v)
    g_q = rms_norm(split_heads(g_q, GQA_HEADS), lp['q_norm'])
    g_k = rms_norm(split_heads(g_k, GQA_KV_HEADS), lp['k_norm'])
    g_v = split_heads(g_v, GQA_KV_HEADS)
    g_o = block_attention(g_q, g_k, g_v)
    h0 = jnp.zeros((h.shape[0], SSM_HEADS, SSM_HEAD_DIM, SSM_STATE), jnp.float32)
    s_o, st_f, st_b = ssm_mixer(z, xbc, dt, lp, h0, h0)
    out = merge_groups(na_o, g_o, s_o) @ lp['w_out']
    return out, (na_k, na_v, g_k, g_v, jnp.stack([st_f, st_b], axis=1))


def latent_mixer(h, lp, na_k_ctx, na_v_ctx, g_k_ctx, g_v_ctx, st_ctx, cos, sin):
    na_q, na_k, na_v, g_q, g_k, g_v, z, xbc, dt = project_in(h, lp['w_in'])
    na_o = neighbourhood_attention(split_heads(na_q, NA_HEADS), split_heads(na_k, NA_HEADS),
                                   split_heads(na_v, NA_HEADS), na_k_ctx, na_v_ctx, lp['na_rpb'])
    g_q = apply_rope(rms_norm(split_heads(g_q, GQA_HEADS), lp['q_norm']), cos, sin)
    g_k = apply_rope(rms_norm(split_heads(g_k, GQA_KV_HEADS), lp['k_norm']), cos, sin)
    keys = jnp.concatenate([g_k_ctx, g_k], axis=1)
    vals = jnp.concatenate([g_v_ctx, split_heads(g_v, GQA_KV_HEADS)], axis=1)
    g_o = block_attention(g_q, keys, vals)
    s_o, _, _ = ssm_mixer(z, xbc, dt, lp, st_ctx[:, 0], st_ctx[:, 1])
    return merge_groups(na_o, g_o, s_o) @ lp['w_out'], None


def adaln_split(cond, w_ada, b_ada):
    m = jax.nn.silu(cond) @ w_ada + b_ada
    return jnp.split(m[..., None, :], 6, axis=-1)


def modulate(x, g, shift, scale):
    return rms_norm(x, g) * (1 + scale) + shift


def trunk_layer(x, cond, lp, mixer):
    sh1, sc1, g1, sh2, sc2, g2 = adaln_split(cond, lp['w_ada'], lp['b_ada'])
    mix, extras = mixer(modulate(x, lp['norm_mix'], sh1, sc1))
    x = x + g1 * mix
    x = x + g2 * moe(modulate(x, lp['norm_ffn'], sh2, sc2), lp)
    return x, extras


def setup_inputs(seed: int = 0) -> dict:
    key = jax.random.key(seed)
    ks = jax.random.split(key, 32)
    D = D_MODEL

    def nrm(k, shape, s):
        return jax.random.normal(k, shape, jnp.float32) * s

    dt0 = jnp.exp(jax.random.uniform(ks[19], (DEPTH, 2, SSM_HEADS), jnp.float32,
                                     minval=math.log(1e-3), maxval=math.log(1e-1)))
    return {
        'x_prompt': nrm(ks[0], (BATCH, SEQ, D), 1.0),
        'x_sample': nrm(ks[1], (DEC_BATCH, DEC_SEQ, D), 1.0),
        'cache_na_k': nrm(ks[2], (DEC_BATCH, DEPTH, PAST_LEN, NA_HEADS, HEAD_DIM), 1.0),
        'cache_na_v': nrm(ks[3], (DEC_BATCH, DEPTH, PAST_LEN, NA_HEADS, HEAD_DIM), 1.0),
        'cache_gqa_k': nrm(ks[4], (DEC_BATCH, DEPTH, PAST_LEN, GQA_KV_HEADS, HEAD_DIM), 1.0),
        'cache_gqa_v': nrm(ks[5], (DEC_BATCH, DEPTH, PAST_LEN, GQA_KV_HEADS, HEAD_DIM), 1.0),
        'state_ssm': nrm(ks[6], (DEC_BATCH, DEPTH, 2, SSM_HEADS, SSM_HEAD_DIM, SSM_STATE), 0.5),
        'c': nrm(ks[7], (DEC_BATCH, D), 1.0),
        'c_ctx': nrm(ks[8], (D,), 1.0),
        'w_ada': nrm(ks[9], (DEPTH, D, 6 * D), 0.5 * D ** -0.5),
        'b_ada': nrm(ks[10], (DEPTH, 6 * D), 0.02),
        'norm_mix': 1.0 + nrm(ks[11], (DEPTH, D), 0.02),
        'norm_ffn': 1.0 + nrm(ks[12], (DEPTH, D), 0.02),
        'w_in': nrm(ks[13], (DEPTH, D, IN_DIM), D ** -0.5),
        'na_rpb': nrm(ks[14], (DEPTH, NA_HEADS, 2 * NA_WIN_ROWS - 1, 2 * NA_WIN_COLS - 1), 0.1),
        'gqa_q_norm': 1.0 + nrm(ks[15], (DEPTH, HEAD_DIM), 0.02),
        'gqa_k_norm': 1.0 + nrm(ks[16], (DEPTH, HEAD_DIM), 0.02),
        'ssm_conv_w': nrm(ks[17], (DEPTH, CONV_W, CONV_DIM), CONV_W ** -0.5),
        'ssm_conv_b': nrm(ks[18], (DEPTH, CONV_DIM), 0.02),
        'ssm_dt_bias': dt0 + jnp.log(-jnp.expm1(-dt0)),
        'ssm_a_log': jnp.log(jax.random.uniform(ks[20], (DEPTH, 2, SSM_HEADS), jnp.float32, minval=1.0, maxval=16.0)),
        'ssm_d': 1.0 + nrm(ks[21], (DEPTH, SSM_HEADS), 0.02),
        'ssm_norm': 1.0 + nrm(ks[22], (DEPTH, SSM_INNER), 0.02),
        'w_out': nrm(ks[23], (DEPTH, D_MIX, D), D_MIX ** -0.5),
        'w_router': nrm(ks[24], (DEPTH, D, N_EXPERTS), D ** -0.5),
        'b_router': nrm(ks[25], (DEPTH, N_EXPERTS), 0.01),
        'w_up': nrm(ks[26], (DEPTH, N_EXPERTS, D, 2 * D_FF), D ** -0.5),
        'b_up': nrm(ks[27], (DEPTH, N_EXPERTS, 2 * D_FF), 0.02),
        'w_down': nrm(ks[28], (DEPTH, N_EXPERTS, D_FF, D), D_FF ** -0.5),
        'b_down': nrm(ks[29], (DEPTH, N_EXPERTS, D), 0.02),
        'final_norm': 1.0 + nrm(ks[30], (D,), 0.02),
    }


def reference(x_prompt, x_sample, cache_na_k, cache_na_v, cache_gqa_k, cache_gqa_v, state_ssm, c, c_ctx,
              w_ada, b_ada, norm_mix, norm_ffn, w_in, na_rpb, gqa_q_norm, gqa_k_norm, ssm_conv_w, ssm_conv_b,
              ssm_dt_bias, ssm_a_log, ssm_d, ssm_norm, w_out, w_router, b_router, w_up, b_up, w_down, b_down,
              final_norm):
    def layer_params(l):
        return {'w_ada': w_ada[l], 'b_ada': b_ada[l], 'norm_mix': norm_mix[l], 'norm_ffn': norm_ffn[l],
                'w_in': w_in[l], 'na_rpb': na_rpb[l], 'q_norm': gqa_q_norm[l], 'k_norm': gqa_k_norm[l],
                'conv_w': ssm_conv_w[l], 'conv_b': ssm_conv_b[l], 'dt_bias': ssm_dt_bias[l],
                'a_log': ssm_a_log[l], 'd_skip': ssm_d[l], 'ssm_norm': ssm_norm[l], 'w_out': w_out[l],
                'w_router': w_router[l], 'b_router': b_router[l], 'w_up': w_up[l], 'b_up': b_up[l],
                'w_down': w_down[l], 'b_down': b_down[l]}

    y_p = x_prompt
    ctx = []
    for l in range(DEPTH):
        lp = layer_params(l)
        y_p, extras = trunk_layer(y_p, c_ctx, lp, functools.partial(context_mixer, lp=lp))
        ctx.append(extras)

    cos, sin = axial_rope(x_sample.shape[1])
    y_s = x_sample
    for l in range(DEPTH):
        lp = layer_params(l)
        mixer = functools.partial(latent_mixer, lp=lp, na_k_ctx=cache_na_k[:, l], na_v_ctx=cache_na_v[:, l],
                                  g_k_ctx=cache_gqa_k[:, l], g_v_ctx=cache_gqa_v[:, l], st_ctx=state_ssm[:, l],
                                  cos=cos, sin=sin)
        y_s, _ = trunk_layer(y_s, c, lp, mixer)

    y_prompt = rms_norm(y_p, final_norm)
    y_sample = rms_norm(y_s, final_norm)
    new_na_k = jnp.stack([e[0] for e in ctx], axis=1)
    new_na_v = jnp.stack([e[1] for e in ctx], axis=1)
    new_gqa_k = jnp.stack([e[2] for e in ctx], axis=1)
    new_gqa_v = jnp.stack([e[3] for e in ctx], axis=1)
    new_state_ssm = jnp.stack([e[4] for e in ctx], axis=1)
    return (y_prompt, y_sample, new_na_k, new_na_v, new_gqa_k, new_gqa_v, new_state_ssm)
```

```python
import functools

import jax
import jax.numpy as jnp
from jax import lax
from jax.experimental import pallas as pl
from jax.experimental.pallas import tpu as pltpu

F32 = jnp.float32
BF16 = jnp.bfloat16
I32 = jnp.int32

D_MODEL = 1024
BATCH = 16
SEQ = 256
DEPTH = 2
DEC_BATCH = 2
DEC_SEQ = 2048
PAST_LEN = 512
GRID_W = 64
HEAD_DIM = 64
NA_HEADS = 4
NA_WIN_ROWS = 8
NA_WIN_COLS = 16
GQA_HEADS = 4
GQA_KV_HEADS = 2
ROPE_THETA = 10000.0
SSM_HEADS = 8
SSM_HEAD_DIM = 64
SSM_STATE = 64
SSM_GROUPS = 2
SSM_INNER = SSM_HEADS * SSM_HEAD_DIM
SSM_BC_DIM = SSM_GROUPS * SSM_STATE
CONV_DIM = SSM_INNER + 2 * SSM_BC_DIM
CONV_W = 5
CHUNK = 128
NA_DIM = NA_HEADS * HEAD_DIM
GQA_Q_DIM = GQA_HEADS * HEAD_DIM
GQA_KV_DIM = GQA_KV_HEADS * HEAD_DIM
D_MIX = NA_DIM + GQA_Q_DIM + SSM_INNER
IN_DIM = 3 * NA_DIM + GQA_Q_DIM + 2 * GQA_KV_DIM + SSM_INNER + CONV_DIM + SSM_HEADS
N_EXPERTS = 32
TOP_K = 4
D_FF = D_MODEL
SWIGLU_LIMIT = 7.0
SWIGLU_ALPHA = 1.702
EPS = 1e-6
NEG_INF = -1e30

LANES = 128
N_CTX_TOK = BATCH * SEQ
N_LAT_TOK = DEC_BATCH * DEC_SEQ
N_TOK = N_CTX_TOK + N_LAT_TOK
N_COND = 1 + DEC_BATCH
COND_ROWS = 16
IN_PAD = 3 * NA_DIM + GQA_Q_DIM + 2 * GQA_KV_DIM + SSM_INNER + CONV_DIM + LANES
ROW_TILE = 256
N_ROW_TILES = N_TOK // ROW_TILE
MOE_BM = 256
N_SLOTS = N_TOK * TOP_K
MOE_NB = N_SLOTS // MOE_BM + N_EXPERTS
N_SEQ = BATCH + DEC_BATCH
N_CHUNKS = N_TOK // CHUNK
N_CTX_CHUNKS = N_CTX_TOK // CHUNK
VMEM_LIMIT = 48 * 1024 * 1024


def _cparams(*sem):
    return pltpu.CompilerParams(dimension_semantics=sem, vmem_limit_bytes=VMEM_LIMIT)


def _sigmoid(x):
    return 1.0 / (1.0 + jnp.exp(-x))


def _dot(a, b):
    return jnp.dot(a, b, preferred_element_type=F32)


def _dot_nt(a, b):
    return lax.dot_general(a, b, (((1,), (1,)), ((), ())), preferred_element_type=F32)


def _dot_exact(a, b):
    return jnp.dot(a, b, preferred_element_type=F32, precision=lax.Precision.HIGHEST)


def _rms(x, g):
    return x * lax.rsqrt(jnp.mean(x * x, axis=-1, keepdims=True) + EPS) * g


def _cond_of_tile(i):
    ctx_tiles = N_CTX_TOK // ROW_TILE
    return jnp.where(i < ctx_tiles, 0, 1 + (i - ctx_tiles) // (DEC_SEQ // ROW_TILE))


def _adaln_kernel(c_ref, w_ref, b_ref, o_ref):
    c = c_ref[...]
    s = (c * _sigmoid(c)).astype(BF16)
    o_ref[0] = _dot(s, w_ref[0].astype(BF16)) + b_ref[0]


def _adaln(conds, w_ada, b_ada):
    tn = 1536
    return pl.pallas_call(
        _adaln_kernel,
        grid=(DEPTH, 6 * D_MODEL // tn),
        in_specs=[pl.BlockSpec((COND_ROWS, D_MODEL), lambda l, j: (0, 0)),
                  pl.BlockSpec((1, D_MODEL, tn), lambda l, j: (l, 0, j)),
                  pl.BlockSpec((1, 1, tn), lambda l, j: (l, 0, j))],
        out_specs=pl.BlockSpec((1, COND_ROWS, tn), lambda l, j: (l, 0, j)),
        out_shape=jax.ShapeDtypeStruct((DEPTH, COND_ROWS, 6 * D_MODEL), F32),
        compiler_params=_cparams("parallel", "parallel"),
        name="adaln",
    )(conds, w_ada, b_ada.reshape(DEPTH, 1, 6 * D_MODEL))


_IN_SPLITS = (3 * NA_DIM, GQA_Q_DIM + 2 * GQA_KV_DIM, SSM_INNER, CONV_DIM, LANES)


def _inproj_kernel(x_ref, m_ref, g_ref, w_ref, qkv_ref, gqa_ref, z_ref, xbc_ref, dt_ref):
    m = m_ref[0]
    h = _rms(x_ref[...], g_ref[...]) * (1.0 + m[1:2]) + m[0:1]
    p = _dot(h.astype(BF16), w_ref[...])
    off = 0
    for ref, width in zip((qkv_ref, gqa_ref, z_ref, xbc_ref, dt_ref), _IN_SPLITS):
        ref[...] = p[:, off:off + width]
        off += width


def _inproj(x, mods, gain, w_in):
    row = lambda w: pl.BlockSpec((ROW_TILE, w), lambda i: (i, 0))
    return pl.pallas_call(
        _inproj_kernel,
        grid=(N_ROW_TILES,),
        in_specs=[row(D_MODEL),
                  pl.BlockSpec((1, 6, D_MODEL), lambda i: (_cond_of_tile(i), 0, 0)),
                  pl.BlockSpec((1, D_MODEL), lambda i: (0, 0)),
                  pl.BlockSpec((D_MODEL, IN_PAD), lambda i: (0, 0))],
        out_specs=[row(w) for w in _IN_SPLITS],
        out_shape=[jax.ShapeDtypeStruct((N_TOK, w), F32) for w in _IN_SPLITS],
        compiler_params=_cparams("parallel"),
        name="inproj",
    )(x, mods, gain, w_in)


def _softmax_pv(scores, values):
    m = scores[0].max(axis=-1, keepdims=True)
    for s in scores[1:]:
        m = jnp.maximum(m, s.max(axis=-1, keepdims=True))
    den = 0.0
    acc = 0.0
    for s, v in zip(scores, values):
        e = jnp.exp(s - m)
        den = den + e.sum(axis=-1, keepdims=True)
        acc = acc + _dot(e.astype(BF16), v)
    return acc / den


def _heads_rms(x, n_heads, g):
    return jnp.concatenate(
        [_rms(x[:, h * HEAD_DIM:(h + 1) * HEAD_DIM], g) for h in range(n_heads)], axis=-1)


def _rope(x, cos, sin_signed):
    w = x.shape[-1]
    lane = lax.broadcasted_iota(I32, x.shape, 1)
    partner = jnp.where((lane & 1) == 0, pltpu.roll(x, w - 1, 1), pltpu.roll(x, 1, 1))
    return x * cos + partner * sin_signed


_ATT_SCALE = HEAD_DIM ** -0.5


def _ctx_attn_kernel(qkv_ref, gqa_ref, qn_ref, kn_ref, nao_ref, go_ref, gk_ref):
    outs = []
    for h in range(NA_HEADS):
        sl = slice(h * HEAD_DIM, (h + 1) * HEAD_DIM)
        q = qkv_ref[:, sl].astype(BF16)
        k = qkv_ref[:, NA_DIM + h * HEAD_DIM:NA_DIM + (h + 1) * HEAD_DIM].astype(BF16)
        v = qkv_ref[:, 2 * NA_DIM + h * HEAD_DIM:2 * NA_DIM + (h + 1) * HEAD_DIM].astype(BF16)
        outs.append(_softmax_pv([_dot_nt(q, k) * _ATT_SCALE], [v]))
    nao_ref[...] = jnp.concatenate(outs, axis=-1).astype(BF16)

    gq = _heads_rms(gqa_ref[:, 0:GQA_Q_DIM], GQA_HEADS, qn_ref[...])
    gk = _heads_rms(gqa_ref[:, GQA_Q_DIM:GQA_Q_DIM + GQA_KV_DIM], GQA_KV_HEADS, kn_ref[...])
    gk_ref[...] = gk
    rep = GQA_HEADS // GQA_KV_HEADS
    outs = []
    for h in range(GQA_HEADS):
        g = h // rep
        q = gq[:, h * HEAD_DIM:(h + 1) * HEAD_DIM].astype(BF16)
        k = gk[:, g * HEAD_DIM:(g + 1) * HEAD_DIM].astype(BF16)
        v0 = GQA_Q_DIM + GQA_KV_DIM + g * HEAD_DIM
        v = gqa_ref[:, v0:v0 + HEAD_DIM].astype(BF16)
        outs.append(_softmax_pv([_dot_nt(q, k) * _ATT_SCALE], [v]))
    go_ref[...] = jnp.concatenate(outs, axis=-1).astype(BF16)


def _ctx_attn(qkv, gqa, q_norm, k_norm):
    return pl.pallas_call(
        _ctx_attn_kernel,
        grid=(BATCH,),
        in_specs=[pl.BlockSpec((SEQ, 3 * NA_DIM), lambda b: (b, 0)),
                  pl.BlockSpec((SEQ, GQA_Q_DIM + 2 * GQA_KV_DIM), lambda b: (b, 0)),
                  pl.BlockSpec((1, HEAD_DIM), lambda b: (0, 0)),
                  pl.BlockSpec((1, HEAD_DIM), lambda b: (0, 0))],
        out_specs=[pl.BlockSpec((SEQ, NA_DIM), lambda b: (b, 0)),
                   pl.BlockSpec((SEQ, GQA_Q_DIM), lambda b: (b, 0)),
                   pl.BlockSpec((SEQ, GQA_KV_DIM), lambda b: (b, 0))],
        out_shape=[jax.ShapeDtypeStruct((N_CTX_TOK, NA_DIM), BF16),
                   jax.ShapeDtypeStruct((N_CTX_TOK, GQA_Q_DIM), BF16),
                   jax.ShapeDtypeStruct((N_CTX_TOK, GQA_KV_DIM), F32)],
        compiler_params=_cparams("parallel"),
        name="ctx_attn",
    )(qkv, gqa, q_norm, k_norm)


GQA_TQ = 256
GQA_KEYS = PAST_LEN + DEC_SEQ


def _lat_gqa_kernel(gqa_ref, ck_ref, cv_ref, cos_ref, sin_ref, qn_ref, kn_ref, o_ref, kbuf, vbuf):
    qb = pl.program_id(1)

    @pl.when(qb == 0)
    def _():
        kbuf[0:PAST_LEN, :] = ck_ref[0].astype(BF16)
        vbuf[0:PAST_LEN, :] = cv_ref[0].astype(BF16)
        k = _heads_rms(gqa_ref[:, GQA_Q_DIM:GQA_Q_DIM + GQA_KV_DIM], GQA_KV_HEADS, kn_ref[...])
        k = _rope(k, cos_ref[:, 0:GQA_KV_DIM], sin_ref[:, 0:GQA_KV_DIM])
        kbuf[PAST_LEN:GQA_KEYS, :] = k.astype(BF16)
        vbuf[PAST_LEN:GQA_KEYS, :] = gqa_ref[:, GQA_Q_DIM + GQA_KV_DIM:].astype(BF16)

    r0 = pl.multiple_of(qb * GQA_TQ, GQA_TQ)
    q = _heads_rms(gqa_ref[pl.ds(r0, GQA_TQ), 0:GQA_Q_DIM], GQA_HEADS, qn_ref[...])
    q = _rope(q, cos_ref[pl.ds(r0, GQA_TQ), :], sin_ref[pl.ds(r0, GQA_TQ), :]).astype(BF16)
    rep = GQA_HEADS // GQA_KV_HEADS
    outs = []
    for h in range(GQA_HEADS):
        g = h // rep
        k = kbuf[:, g * HEAD_DIM:(g + 1) * HEAD_DIM]
        v = vbuf[:, g * HEAD_DIM:(g + 1) * HEAD_DIM]
        s = _dot_nt(q[:, h * HEAD_DIM:(h + 1) * HEAD_DIM], k) * _ATT_SCALE
        outs.append(_softmax_pv([s], [v]))
    o_ref[...] = jnp.concatenate(outs, axis=-1).astype(BF16)


def _lat_gqa(gqa, cache_k, cache_v, cos, sin, q_norm, k_norm):
    lat_blk = N_CTX_TOK // DEC_SEQ
    return pl.pallas_call(
        _lat_gqa_kernel,
        grid=(DEC_BATCH, DEC_SEQ // GQA_TQ),
        in_specs=[pl.BlockSpec((DEC_SEQ, GQA_Q_DIM + 2 * GQA_KV_DIM), lambda b, q: (lat_blk + b, 0)),
                  pl.BlockSpec((1, PAST_LEN, GQA_KV_DIM), lambda b, q: (b, 0, 0)),
                  pl.BlockSpec((1, PAST_LEN, GQA_KV_DIM), lambda b, q: (b, 0, 0)),
                  pl.BlockSpec((DEC_SEQ, GQA_Q_DIM), lambda b, q: (0, 0)),
                  pl.BlockSpec((DEC_SEQ, GQA_Q_DIM), lambda b, q: (0, 0)),
                  pl.BlockSpec((1, HEAD_DIM), lambda b, q: (0, 0)),
                  pl.BlockSpec((1, HEAD_DIM), lambda b, q: (0, 0))],
        out_specs=pl.BlockSpec((GQA_TQ, GQA_Q_DIM), lambda b, q: (b * (DEC_SEQ // GQA_TQ) + q, 0)),
        out_shape=jax.ShapeDtypeStruct((N_LAT_TOK, GQA_Q_DIM), BF16),
        scratch_shapes=[pltpu.VMEM((GQA_KEYS, GQA_KV_DIM), BF16),
                        pltpu.VMEM((GQA_KEYS, GQA_KV_DIM), BF16)],
        compiler_params=_cparams("arbitrary", "arbitrary"),
        name="lat_gqa",
    )(gqa, cache_k, cache_v, cos, sin, q_norm, k_norm)


def _rope_tables():
    t = jnp.arange(DEC_SEQ)
    row = (t // GRID_W).astype(F32)
    col = (t % GRID_W).astype(F32)
    axis_dim = HEAD_DIM // 2
    inv_freq = ROPE_THETA ** (-jnp.arange(0, axis_dim, 2, dtype=F32) / axis_dim)
    ang = jnp.concatenate([row[:, None] * inv_freq, col[:, None] * inv_freq], axis=-1)
    cos = jnp.repeat(jnp.cos(ang), 2, axis=-1)
    sin = jnp.repeat(jnp.sin(ang), 2, axis=-1) * jnp.tile(jnp.array([-1.0, 1.0], F32), HEAD_DIM // 2)
    return jnp.tile(cos, (1, GQA_HEADS)), jnp.tile(sin, (1, GQA_HEADS))


NA_ROWS = DEC_SEQ // GRID_W
NA_KEYS = NA_WIN_ROWS * GRID_W


def _lat_na_kernel(qkv_ref, ck_ref, cv_ref, bias_ref, o_ref):
    r = pl.program_id(1)
    r0 = jnp.clip(r - NA_WIN_ROWS // 2, 0, NA_ROWS - NA_WIN_ROWS)
    q0 = pl.multiple_of(r * GRID_W, GRID_W)
    k0 = pl.multiple_of(r0 * GRID_W, GRID_W)
    outs = []
    for h in range(NA_HEADS):
        c0 = h * HEAD_DIM
        q = qkv_ref[pl.ds(q0, GRID_W), c0:c0 + HEAD_DIM].astype(BF16)
        k = qkv_ref[pl.ds(k0, NA_KEYS), NA_DIM + c0:NA_DIM + c0 + HEAD_DIM].astype(BF16)
        v = qkv_ref[pl.ds(k0, NA_KEYS), 2 * NA_DIM + c0:2 * NA_DIM + c0 + HEAD_DIM].astype(BF16)
        kc = ck_ref[0, :, c0:c0 + HEAD_DIM].astype(BF16)
        vc = cv_ref[0, :, c0:c0 + HEAD_DIM].astype(BF16)
        s_nb = _dot_nt(q, k) * _ATT_SCALE + bias_ref[0, h]
        s_ctx = _dot_nt(q, kc) * _ATT_SCALE
        outs.append(_softmax_pv([s_nb, s_ctx], [v, vc]))
    o_ref[...] = jnp.concatenate(outs, axis=-1).astype(BF16)


def _na_row_offset(r):
    return r - jnp.clip(r - NA_WIN_ROWS // 2, 0, NA_ROWS - NA_WIN_ROWS)


def _lat_na(qkv, cache_k, cache_v, bias):
    lat_blk = N_CTX_TOK // DEC_SEQ
    return pl.pallas_call(
        _lat_na_kernel,
        grid=(DEC_BATCH, NA_ROWS),
        in_specs=[pl.BlockSpec((DEC_SEQ, 3 * NA_DIM), lambda b, r: (lat_blk + b, 0)),
                  pl.BlockSpec((1, PAST_LEN, NA_DIM), lambda b, r: (b, 0, 0)),
                  pl.BlockSpec((1, PAST_LEN, NA_DIM), lambda b, r: (b, 0, 0)),
                  pl.BlockSpec((1, NA_HEADS, GRID_W, NA_KEYS), lambda b, r: (_na_row_offset(r), 0, 0, 0))],
        out_specs=pl.BlockSpec((GRID_W, NA_DIM), lambda b, r: (b * NA_ROWS + r, 0)),
        out_shape=jax.ShapeDtypeStruct((N_LAT_TOK, NA_DIM), BF16),
        compiler_params=_cparams("parallel", "arbitrary"),
        name="lat_na",
    )(qkv, cache_k, cache_v, bias)


def _na_bias_tables(rpb):
    d = jnp.arange(NA_WIN_ROWS)[:, None]
    kr = jnp.arange(NA_WIN_ROWS)[None, :]
    dr = kr - d + NA_WIN_ROWS - 1
    qc = jnp.arange(GRID_W)[:, None]
    kc = jnp.arange(GRID_W)[None, :]
    col0 = jnp.clip(qc - NA_WIN_COLS // 2, 0, GRID_W - NA_WIN_COLS)
    in_win = (kc >= col0) & (kc < col0 + NA_WIN_COLS)
    dc = jnp.clip(kc - qc + NA_WIN_COLS - 1, 0, 2 * NA_WIN_COLS - 2)
    b = rpb.astype(F32)[:, dr[:, :, None, None], dc[None, None, :, :]]
    b = jnp.where(in_win[None, None, None], b, NEG_INF)
    return jnp.transpose(b, (1, 0, 3, 2, 4)).reshape(NA_WIN_ROWS, NA_HEADS, GRID_W, NA_KEYS)


CONV_TB = 1024


def _conv_kernel(x_ref, w_ref, b_ref, o_ref):
    i = pl.program_id(0)
    seq = jnp.where(i < N_CTX_TOK // CONV_TB, SEQ, DEC_SEQ)
    x = x_ref[...]
    pos = (lax.broadcasted_iota(I32, (CONV_TB, 1), 0) + i * CONV_TB) & (seq - 1)
    half = CONV_W // 2
    acc = x * w_ref[half:half + 1, :]
    for s in range(-half, half + 1):
        if s == 0:
            continue
        shifted = pltpu.roll(x, (-s) % CONV_TB, 0)
        valid = (pos + s >= 0) & (pos + s < seq)
        acc = acc + jnp.where(valid, shifted, 0.0) * w_ref[half + s:half + s + 1, :]
    acc = acc + b_ref[...]
    o_ref[...] = acc * _sigmoid(acc)


def _conv_act(xbc, conv_w, conv_b):
    return pl.pallas_call(
        _conv_kernel,
        grid=(N_TOK // CONV_TB,),
        in_specs=[pl.BlockSpec((CONV_TB, CONV_DIM), lambda i: (i, 0)),
                  pl.BlockSpec((CONV_W, CONV_DIM), lambda i: (0, 0)),
                  pl.BlockSpec((1, CONV_DIM), lambda i: (0, 0))],
        out_specs=pl.BlockSpec((CONV_TB, CONV_DIM), lambda i: (i, 0)),
        out_shape=jax.ShapeDtypeStruct((N_TOK, CONV_DIM), F32),
        compiler_params=_cparams("parallel"),
        name="conv_act",
    )(xbc, conv_w, conv_b)


def _chunk_seq(g):
    ctx_n = SEQ // CHUNK
    lat_n = DEC_SEQ // CHUNK
    is_ctx = g < N_CTX_CHUNKS
    gl = g - N_CTX_CHUNKS
    sid = jnp.where(is_ctx, g // ctx_n, BATCH + gl // lat_n)
    cin = jnp.where(is_ctx, g % ctx_n, gl % lat_n)
    n = jnp.where(is_ctx, ctx_n, lat_n)
    return sid, cin, n


def _ssd_kernel(xa_ref, dt_ref, h0_ref, dtb_ref, alog_ref, y_ref, st_ref, *, reverse):
    g = pl.program_id(0)
    gg = N_CHUNKS - 1 - g if reverse else g
    _, cin, n = _chunk_seq(gg)
    first = cin == (n - 1 if reverse else 0)

    @pl.when(first)
    def _():
        st_ref[0] = h0_ref[0]

    x = dt_ref[...] + dtb_ref[...]
    dt = jnp.maximum(x, 0.0) + jnp.log1p(jnp.exp(-jnp.abs(x)))
    dta = dt * -jnp.exp(alog_ref[...])
    ii = lax.broadcasted_iota(I32, (CHUNK, CHUNK), 0)
    jj = lax.broadcasted_iota(I32, (CHUNK, CHUNK), 1)
    tri = (jj >= ii) if reverse else (jj <= ii)
    cum = _dot_exact(tri.astype(F32), dta)
    cum_t = cum.T
    edge = 0 if reverse else CHUNK - 1
    tot = cum[edge:edge + 1, :]
    rep = SSM_HEADS // SSM_GROUPS
    for grp in range(SSM_GROUPS):
        bg = xa_ref[:, SSM_INNER + grp * SSM_STATE:SSM_INNER + (grp + 1) * SSM_STATE].astype(BF16)
        c0 = SSM_INNER + SSM_BC_DIM + grp * SSM_STATE
        cg = xa_ref[:, c0:c0 + SSM_STATE].astype(BF16)
        cb = _dot_nt(cg, bg)
        for h in range(grp * rep, (grp + 1) * rep):
            hs = slice(h * SSM_HEAD_DIM, (h + 1) * SSM_HEAD_DIM)
            col = cum[:, h:h + 1]
            row = cum_t[h:h + 1, :]
            decay = jnp.where(tri, jnp.exp(jnp.minimum(col - row, 0.0)), 0.0)
            xdt = xa_ref[:, hs] * dt[:, h:h + 1]
            state = st_ref[0, hs, :]
            y = _dot((cb * decay).astype(BF16), xdt.astype(BF16))
            y = y + _dot_nt(cg, state.astype(BF16)) * jnp.exp(col)
            y_ref[:, hs] = y
            toth = tot[:, h:h + 1]
            w = (xdt * jnp.exp(toth - col)).astype(BF16)
            upd = lax.dot_general(w, bg, (((0,), (0,)), ((), ())), preferred_element_type=F32)
            st_ref[0, hs, :] = state * jnp.exp(toth) + upd


def _ssd(xa, dt_raw, h0, dt_bias, a_log, reverse):
    order = (lambda g: N_CHUNKS - 1 - g) if reverse else (lambda g: g)
    seq_of = lambda g: _chunk_seq(order(g))[0]
    return pl.pallas_call(
        functools.partial(_ssd_kernel, reverse=reverse),
        grid=(N_CHUNKS,),
        in_specs=[pl.BlockSpec((CHUNK, CONV_DIM), lambda g: (order(g), 0)),
                  pl.BlockSpec((CHUNK, LANES), lambda g: (order(g), 0)),
                  pl.BlockSpec((1, SSM_INNER, SSM_STATE), lambda g: (seq_of(g), 0, 0)),
                  pl.BlockSpec((1, LANES), lambda g: (0, 0)),
                  pl.BlockSpec((1, LANES), lambda g: (0, 0))],
        out_specs=[pl.BlockSpec((CHUNK, SSM_INNER), lambda g: (order(g), 0)),
                   pl.BlockSpec((1, SSM_INNER, SSM_STATE), lambda g: (seq_of(g), 0, 0))],
        out_shape=[jax.ShapeDtypeStruct((N_TOK, SSM_INNER), F32),
                   jax.ShapeDtypeStruct((N_SEQ, SSM_INNER, SSM_STATE), F32)],
        compiler_params=_cparams("arbitrary"),
        name="ssd_bwd" if reverse else "ssd_fwd",
    )(xa, dt_raw, h0, dt_bias, a_log)


def _outproj_kernel(x_ref, nao_ref, go_ref, yf_ref, yb_ref, xs_ref, z_ref, dsk_ref, sn_ref, wo_ref,
                    m_ref, g2_ref, wr_ref, br_ref, xo_ref, h_ref, idx_ref, gate_ref, sel_ref):
    m = m_ref[0]
    z = z_ref[...]
    y = (yf_ref[...] + yb_ref[...] + xs_ref[...] * dsk_ref[...]) * (z * _sigmoid(z))
    s_o = _rms(y, sn_ref[...]).astype(BF16)
    mix = (_dot(nao_ref[...], wo_ref[0:NA_DIM, :])
           + _dot(go_ref[...], wo_ref[NA_DIM:NA_DIM + GQA_Q_DIM, :])
           + _dot(s_o, wo_ref[NA_DIM + GQA_Q_DIM:, :]))
    x = x_ref[...] + m[2:3] * mix
    xo_ref[...] = x
    h = _rms(x, g2_ref[...]) * (1.0 + m[4:5]) + m[3:4]
    h_ref[...] = h

    logits = _dot_exact(h, wr_ref[...]) + br_ref[...]
    lane = lax.broadcasted_iota(I32, logits.shape, 1).astype(F32)
    vals, idxs = [], []
    for _ in range(TOP_K):
        v = logits.max(axis=-1, keepdims=True)
        i = jnp.where(logits == v, lane, float(LANES)).min(axis=-1, keepdims=True)
        vals.append(v)
        idxs.append(i)
        logits = jnp.where(lane == i, -jnp.inf, logits)
    es = [jnp.exp(v - vals[0]) for v in vals]
    den = es[0] + es[1] + es[2] + es[3]
    idx_out = jnp.zeros(lane.shape, F32)
    gate_out = jnp.zeros(lane.shape, F32)
    sel = jnp.zeros(lane.shape, F32)
    for k in range(TOP_K):
        idx_out = jnp.where(lane == float(k), idxs[k], idx_out)
        gate_out = jnp.where(lane == float(k), es[k] / den, gate_out)
        sel = jnp.where(lane == idxs[k], 1.0, sel)
    idx_ref[...] = idx_out.astype(I32)
    gate_ref[...] = gate_out
    sel_ref[...] = sel.astype(BF16)


def _outproj(x, nao, go, yf, yb, xa, z, d_skip, ssm_norm, w_out, mods, gain2, w_router, b_router):
    row = lambda w: pl.BlockSpec((ROW_TILE, w), lambda i: (i, 0))
    full = lambda a, b: pl.BlockSpec((a, b), lambda i: (0, 0))
    return pl.pallas_call(
        _outproj_kernel,
        grid=(N_ROW_TILES,),
        in_specs=[row(D_MODEL), row(NA_DIM), row(GQA_Q_DIM), row(SSM_INNER), row(SSM_INNER),
                  row(SSM_INNER), row(SSM_INNER), full(1, SSM_INNER), full(1, SSM_INNER),
                  full(D_MIX, D_MODEL),
                  pl.BlockSpec((1, 6, D_MODEL), lambda i: (_cond_of_tile(i), 0, 0)),
                  full(1, D_MODEL), full(D_MODEL, LANES), full(1, LANES)],
        out_specs=[row(D_MODEL), row(D_MODEL), row(LANES), row(LANES), row(LANES)],
        out_shape=[jax.ShapeDtypeStruct((N_TOK, D_MODEL), F32),
                   jax.ShapeDtypeStruct((N_TOK, D_MODEL), F32),
                   jax.ShapeDtypeStruct((N_TOK, LANES), I32),
                   jax.ShapeDtypeStruct((N_TOK, LANES), F32),
                   jax.ShapeDtypeStruct((N_TOK, LANES), BF16)],
        compiler_params=_cparams("parallel"),
        name="outproj_router",
    )(x, nao, go, yf, yb, xa, z, d_skip, ssm_norm, w_out, mods, gain2, w_router, b_router)


RANK_TB = 512


def _rank_kernel(sel_ref, rank_ref, cnt_ref, carry):
    @pl.when(pl.program_id(0) == 0)
    def _():
        carry[...] = jnp.zeros_like(carry)

    sel = sel_ref[...]
    ii = lax.broadcasted_iota(I32, (RANK_TB, RANK_TB), 0)
    jj = lax.broadcasted_iota(I32, (RANK_TB, RANK_TB), 1)
    before = (jj < ii).astype(BF16)
    rank_ref[...] = _dot(before, sel) + carry[0:1, :]
    carry[...] = carry[...] + _dot(jnp.ones((8, RANK_TB), BF16), sel)
    cnt_ref[...] = carry[...]


def _ranks(sel):
    return pl.pallas_call(
        _rank_kernel,
        grid=(N_TOK // RANK_TB,),
        in_specs=[pl.BlockSpec((RANK_TB, LANES), lambda i: (i, 0))],
        out_specs=[pl.BlockSpec((RANK_TB, LANES), lambda i: (i, 0)),
                   pl.BlockSpec((8, LANES), lambda i: (0, 0))],
        out_shape=[jax.ShapeDtypeStruct((N_TOK, LANES), F32),
                   jax.ShapeDtypeStruct((8, LANES), F32)],
        scratch_shapes=[pltpu.VMEM((8, LANES), F32)],
        compiler_params=_cparams("arbitrary"),
        name="moe_ranks",
    )(sel)


def _row_copy(src, s, dst, d, sem):
    return pltpu.make_async_copy(src.at[pl.ds(s, 1), :], dst.at[pl.ds(d, 1), :], sem)


def _dispatch_kernel(dest_ref, h_ref, init_ref, out_ref, sem):
    del init_ref
    base = pl.program_id(0) * ROW_TILE * TOP_K

    def issue(t, carry):
        for k in range(TOP_K):
            _row_copy(h_ref, t, out_ref, dest_ref[base + t * TOP_K + k], sem).start()
        return carry

    def drain(t, carry):
        for k in range(TOP_K):
            _row_copy(h_ref, 0, out_ref, 0, sem).wait()
        return carry

    lax.fori_loop(0, ROW_TILE, issue, 0)
    lax.fori_loop(0, ROW_TILE, drain, 0)


def _dispatch(dest, h):
    init = jnp.zeros((MOE_NB * MOE_BM, D_MODEL), F32)
    return pl.pallas_call(
        _dispatch_kernel,
        grid_spec=pltpu.PrefetchScalarGridSpec(
            num_scalar_prefetch=1,
            grid=(N_ROW_TILES,),
            in_specs=[pl.BlockSpec((ROW_TILE, D_MODEL), lambda i, d: (i, 0)),
                      pl.BlockSpec(memory_space=pl.ANY)],
            out_specs=pl.BlockSpec(memory_space=pl.ANY),
            scratch_shapes=[pltpu.SemaphoreType.DMA(())]),
        out_shape=jax.ShapeDtypeStruct((MOE_NB * MOE_BM, D_MODEL), F32),
        input_output_aliases={2: 0},
        compiler_params=_cparams("arbitrary"),
        name="moe_dispatch",
    )(dest, h, init)


def _expert_kernel(be_ref, nu_ref, x_ref, wu_ref, bu_ref, wd_ref, bd_ref, y_ref):
    used = pl.program_id(0) < nu_ref[0]

    @pl.when(jnp.logical_not(used))
    def _():
        y_ref[...] = jnp.zeros_like(y_ref)

    @pl.when(used)
    def _():
        up = _dot(x_ref[...].astype(BF16), wu_ref[0]) + bu_ref[0]
        g = jnp.minimum(up[:, 0:D_FF], SWIGLU_LIMIT)
        lin = jnp.clip(up[:, D_FF:], -SWIGLU_LIMIT, SWIGLU_LIMIT)
        act = g * _sigmoid(SWIGLU_ALPHA * g) * (lin + 1.0)
        y_ref[...] = _dot(act.astype(BF16), wd_ref[0]) + bd_ref[0]


def _experts(blk_expert, n_used, xs, w_up, b_up, w_down, b_down):
    blk = lambda b, nu: jnp.minimum(b, nu[0] - 1)
    return pl.pallas_call(
        _expert_kernel,
        grid_spec=pltpu.PrefetchScalarGridSpec(
            num_scalar_prefetch=2,
            grid=(MOE_NB,),
            in_specs=[pl.BlockSpec((MOE_BM, D_MODEL), lambda b, be, nu: (blk(b, nu), 0)),
                      pl.BlockSpec((1, D_MODEL, 2 * D_FF), lambda b, be, nu: (be[blk(b, nu)], 0, 0)),
                      pl.BlockSpec((1, 1, 2 * D_FF), lambda b, be, nu: (be[blk(b, nu)], 0, 0)),
                      pl.BlockSpec((1, D_FF, D_MODEL), lambda b, be, nu: (be[blk(b, nu)], 0, 0)),
                      pl.BlockSpec((1, 1, D_MODEL), lambda b, be, nu: (be[blk(b, nu)], 0, 0))],
            out_specs=pl.BlockSpec((MOE_BM, D_MODEL), lambda b, be, nu: (b, 0))),
        out_shape=jax.ShapeDtypeStruct((MOE_NB * MOE_BM, D_MODEL), F32),
        compiler_params=_cparams("arbitrary"),
        name="moe_experts",
    )(blk_expert, n_used, xs, w_up, b_up, w_down, b_down)


def _combine_kernel(dest_ref, x_ref, gate_ref, m_ref, fn_ref, ys_ref, o_ref, buf, sem, *, final):
    base = pl.program_id(0) * ROW_TILE * TOP_K

    def issue(t, carry):
        for k in range(TOP_K):
            _row_copy(ys_ref, dest_ref[base + t * TOP_K + k], buf.at[k], t, sem).start()
        return carry

    def drain(t, carry):
        for k in range(TOP_K):
            _row_copy(ys_ref, 0, buf.at[k], 0, sem).wait()
        return carry

    lax.fori_loop(0, ROW_TILE, issue, 0)
    lax.fori_loop(0, ROW_TILE, drain, 0)
    gate = gate_ref[...]
    acc = buf[0] * gate[:, 0:1]
    for k in range(1, TOP_K):
        acc = acc + buf[k] * gate[:, k:k + 1]
    x = x_ref[...] + m_ref[0][5:6] * acc
    o_ref[...] = _rms(x, fn_ref[...]) if final else x


def _combine(dest, x, gates, mods, final_norm, ys, final):
    return pl.pallas_call(
        functools.partial(_combine_kernel, final=final),
        grid_spec=pltpu.PrefetchScalarGridSpec(
            num_scalar_prefetch=1,
            grid=(N_ROW_TILES,),
            in_specs=[pl.BlockSpec((ROW_TILE, D_MODEL), lambda i, d: (i, 0)),
                      pl.BlockSpec((ROW_TILE, LANES), lambda i, d: (i, 0)),
                      pl.BlockSpec((1, 6, D_MODEL), lambda i, d: (_cond_of_tile(i), 0, 0)),
                      pl.BlockSpec((1, D_MODEL), lambda i, d: (0, 0)),
                      pl.BlockSpec(memory_space=pl.ANY)],
            out_specs=pl.BlockSpec((ROW_TILE, D_MODEL), lambda i, d: (i, 0)),
            scratch_shapes=[pltpu.VMEM((TOP_K, ROW_TILE, D_MODEL), F32),
                            pltpu.SemaphoreType.DMA(())]),
        out_shape=jax.ShapeDtypeStruct((N_TOK, D_MODEL), F32),
        compiler_params=_cparams("arbitrary"),
        name="moe_combine",
    )(dest, x, gates, mods, final_norm, ys)


def _moe_plan(rank, cnt, top_idx):
    sizes = cnt[0, :N_EXPERTS].astype(I32)
    padded = (sizes + MOE_BM - 1) // MOE_BM * MOE_BM
    pad_ends = jnp.cumsum(padded)
    pad_starts = pad_ends - padded
    row = rank[:, :N_EXPERTS].astype(I32) + pad_starts[None, :]
    dest = jnp.take_along_axis(row, top_idx, axis=1).reshape(-1)
    blk_expert = jnp.minimum(
        jnp.searchsorted(pad_ends, jnp.arange(MOE_NB, dtype=I32) * MOE_BM, side='right'),
        N_EXPERTS - 1).astype(I32)
    n_used = (pad_ends[-1:] // MOE_BM).astype(I32)
    return dest, blk_expert, n_used


def _pad_lanes(v, fill=0.0):
    return jnp.pad(v, ((0, 0), (0, LANES - v.shape[-1])), constant_values=fill)


def kernel(x_prompt, x_sample, cache_na_k, cache_na_v, cache_gqa_k, cache_gqa_v, state_ssm, c, c_ctx, w_ada, b_ada, norm_mix, norm_ffn, w_in, na_rpb, gqa_q_norm, gqa_k_norm, ssm_conv_w, ssm_conv_b, ssm_dt_bias, ssm_a_log, ssm_d, ssm_norm, w_out, w_router, b_router, w_up, b_up, w_down, b_down, final_norm):
    x = jnp.concatenate([x_prompt.reshape(N_CTX_TOK, D_MODEL), x_sample.reshape(N_LAT_TOK, D_MODEL)], axis=0)
    conds = jnp.concatenate([c_ctx[None], c, jnp.zeros((COND_ROWS - N_COND, D_MODEL), F32)], axis=0)
    mods = _adaln(conds, w_ada, b_ada).reshape(DEPTH, COND_ROWS, 6, D_MODEL)
    cos, sin = _rope_tables()

    w_in_b = jnp.pad(w_in, ((0, 0), (0, 0), (0, IN_PAD - IN_DIM))).astype(BF16)
    w_out_b = w_out.astype(BF16)
    w_up_b = jnp.concatenate([w_up[..., 0::2], w_up[..., 1::2]], axis=-1).astype(BF16)
    b_up_s = jnp.concatenate([b_up[..., 0::2], b_up[..., 1::2]], axis=-1).reshape(DEPTH, N_EXPERTS, 1, 2 * D_FF)
    w_down_b = w_down.astype(BF16)
    b_down_s = b_down.reshape(DEPTH, N_EXPERTS, 1, D_MODEL)

    ctx_out = []
    for l in range(DEPTH):
        qkv, gqa, z, xbc, dt_raw = _inproj(x, mods[l], norm_mix[l][None], w_in_b[l])
        qn, kn = gqa_q_norm[l][None], gqa_k_norm[l][None]

        nao_c, go_c, gk_c = _ctx_attn(qkv, gqa, qn, kn)
        go_l = _lat_gqa(gqa, cache_gqa_k[:, l].reshape(DEC_BATCH, PAST_LEN, GQA_KV_DIM),
                        cache_gqa_v[:, l].reshape(DEC_BATCH, PAST_LEN, GQA_KV_DIM), cos, sin, qn, kn)
        nao_l = _lat_na(qkv, cache_na_k[:, l].reshape(DEC_BATCH, PAST_LEN, NA_DIM),
                        cache_na_v[:, l].reshape(DEC_BATCH, PAST_LEN, NA_DIM), _na_bias_tables(na_rpb[l]))

        xa = _conv_act(xbc, ssm_conv_w[l], ssm_conv_b[l][None])
        zeros = jnp.zeros((BATCH, SSM_INNER, SSM_STATE), F32)
        ys, sts = [], []
        for d in range(2):
            h0 = jnp.concatenate([zeros, state_ssm[:, l, d].reshape(DEC_BATCH, SSM_INNER, SSM_STATE)], axis=0)
            y, st = _ssd(xa, dt_raw, h0, _pad_lanes(ssm_dt_bias[l, d][None]), _pad_lanes(ssm_a_log[l, d][None]),
                         reverse=bool(d))
            ys.append(y)
            sts.append(st[:BATCH].reshape(BATCH, SSM_HEADS, SSM_HEAD_DIM, SSM_STATE))

        x, h, top_idx, gates, sel = _outproj(
            x, jnp.concatenate([nao_c, nao_l], axis=0), jnp.concatenate([go_c, go_l], axis=0),
            ys[0], ys[1], xa, z, jnp.repeat(ssm_d[l], SSM_HEAD_DIM)[None], ssm_norm[l][None], w_out_b[l],
            mods[l], norm_ffn[l][None], _pad_lanes(w_router[l]), _pad_lanes(b_router[l][None], NEG_INF))

        rank, cnt = _ranks(sel)
        dest, blk_expert, n_used = _moe_plan(rank, cnt, top_idx[:, :TOP_K])
        y_sorted = _experts(blk_expert, n_used, _dispatch(dest, h), w_up_b[l], b_up_s[l], w_down_b[l], b_down_s[l])
        x = _combine(dest, x, gates, mods[l], final_norm[None], y_sorted, final=(l == DEPTH - 1))

        ctx_out.append((
            qkv[:N_CTX_TOK, NA_DIM:2 * NA_DIM].reshape(BATCH, SEQ, NA_HEADS, HEAD_DIM),
            qkv[:N_CTX_TOK, 2 * NA_DIM:].reshape(BATCH, SEQ, NA_HEADS, HEAD_DIM),
            gk_c.reshape(BATCH, SEQ, GQA_KV_HEADS, HEAD_DIM),
            gqa[:N_CTX_TOK, GQA_Q_DIM + GQA_KV_DIM:].reshape(BATCH, SEQ, GQA_KV_HEADS, HEAD_DIM),
            jnp.stack(sts, axis=1)))

    y_prompt = x[:N_CTX_TOK].reshape(BATCH, SEQ, D_MODEL)
    y_sample = x[N_CTX_TOK:].reshape(DEC_BATCH, DEC_SEQ, D_MODEL)
    return (y_prompt, y_sample) + tuple(jnp.stack([e[i] for e in ctx_out], axis=1) for i in range(5))
```

```python
import functools

import jax
import jax.numpy as jnp
from jax import lax
from jax.experimental import pallas as pl
from jax.experimental.pallas import tpu as pltpu

F32 = jnp.float32
BF16 = jnp.bfloat16
I32 = jnp.int32

D_MODEL = 1024
BATCH = 16
SEQ = 256
DEPTH = 2
DEC_BATCH = 2
DEC_SEQ = 2048
PAST_LEN = 512
GRID_W = 64
HEAD_DIM = 64
NA_HEADS = 4
NA_WIN_ROWS = 8
NA_WIN_COLS = 16
GQA_HEADS = 4
GQA_KV_HEADS = 2
ROPE_THETA = 10000.0
SSM_HEADS = 8
SSM_HEAD_DIM = 64
SSM_STATE = 64
SSM_GROUPS = 2
SSM_INNER = SSM_HEADS * SSM_HEAD_DIM
SSM_BC_DIM = SSM_GROUPS * SSM_STATE
CONV_DIM = SSM_INNER + 2 * SSM_BC_DIM
CONV_W = 5
CHUNK = 128
NA_DIM = NA_HEADS * HEAD_DIM
GQA_Q_DIM = GQA_HEADS * HEAD_DIM
GQA_KV_DIM = GQA_KV_HEADS * HEAD_DIM
D_MIX = NA_DIM + GQA_Q_DIM + SSM_INNER
IN_DIM = 3 * NA_DIM + GQA_Q_DIM + 2 * GQA_KV_DIM + SSM_INNER + CONV_DIM + SSM_HEADS
N_EXPERTS = 32
TOP_K = 4
D_FF = D_MODEL
SWIGLU_LIMIT = 7.0
SWIGLU_ALPHA = 1.702
EPS = 1e-6
NEG_INF = -1e30

LANES = 128
N_CTX_TOK = BATCH * SEQ
N_LAT_TOK = DEC_BATCH * DEC_SEQ
N_TOK = N_CTX_TOK + N_LAT_TOK
N_COND = 1 + DEC_BATCH
COND_ROWS = 16
IN_PAD = 3 * NA_DIM + GQA_Q_DIM + 2 * GQA_KV_DIM + SSM_INNER + CONV_DIM + LANES
ROW_TILE = 256
N_ROW_TILES = N_TOK // ROW_TILE
MOE_BM = 256
N_SLOTS = N_TOK * TOP_K
MOE_NB = N_SLOTS // MOE_BM + N_EXPERTS
UP_GROUP = 256
N_SEQ = BATCH + DEC_BATCH
N_CHUNKS = N_TOK // CHUNK
N_CTX_CHUNKS = N_CTX_TOK // CHUNK
VMEM_LIMIT = 56 * 1024 * 1024


def _cparams(*sem):
    return pltpu.CompilerParams(dimension_semantics=sem, vmem_limit_bytes=VMEM_LIMIT)


def _sigmoid(x):
    return 1.0 / (1.0 + jnp.exp(-x))


def _dot(a, b):
    return jnp.dot(a, b, preferred_element_type=F32)


def _dot_nt(a, b):
    return lax.dot_general(a, b, (((1,), (1,)), ((), ())), preferred_element_type=F32)


def _dot_exact(a, b):
    return jnp.dot(a, b, preferred_element_type=F32, precision=lax.Precision.HIGHEST)


def _rms(x, g):
    return x * lax.rsqrt(jnp.mean(x * x, axis=-1, keepdims=True) + EPS) * g


def _cond_of_tile(i):
    ctx_tiles = N_CTX_TOK // ROW_TILE
    return jnp.where(i < ctx_tiles, 0, 1 + (i - ctx_tiles) // (DEC_SEQ // ROW_TILE))


def _adaln_kernel(c_ref, w_ref, b_ref, o_ref):
    c = c_ref[...]
    s = (c * _sigmoid(c)).astype(BF16)
    o_ref[0] = _dot(s, w_ref[0].astype(BF16)) + b_ref[0]


def _adaln(conds, w_ada, b_ada):
    tn = 1536
    return pl.pallas_call(
        _adaln_kernel,
        grid=(DEPTH, 6 * D_MODEL // tn),
        in_specs=[pl.BlockSpec((COND_ROWS, D_MODEL), lambda l, j: (0, 0)),
                  pl.BlockSpec((1, D_MODEL, tn), lambda l, j: (l, 0, j)),
                  pl.BlockSpec((1, 1, tn), lambda l, j: (l, 0, j))],
        out_specs=pl.BlockSpec((1, COND_ROWS, tn), lambda l, j: (l, 0, j)),
        out_shape=jax.ShapeDtypeStruct((DEPTH, COND_ROWS, 6 * D_MODEL), F32),
        compiler_params=_cparams("parallel", "parallel"),
        name="adaln",
    )(conds, w_ada, b_ada.reshape(DEPTH, 1, 6 * D_MODEL))


_IN_SPLITS = (3 * NA_DIM, GQA_Q_DIM + 2 * GQA_KV_DIM, SSM_INNER, CONV_DIM, LANES)


def _inproj_kernel(x_ref, m_ref, g_ref, w_ref, qkv_ref, gqa_ref, z_ref, xbc_ref, dt_ref):
    m = m_ref[0]
    h = _rms(x_ref[...], g_ref[...]) * (1.0 + m[1:2]) + m[0:1]
    p = _dot(h.astype(BF16), w_ref[...])
    off = 0
    for ref, width in zip((qkv_ref, gqa_ref, z_ref, xbc_ref, dt_ref), _IN_SPLITS):
        ref[...] = p[:, off:off + width]
        off += width


def _inproj(x, mods, gain, w_in):
    row = lambda w: pl.BlockSpec((ROW_TILE, w), lambda i: (i, 0))
    return pl.pallas_call(
        _inproj_kernel,
        grid=(N_ROW_TILES,),
        in_specs=[row(D_MODEL),
                  pl.BlockSpec((1, 6, D_MODEL), lambda i: (_cond_of_tile(i), 0, 0)),
                  pl.BlockSpec((1, D_MODEL), lambda i: (0, 0)),
                  pl.BlockSpec((D_MODEL, IN_PAD), lambda i: (0, 0))],
        out_specs=[row(w) for w in _IN_SPLITS],
        out_shape=[jax.ShapeDtypeStruct((N_TOK, w), F32) for w in _IN_SPLITS],
        compiler_params=_cparams("parallel"),
        name="inproj",
    )(x, mods, gain, w_in)


def _softmax_pv(scores, values):
    m = scores[0].max(axis=-1, keepdims=True)
    for s in scores[1:]:
        m = jnp.maximum(m, s.max(axis=-1, keepdims=True))
    den = 0.0
    acc = 0.0
    for s, v in zip(scores, values):
        e = jnp.exp(s - m)
        den = den + e.sum(axis=-1, keepdims=True)
        acc = acc + _dot(e.astype(BF16), v)
    return acc / den


def _heads_rms(x, n_heads, g):
    return jnp.concatenate(
        [_rms(x[:, h * HEAD_DIM:(h + 1) * HEAD_DIM], g) for h in range(n_heads)], axis=-1)


def _rope(x, cos, sin_signed):
    w = x.shape[-1]
    lane = lax.broadcasted_iota(I32, x.shape, 1)
    partner = jnp.where((lane & 1) == 0, pltpu.roll(x, w - 1, 1), pltpu.roll(x, 1, 1))
    return x * cos + partner * sin_signed


_ATT_SCALE = HEAD_DIM ** -0.5


def _ctx_attn_kernel(qkv_ref, gqa_ref, qn_ref, kn_ref, nao_ref, go_ref, gk_ref):
    outs = []
    for h in range(NA_HEADS):
        sl = slice(h * HEAD_DIM, (h + 1) * HEAD_DIM)
        q = qkv_ref[:, sl].astype(BF16)
        k = qkv_ref[:, NA_DIM + h * HEAD_DIM:NA_DIM + (h + 1) * HEAD_DIM].astype(BF16)
        v = qkv_ref[:, 2 * NA_DIM + h * HEAD_DIM:2 * NA_DIM + (h + 1) * HEAD_DIM].astype(BF16)
        outs.append(_softmax_pv([_dot_nt(q, k) * _ATT_SCALE], [v]))
    nao_ref[...] = jnp.concatenate(outs, axis=-1).astype(BF16)

    gq = _heads_rms(gqa_ref[:, 0:GQA_Q_DIM], GQA_HEADS, qn_ref[...])
    gk = _heads_rms(gqa_ref[:, GQA_Q_DIM:GQA_Q_DIM + GQA_KV_DIM], GQA_KV_HEADS, kn_ref[...])
    gk_ref[...] = gk
    rep = GQA_HEADS // GQA_KV_HEADS
    outs = []
    for h in range(GQA_HEADS):
        g = h // rep
        q = gq[:, h * HEAD_DIM:(h + 1) * HEAD_DIM].astype(BF16)
        k = gk[:, g * HEAD_DIM:(g + 1) * HEAD_DIM].astype(BF16)
        v0 = GQA_Q_DIM + GQA_KV_DIM + g * HEAD_DIM
        v = gqa_ref[:, v0:v0 + HEAD_DIM].astype(BF16)
        outs.append(_softmax_pv([_dot_nt(q, k) * _ATT_SCALE], [v]))
    go_ref[...] = jnp.concatenate(outs, axis=-1).astype(BF16)


def _ctx_attn(qkv, gqa, q_norm, k_norm):
    return pl.pallas_call(
        _ctx_attn_kernel,
        grid=(BATCH,),
        in_specs=[pl.BlockSpec((SEQ, 3 * NA_DIM), lambda b: (b, 0)),
                  pl.BlockSpec((SEQ, GQA_Q_DIM + 2 * GQA_KV_DIM), lambda b: (b, 0)),
                  pl.BlockSpec((1, HEAD_DIM), lambda b: (0, 0)),
                  pl.BlockSpec((1, HEAD_DIM), lambda b: (0, 0))],
        out_specs=[pl.BlockSpec((SEQ, NA_DIM), lambda b: (b, 0)),
                   pl.BlockSpec((SEQ, GQA_Q_DIM), lambda b: (b, 0)),
                   pl.BlockSpec((SEQ, GQA_KV_DIM), lambda b: (b, 0))],
        out_shape=[jax.ShapeDtypeStruct((N_CTX_TOK, NA_DIM), BF16),
                   jax.ShapeDtypeStruct((N_CTX_TOK, GQA_Q_DIM), BF16),
                   jax.ShapeDtypeStruct((N_CTX_TOK, GQA_KV_DIM), F32)],
        compiler_params=_cparams("parallel"),
        name="ctx_attn",
    )(qkv, gqa, q_norm, k_norm)


GQA_TQ = 256
GQA_KEYS = PAST_LEN + DEC_SEQ


def _lat_gqa_kernel(gqa_ref, ck_ref, cv_ref, cos_ref, sin_ref, qn_ref, kn_ref, o_ref, kbuf, vbuf):
    qb = pl.program_id(1)

    @pl.when(qb == 0)
    def _():
        kbuf[0:PAST_LEN, :] = ck_ref[0].astype(BF16)
        vbuf[0:PAST_LEN, :] = cv_ref[0].astype(BF16)
        k = _heads_rms(gqa_ref[:, GQA_Q_DIM:GQA_Q_DIM + GQA_KV_DIM], GQA_KV_HEADS, kn_ref[...])
        k = _rope(k, cos_ref[:, 0:GQA_KV_DIM], sin_ref[:, 0:GQA_KV_DIM])
        kbuf[PAST_LEN:GQA_KEYS, :] = k.astype(BF16)
        vbuf[PAST_LEN:GQA_KEYS, :] = gqa_ref[:, GQA_Q_DIM + GQA_KV_DIM:].astype(BF16)

    r0 = pl.multiple_of(qb * GQA_TQ, GQA_TQ)
    q = _heads_rms(gqa_ref[pl.ds(r0, GQA_TQ), 0:GQA_Q_DIM], GQA_HEADS, qn_ref[...])
    q = _rope(q, cos_ref[pl.ds(r0, GQA_TQ), :], sin_ref[pl.ds(r0, GQA_TQ), :]).astype(BF16)
    rep = GQA_HEADS // GQA_KV_HEADS
    outs = []
    for h in range(GQA_HEADS):
        g = h // rep
        k = kbuf[:, g * HEAD_DIM:(g + 1) * HEAD_DIM]
        v = vbuf[:, g * HEAD_DIM:(g + 1) * HEAD_DIM]
        s = _dot_nt(q[:, h * HEAD_DIM:(h + 1) * HEAD_DIM], k) * _ATT_SCALE
        outs.append(_softmax_pv([s], [v]))
    o_ref[...] = jnp.concatenate(outs, axis=-1).astype(BF16)


def _lat_gqa(gqa, cache_k, cache_v, cos, sin, q_norm, k_norm):
    lat_blk = N_CTX_TOK // DEC_SEQ
    return pl.pallas_call(
        _lat_gqa_kernel,
        grid=(DEC_BATCH, DEC_SEQ // GQA_TQ),
        in_specs=[pl.BlockSpec((DEC_SEQ, GQA_Q_DIM + 2 * GQA_KV_DIM), lambda b, q: (lat_blk + b, 0)),
                  pl.BlockSpec((1, PAST_LEN, GQA_KV_DIM), lambda b, q: (b, 0, 0)),
                  pl.BlockSpec((1, PAST_LEN, GQA_KV_DIM), lambda b, q: (b, 0, 0)),
                  pl.BlockSpec((DEC_SEQ, GQA_Q_DIM), lambda b, q: (0, 0)),
                  pl.BlockSpec((DEC_SEQ, GQA_Q_DIM), lambda b, q: (0, 0)),
                  pl.BlockSpec((1, HEAD_DIM), lambda b, q: (0, 0)),
                  pl.BlockSpec((1, HEAD_DIM), lambda b, q: (0, 0))],
        out_specs=pl.BlockSpec((GQA_TQ, GQA_Q_DIM), lambda b, q: (b * (DEC_SEQ // GQA_TQ) + q, 0)),
        out_shape=jax.ShapeDtypeStruct((N_LAT_TOK, GQA_Q_DIM), BF16),
        scratch_shapes=[pltpu.VMEM((GQA_KEYS, GQA_KV_DIM), BF16),
                        pltpu.VMEM((GQA_KEYS, GQA_KV_DIM), BF16)],
        compiler_params=_cparams("arbitrary", "arbitrary"),
        name="lat_gqa",
    )(gqa, cache_k, cache_v, cos, sin, q_norm, k_norm)


def _rope_tables():
    t = jnp.arange(DEC_SEQ)
    row = (t // GRID_W).astype(F32)
    col = (t % GRID_W).astype(F32)
    axis_dim = HEAD_DIM // 2
    inv_freq = ROPE_THETA ** (-jnp.arange(0, axis_dim, 2, dtype=F32) / axis_dim)
    ang = jnp.concatenate([row[:, None] * inv_freq, col[:, None] * inv_freq], axis=-1)
    cos = jnp.repeat(jnp.cos(ang), 2, axis=-1)
    sin = jnp.repeat(jnp.sin(ang), 2, axis=-1) * jnp.tile(jnp.array([-1.0, 1.0], F32), HEAD_DIM // 2)
    return jnp.tile(cos, (1, GQA_HEADS)), jnp.tile(sin, (1, GQA_HEADS))


NA_ROWS = DEC_SEQ // GRID_W
NA_KEYS = NA_WIN_ROWS * GRID_W


def _lat_na_kernel(qkv_ref, ck_ref, cv_ref, bias_ref, o_ref):
    r = pl.program_id(1)
    r0 = jnp.clip(r - NA_WIN_ROWS // 2, 0, NA_ROWS - NA_WIN_ROWS)
    q0 = pl.multiple_of(r * GRID_W, GRID_W)
    k0 = pl.multiple_of(r0 * GRID_W, GRID_W)
    outs = []
    for h in range(NA_HEADS):
        c0 = h * HEAD_DIM
        q = qkv_ref[pl.ds(q0, GRID_W), c0:c0 + HEAD_DIM].astype(BF16)
        k = qkv_ref[pl.ds(k0, NA_KEYS), NA_DIM + c0:NA_DIM + c0 + HEAD_DIM].astype(BF16)
        v = qkv_ref[pl.ds(k0, NA_KEYS), 2 * NA_DIM + c0:2 * NA_DIM + c0 + HEAD_DIM].astype(BF16)
        kc = ck_ref[0, :, c0:c0 + HEAD_DIM].astype(BF16)
        vc = cv_ref[0, :, c0:c0 + HEAD_DIM].astype(BF16)
        s_nb = _dot_nt(q, k) * _ATT_SCALE + bias_ref[0, h]
        s_ctx = _dot_nt(q, kc) * _ATT_SCALE
        outs.append(_softmax_pv([s_nb, s_ctx], [v, vc]))
    o_ref[...] = jnp.concatenate(outs, axis=-1).astype(BF16)


def _na_row_offset(r):
    return r - jnp.clip(r - NA_WIN_ROWS // 2, 0, NA_ROWS - NA_WIN_ROWS)


def _lat_na(qkv, cache_k, cache_v, bias):
    lat_blk = N_CTX_TOK // DEC_SEQ
    return pl.pallas_call(
        _lat_na_kernel,
        grid=(DEC_BATCH, NA_ROWS),
        in_specs=[pl.BlockSpec((DEC_SEQ, 3 * NA_DIM), lambda b, r: (lat_blk + b, 0)),
                  pl.BlockSpec((1, PAST_LEN, NA_DIM), lambda b, r: (b, 0, 0)),
                  pl.BlockSpec((1, PAST_LEN, NA_DIM), lambda b, r: (b, 0, 0)),
                  pl.BlockSpec((1, NA_HEADS, GRID_W, NA_KEYS), lambda b, r: (_na_row_offset(r), 0, 0, 0))],
        out_specs=pl.BlockSpec((GRID_W, NA_DIM), lambda b, r: (b * NA_ROWS + r, 0)),
        out_shape=jax.ShapeDtypeStruct((N_LAT_TOK, NA_DIM), BF16),
        compiler_params=_cparams("parallel", "arbitrary"),
        name="lat_na",
    )(qkv, cache_k, cache_v, bias)


def _na_bias_tables(rpb):
    d = jnp.arange(NA_WIN_ROWS)[:, None]
    kr = jnp.arange(NA_WIN_ROWS)[None, :]
    dr = kr - d + NA_WIN_ROWS - 1
    qc = jnp.arange(GRID_W)[:, None]
    kc = jnp.arange(GRID_W)[None, :]
    col0 = jnp.clip(qc - NA_WIN_COLS // 2, 0, GRID_W - NA_WIN_COLS)
    in_win = (kc >= col0) & (kc < col0 + NA_WIN_COLS)
    dc = jnp.clip(kc - qc + NA_WIN_COLS - 1, 0, 2 * NA_WIN_COLS - 2)
    b = rpb.astype(F32)[:, dr[:, :, None, None], dc[None, None, :, :]]
    b = jnp.where(in_win[None, None, None], b, NEG_INF)
    return jnp.transpose(b, (1, 0, 3, 2, 4)).reshape(NA_WIN_ROWS, NA_HEADS, GRID_W, NA_KEYS)


CONV_TB = 1024
CONV_HALO = 8
CONV_HALO_BLOCKS = CONV_TB // CONV_HALO


def _conv_kernel(prev_ref, x_ref, next_ref, w_ref, b_ref, o_ref):
    i = pl.program_id(0)
    seq = jnp.where(i < N_CTX_TOK // CONV_TB, SEQ, DEC_SEQ)
    x = x_ref[...]
    ext = jnp.concatenate([prev_ref[...], x, next_ref[...]], axis=0)
    n_ext = CONV_TB + 2 * CONV_HALO
    pos = (lax.broadcasted_iota(I32, (CONV_TB, 1), 0) + i * CONV_TB) & (seq - 1)
    half = CONV_W // 2
    acc = x * w_ref[half:half + 1, :]
    for s in range(-half, half + 1):
        if s == 0:
            continue
        shifted = pltpu.roll(ext, (-s) % n_ext, 0)[CONV_HALO:CONV_HALO + CONV_TB]
        valid = (pos + s >= 0) & (pos + s < seq)
        acc = acc + jnp.where(valid, shifted, 0.0) * w_ref[half + s:half + s + 1, :]
    acc = acc + b_ref[...]
    o_ref[...] = acc * _sigmoid(acc)


def _conv_act(xbc, conv_w, conv_b):
    return pl.pallas_call(
        _conv_kernel,
        grid=(N_TOK // CONV_TB,),
        in_specs=[pl.BlockSpec((CONV_HALO, CONV_DIM),
                               lambda i: (jnp.maximum(i * CONV_HALO_BLOCKS - 1, 0), 0)),
                  pl.BlockSpec((CONV_TB, CONV_DIM), lambda i: (i, 0)),
                  pl.BlockSpec((CONV_HALO, CONV_DIM),
                               lambda i: (jnp.minimum((i + 1) * CONV_HALO_BLOCKS, N_TOK // CONV_HALO - 1), 0)),
                  pl.BlockSpec((CONV_W, CONV_DIM), lambda i: (0, 0)),
                  pl.BlockSpec((1, CONV_DIM), lambda i: (0, 0))],
        out_specs=pl.BlockSpec((CONV_TB, CONV_DIM), lambda i: (i, 0)),
        out_shape=jax.ShapeDtypeStruct((N_TOK, CONV_DIM), F32),
        compiler_params=_cparams("parallel"),
        name="conv_act",
    )(xbc, xbc, xbc, conv_w, conv_b)


def _chunk_seq(g):
    ctx_n = SEQ // CHUNK
    lat_n = DEC_SEQ // CHUNK
    is_ctx = g < N_CTX_CHUNKS
    gl = g - N_CTX_CHUNKS
    sid = jnp.where(is_ctx, g // ctx_n, BATCH + gl // lat_n)
    cin = jnp.where(is_ctx, g % ctx_n, gl % lat_n)
    n = jnp.where(is_ctx, ctx_n, lat_n)
    return sid, cin, n


def _ssd_kernel(xa_ref, dt_ref, h0_ref, dtb_ref, alog_ref, y_ref, st_ref, *, reverse):
    g = pl.program_id(0)
    gg = N_CHUNKS - 1 - g if reverse else g
    _, cin, n = _chunk_seq(gg)
    first = cin == (n - 1 if reverse else 0)

    @pl.when(first)
    def _():
        st_ref[0] = h0_ref[0]

    x = dt_ref[...] + dtb_ref[...]
    dt = jnp.maximum(x, 0.0) + jnp.log1p(jnp.exp(-jnp.abs(x)))
    dta = dt * -jnp.exp(alog_ref[...])
    ii = lax.broadcasted_iota(I32, (CHUNK, CHUNK), 0)
    jj = lax.broadcasted_iota(I32, (CHUNK, CHUNK), 1)
    tri = (jj >= ii) if reverse else (jj <= ii)
    cum = _dot_exact(tri.astype(F32), dta)
    cum_t = cum.T
    edge = 0 if reverse else CHUNK - 1
    tot = cum[edge:edge + 1, :]
    rep = SSM_HEADS // SSM_GROUPS
    for grp in range(SSM_GROUPS):
        bg = xa_ref[:, SSM_INNER + grp * SSM_STATE:SSM_INNER + (grp + 1) * SSM_STATE].astype(BF16)
        c0 = SSM_INNER + SSM_BC_DIM + grp * SSM_STATE
        cg = xa_ref[:, c0:c0 + SSM_STATE].astype(BF16)
        cb = _dot_nt(cg, bg)
        for h in range(grp * rep, (grp + 1) * rep):
            hs = slice(h * SSM_HEAD_DIM, (h + 1) * SSM_HEAD_DIM)
            col = cum[:, h:h + 1]
            row = cum_t[h:h + 1, :]
            decay = jnp.where(tri, jnp.exp(jnp.minimum(col - row, 0.0)), 0.0)
            xdt = xa_ref[:, hs] * dt[:, h:h + 1]
            state = st_ref[0, hs, :]
            y = _dot((cb * decay).astype(BF16), xdt.astype(BF16))
            y = y + _dot_nt(cg, state.astype(BF16)) * jnp.exp(col)
            y_ref[:, hs] = y
            toth = tot[:, h:h + 1]
            w = (xdt * jnp.exp(toth - col)).astype(BF16)
            upd = lax.dot_general(w, bg, (((0,), (0,)), ((), ())), preferred_element_type=F32)
            st_ref[0, hs, :] = state * jnp.exp(toth) + upd


def _ssd(xa, dt_raw, h0, dt_bias, a_log, reverse):
    order = (lambda g: N_CHUNKS - 1 - g) if reverse else (lambda g: g)
    seq_of = lambda g: _chunk_seq(order(g))[0]
    return pl.pallas_call(
        functools.partial(_ssd_kernel, reverse=reverse),
        grid=(N_CHUNKS,),
        in_specs=[pl.BlockSpec((CHUNK, CONV_DIM), lambda g: (order(g), 0)),
                  pl.BlockSpec((CHUNK, LANES), lambda g: (order(g), 0)),
                  pl.BlockSpec((1, SSM_INNER, SSM_STATE), lambda g: (seq_of(g), 0, 0)),
                  pl.BlockSpec((1, LANES), lambda g: (0, 0)),
                  pl.BlockSpec((1, LANES), lambda g: (0, 0))],
        out_specs=[pl.BlockSpec((CHUNK, SSM_INNER), lambda g: (order(g), 0)),
                   pl.BlockSpec((1, SSM_INNER, SSM_STATE), lambda g: (seq_of(g), 0, 0))],
        out_shape=[jax.ShapeDtypeStruct((N_TOK, SSM_INNER), F32),
                   jax.ShapeDtypeStruct((N_SEQ, SSM_INNER, SSM_STATE), F32)],
        compiler_params=_cparams("arbitrary"),
        name="ssd_bwd" if reverse else "ssd_fwd",
    )(xa, dt_raw, h0, dt_bias, a_log)


def _outproj_kernel(x_ref, nao_ref, go_ref, yf_ref, yb_ref, xs_ref, z_ref, dsk_ref, sn_ref, wo_ref,
                    m_ref, g2_ref, wr_ref, br_ref, xo_ref, h_ref, idx_ref, gate_ref, sel_ref):
    m = m_ref[0]
    z = z_ref[...]
    y = (yf_ref[...] + yb_ref[...] + xs_ref[...] * dsk_ref[...]) * (z * _sigmoid(z))
    s_o = _rms(y, sn_ref[...]).astype(BF16)
    mix = (_dot(nao_ref[...], wo_ref[0:NA_DIM, :])
           + _dot(go_ref[...], wo_ref[NA_DIM:NA_DIM + GQA_Q_DIM, :])
           + _dot(s_o, wo_ref[NA_DIM + GQA_Q_DIM:, :]))
    x = x_ref[...] + m[2:3] * mix
    xo_ref[...] = x
    h = _rms(x, g2_ref[...]) * (1.0 + m[4:5]) + m[3:4]
    h_ref[...] = h

    logits = _dot_exact(h, wr_ref[...]) + br_ref[...]
    lane = lax.broadcasted_iota(I32, logits.shape, 1).astype(F32)
    vals, idxs = [], []
    for _ in range(TOP_K):
        v = logits.max(axis=-1, keepdims=True)
        i = jnp.where(logits == v, lane, float(LANES)).min(axis=-1, keepdims=True)
        vals.append(v)
        idxs.append(i)
        logits = jnp.where(lane == i, -jnp.inf, logits)
    es = [jnp.exp(v - vals[0]) for v in vals]
    den = es[0] + es[1] + es[2] + es[3]
    idx_out = jnp.zeros(lane.shape, F32)
    gate_out = jnp.zeros(lane.shape, F32)
    sel = jnp.zeros(lane.shape, F32)
    for k in range(TOP_K):
        idx_out = jnp.where(lane == float(k), idxs[k], idx_out)
        gate_out = jnp.where(lane == float(k), es[k] / den, gate_out)
        sel = jnp.where(lane == idxs[k], 1.0, sel)
    idx_ref[...] = idx_out.astype(I32)
    gate_ref[...] = gate_out
    sel_ref[...] = sel.astype(BF16)


def _outproj(x, nao, go, yf, yb, xa, z, d_skip, ssm_norm, w_out, mods, gain2, w_router, b_router):
    row = lambda w: pl.BlockSpec((ROW_TILE, w), lambda i: (i, 0))
    full = lambda a, b: pl.BlockSpec((a, b), lambda i: (0, 0))
    return pl.pallas_call(
        _outproj_kernel,
        grid=(N_ROW_TILES,),
        in_specs=[row(D_MODEL), row(NA_DIM), row(GQA_Q_DIM), row(SSM_INNER), row(SSM_INNER),
                  row(SSM_INNER), row(SSM_INNER), full(1, SSM_INNER), full(1, SSM_INNER),
                  full(D_MIX, D_MODEL),
                  pl.BlockSpec((1, 6, D_MODEL), lambda i: (_cond_of_tile(i), 0, 0)),
                  full(1, D_MODEL), full(D_MODEL, LANES), full(1, LANES)],
        out_specs=[row(D_MODEL), row(D_MODEL), row(LANES), row(LANES), row(LANES)],
        out_shape=[jax.ShapeDtypeStruct((N_TOK, D_MODEL), F32),
                   jax.ShapeDtypeStruct((N_TOK, D_MODEL), F32),
                   jax.ShapeDtypeStruct((N_TOK, LANES), I32),
                   jax.ShapeDtypeStruct((N_TOK, LANES), F32),
                   jax.ShapeDtypeStruct((N_TOK, LANES), BF16)],
        compiler_params=_cparams("parallel"),
        name="outproj_router",
    )(x, nao, go, yf, yb, xa, z, d_skip, ssm_norm, w_out, mods, gain2, w_router, b_router)


RANK_TB = 512


def _rank_kernel(sel_ref, rank_ref, cnt_ref, carry):
    @pl.when(pl.program_id(0) == 0)
    def _():
        carry[...] = jnp.zeros_like(carry)

    sel = sel_ref[...]
    ii = lax.broadcasted_iota(I32, (RANK_TB, RANK_TB), 0)
    jj = lax.broadcasted_iota(I32, (RANK_TB, RANK_TB), 1)
    before = (jj < ii).astype(BF16)
    rank_ref[...] = _dot(before, sel) + carry[0:1, :]
    carry[...] = carry[...] + _dot(jnp.ones((8, RANK_TB), BF16), sel)
    cnt_ref[...] = carry[...]


def _ranks(sel):
    return pl.pallas_call(
        _rank_kernel,
        grid=(N_TOK // RANK_TB,),
        in_specs=[pl.BlockSpec((RANK_TB, LANES), lambda i: (i, 0))],
        out_specs=[pl.BlockSpec((RANK_TB, LANES), lambda i: (i, 0)),
                   pl.BlockSpec((8, LANES), lambda i: (0, 0))],
        out_shape=[jax.ShapeDtypeStruct((N_TOK, LANES), F32),
                   jax.ShapeDtypeStruct((8, LANES), F32)],
        scratch_shapes=[pltpu.VMEM((8, LANES), F32)],
        compiler_params=_cparams("arbitrary"),
        name="moe_ranks",
    )(sel)


def _row_copy(src, s, dst, d, sem):
    return pltpu.make_async_copy(src.at[pl.ds(s, 1), :], dst.at[pl.ds(d, 1), :], sem)


def _dispatch_kernel(dest_ref, h_ref, init_ref, out_ref, sem):
    del init_ref
    base = pl.program_id(0) * ROW_TILE * TOP_K

    def issue(t, carry):
        for k in range(TOP_K):
            _row_copy(h_ref, t, out_ref, dest_ref[base + t * TOP_K + k], sem).start()
        return carry

    def drain(t, carry):
        for k in range(TOP_K):
            _row_copy(h_ref, 0, out_ref, 0, sem).wait()
        return carry

    lax.fori_loop(0, ROW_TILE, issue, 0)
    lax.fori_loop(0, ROW_TILE, drain, 0)


def _dispatch(dest, h):
    init = jnp.zeros((MOE_NB * MOE_BM, D_MODEL), F32)
    return pl.pallas_call(
        _dispatch_kernel,
        grid_spec=pltpu.PrefetchScalarGridSpec(
            num_scalar_prefetch=1,
            grid=(N_ROW_TILES,),
            in_specs=[pl.BlockSpec((ROW_TILE, D_MODEL), lambda i, d: (i, 0)),
                      pl.BlockSpec(memory_space=pl.ANY)],
            out_specs=pl.BlockSpec(memory_space=pl.ANY),
            scratch_shapes=[pltpu.SemaphoreType.DMA(())]),
        out_shape=jax.ShapeDtypeStruct((MOE_NB * MOE_BM, D_MODEL), F32),
        input_output_aliases={2: 0},
        compiler_params=_cparams("arbitrary"),
        name="moe_dispatch",
    )(dest, h, init)


def _expert_kernel(be_ref, nu_ref, new_ref, x_ref, wu_ref, bu_ref, wd_ref, bd_ref, y_ref, wu_s, wd_s):
    b = pl.program_id(0)
    used = b < nu_ref[0]

    @pl.when(jnp.logical_not(used))
    def _():
        y_ref[...] = jnp.zeros_like(y_ref)

    @pl.when(jnp.logical_and(used, new_ref[b] == 1))
    def _():
        r = lax.broadcasted_iota(I32, (UP_GROUP, UP_GROUP), 0)
        c = lax.broadcasted_iota(I32, (UP_GROUP, UP_GROUP), 1)
        src = jnp.where(c < UP_GROUP // 2, 2 * c, 2 * (c - UP_GROUP // 2) + 1)
        perm = (r == src).astype(BF16)
        for g in range(2 * D_FF // UP_GROUP):
            cols = slice(g * UP_GROUP, (g + 1) * UP_GROUP)
            wu_s[:, cols] = _dot(wu_ref[0, :, cols].astype(BF16), perm).astype(BF16)
        wd_s[...] = wd_ref[0].astype(BF16)

    @pl.when(used)
    def _():
        up = _dot(x_ref[...].astype(BF16), wu_s[...]) + bu_ref[0]
        half = UP_GROUP // 2
        acts = []
        for g in range(2 * D_FF // UP_GROUP):
            gate = jnp.minimum(up[:, g * UP_GROUP:g * UP_GROUP + half], SWIGLU_LIMIT)
            lin = jnp.clip(up[:, g * UP_GROUP + half:(g + 1) * UP_GROUP], -SWIGLU_LIMIT, SWIGLU_LIMIT)
            acts.append((gate * _sigmoid(SWIGLU_ALPHA * gate) * (lin + 1.0)).astype(BF16))
        y_ref[...] = _dot(jnp.concatenate(acts, axis=-1), wd_s[...]) + bd_ref[0]


def _experts(blk_expert, n_used, blk_new, xs, w_up, b_up, w_down, b_down):
    blk = lambda b, nu: jnp.minimum(b, nu[0] - 1)
    return pl.pallas_call(
        _expert_kernel,
        grid_spec=pltpu.PrefetchScalarGridSpec(
            num_scalar_prefetch=3,
            grid=(MOE_NB,),
            in_specs=[pl.BlockSpec((MOE_BM, D_MODEL), lambda b, be, nu, nw: (blk(b, nu), 0)),
                      pl.BlockSpec((1, D_MODEL, 2 * D_FF), lambda b, be, nu, nw: (be[blk(b, nu)], 0, 0)),
                      pl.BlockSpec((1, 1, 2 * D_FF), lambda b, be, nu, nw: (be[blk(b, nu)], 0, 0)),
                      pl.BlockSpec((1, D_FF, D_MODEL), lambda b, be, nu, nw: (be[blk(b, nu)], 0, 0)),
                      pl.BlockSpec((1, 1, D_MODEL), lambda b, be, nu, nw: (be[blk(b, nu)], 0, 0))],
            out_specs=pl.BlockSpec((MOE_BM, D_MODEL), lambda b, be, nu, nw: (b, 0)),
            scratch_shapes=[pltpu.VMEM((D_MODEL, 2 * D_FF), BF16),
                            pltpu.VMEM((D_FF, D_MODEL), BF16)]),
        out_shape=jax.ShapeDtypeStruct((MOE_NB * MOE_BM, D_MODEL), F32),
        compiler_params=_cparams("arbitrary"),
        name="moe_experts",
    )(blk_expert, n_used, blk_new, xs, w_up, b_up, w_down, b_down)


def _combine_kernel(dest_ref, x_ref, gate_ref, m_ref, fn_ref, ys_ref, o_ref, buf, sem, *, final):
    base = pl.program_id(0) * ROW_TILE * TOP_K

    def issue(t, carry):
        for k in range(TOP_K):
            _row_copy(ys_ref, dest_ref[base + t * TOP_K + k], buf.at[k], t, sem).start()
        return carry

    def drain(t, carry):
        for k in range(TOP_K):
            _row_copy(ys_ref, 0, buf.at[k], 0, sem).wait()
        return carry

    lax.fori_loop(0, ROW_TILE, issue, 0)
    lax.fori_loop(0, ROW_TILE, drain, 0)
    gate = gate_ref[...]
    acc = buf[0] * gate[:, 0:1]
    for k in range(1, TOP_K):
        acc = acc + buf[k] * gate[:, k:k + 1]
    x = x_ref[...] + m_ref[0][5:6] * acc
    o_ref[...] = _rms(x, fn_ref[...]) if final else x


def _combine(dest, x, gates, mods, final_norm, ys, final):
    return pl.pallas_call(
        functools.partial(_combine_kernel, final=final),
        grid_spec=pltpu.PrefetchScalarGridSpec(
            num_scalar_prefetch=1,
            grid=(N_ROW_TILES,),
            in_specs=[pl.BlockSpec((ROW_TILE, D_MODEL), lambda i, d: (i, 0)),
                      pl.BlockSpec((ROW_TILE, LANES), lambda i, d: (i, 0)),
                      pl.BlockSpec((1, 6, D_MODEL), lambda i, d: (_cond_of_tile(i), 0, 0)),
                      pl.BlockSpec((1, D_MODEL), lambda i, d: (0, 0)),
                      pl.BlockSpec(memory_space=pl.ANY)],
            out_specs=pl.BlockSpec((ROW_TILE, D_MODEL), lambda i, d: (i, 0)),
            scratch_shapes=[pltpu.VMEM((TOP_K, ROW_TILE, D_MODEL), F32),
                            pltpu.SemaphoreType.DMA(())]),
        out_shape=jax.ShapeDtypeStruct((N_TOK, D_MODEL), F32),
        compiler_params=_cparams("arbitrary"),
        name="moe_combine",
    )(dest, x, gates, mods, final_norm, ys)


def _moe_plan(rank, cnt, top_idx):
    sizes = cnt[0, :N_EXPERTS].astype(I32)
    padded = (sizes + MOE_BM - 1) // MOE_BM * MOE_BM
    pad_ends = jnp.cumsum(padded)
    pad_starts = pad_ends - padded
    row = rank[:, :N_EXPERTS].astype(I32) + pad_starts[None, :]
    dest = jnp.take_along_axis(row, top_idx, axis=1).reshape(-1)
    blk_start = jnp.arange(MOE_NB, dtype=I32) * MOE_BM
    blk_expert = jnp.minimum(jnp.sum(pad_ends[None, :] <= blk_start[:, None], axis=1), N_EXPERTS - 1).astype(I32)
    blk_new = jnp.concatenate([jnp.ones((1,), I32), (blk_expert[1:] != blk_expert[:-1]).astype(I32)])
    n_used = (pad_ends[-1:] // MOE_BM).astype(I32)
    return dest, blk_expert, n_used, blk_new


def _pad_lanes(v, fill=0.0):
    return jnp.pad(v, ((0, 0), (0, LANES - v.shape[-1])), constant_values=fill)


def kernel(x_prompt, x_sample, cache_na_k, cache_na_v, cache_gqa_k, cache_gqa_v, state_ssm, c, c_ctx, w_ada, b_ada, norm_mix, norm_ffn, w_in, na_rpb, gqa_q_norm, gqa_k_norm, ssm_conv_w, ssm_conv_b, ssm_dt_bias, ssm_a_log, ssm_d, ssm_norm, w_out, w_router, b_router, w_up, b_up, w_down, b_down, final_norm):
    x = jnp.concatenate([x_prompt.reshape(N_CTX_TOK, D_MODEL), x_sample.reshape(N_LAT_TOK, D_MODEL)], axis=0)
    conds = jnp.concatenate([c_ctx[None], c, jnp.zeros((COND_ROWS - N_COND, D_MODEL), F32)], axis=0)
    mods = _adaln(conds, w_ada, b_ada).reshape(DEPTH, COND_ROWS, 6, D_MODEL)
    cos, sin = _rope_tables()

    w_in_b = jnp.pad(w_in, ((0, 0), (0, 0), (0, IN_PAD - IN_DIM))).astype(BF16)
    w_out_b = w_out.astype(BF16)
    b_up_s = b_up.reshape(DEPTH, N_EXPERTS, 2 * D_FF // UP_GROUP, UP_GROUP // 2, 2)
    b_up_s = jnp.swapaxes(b_up_s, -1, -2).reshape(DEPTH, N_EXPERTS, 1, 2 * D_FF)
    b_down_s = b_down.reshape(DEPTH, N_EXPERTS, 1, D_MODEL)

    ctx_out = []
    for l in range(DEPTH):
        qkv, gqa, z, xbc, dt_raw = _inproj(x, mods[l], norm_mix[l][None], w_in_b[l])
        qn, kn = gqa_q_norm[l][None], gqa_k_norm[l][None]

        nao_c, go_c, gk_c = _ctx_attn(qkv, gqa, qn, kn)
        go_l = _lat_gqa(gqa, cache_gqa_k[:, l].reshape(DEC_BATCH, PAST_LEN, GQA_KV_DIM),
                        cache_gqa_v[:, l].reshape(DEC_BATCH, PAST_LEN, GQA_KV_DIM), cos, sin, qn, kn)
        nao_l = _lat_na(qkv, cache_na_k[:, l].reshape(DEC_BATCH, PAST_LEN, NA_DIM),
                        cache_na_v[:, l].reshape(DEC_BATCH, PAST_LEN, NA_DIM), _na_bias_tables(na_rpb[l]))

        xa = _conv_act(xbc, ssm_conv_w[l], ssm_conv_b[l][None])
        zeros = jnp.zeros((BATCH, SSM_INNER, SSM_STATE), F32)
        ys, sts = [], []
        for d in range(2):
            h0 = jnp.concatenate([zeros, state_ssm[:, l, d].reshape(DEC_BATCH, SSM_INNER, SSM_STATE)], axis=0)
            y, st = _ssd(xa, dt_raw, h0, _pad_lanes(ssm_dt_bias[l, d][None]), _pad_lanes(ssm_a_log[l, d][None]),
                         reverse=bool(d))
            ys.append(y)
            sts.append(st[:BATCH].reshape(BATCH, SSM_HEADS, SSM_HEAD_DIM, SSM_STATE))

        x, h, top_idx, gates, sel = _outproj(
            x, jnp.concatenate([nao_c, nao_l], axis=0), jnp.concatenate([go_c, go_l], axis=0),
            ys[0], ys[1], xa, z, jnp.repeat(ssm_d[l], SSM_HEAD_DIM)[None], ssm_norm[l][None], w_out_b[l],
            mods[l], norm_ffn[l][None], _pad_lanes(w_router[l]), _pad_lanes(b_router[l][None], NEG_INF))

        rank, cnt = _ranks(sel)
        dest, blk_expert, n_used, blk_new = _moe_plan(rank, cnt, top_idx[:, :TOP_K])
        y_sorted = _experts(blk_expert, n_used, blk_new, _dispatch(dest, h), w_up[l], b_up_s[l], w_down[l], b_down_s[l])
        x = _combine(dest, x, gates, mods[l], final_norm[None], y_sorted, final=(l == DEPTH - 1))

        ctx_out.append((
            qkv[:N_CTX_TOK, NA_DIM:2 * NA_DIM].reshape(BATCH, SEQ, NA_HEADS, HEAD_DIM),
            qkv[:N_CTX_TOK, 2 * NA_DIM:].reshape(BATCH, SEQ, NA_HEADS, HEAD_DIM),
            gk_c.reshape(BATCH, SEQ, GQA_KV_HEADS, HEAD_DIM),
            gqa[:N_CTX_TOK, GQA_Q_DIM + GQA_KV_DIM:].reshape(BATCH, SEQ, GQA_KV_HEADS, HEAD_DIM),
            jnp.stack(sts, axis=1)))

    y_prompt = x[:N_CTX_TOK].reshape(BATCH, SEQ, D_MODEL)
    y_sample = x[N_CTX_TOK:].reshape(DEC_BATCH, DEC_SEQ, D_MODEL)
    return (y_prompt, y_sample) + tuple(jnp.stack([e[i] for e in ctx_out], axis=1) for i in range(5))
```

```python
import functools

import numpy as np
import jax
import jax.numpy as jnp
from jax import lax
from jax.experimental import pallas as pl
from jax.experimental.pallas import tpu as pltpu

F32 = jnp.float32
BF16 = jnp.bfloat16
I32 = jnp.int32

D_MODEL = 1024
BATCH = 16
SEQ = 256
DEPTH = 2
DEC_BATCH = 2
DEC_SEQ = 2048
PAST_LEN = 512
GRID_W = 64
HEAD_DIM = 64
NA_HEADS = 4
NA_WIN_ROWS = 8
NA_WIN_COLS = 16
GQA_HEADS = 4
GQA_KV_HEADS = 2
ROPE_THETA = 10000.0
SSM_HEADS = 8
SSM_HEAD_DIM = 64
SSM_STATE = 64
SSM_GROUPS = 2
SSM_INNER = SSM_HEADS * SSM_HEAD_DIM
SSM_BC_DIM = SSM_GROUPS * SSM_STATE
CONV_DIM = SSM_INNER + 2 * SSM_BC_DIM
CONV_W = 5
CHUNK = 128
NA_DIM = NA_HEADS * HEAD_DIM
GQA_Q_DIM = GQA_HEADS * HEAD_DIM
GQA_KV_DIM = GQA_KV_HEADS * HEAD_DIM
D_MIX = NA_DIM + GQA_Q_DIM + SSM_INNER
IN_DIM = 3 * NA_DIM + GQA_Q_DIM + 2 * GQA_KV_DIM + SSM_INNER + CONV_DIM + SSM_HEADS
N_EXPERTS = 32
TOP_K = 4
D_FF = D_MODEL
SWIGLU_LIMIT = 7.0
SWIGLU_ALPHA = 1.702
EPS = 1e-6
NEG_INF = -1e30

LANES = 128
N_CTX_TOK = BATCH * SEQ
N_LAT_TOK = DEC_BATCH * DEC_SEQ
N_TOK = N_CTX_TOK + N_LAT_TOK
N_COND = 1 + DEC_BATCH
COND_ROWS = 16
IN_PAD = 3 * NA_DIM + GQA_Q_DIM + 2 * GQA_KV_DIM + SSM_INNER + CONV_DIM + LANES
ROW_TILE = 256
N_ROW_TILES = N_TOK // ROW_TILE
MOE_BM = 256
N_SLOTS = N_TOK * TOP_K
MOE_NB = N_SLOTS // MOE_BM + N_EXPERTS
UP_GROUP = 256
N_SEQ = BATCH + DEC_BATCH
N_CHUNKS = N_TOK // CHUNK
N_CTX_CHUNKS = N_CTX_TOK // CHUNK
VMEM_LIMIT = 56 * 1024 * 1024


def _cparams(*sem):
    return pltpu.CompilerParams(dimension_semantics=sem, vmem_limit_bytes=VMEM_LIMIT)


def _sigmoid(x):
    return 1.0 / (1.0 + jnp.exp(-x))


def _dot(a, b):
    return jnp.dot(a, b, preferred_element_type=F32)


def _dot_nt(a, b):
    return lax.dot_general(a, b, (((1,), (1,)), ((), ())), preferred_element_type=F32)


def _dot_exact(a, b):
    return jnp.dot(a, b, preferred_element_type=F32, precision=lax.Precision.HIGHEST)


def _rms(x, g):
    return x * lax.rsqrt(jnp.mean(x * x, axis=-1, keepdims=True) + EPS) * g


def _cond_of_tile(i):
    ctx_tiles = N_CTX_TOK // ROW_TILE
    return jnp.where(i < ctx_tiles, 0, 1 + (i - ctx_tiles) // (DEC_SEQ // ROW_TILE))


def _adaln_kernel(c_ref, w_ref, b_ref, o_ref):
    c = c_ref[...]
    s = (c * _sigmoid(c)).astype(BF16)
    o_ref[0] = _dot(s, w_ref[0].astype(BF16)) + b_ref[0]


def _adaln(conds, w_ada, b_ada):
    tn = 1536
    return pl.pallas_call(
        _adaln_kernel,
        grid=(DEPTH, 6 * D_MODEL // tn),
        in_specs=[pl.BlockSpec((COND_ROWS, D_MODEL), lambda l, j: (0, 0)),
                  pl.BlockSpec((1, D_MODEL, tn), lambda l, j: (l, 0, j)),
                  pl.BlockSpec((1, 1, tn), lambda l, j: (l, 0, j))],
        out_specs=pl.BlockSpec((1, COND_ROWS, tn), lambda l, j: (l, 0, j)),
        out_shape=jax.ShapeDtypeStruct((DEPTH, COND_ROWS, 6 * D_MODEL), F32),
        compiler_params=_cparams("parallel", "parallel"),
        name="adaln",
    )(conds, w_ada, b_ada.reshape(DEPTH, 1, 6 * D_MODEL))


_IN_SPLITS = (3 * NA_DIM, GQA_Q_DIM + 2 * GQA_KV_DIM, SSM_INNER, CONV_DIM, LANES)


def _inproj_kernel(x_ref, m_ref, g_ref, w_ref, qkv_ref, gqa_ref, z_ref, xbc_ref, dt_ref):
    m = m_ref[0]
    h = _rms(x_ref[...], g_ref[...]) * (1.0 + m[1:2]) + m[0:1]
    p = _dot(h.astype(BF16), w_ref[...])
    off = 0
    for ref, width in zip((qkv_ref, gqa_ref, z_ref, xbc_ref, dt_ref), _IN_SPLITS):
        ref[...] = p[:, off:off + width]
        off += width


def _inproj(x, mods, gain, w_in):
    row = lambda w: pl.BlockSpec((ROW_TILE, w), lambda i: (i, 0))
    return pl.pallas_call(
        _inproj_kernel,
        grid=(N_ROW_TILES,),
        in_specs=[row(D_MODEL),
                  pl.BlockSpec((1, 6, D_MODEL), lambda i: (_cond_of_tile(i), 0, 0)),
                  pl.BlockSpec((1, D_MODEL), lambda i: (0, 0)),
                  pl.BlockSpec((D_MODEL, IN_PAD), lambda i: (0, 0))],
        out_specs=[row(w) for w in _IN_SPLITS],
        out_shape=[jax.ShapeDtypeStruct((N_TOK, w), F32) for w in _IN_SPLITS],
        compiler_params=_cparams("parallel"),
        name="inproj",
    )(x, mods, gain, w_in)


def _softmax_pv(scores, values):
    m = scores[0].max(axis=-1, keepdims=True)
    for s in scores[1:]:
        m = jnp.maximum(m, s.max(axis=-1, keepdims=True))
    den = 0.0
    acc = 0.0
    for s, v in zip(scores, values):
        e = jnp.exp(s - m)
        den = den + e.sum(axis=-1, keepdims=True)
        acc = acc + _dot(e.astype(BF16), v)
    return acc / den


def _heads_rms(x, n_heads, g):
    return jnp.concatenate(
        [_rms(x[:, h * HEAD_DIM:(h + 1) * HEAD_DIM], g) for h in range(n_heads)], axis=-1)


def _rope(x, cos, sin_signed):
    w = x.shape[-1]
    lane = lax.broadcasted_iota(I32, x.shape, 1)
    partner = jnp.where((lane & 1) == 0, pltpu.roll(x, w - 1, 1), pltpu.roll(x, 1, 1))
    return x * cos + partner * sin_signed


_ATT_SCALE = HEAD_DIM ** -0.5


def _ctx_attn_kernel(qkv_ref, gqa_ref, qn_ref, kn_ref, nao_ref, go_ref, gk_ref):
    outs = []
    for h in range(NA_HEADS):
        sl = slice(h * HEAD_DIM, (h + 1) * HEAD_DIM)
        q = qkv_ref[:, sl].astype(BF16)
        k = qkv_ref[:, NA_DIM + h * HEAD_DIM:NA_DIM + (h + 1) * HEAD_DIM].astype(BF16)
        v = qkv_ref[:, 2 * NA_DIM + h * HEAD_DIM:2 * NA_DIM + (h + 1) * HEAD_DIM].astype(BF16)
        outs.append(_softmax_pv([_dot_nt(q, k) * _ATT_SCALE], [v]))
    nao_ref[...] = jnp.concatenate(outs, axis=-1).astype(BF16)

    gq = _heads_rms(gqa_ref[:, 0:GQA_Q_DIM], GQA_HEADS, qn_ref[...])
    gk = _heads_rms(gqa_ref[:, GQA_Q_DIM:GQA_Q_DIM + GQA_KV_DIM], GQA_KV_HEADS, kn_ref[...])
    gk_ref[...] = gk
    rep = GQA_HEADS // GQA_KV_HEADS
    outs = []
    for h in range(GQA_HEADS):
        g = h // rep
        q = gq[:, h * HEAD_DIM:(h + 1) * HEAD_DIM].astype(BF16)
        k = gk[:, g * HEAD_DIM:(g + 1) * HEAD_DIM].astype(BF16)
        v0 = GQA_Q_DIM + GQA_KV_DIM + g * HEAD_DIM
        v = gqa_ref[:, v0:v0 + HEAD_DIM].astype(BF16)
        outs.append(_softmax_pv([_dot_nt(q, k) * _ATT_SCALE], [v]))
    go_ref[...] = jnp.concatenate(outs, axis=-1).astype(BF16)


def _ctx_attn(qkv, gqa, q_norm, k_norm):
    return pl.pallas_call(
        _ctx_attn_kernel,
        grid=(BATCH,),
        in_specs=[pl.BlockSpec((SEQ, 3 * NA_DIM), lambda b: (b, 0)),
                  pl.BlockSpec((SEQ, GQA_Q_DIM + 2 * GQA_KV_DIM), lambda b: (b, 0)),
                  pl.BlockSpec((1, HEAD_DIM), lambda b: (0, 0)),
                  pl.BlockSpec((1, HEAD_DIM), lambda b: (0, 0))],
        out_specs=[pl.BlockSpec((SEQ, NA_DIM), lambda b: (b, 0)),
                   pl.BlockSpec((SEQ, GQA_Q_DIM), lambda b: (b, 0)),
                   pl.BlockSpec((SEQ, GQA_KV_DIM), lambda b: (b, 0))],
        out_shape=[jax.ShapeDtypeStruct((N_CTX_TOK, NA_DIM), BF16),
                   jax.ShapeDtypeStruct((N_CTX_TOK, GQA_Q_DIM), BF16),
                   jax.ShapeDtypeStruct((N_CTX_TOK, GQA_KV_DIM), F32)],
        compiler_params=_cparams("parallel"),
        name="ctx_attn",
    )(qkv, gqa, q_norm, k_norm)


GQA_TQ = 256
GQA_KEYS = PAST_LEN + DEC_SEQ


def _lat_gqa_kernel(gqa_ref, ck_ref, cv_ref, cos_ref, sin_ref, qn_ref, kn_ref, o_ref, kbuf, vbuf):
    qb = pl.program_id(1)

    @pl.when(qb == 0)
    def _():
        kbuf[0:PAST_LEN, :] = ck_ref[0].astype(BF16)
        vbuf[0:PAST_LEN, :] = cv_ref[0].astype(BF16)
        k = _heads_rms(gqa_ref[:, GQA_Q_DIM:GQA_Q_DIM + GQA_KV_DIM], GQA_KV_HEADS, kn_ref[...])
        k = _rope(k, cos_ref[:, 0:GQA_KV_DIM], sin_ref[:, 0:GQA_KV_DIM])
        kbuf[PAST_LEN:GQA_KEYS, :] = k.astype(BF16)
        vbuf[PAST_LEN:GQA_KEYS, :] = gqa_ref[:, GQA_Q_DIM + GQA_KV_DIM:].astype(BF16)

    r0 = pl.multiple_of(qb * GQA_TQ, GQA_TQ)
    q = _heads_rms(gqa_ref[pl.ds(r0, GQA_TQ), 0:GQA_Q_DIM], GQA_HEADS, qn_ref[...])
    q = _rope(q, cos_ref[pl.ds(r0, GQA_TQ), :], sin_ref[pl.ds(r0, GQA_TQ), :]).astype(BF16)
    rep = GQA_HEADS // GQA_KV_HEADS
    outs = []
    for h in range(GQA_HEADS):
        g = h // rep
        k = kbuf[:, g * HEAD_DIM:(g + 1) * HEAD_DIM]
        v = vbuf[:, g * HEAD_DIM:(g + 1) * HEAD_DIM]
        s = _dot_nt(q[:, h * HEAD_DIM:(h + 1) * HEAD_DIM], k) * _ATT_SCALE
        outs.append(_softmax_pv([s], [v]))
    o_ref[...] = jnp.concatenate(outs, axis=-1).astype(BF16)


def _lat_gqa(gqa, cache_k, cache_v, cos, sin, q_norm, k_norm):
    lat_blk = N_CTX_TOK // DEC_SEQ
    return pl.pallas_call(
        _lat_gqa_kernel,
        grid=(DEC_BATCH, DEC_SEQ // GQA_TQ),
        in_specs=[pl.BlockSpec((DEC_SEQ, GQA_Q_DIM + 2 * GQA_KV_DIM), lambda b, q: (lat_blk + b, 0)),
                  pl.BlockSpec((1, PAST_LEN, GQA_KV_DIM), lambda b, q: (b, 0, 0)),
                  pl.BlockSpec((1, PAST_LEN, GQA_KV_DIM), lambda b, q: (b, 0, 0)),
                  pl.BlockSpec((DEC_SEQ, GQA_Q_DIM), lambda b, q: (0, 0)),
                  pl.BlockSpec((DEC_SEQ, GQA_Q_DIM), lambda b, q: (0, 0)),
                  pl.BlockSpec((1, HEAD_DIM), lambda b, q: (0, 0)),
                  pl.BlockSpec((1, HEAD_DIM), lambda b, q: (0, 0))],
        out_specs=pl.BlockSpec((GQA_TQ, GQA_Q_DIM), lambda b, q: (b * (DEC_SEQ // GQA_TQ) + q, 0)),
        out_shape=jax.ShapeDtypeStruct((N_LAT_TOK, GQA_Q_DIM), BF16),
        scratch_shapes=[pltpu.VMEM((GQA_KEYS, GQA_KV_DIM), BF16),
                        pltpu.VMEM((GQA_KEYS, GQA_KV_DIM), BF16)],
        compiler_params=_cparams("arbitrary", "arbitrary"),
        name="lat_gqa",
    )(gqa, cache_k, cache_v, cos, sin, q_norm, k_norm)


def _rope_tables():
    t = jnp.arange(DEC_SEQ)
    row = (t // GRID_W).astype(F32)
    col = (t % GRID_W).astype(F32)
    axis_dim = HEAD_DIM // 2
    inv_freq = ROPE_THETA ** (-jnp.arange(0, axis_dim, 2, dtype=F32) / axis_dim)
    ang = jnp.concatenate([row[:, None] * inv_freq, col[:, None] * inv_freq], axis=-1)
    cos = jnp.repeat(jnp.cos(ang), 2, axis=-1)
    sin = jnp.repeat(jnp.sin(ang), 2, axis=-1) * jnp.tile(jnp.array([-1.0, 1.0], F32), HEAD_DIM // 2)
    return jnp.tile(cos, (1, GQA_HEADS)), jnp.tile(sin, (1, GQA_HEADS))


NA_ROWS = DEC_SEQ // GRID_W
NA_KEYS = NA_WIN_ROWS * GRID_W


def _lat_na_kernel(qkv_ref, ck_ref, cv_ref, bias_ref, o_ref):
    r = pl.program_id(1)
    r0 = jnp.clip(r - NA_WIN_ROWS // 2, 0, NA_ROWS - NA_WIN_ROWS)
    q0 = pl.multiple_of(r * GRID_W, GRID_W)
    k0 = pl.multiple_of(r0 * GRID_W, GRID_W)
    outs = []
    for h in range(NA_HEADS):
        c0 = h * HEAD_DIM
        q = qkv_ref[pl.ds(q0, GRID_W), c0:c0 + HEAD_DIM].astype(BF16)
        k = qkv_ref[pl.ds(k0, NA_KEYS), NA_DIM + c0:NA_DIM + c0 + HEAD_DIM].astype(BF16)
        v = qkv_ref[pl.ds(k0, NA_KEYS), 2 * NA_DIM + c0:2 * NA_DIM + c0 + HEAD_DIM].astype(BF16)
        kc = ck_ref[0, :, c0:c0 + HEAD_DIM].astype(BF16)
        vc = cv_ref[0, :, c0:c0 + HEAD_DIM].astype(BF16)
        s_nb = _dot_nt(q, k) * _ATT_SCALE + bias_ref[0, h]
        s_ctx = _dot_nt(q, kc) * _ATT_SCALE
        outs.append(_softmax_pv([s_nb, s_ctx], [v, vc]))
    o_ref[...] = jnp.concatenate(outs, axis=-1).astype(BF16)


def _na_row_offset(r):
    return r - jnp.clip(r - NA_WIN_ROWS // 2, 0, NA_ROWS - NA_WIN_ROWS)


def _lat_na(qkv, cache_k, cache_v, bias):
    lat_blk = N_CTX_TOK // DEC_SEQ
    return pl.pallas_call(
        _lat_na_kernel,
        grid=(DEC_BATCH, NA_ROWS),
        in_specs=[pl.BlockSpec((DEC_SEQ, 3 * NA_DIM), lambda b, r: (lat_blk + b, 0)),
                  pl.BlockSpec((1, PAST_LEN, NA_DIM), lambda b, r: (b, 0, 0)),
                  pl.BlockSpec((1, PAST_LEN, NA_DIM), lambda b, r: (b, 0, 0)),
                  pl.BlockSpec((1, NA_HEADS, GRID_W, NA_KEYS), lambda b, r: (_na_row_offset(r), 0, 0, 0))],
        out_specs=pl.BlockSpec((GRID_W, NA_DIM), lambda b, r: (b * NA_ROWS + r, 0)),
        out_shape=jax.ShapeDtypeStruct((N_LAT_TOK, NA_DIM), BF16),
        compiler_params=_cparams("parallel", "arbitrary"),
        name="lat_na",
    )(qkv, cache_k, cache_v, bias)


def _na_bias_tables(rpb):
    d = np.arange(NA_WIN_ROWS)[:, None]
    kr = np.arange(NA_WIN_ROWS)[None, :]
    dr = kr - d + NA_WIN_ROWS - 1
    qc = np.arange(GRID_W)[:, None]
    kc = np.arange(GRID_W)[None, :]
    col0 = np.clip(qc - NA_WIN_COLS // 2, 0, GRID_W - NA_WIN_COLS)
    in_win = (kc >= col0) & (kc < col0 + NA_WIN_COLS)
    dc = np.clip(kc - qc + NA_WIN_COLS - 1, 0, 2 * NA_WIN_COLS - 2)
    row_hot = (dr[:, :, None] == np.arange(2 * NA_WIN_ROWS - 1)).astype(np.float32)
    col_hot = (dc[:, :, None] == np.arange(2 * NA_WIN_COLS - 1)).astype(np.float32)
    b = jnp.einsum('hac,dka,qxc->dhqkx', rpb.astype(F32), row_hot, col_hot, precision=lax.Precision.HIGHEST)
    b = jnp.where(in_win[None, None, :, None, :], b, NEG_INF)
    return b.reshape(NA_WIN_ROWS, NA_HEADS, GRID_W, NA_KEYS)


CONV_TB = 1024
CONV_HALO = 8
CONV_HALO_BLOCKS = CONV_TB // CONV_HALO


def _conv_kernel(prev_ref, x_ref, next_ref, w_ref, b_ref, o_ref):
    i = pl.program_id(0)
    seq = jnp.where(i < N_CTX_TOK // CONV_TB, SEQ, DEC_SEQ)
    x = x_ref[...]
    ext = jnp.concatenate([prev_ref[...], x, next_ref[...]], axis=0)
    n_ext = CONV_TB + 2 * CONV_HALO
    pos = (lax.broadcasted_iota(I32, (CONV_TB, 1), 0) + i * CONV_TB) & (seq - 1)
    half = CONV_W // 2
    acc = x * w_ref[half:half + 1, :]
    for s in range(-half, half + 1):
        if s == 0:
            continue
        shifted = pltpu.roll(ext, (-s) % n_ext, 0)[CONV_HALO:CONV_HALO + CONV_TB]
        valid = (pos + s >= 0) & (pos + s < seq)
        acc = acc + jnp.where(valid, shifted, 0.0) * w_ref[half + s:half + s + 1, :]
    acc = acc + b_ref[...]
    o_ref[...] = acc * _sigmoid(acc)


def _conv_act(xbc, conv_w, conv_b):
    return pl.pallas_call(
        _conv_kernel,
        grid=(N_TOK // CONV_TB,),
        in_specs=[pl.BlockSpec((CONV_HALO, CONV_DIM),
                               lambda i: (jnp.maximum(i * CONV_HALO_BLOCKS - 1, 0), 0)),
                  pl.BlockSpec((CONV_TB, CONV_DIM), lambda i: (i, 0)),
                  pl.BlockSpec((CONV_HALO, CONV_DIM),
                               lambda i: (jnp.minimum((i + 1) * CONV_HALO_BLOCKS, N_TOK // CONV_HALO - 1), 0)),
                  pl.BlockSpec((CONV_W, CONV_DIM), lambda i: (0, 0)),
                  pl.BlockSpec((1, CONV_DIM), lambda i: (0, 0))],
        out_specs=pl.BlockSpec((CONV_TB, CONV_DIM), lambda i: (i, 0)),
        out_shape=jax.ShapeDtypeStruct((N_TOK, CONV_DIM), F32),
        compiler_params=_cparams("parallel"),
        name="conv_act",
    )(xbc, xbc, xbc, conv_w, conv_b)


def _chunk_seq(g):
    ctx_n = SEQ // CHUNK
    lat_n = DEC_SEQ // CHUNK
    is_ctx = g < N_CTX_CHUNKS
    gl = g - N_CTX_CHUNKS
    sid = jnp.where(is_ctx, g // ctx_n, BATCH + gl // lat_n)
    cin = jnp.where(is_ctx, g % ctx_n, gl % lat_n)
    n = jnp.where(is_ctx, ctx_n, lat_n)
    return sid, cin, n


def _ssd_init(h0_ref, st_ref, gg, reverse):
    _, cin, n = _chunk_seq(gg)

    @pl.when(cin == (n - 1 if reverse else 0))
    def _():
        st_ref[0] = h0_ref[0]


def _ssd_chunk(xa_ref, dt_ref, dtb, alog, y_ref, st_ref, reverse):
    x = dt_ref[...] + dtb
    dt = jnp.maximum(x, 0.0) + jnp.log1p(jnp.exp(-jnp.abs(x)))
    dta = dt * -jnp.exp(alog)
    ii = lax.broadcasted_iota(I32, (CHUNK, CHUNK), 0)
    jj = lax.broadcasted_iota(I32, (CHUNK, CHUNK), 1)
    tri = (jj >= ii) if reverse else (jj <= ii)
    cum = _dot_exact(tri.astype(F32), dta)
    cum_t = cum.T
    edge = 0 if reverse else CHUNK - 1
    tot = cum[edge:edge + 1, :]
    rep = SSM_HEADS // SSM_GROUPS
    for grp in range(SSM_GROUPS):
        bg = xa_ref[:, SSM_INNER + grp * SSM_STATE:SSM_INNER + (grp + 1) * SSM_STATE].astype(BF16)
        c0 = SSM_INNER + SSM_BC_DIM + grp * SSM_STATE
        cg = xa_ref[:, c0:c0 + SSM_STATE].astype(BF16)
        cb = _dot_nt(cg, bg)
        for h in range(grp * rep, (grp + 1) * rep):
            hs = slice(h * SSM_HEAD_DIM, (h + 1) * SSM_HEAD_DIM)
            col = cum[:, h:h + 1]
            row = cum_t[h:h + 1, :]
            decay = jnp.where(tri, jnp.exp(jnp.minimum(col - row, 0.0)), 0.0)
            xdt = xa_ref[:, hs] * dt[:, h:h + 1]
            state = st_ref[0, hs, :]
            y = _dot((cb * decay).astype(BF16), xdt.astype(BF16))
            y = y + _dot_nt(cg, state.astype(BF16)) * jnp.exp(col)
            y_ref[:, hs] = y
            toth = tot[:, h:h + 1]
            w = (xdt * jnp.exp(toth - col)).astype(BF16)
            upd = lax.dot_general(w, bg, (((0,), (0,)), ((), ())), preferred_element_type=F32)
            st_ref[0, hs, :] = state * jnp.exp(toth) + upd


def _ssd_kernel(xaf_ref, dtf_ref, h0f_ref, xab_ref, dtb_ref, h0b_ref, bias_ref, alog_ref,
                yf_ref, stf_ref, yb_ref, stb_ref):
    g = pl.program_id(0)
    _ssd_init(h0f_ref, stf_ref, g, False)
    _ssd_chunk(xaf_ref, dtf_ref, bias_ref[0:1, :], alog_ref[0:1, :], yf_ref, stf_ref, False)
    _ssd_init(h0b_ref, stb_ref, N_CHUNKS - 1 - g, True)
    _ssd_chunk(xab_ref, dtb_ref, bias_ref[1:2, :], alog_ref[1:2, :], yb_ref, stb_ref, True)


def _ssd(xa, dt_raw, h0_fwd, h0_bwd, dt_bias, a_log):
    rev = lambda g: N_CHUNKS - 1 - g
    chunk = lambda w, order: pl.BlockSpec((CHUNK, w), lambda g: (order(g), 0))
    state = lambda order: pl.BlockSpec((1, SSM_INNER, SSM_STATE), lambda g: (_chunk_seq(order(g))[0], 0, 0))
    same = lambda g: g
    y_shape = jax.ShapeDtypeStruct((N_TOK, SSM_INNER), F32)
    st_shape = jax.ShapeDtypeStruct((N_SEQ, SSM_INNER, SSM_STATE), F32)
    return pl.pallas_call(
        _ssd_kernel,
        grid=(N_CHUNKS,),
        in_specs=[chunk(CONV_DIM, same), chunk(LANES, same), state(same),
                  chunk(CONV_DIM, rev), chunk(LANES, rev), state(rev),
                  pl.BlockSpec((2, LANES), lambda g: (0, 0)),
                  pl.BlockSpec((2, LANES), lambda g: (0, 0))],
        out_specs=[chunk(SSM_INNER, same), state(same), chunk(SSM_INNER, rev), state(rev)],
        out_shape=[y_shape, st_shape, y_shape, st_shape],
        compiler_params=_cparams("arbitrary"),
        name="ssd",
    )(xa, dt_raw, h0_fwd, xa, dt_raw, h0_bwd, dt_bias, a_log)


def _outproj_kernel(x_ref, nao_ref, go_ref, yf_ref, yb_ref, xs_ref, z_ref, dsk_ref, sn_ref, wo_ref,
                    m_ref, g2_ref, wr_ref, br_ref, xo_ref, h_ref, idx_ref, gate_ref, sel_ref):
    m = m_ref[0]
    z = z_ref[...]
    y = (yf_ref[...] + yb_ref[...] + xs_ref[...] * dsk_ref[...]) * (z * _sigmoid(z))
    s_o = _rms(y, sn_ref[...]).astype(BF16)
    mix = (_dot(nao_ref[...], wo_ref[0:NA_DIM, :])
           + _dot(go_ref[...], wo_ref[NA_DIM:NA_DIM + GQA_Q_DIM, :])
           + _dot(s_o, wo_ref[NA_DIM + GQA_Q_DIM:, :]))
    x = x_ref[...] + m[2:3] * mix
    xo_ref[...] = x
    h = _rms(x, g2_ref[...]) * (1.0 + m[4:5]) + m[3:4]
    h_ref[...] = h

    logits = _dot_exact(h, wr_ref[...]) + br_ref[...]
    lane = lax.broadcasted_iota(I32, logits.shape, 1).astype(F32)
    vals, idxs = [], []
    for _ in range(TOP_K):
        v = logits.max(axis=-1, keepdims=True)
        i = jnp.where(logits == v, lane, float(LANES)).min(axis=-1, keepdims=True)
        vals.append(v)
        idxs.append(i)
        logits = jnp.where(lane == i, -jnp.inf, logits)
    es = [jnp.exp(v - vals[0]) for v in vals]
    den = es[0] + es[1] + es[2] + es[3]
    idx_out = jnp.zeros(lane.shape, F32)
    gate_out = jnp.zeros(lane.shape, F32)
    sel = jnp.zeros(lane.shape, F32)
    for k in range(TOP_K):
        idx_out = jnp.where(lane == float(k), idxs[k], idx_out)
        gate_out = jnp.where(lane == float(k), es[k] / den, gate_out)
        sel = jnp.where(lane == idxs[k], 1.0, sel)
    idx_ref[...] = idx_out.astype(I32)
    gate_ref[...] = gate_out
    sel_ref[...] = sel.astype(BF16)


def _outproj(x, nao, go, yf, yb, xa, z, d_skip, ssm_norm, w_out, mods, gain2, w_router, b_router):
    row = lambda w: pl.BlockSpec((ROW_TILE, w), lambda i: (i, 0))
    full = lambda a, b: pl.BlockSpec((a, b), lambda i: (0, 0))
    return pl.pallas_call(
        _outproj_kernel,
        grid=(N_ROW_TILES,),
        in_specs=[row(D_MODEL), row(NA_DIM), row(GQA_Q_DIM), row(SSM_INNER), row(SSM_INNER),
                  row(SSM_INNER), row(SSM_INNER), full(1, SSM_INNER), full(1, SSM_INNER),
                  full(D_MIX, D_MODEL),
                  pl.BlockSpec((1, 6, D_MODEL), lambda i: (_cond_of_tile(i), 0, 0)),
                  full(1, D_MODEL), full(D_MODEL, LANES), full(1, LANES)],
        out_specs=[row(D_MODEL), row(D_MODEL), row(LANES), row(LANES), row(LANES)],
        out_shape=[jax.ShapeDtypeStruct((N_TOK, D_MODEL), F32),
                   jax.ShapeDtypeStruct((N_TOK, D_MODEL), F32),
                   jax.ShapeDtypeStruct((N_TOK, LANES), I32),
                   jax.ShapeDtypeStruct((N_TOK, LANES), F32),
                   jax.ShapeDtypeStruct((N_TOK, LANES), BF16)],
        compiler_params=_cparams("parallel"),
        name="outproj_router",
    )(x, nao, go, yf, yb, xa, z, d_skip, ssm_norm, w_out, mods, gain2, w_router, b_router)


RANK_TB = 512


def _rank_kernel(sel_ref, rank_ref, cnt_ref, carry):
    @pl.when(pl.program_id(0) == 0)
    def _():
        carry[...] = jnp.zeros_like(carry)

    sel = sel_ref[...]
    ii = lax.broadcasted_iota(I32, (RANK_TB, RANK_TB), 0)
    jj = lax.broadcasted_iota(I32, (RANK_TB, RANK_TB), 1)
    before = (jj < ii).astype(BF16)
    rank_ref[...] = _dot(before, sel) + carry[0:1, :]
    carry[...] = carry[...] + _dot(jnp.ones((8, RANK_TB), BF16), sel)
    cnt_ref[...] = carry[...]


def _ranks(sel):
    return pl.pallas_call(
        _rank_kernel,
        grid=(N_TOK // RANK_TB,),
        in_specs=[pl.BlockSpec((RANK_TB, LANES), lambda i: (i, 0))],
        out_specs=[pl.BlockSpec((RANK_TB, LANES), lambda i: (i, 0)),
                   pl.BlockSpec((8, LANES), lambda i: (0, 0))],
        out_shape=[jax.ShapeDtypeStruct((N_TOK, LANES), F32),
                   jax.ShapeDtypeStruct((8, LANES), F32)],
        scratch_shapes=[pltpu.VMEM((8, LANES), F32)],
        compiler_params=_cparams("arbitrary"),
        name="moe_ranks",
    )(sel)


def _row_copy(src, s, dst, d, sem):
    return pltpu.make_async_copy(src.at[pl.ds(s, 1), :], dst.at[pl.ds(d, 1), :], sem)


def _dispatch_kernel(dest_ref, h_ref, init_ref, out_ref, sem):
    del init_ref
    base = pl.program_id(0) * ROW_TILE * TOP_K

    def issue(t, carry):
        for k in range(TOP_K):
            _row_copy(h_ref, t, out_ref, dest_ref[base + t * TOP_K + k], sem).start()
        return carry

    def drain(t, carry):
        for k in range(TOP_K):
            _row_copy(h_ref, 0, out_ref, 0, sem).wait()
        return carry

    lax.fori_loop(0, ROW_TILE, issue, 0)
    lax.fori_loop(0, ROW_TILE, drain, 0)


def _dispatch(dest, h):
    init = jnp.zeros((MOE_NB * MOE_BM, D_MODEL), F32)
    return pl.pallas_call(
        _dispatch_kernel,
        grid_spec=pltpu.PrefetchScalarGridSpec(
            num_scalar_prefetch=1,
            grid=(N_ROW_TILES,),
            in_specs=[pl.BlockSpec((ROW_TILE, D_MODEL), lambda i, d: (i, 0)),
                      pl.BlockSpec(memory_space=pl.ANY)],
            out_specs=pl.BlockSpec(memory_space=pl.ANY),
            scratch_shapes=[pltpu.SemaphoreType.DMA(())]),
        out_shape=jax.ShapeDtypeStruct((MOE_NB * MOE_BM, D_MODEL), F32),
        input_output_aliases={2: 0},
        compiler_params=_cparams("arbitrary"),
        name="moe_dispatch",
    )(dest, h, init)


def _expert_kernel(be_ref, nu_ref, new_ref, x_ref, wu_ref, bu_ref, wd_ref, bd_ref, y_ref, wu_s, wd_s):
    b = pl.program_id(0)
    used = b < nu_ref[0]

    @pl.when(jnp.logical_not(used))
    def _():
        y_ref[...] = jnp.zeros_like(y_ref)

    @pl.when(jnp.logical_and(used, new_ref[b] == 1))
    def _():
        r = lax.broadcasted_iota(I32, (UP_GROUP, UP_GROUP), 0)
        c = lax.broadcasted_iota(I32, (UP_GROUP, UP_GROUP), 1)
        src = jnp.where(c < UP_GROUP // 2, 2 * c, 2 * (c - UP_GROUP // 2) + 1)
        perm = (r == src).astype(BF16)
        for g in range(2 * D_FF // UP_GROUP):
            cols = slice(g * UP_GROUP, (g + 1) * UP_GROUP)
            wu_s[:, cols] = _dot(wu_ref[0, 0, :, cols].astype(BF16), perm).astype(BF16)
        wd_s[...] = wd_ref[0, 0].astype(BF16)

    @pl.when(used)
    def _():
        up = _dot(x_ref[...].astype(BF16), wu_s[...]) + bu_ref[0]
        half = UP_GROUP // 2
        acts = []
        for g in range(2 * D_FF // UP_GROUP):
            gate = jnp.minimum(up[:, g * UP_GROUP:g * UP_GROUP + half], SWIGLU_LIMIT)
            lin = jnp.clip(up[:, g * UP_GROUP + half:(g + 1) * UP_GROUP], -SWIGLU_LIMIT, SWIGLU_LIMIT)
            acts.append((gate * _sigmoid(SWIGLU_ALPHA * gate) * (lin + 1.0)).astype(BF16))
        y_ref[...] = _dot(jnp.concatenate(acts, axis=-1), wd_s[...]) + bd_ref[0]


def _experts(layer, blk_expert, n_used, blk_new, xs, w_up, b_up, w_down, b_down):
    blk = lambda b, nu: jnp.maximum(jnp.minimum(b, nu[0] - 1), 0)
    return pl.pallas_call(
        _expert_kernel,
        grid_spec=pltpu.PrefetchScalarGridSpec(
            num_scalar_prefetch=3,
            grid=(MOE_NB,),
            in_specs=[pl.BlockSpec((MOE_BM, D_MODEL), lambda b, be, nu, nw: (blk(b, nu), 0)),
                      pl.BlockSpec((1, 1, D_MODEL, 2 * D_FF), lambda b, be, nu, nw: (layer, be[blk(b, nu)], 0, 0)),
                      pl.BlockSpec((1, 1, 2 * D_FF), lambda b, be, nu, nw: (be[blk(b, nu)], 0, 0)),
                      pl.BlockSpec((1, 1, D_FF, D_MODEL), lambda b, be, nu, nw: (layer, be[blk(b, nu)], 0, 0)),
                      pl.BlockSpec((1, 1, D_MODEL), lambda b, be, nu, nw: (be[blk(b, nu)], 0, 0))],
            out_specs=pl.BlockSpec((MOE_BM, D_MODEL), lambda b, be, nu, nw: (b, 0)),
            scratch_shapes=[pltpu.VMEM((D_MODEL, 2 * D_FF), BF16),
                            pltpu.VMEM((D_FF, D_MODEL), BF16)]),
        out_shape=jax.ShapeDtypeStruct((MOE_NB * MOE_BM, D_MODEL), F32),
        compiler_params=_cparams("arbitrary"),
        name="moe_experts",
    )(blk_expert, n_used, blk_new, xs, w_up, b_up, w_down, b_down)


def _combine_kernel(dest_ref, x_ref, gate_ref, m_ref, fn_ref, ys_ref, o_ref, buf, sem, *, final):
    base = pl.program_id(0) * ROW_TILE * TOP_K

    def issue(t, carry):
        for k in range(TOP_K):
            _row_copy(ys_ref, dest_ref[base + t * TOP_K + k], buf.at[k], t, sem).start()
        return carry

    def drain(t, carry):
        for k in range(TOP_K):
            _row_copy(ys_ref, 0, buf.at[k], 0, sem).wait()
        return carry

    lax.fori_loop(0, ROW_TILE, issue, 0)
    lax.fori_loop(0, ROW_TILE, drain, 0)
    gate = gate_ref[...]
    acc = buf[0] * gate[:, 0:1]
    for k in range(1, TOP_K):
        acc = acc + buf[k] * gate[:, k:k + 1]
    x = x_ref[...] + m_ref[0][5:6] * acc
    o_ref[...] = _rms(x, fn_ref[...]) if final else x


def _combine(dest, x, gates, mods, final_norm, ys, final):
    return pl.pallas_call(
        functools.partial(_combine_kernel, final=final),
        grid_spec=pltpu.PrefetchScalarGridSpec(
            num_scalar_prefetch=1,
            grid=(N_ROW_TILES,),
            in_specs=[pl.BlockSpec((ROW_TILE, D_MODEL), lambda i, d: (i, 0)),
                      pl.BlockSpec((ROW_TILE, LANES), lambda i, d: (i, 0)),
                      pl.BlockSpec((1, 6, D_MODEL), lambda i, d: (_cond_of_tile(i), 0, 0)),
                      pl.BlockSpec((1, D_MODEL), lambda i, d: (0, 0)),
                      pl.BlockSpec(memory_space=pl.ANY)],
            out_specs=pl.BlockSpec((ROW_TILE, D_MODEL), lambda i, d: (i, 0)),
            scratch_shapes=[pltpu.VMEM((TOP_K, ROW_TILE, D_MODEL), F32),
                            pltpu.SemaphoreType.DMA(())]),
        out_shape=jax.ShapeDtypeStruct((N_TOK, D_MODEL), F32),
        compiler_params=_cparams("arbitrary"),
        name="moe_combine",
    )(dest, x, gates, mods, final_norm, ys)


def _moe_plan(rank, cnt, top_idx):
    sizes = cnt[0, :N_EXPERTS].astype(I32)
    padded = (sizes + MOE_BM - 1) // MOE_BM * MOE_BM
    pad_ends = jnp.cumsum(padded)
    pad_starts = pad_ends - padded
    row = rank[:, :N_EXPERTS].astype(I32) + pad_starts[None, :]
    dest = jnp.take_along_axis(row, top_idx, axis=1).reshape(-1)
    blk_start = jnp.arange(MOE_NB, dtype=I32) * MOE_BM
    blk_expert = jnp.minimum(jnp.sum(pad_ends[None, :] <= blk_start[:, None], axis=1), N_EXPERTS - 1).astype(I32)
    blk_new = jnp.concatenate([jnp.ones((1,), I32), (blk_expert[1:] != blk_expert[:-1]).astype(I32)])
    n_used = (pad_ends[-1:] // MOE_BM).astype(I32)
    return dest, blk_expert, n_used, blk_new


def _pad_lanes(v, fill=0.0):
    return jnp.pad(v, ((0, 0), (0, LANES - v.shape[-1])), constant_values=fill)


def kernel(x_prompt, x_sample, cache_na_k, cache_na_v, cache_gqa_k, cache_gqa_v, state_ssm, c, c_ctx, w_ada, b_ada, norm_mix, norm_ffn, w_in, na_rpb, gqa_q_norm, gqa_k_norm, ssm_conv_w, ssm_conv_b, ssm_dt_bias, ssm_a_log, ssm_d, ssm_norm, w_out, w_router, b_router, w_up, b_up, w_down, b_down, final_norm):
    x = jnp.concatenate([x_prompt.reshape(N_CTX_TOK, D_MODEL), x_sample.reshape(N_LAT_TOK, D_MODEL)], axis=0)
    conds = jnp.concatenate([c_ctx[None], c, jnp.zeros((COND_ROWS - N_COND, D_MODEL), F32)], axis=0)
    mods = _adaln(conds, w_ada, b_ada).reshape(DEPTH, COND_ROWS, 6, D_MODEL)
    cos, sin = _rope_tables()

    w_in_b = jnp.pad(w_in, ((0, 0), (0, 0), (0, IN_PAD - IN_DIM))).astype(BF16)
    w_out_b = w_out.astype(BF16)
    b_up_s = b_up.reshape(DEPTH, N_EXPERTS, 2 * D_FF // UP_GROUP, UP_GROUP // 2, 2)
    b_up_s = jnp.swapaxes(b_up_s, -1, -2).reshape(DEPTH, N_EXPERTS, 1, 2 * D_FF)
    b_down_s = b_down.reshape(DEPTH, N_EXPERTS, 1, D_MODEL)

    ctx_out = []
    for l in range(DEPTH):
        qkv, gqa, z, xbc, dt_raw = _inproj(x, mods[l], norm_mix[l][None], w_in_b[l])
        qn, kn = gqa_q_norm[l][None], gqa_k_norm[l][None]

        nao_c, go_c, gk_c = _ctx_attn(qkv, gqa, qn, kn)
        go_l = _lat_gqa(gqa, cache_gqa_k[:, l].reshape(DEC_BATCH, PAST_LEN, GQA_KV_DIM),
                        cache_gqa_v[:, l].reshape(DEC_BATCH, PAST_LEN, GQA_KV_DIM), cos, sin, qn, kn)
        nao_l = _lat_na(qkv, cache_na_k[:, l].reshape(DEC_BATCH, PAST_LEN, NA_DIM),
                        cache_na_v[:, l].reshape(DEC_BATCH, PAST_LEN, NA_DIM), _na_bias_tables(na_rpb[l]))

        xa = _conv_act(xbc, ssm_conv_w[l], ssm_conv_b[l][None])
        zeros = jnp.zeros((BATCH, SSM_INNER, SSM_STATE), F32)
        h0 = [jnp.concatenate([zeros, state_ssm[:, l, d].reshape(DEC_BATCH, SSM_INNER, SSM_STATE)], axis=0)
              for d in range(2)]
        y_f, st_f, y_b, st_b = _ssd(xa, dt_raw, h0[0], h0[1], _pad_lanes(ssm_dt_bias[l]), _pad_lanes(ssm_a_log[l]))
        ys = [y_f, y_b]
        sts = [st[:BATCH].reshape(BATCH, SSM_HEADS, SSM_HEAD_DIM, SSM_STATE) for st in (st_f, st_b)]

        x, h, top_idx, gates, sel = _outproj(
            x, jnp.concatenate([nao_c, nao_l], axis=0), jnp.concatenate([go_c, go_l], axis=0),
            ys[0], ys[1], xa, z, jnp.repeat(ssm_d[l], SSM_HEAD_DIM)[None], ssm_norm[l][None], w_out_b[l],
            mods[l], norm_ffn[l][None], _pad_lanes(w_router[l]), _pad_lanes(b_router[l][None], NEG_INF))

        rank, cnt = _ranks(sel)
        dest, blk_expert, n_used, blk_new = _moe_plan(rank, cnt, top_idx[:, :TOP_K])
        y_sorted = _experts(l, blk_expert, n_used, blk_new, _dispatch(dest, h), w_up, b_up_s[l], w_down, b_down_s[l])
        x = _combine(dest, x, gates, mods[l], final_norm[None], y_sorted, final=(l == DEPTH - 1))

        ctx_out.append((
            qkv[:N_CTX_TOK, NA_DIM:2 * NA_DIM].reshape(BATCH, SEQ, NA_HEADS, HEAD_DIM),
            qkv[:N_CTX_TOK, 2 * NA_DIM:].reshape(BATCH, SEQ, NA_HEADS, HEAD_DIM),
            gk_c.reshape(BATCH, SEQ, GQA_KV_HEADS, HEAD_DIM),
            gqa[:N_CTX_TOK, GQA_Q_DIM + GQA_KV_DIM:].reshape(BATCH, SEQ, GQA_KV_HEADS, HEAD_DIM),
            jnp.stack(sts, axis=1)))

    y_prompt = x[:N_CTX_TOK].reshape(BATCH, SEQ, D_MODEL)
    y_sample = x[N_CTX_TOK:].reshape(DEC_BATCH, DEC_SEQ, D_MODEL)
    return (y_prompt, y_sample) + tuple(jnp.stack([e[i] for e in ctx_out], axis=1) for i in range(5))
```

```python
import functools

import numpy as np
import jax
import jax.numpy as jnp
from jax import lax
from jax.experimental import pallas as pl
from jax.experimental.pallas import tpu as pltpu

F32 = jnp.float32
BF16 = jnp.bfloat16
I32 = jnp.int32

D_MODEL = 1024
BATCH = 16
SEQ = 256
DEPTH = 2
DEC_BATCH = 2
DEC_SEQ = 2048
PAST_LEN = 512
GRID_W = 64
HEAD_DIM = 64
NA_HEADS = 4
NA_WIN_ROWS = 8
NA_WIN_COLS = 16
GQA_HEADS = 4
GQA_KV_HEADS = 2
ROPE_THETA = 10000.0
SSM_HEADS = 8
SSM_HEAD_DIM = 64
SSM_STATE = 64
SSM_GROUPS = 2
SSM_INNER = SSM_HEADS * SSM_HEAD_DIM
SSM_BC_DIM = SSM_GROUPS * SSM_STATE
CONV_DIM = SSM_INNER + 2 * SSM_BC_DIM
CONV_W = 5
CHUNK = 128
NA_DIM = NA_HEADS * HEAD_DIM
GQA_Q_DIM = GQA_HEADS * HEAD_DIM
GQA_KV_DIM = GQA_KV_HEADS * HEAD_DIM
D_MIX = NA_DIM + GQA_Q_DIM + SSM_INNER
IN_DIM = 3 * NA_DIM + GQA_Q_DIM + 2 * GQA_KV_DIM + SSM_INNER + CONV_DIM + SSM_HEADS
N_EXPERTS = 32
TOP_K = 4
D_FF = D_MODEL
SWIGLU_LIMIT = 7.0
SWIGLU_ALPHA = 1.702
EPS = 1e-6
NEG_INF = -1e30

LANES = 128
N_CTX_TOK = BATCH * SEQ
N_LAT_TOK = DEC_BATCH * DEC_SEQ
N_TOK = N_CTX_TOK + N_LAT_TOK
N_COND = 1 + DEC_BATCH
COND_ROWS = 16
IN_PAD = 3 * NA_DIM + GQA_Q_DIM + 2 * GQA_KV_DIM + SSM_INNER + CONV_DIM + LANES
ROW_TILE = 256
N_ROW_TILES = N_TOK // ROW_TILE
MOE_BM = 256
N_SLOTS = N_TOK * TOP_K
MOE_NB = N_SLOTS // MOE_BM + N_EXPERTS
MOE_DUMP = N_EXPERTS * MOE_BM
UP_GROUP = 256
N_SEQ = BATCH + DEC_BATCH
N_CHUNKS = N_TOK // CHUNK
N_CTX_CHUNKS = N_CTX_TOK // CHUNK
VMEM_LIMIT = 56 * 1024 * 1024


def _cparams(*sem):
    return pltpu.CompilerParams(dimension_semantics=sem, vmem_limit_bytes=VMEM_LIMIT)


def _sigmoid(x):
    return 1.0 / (1.0 + jnp.exp(-x))


def _dot(a, b):
    return jnp.dot(a, b, preferred_element_type=F32)


def _dot_nt(a, b):
    return lax.dot_general(a, b, (((1,), (1,)), ((), ())), preferred_element_type=F32)


def _dot_exact(a, b):
    return jnp.dot(a, b, preferred_element_type=F32, precision=lax.Precision.HIGHEST)


def _rms(x, g):
    return x * lax.rsqrt(jnp.mean(x * x, axis=-1, keepdims=True) + EPS) * g


def _cond_of_tile(i):
    ctx_tiles = N_CTX_TOK // ROW_TILE
    return jnp.where(i < ctx_tiles, 0, 1 + (i - ctx_tiles) // (DEC_SEQ // ROW_TILE))


def _adaln_kernel(c_ref, w_ref, b_ref, o_ref):
    c = c_ref[...]
    s = (c * _sigmoid(c)).astype(BF16)
    o_ref[0] = _dot(s, w_ref[0].astype(BF16)) + b_ref[0]


def _adaln(conds, w_ada, b_ada):
    tn = 1536
    return pl.pallas_call(
        _adaln_kernel,
        grid=(DEPTH, 6 * D_MODEL // tn),
        in_specs=[pl.BlockSpec((COND_ROWS, D_MODEL), lambda l, j: (0, 0)),
                  pl.BlockSpec((1, D_MODEL, tn), lambda l, j: (l, 0, j)),
                  pl.BlockSpec((1, 1, tn), lambda l, j: (l, 0, j))],
        out_specs=pl.BlockSpec((1, COND_ROWS, tn), lambda l, j: (l, 0, j)),
        out_shape=jax.ShapeDtypeStruct((DEPTH, COND_ROWS, 6 * D_MODEL), F32),
        compiler_params=_cparams("parallel", "parallel"),
        name="adaln",
    )(conds, w_ada, b_ada.reshape(DEPTH, 1, 6 * D_MODEL))


_IN_SPLITS = (3 * NA_DIM, GQA_Q_DIM + 2 * GQA_KV_DIM, SSM_INNER, CONV_DIM, LANES)


def _inproj_kernel(x_ref, m_ref, g_ref, w_ref, qkv_ref, gqa_ref, z_ref, xbc_ref, dt_ref):
    m = m_ref[0]
    h = _rms(x_ref[...], g_ref[...]) * (1.0 + m[1:2]) + m[0:1]
    p = _dot(h.astype(BF16), w_ref[...])
    off = 0
    for ref, width in zip((qkv_ref, gqa_ref, z_ref, xbc_ref, dt_ref), _IN_SPLITS):
        ref[...] = p[:, off:off + width]
        off += width


def _inproj(x, mods, gain, w_in):
    row = lambda w: pl.BlockSpec((ROW_TILE, w), lambda i: (i, 0))
    return pl.pallas_call(
        _inproj_kernel,
        grid=(N_ROW_TILES,),
        in_specs=[row(D_MODEL),
                  pl.BlockSpec((1, 6, D_MODEL), lambda i: (_cond_of_tile(i), 0, 0)),
                  pl.BlockSpec((1, D_MODEL), lambda i: (0, 0)),
                  pl.BlockSpec((D_MODEL, IN_PAD), lambda i: (0, 0))],
        out_specs=[row(w) for w in _IN_SPLITS],
        out_shape=[jax.ShapeDtypeStruct((N_TOK, w), F32) for w in _IN_SPLITS],
        compiler_params=_cparams("parallel"),
        name="inproj",
    )(x, mods, gain, w_in)


def _softmax_pv(scores, values):
    m = scores[0].max(axis=-1, keepdims=True)
    for s in scores[1:]:
        m = jnp.maximum(m, s.max(axis=-1, keepdims=True))
    den = 0.0
    acc = 0.0
    for s, v in zip(scores, values):
        e = jnp.exp(s - m)
        den = den + e.sum(axis=-1, keepdims=True)
        acc = acc + _dot(e.astype(BF16), v)
    return acc / den


def _heads_rms(x, n_heads, g):
    return jnp.concatenate(
        [_rms(x[:, h * HEAD_DIM:(h + 1) * HEAD_DIM], g) for h in range(n_heads)], axis=-1)


def _rope(x, cos, sin_signed):
    w = x.shape[-1]
    lane = lax.broadcasted_iota(I32, x.shape, 1)
    partner = jnp.where((lane & 1) == 0, pltpu.roll(x, w - 1, 1), pltpu.roll(x, 1, 1))
    return x * cos + partner * sin_signed


_ATT_SCALE = HEAD_DIM ** -0.5


def _ctx_attn_kernel(qkv_ref, gqa_ref, qn_ref, kn_ref, nao_ref, go_ref, gk_ref):
    outs = []
    for h in range(NA_HEADS):
        sl = slice(h * HEAD_DIM, (h + 1) * HEAD_DIM)
        q = qkv_ref[:, sl].astype(BF16)
        k = qkv_ref[:, NA_DIM + h * HEAD_DIM:NA_DIM + (h + 1) * HEAD_DIM].astype(BF16)
        v = qkv_ref[:, 2 * NA_DIM + h * HEAD_DIM:2 * NA_DIM + (h + 1) * HEAD_DIM].astype(BF16)
        outs.append(_softmax_pv([_dot_nt(q, k) * _ATT_SCALE], [v]))
    nao_ref[...] = jnp.concatenate(outs, axis=-1).astype(BF16)

    gq = _heads_rms(gqa_ref[:, 0:GQA_Q_DIM], GQA_HEADS, qn_ref[...])
    gk = _heads_rms(gqa_ref[:, GQA_Q_DIM:GQA_Q_DIM + GQA_KV_DIM], GQA_KV_HEADS, kn_ref[...])
    gk_ref[...] = gk
    rep = GQA_HEADS // GQA_KV_HEADS
    outs = []
    for h in range(GQA_HEADS):
        g = h // rep
        q = gq[:, h * HEAD_DIM:(h + 1) * HEAD_DIM].astype(BF16)
        k = gk[:, g * HEAD_DIM:(g + 1) * HEAD_DIM].astype(BF16)
        v0 = GQA_Q_DIM + GQA_KV_DIM + g * HEAD_DIM
        v = gqa_ref[:, v0:v0 + HEAD_DIM].astype(BF16)
        outs.append(_softmax_pv([_dot_nt(q, k) * _ATT_SCALE], [v]))
    go_ref[...] = jnp.concatenate(outs, axis=-1).astype(BF16)


def _ctx_attn(qkv, gqa, q_norm, k_norm):
    return pl.pallas_call(
        _ctx_attn_kernel,
        grid=(BATCH,),
        in_specs=[pl.BlockSpec((SEQ, 3 * NA_DIM), lambda b: (b, 0)),
                  pl.BlockSpec((SEQ, GQA_Q_DIM + 2 * GQA_KV_DIM), lambda b: (b, 0)),
                  pl.BlockSpec((1, HEAD_DIM), lambda b: (0, 0)),
                  pl.BlockSpec((1, HEAD_DIM), lambda b: (0, 0))],
        out_specs=[pl.BlockSpec((SEQ, NA_DIM), lambda b: (b, 0)),
                   pl.BlockSpec((SEQ, GQA_Q_DIM), lambda b: (b, 0)),
                   pl.BlockSpec((SEQ, GQA_KV_DIM), lambda b: (b, 0))],
        out_shape=[jax.ShapeDtypeStruct((N_CTX_TOK, NA_DIM), BF16),
                   jax.ShapeDtypeStruct((N_CTX_TOK, GQA_Q_DIM), BF16),
                   jax.ShapeDtypeStruct((N_CTX_TOK, GQA_KV_DIM), F32)],
        compiler_params=_cparams("parallel"),
        name="ctx_attn",
    )(qkv, gqa, q_norm, k_norm)


GQA_TQ = 256
GQA_KEYS = PAST_LEN + DEC_SEQ


def _lat_gqa_kernel(gqa_ref, ck_ref, cv_ref, cos_ref, sin_ref, qn_ref, kn_ref, o_ref, kbuf, vbuf):
    qb = pl.program_id(1)

    @pl.when(qb == 0)
    def _():
        kbuf[0:PAST_LEN, :] = ck_ref[0].astype(BF16)
        vbuf[0:PAST_LEN, :] = cv_ref[0].astype(BF16)
        k = _heads_rms(gqa_ref[:, GQA_Q_DIM:GQA_Q_DIM + GQA_KV_DIM], GQA_KV_HEADS, kn_ref[...])
        k = _rope(k, cos_ref[:, 0:GQA_KV_DIM], sin_ref[:, 0:GQA_KV_DIM])
        kbuf[PAST_LEN:GQA_KEYS, :] = k.astype(BF16)
        vbuf[PAST_LEN:GQA_KEYS, :] = gqa_ref[:, GQA_Q_DIM + GQA_KV_DIM:].astype(BF16)

    r0 = pl.multiple_of(qb * GQA_TQ, GQA_TQ)
    q = _heads_rms(gqa_ref[pl.ds(r0, GQA_TQ), 0:GQA_Q_DIM], GQA_HEADS, qn_ref[...])
    q = _rope(q, cos_ref[pl.ds(r0, GQA_TQ), :], sin_ref[pl.ds(r0, GQA_TQ), :]).astype(BF16)
    rep = GQA_HEADS // GQA_KV_HEADS
    outs = []
    for h in range(GQA_HEADS):
        g = h // rep
        k = kbuf[:, g * HEAD_DIM:(g + 1) * HEAD_DIM]
        v = vbuf[:, g * HEAD_DIM:(g + 1) * HEAD_DIM]
        s = _dot_nt(q[:, h * HEAD_DIM:(h + 1) * HEAD_DIM], k) * _ATT_SCALE
        outs.append(_softmax_pv([s], [v]))
    o_ref[...] = jnp.concatenate(outs, axis=-1).astype(BF16)


def _lat_gqa(gqa, cache_k, cache_v, cos, sin, q_norm, k_norm):
    lat_blk = N_CTX_TOK // DEC_SEQ
    return pl.pallas_call(
        _lat_gqa_kernel,
        grid=(DEC_BATCH, DEC_SEQ // GQA_TQ),
        in_specs=[pl.BlockSpec((DEC_SEQ, GQA_Q_DIM + 2 * GQA_KV_DIM), lambda b, q: (lat_blk + b, 0)),
                  pl.BlockSpec((1, PAST_LEN, GQA_KV_DIM), lambda b, q: (b, 0, 0)),
                  pl.BlockSpec((1, PAST_LEN, GQA_KV_DIM), lambda b, q: (b, 0, 0)),
                  pl.BlockSpec((DEC_SEQ, GQA_Q_DIM), lambda b, q: (0, 0)),
                  pl.BlockSpec((DEC_SEQ, GQA_Q_DIM), lambda b, q: (0, 0)),
                  pl.BlockSpec((1, HEAD_DIM), lambda b, q: (0, 0)),
                  pl.BlockSpec((1, HEAD_DIM), lambda b, q: (0, 0))],
        out_specs=pl.BlockSpec((GQA_TQ, GQA_Q_DIM), lambda b, q: (b * (DEC_SEQ // GQA_TQ) + q, 0)),
        out_shape=jax.ShapeDtypeStruct((N_LAT_TOK, GQA_Q_DIM), BF16),
        scratch_shapes=[pltpu.VMEM((GQA_KEYS, GQA_KV_DIM), BF16),
                        pltpu.VMEM((GQA_KEYS, GQA_KV_DIM), BF16)],
        compiler_params=_cparams("arbitrary", "arbitrary"),
        name="lat_gqa",
    )(gqa, cache_k, cache_v, cos, sin, q_norm, k_norm)


def _rope_tables():
    t = jnp.arange(DEC_SEQ)
    row = (t // GRID_W).astype(F32)
    col = (t % GRID_W).astype(F32)
    axis_dim = HEAD_DIM // 2
    inv_freq = ROPE_THETA ** (-jnp.arange(0, axis_dim, 2, dtype=F32) / axis_dim)
    ang = jnp.concatenate([row[:, None] * inv_freq, col[:, None] * inv_freq], axis=-1)
    cos = jnp.repeat(jnp.cos(ang), 2, axis=-1)
    sin = jnp.repeat(jnp.sin(ang), 2, axis=-1) * jnp.tile(jnp.array([-1.0, 1.0], F32), HEAD_DIM // 2)
    return jnp.tile(cos, (1, GQA_HEADS)), jnp.tile(sin, (1, GQA_HEADS))


NA_ROWS = DEC_SEQ // GRID_W
NA_KEYS = NA_WIN_ROWS * GRID_W


def _lat_na_kernel(qkv_ref, ck_ref, cv_ref, bias_ref, o_ref):
    r = pl.program_id(1)
    r0 = jnp.clip(r - NA_WIN_ROWS // 2, 0, NA_ROWS - NA_WIN_ROWS)
    q0 = pl.multiple_of(r * GRID_W, GRID_W)
    k0 = pl.multiple_of(r0 * GRID_W, GRID_W)
    outs = []
    for h in range(NA_HEADS):
        c0 = h * HEAD_DIM
        q = qkv_ref[pl.ds(q0, GRID_W), c0:c0 + HEAD_DIM].astype(BF16)
        k = qkv_ref[pl.ds(k0, NA_KEYS), NA_DIM + c0:NA_DIM + c0 + HEAD_DIM].astype(BF16)
        v = qkv_ref[pl.ds(k0, NA_KEYS), 2 * NA_DIM + c0:2 * NA_DIM + c0 + HEAD_DIM].astype(BF16)
        kc = ck_ref[0, :, c0:c0 + HEAD_DIM].astype(BF16)
        vc = cv_ref[0, :, c0:c0 + HEAD_DIM].astype(BF16)
        s_nb = _dot_nt(q, k) * _ATT_SCALE + bias_ref[0, h]
        s_ctx = _dot_nt(q, kc) * _ATT_SCALE
        outs.append(_softmax_pv([s_nb, s_ctx], [v, vc]))
    o_ref[...] = jnp.concatenate(outs, axis=-1).astype(BF16)


def _na_row_offset(r):
    return r - jnp.clip(r - NA_WIN_ROWS // 2, 0, NA_ROWS - NA_WIN_ROWS)


def _lat_na(qkv, cache_k, cache_v, bias):
    lat_blk = N_CTX_TOK // DEC_SEQ
    return pl.pallas_call(
        _lat_na_kernel,
        grid=(DEC_BATCH, NA_ROWS),
        in_specs=[pl.BlockSpec((DEC_SEQ, 3 * NA_DIM), lambda b, r: (lat_blk + b, 0)),
                  pl.BlockSpec((1, PAST_LEN, NA_DIM), lambda b, r: (b, 0, 0)),
                  pl.BlockSpec((1, PAST_LEN, NA_DIM), lambda b, r: (b, 0, 0)),
                  pl.BlockSpec((1, NA_HEADS, GRID_W, NA_KEYS), lambda b, r: (_na_row_offset(r), 0, 0, 0))],
        out_specs=pl.BlockSpec((GRID_W, NA_DIM), lambda b, r: (b * NA_ROWS + r, 0)),
        out_shape=jax.ShapeDtypeStruct((N_LAT_TOK, NA_DIM), BF16),
        compiler_params=_cparams("parallel", "arbitrary"),
        name="lat_na",
    )(qkv, cache_k, cache_v, bias)


def _na_bias_tables(rpb):
    d = np.arange(NA_WIN_ROWS)[:, None]
    kr = np.arange(NA_WIN_ROWS)[None, :]
    dr = kr - d + NA_WIN_ROWS - 1
    qc = np.arange(GRID_W)[:, None]
    kc = np.arange(GRID_W)[None, :]
    col0 = np.clip(qc - NA_WIN_COLS // 2, 0, GRID_W - NA_WIN_COLS)
    in_win = (kc >= col0) & (kc < col0 + NA_WIN_COLS)
    dc = np.clip(kc - qc + NA_WIN_COLS - 1, 0, 2 * NA_WIN_COLS - 2)
    row_hot = (dr[:, :, None] == np.arange(2 * NA_WIN_ROWS - 1)).astype(np.float32)
    col_hot = (dc[:, :, None] == np.arange(2 * NA_WIN_COLS - 1)).astype(np.float32)
    b = jnp.einsum('hac,dka,qxc->dhqkx', rpb.astype(F32), row_hot, col_hot, precision=lax.Precision.HIGHEST)
    b = jnp.where(in_win[None, None, :, None, :], b, NEG_INF)
    return b.reshape(NA_WIN_ROWS, NA_HEADS, GRID_W, NA_KEYS)


CONV_TB = 1024
CONV_HALO = 8
CONV_HALO_BLOCKS = CONV_TB // CONV_HALO


def _conv_kernel(prev_ref, x_ref, next_ref, w_ref, b_ref, o_ref):
    i = pl.program_id(0)
    seq = jnp.where(i < N_CTX_TOK // CONV_TB, SEQ, DEC_SEQ)
    x = x_ref[...]
    ext = jnp.concatenate([prev_ref[...], x, next_ref[...]], axis=0)
    n_ext = CONV_TB + 2 * CONV_HALO
    pos = (lax.broadcasted_iota(I32, (CONV_TB, 1), 0) + i * CONV_TB) & (seq - 1)
    half = CONV_W // 2
    acc = x * w_ref[half:half + 1, :]
    for s in range(-half, half + 1):
        if s == 0:
            continue
        shifted = pltpu.roll(ext, (-s) % n_ext, 0)[CONV_HALO:CONV_HALO + CONV_TB]
        valid = (pos + s >= 0) & (pos + s < seq)
        acc = acc + jnp.where(valid, shifted, 0.0) * w_ref[half + s:half + s + 1, :]
    acc = acc + b_ref[...]
    o_ref[...] = acc * _sigmoid(acc)


def _conv_act(xbc, conv_w, conv_b):
    return pl.pallas_call(
        _conv_kernel,
        grid=(N_TOK // CONV_TB,),
        in_specs=[pl.BlockSpec((CONV_HALO, CONV_DIM),
                               lambda i: (jnp.maximum(i * CONV_HALO_BLOCKS - 1, 0), 0)),
                  pl.BlockSpec((CONV_TB, CONV_DIM), lambda i: (i, 0)),
                  pl.BlockSpec((CONV_HALO, CONV_DIM),
                               lambda i: (jnp.minimum((i + 1) * CONV_HALO_BLOCKS, N_TOK // CONV_HALO - 1), 0)),
                  pl.BlockSpec((CONV_W, CONV_DIM), lambda i: (0, 0)),
                  pl.BlockSpec((1, CONV_DIM), lambda i: (0, 0))],
        out_specs=pl.BlockSpec((CONV_TB, CONV_DIM), lambda i: (i, 0)),
        out_shape=jax.ShapeDtypeStruct((N_TOK, CONV_DIM), F32),
        compiler_params=_cparams("parallel"),
        name="conv_act",
    )(xbc, xbc, xbc, conv_w, conv_b)


def _chunk_seq(g):
    ctx_n = SEQ // CHUNK
    lat_n = DEC_SEQ // CHUNK
    is_ctx = g < N_CTX_CHUNKS
    gl = g - N_CTX_CHUNKS
    sid = jnp.where(is_ctx, g // ctx_n, BATCH + gl // lat_n)
    cin = jnp.where(is_ctx, g % ctx_n, gl % lat_n)
    n = jnp.where(is_ctx, ctx_n, lat_n)
    return sid, cin, n


def _ssd_init(h0_ref, st_ref, gg, reverse):
    _, cin, n = _chunk_seq(gg)

    @pl.when(cin == (n - 1 if reverse else 0))
    def _():
        st_ref[0] = h0_ref[0]


def _ssd_chunk(xa_ref, dt_ref, dtb, alog, y_ref, st_ref, reverse):
    x = dt_ref[...] + dtb
    dt = jnp.maximum(x, 0.0) + jnp.log1p(jnp.exp(-jnp.abs(x)))
    dta = dt * -jnp.exp(alog)
    ii = lax.broadcasted_iota(I32, (CHUNK, CHUNK), 0)
    jj = lax.broadcasted_iota(I32, (CHUNK, CHUNK), 1)
    tri = (jj >= ii) if reverse else (jj <= ii)
    cum = _dot_exact(tri.astype(F32), dta)
    cum_t = cum.T
    edge = 0 if reverse else CHUNK - 1
    tot = cum[edge:edge + 1, :]
    rep = SSM_HEADS // SSM_GROUPS
    for grp in range(SSM_GROUPS):
        bg = xa_ref[:, SSM_INNER + grp * SSM_STATE:SSM_INNER + (grp + 1) * SSM_STATE].astype(BF16)
        c0 = SSM_INNER + SSM_BC_DIM + grp * SSM_STATE
        cg = xa_ref[:, c0:c0 + SSM_STATE].astype(BF16)
        cb = _dot_nt(cg, bg)
        for h in range(grp * rep, (grp + 1) * rep):
            hs = slice(h * SSM_HEAD_DIM, (h + 1) * SSM_HEAD_DIM)
            col = cum[:, h:h + 1]
            row = cum_t[h:h + 1, :]
            decay = jnp.where(tri, jnp.exp(jnp.minimum(col - row, 0.0)), 0.0)
            xdt = xa_ref[:, hs] * dt[:, h:h + 1]
            state = st_ref[0, hs, :]
            y = _dot((cb * decay).astype(BF16), xdt.astype(BF16))
            y = y + _dot_nt(cg, state.astype(BF16)) * jnp.exp(col)
            y_ref[:, hs] = y
            toth = tot[:, h:h + 1]
            w = (xdt * jnp.exp(toth - col)).astype(BF16)
            upd = lax.dot_general(w, bg, (((0,), (0,)), ((), ())), preferred_element_type=F32)
            st_ref[0, hs, :] = state * jnp.exp(toth) + upd


def _ssd_kernel(xaf_ref, dtf_ref, h0f_ref, xab_ref, dtb_ref, h0b_ref, bias_ref, alog_ref,
                yf_ref, stf_ref, yb_ref, stb_ref):
    g = pl.program_id(0)
    _ssd_init(h0f_ref, stf_ref, g, False)
    _ssd_chunk(xaf_ref, dtf_ref, bias_ref[0:1, :], alog_ref[0:1, :], yf_ref, stf_ref, False)
    _ssd_init(h0b_ref, stb_ref, N_CHUNKS - 1 - g, True)
    _ssd_chunk(xab_ref, dtb_ref, bias_ref[1:2, :], alog_ref[1:2, :], yb_ref, stb_ref, True)


def _ssd(xa, dt_raw, h0_fwd, h0_bwd, dt_bias, a_log):
    rev = lambda g: N_CHUNKS - 1 - g
    chunk = lambda w, order: pl.BlockSpec((CHUNK, w), lambda g: (order(g), 0))
    state = lambda order: pl.BlockSpec((1, SSM_INNER, SSM_STATE), lambda g: (_chunk_seq(order(g))[0], 0, 0))
    same = lambda g: g
    y_shape = jax.ShapeDtypeStruct((N_TOK, SSM_INNER), F32)
    st_shape = jax.ShapeDtypeStruct((N_SEQ, SSM_INNER, SSM_STATE), F32)
    return pl.pallas_call(
        _ssd_kernel,
        grid=(N_CHUNKS,),
        in_specs=[chunk(CONV_DIM, same), chunk(LANES, same), state(same),
                  chunk(CONV_DIM, rev), chunk(LANES, rev), state(rev),
                  pl.BlockSpec((2, LANES), lambda g: (0, 0)),
                  pl.BlockSpec((2, LANES), lambda g: (0, 0))],
        out_specs=[chunk(SSM_INNER, same), state(same), chunk(SSM_INNER, rev), state(rev)],
        out_shape=[y_shape, st_shape, y_shape, st_shape],
        compiler_params=_cparams("arbitrary"),
        name="ssd",
    )(xa, dt_raw, h0_fwd, xa, dt_raw, h0_bwd, dt_bias, a_log)


def _outproj_kernel(x_ref, nao_ref, go_ref, yf_ref, yb_ref, xs_ref, z_ref, dsk_ref, sn_ref, wo_ref,
                    m_ref, g2_ref, wr_ref, br_ref, xo_ref, h_ref, idx_ref, gate_ref, sel_ref):
    m = m_ref[0]
    z = z_ref[...]
    y = (yf_ref[...] + yb_ref[...] + xs_ref[...] * dsk_ref[...]) * (z * _sigmoid(z))
    s_o = _rms(y, sn_ref[...]).astype(BF16)
    mix = (_dot(nao_ref[...], wo_ref[0:NA_DIM, :])
           + _dot(go_ref[...], wo_ref[NA_DIM:NA_DIM + GQA_Q_DIM, :])
           + _dot(s_o, wo_ref[NA_DIM + GQA_Q_DIM:, :]))
    x = x_ref[...] + m[2:3] * mix
    xo_ref[...] = x
    h = _rms(x, g2_ref[...]) * (1.0 + m[4:5]) + m[3:4]
    h_ref[...] = h

    logits = _dot_exact(h, wr_ref[...]) + br_ref[...]
    lane = lax.broadcasted_iota(I32, logits.shape, 1).astype(F32)
    vals, idxs = [], []
    for _ in range(TOP_K):
        v = logits.max(axis=-1, keepdims=True)
        i = jnp.where(logits == v, lane, float(LANES)).min(axis=-1, keepdims=True)
        vals.append(v)
        idxs.append(i)
        logits = jnp.where(lane == i, -jnp.inf, logits)
    es = [jnp.exp(v - vals[0]) for v in vals]
    den = es[0] + es[1] + es[2] + es[3]
    idx_out = jnp.zeros(lane.shape, F32)
    gate_out = jnp.zeros(lane.shape, F32)
    sel = jnp.zeros(lane.shape, F32)
    for k in range(TOP_K):
        idx_out = jnp.where(lane == float(k), idxs[k], idx_out)
        gate_out = jnp.where(lane == float(k), es[k] / den, gate_out)
        sel = jnp.where(lane == idxs[k], 1.0, sel)
    idx_ref[...] = idx_out.astype(I32)
    gate_ref[...] = gate_out
    sel_ref[...] = sel.astype(BF16)


def _outproj(x, nao, go, yf, yb, xa, z, d_skip, ssm_norm, w_out, mods, gain2, w_router, b_router):
    row = lambda w: pl.BlockSpec((ROW_TILE, w), lambda i: (i, 0))
    full = lambda a, b: pl.BlockSpec((a, b), lambda i: (0, 0))
    return pl.pallas_call(
        _outproj_kernel,
        grid=(N_ROW_TILES,),
        in_specs=[row(D_MODEL), row(NA_DIM), row(GQA_Q_DIM), row(SSM_INNER), row(SSM_INNER),
                  row(SSM_INNER), row(SSM_INNER), full(1, SSM_INNER), full(1, SSM_INNER),
                  full(D_MIX, D_MODEL),
                  pl.BlockSpec((1, 6, D_MODEL), lambda i: (_cond_of_tile(i), 0, 0)),
                  full(1, D_MODEL), full(D_MODEL, LANES), full(1, LANES)],
        out_specs=[row(D_MODEL), row(D_MODEL), row(LANES), row(LANES), row(LANES)],
        out_shape=[jax.ShapeDtypeStruct((N_TOK, D_MODEL), F32),
                   jax.ShapeDtypeStruct((N_TOK, D_MODEL), F32),
                   jax.ShapeDtypeStruct((N_TOK, LANES), I32),
                   jax.ShapeDtypeStruct((N_TOK, LANES), F32),
                   jax.ShapeDtypeStruct((N_TOK, LANES), BF16)],
        compiler_params=_cparams("parallel"),
        name="outproj_router",
    )(x, nao, go, yf, yb, xa, z, d_skip, ssm_norm, w_out, mods, gain2, w_router, b_router)


RANK_TB = 512


def _rank_kernel(sel_ref, rank_ref, cnt_ref, carry):
    @pl.when(pl.program_id(0) == 0)
    def _():
        carry[...] = jnp.zeros_like(carry)

    sel = sel_ref[...]
    ii = lax.broadcasted_iota(I32, (RANK_TB, RANK_TB), 0)
    jj = lax.broadcasted_iota(I32, (RANK_TB, RANK_TB), 1)
    before = (jj < ii).astype(BF16)
    rank_ref[...] = _dot(before, sel) + carry[0:1, :]
    carry[...] = carry[...] + _dot(jnp.ones((8, RANK_TB), BF16), sel)
    cnt_ref[...] = carry[...]


def _ranks(sel):
    return pl.pallas_call(
        _rank_kernel,
        grid=(N_TOK // RANK_TB,),
        in_specs=[pl.BlockSpec((RANK_TB, LANES), lambda i: (i, 0))],
        out_specs=[pl.BlockSpec((RANK_TB, LANES), lambda i: (i, 0)),
                   pl.BlockSpec((8, LANES), lambda i: (0, 0))],
        out_shape=[jax.ShapeDtypeStruct((N_TOK, LANES), F32),
                   jax.ShapeDtypeStruct((8, LANES), F32)],
        scratch_shapes=[pltpu.VMEM((8, LANES), F32)],
        compiler_params=_cparams("arbitrary"),
        name="moe_ranks",
    )(sel)


def _row_copy(src, s, dst, d, sem):
    return pltpu.make_async_copy(src.at[pl.ds(s, 1), :], dst.at[pl.ds(d, 1), :], sem)


def _slot_row(t, k):
    return (t // ROW_TILE) * (ROW_TILE * TOP_K) + k * ROW_TILE + t % ROW_TILE


def _slot_token(row):
    return (row // (ROW_TILE * TOP_K)) * ROW_TILE + row % ROW_TILE


def _invert_kernel(dest_ref, size_ref, start_ref, src_ref):
    def real(s, carry):
        src_ref[dest_ref[s]] = _slot_row(s // TOP_K, s % TOP_K)
        return carry

    lax.fori_loop(0, N_SLOTS, real, 0, unroll=8)

    def per_expert(e, n_pad):
        lo = start_ref[e] + size_ref[e]
        hi = start_ref[e + 1]

        def pad(p, carry):
            src_ref[p] = N_SLOTS + n_pad + (p - lo)
            return carry

        lax.fori_loop(lo, hi, pad, 0)
        return n_pad + (hi - lo)

    lax.fori_loop(0, N_EXPERTS, per_expert, 0)

    def tail(p, carry):
        src_ref[p] = 0
        return carry

    lax.fori_loop(start_ref[N_EXPERTS], MOE_NB * MOE_BM, tail, 0)


def _invert(dest, sizes, starts):
    return pl.pallas_call(
        _invert_kernel,
        grid_spec=pltpu.PrefetchScalarGridSpec(
            num_scalar_prefetch=3,
            grid=(1,),
            in_specs=[],
            out_specs=pl.BlockSpec(memory_space=pltpu.SMEM)),
        out_shape=jax.ShapeDtypeStruct((MOE_NB * MOE_BM,), I32),
        compiler_params=_cparams("arbitrary"),
        name="moe_invert",
    )(dest, sizes, starts)


def _expert_kernel(be_ref, nu_ref, new_ref, src_ref, h_ref, wu_ref, bu_ref, wd_ref, bd_ref, y_ref,
                   wu_s, wd_s, xbuf, ybuf, gsem, ssem):
    b = pl.program_id(0)
    nu = nu_ref[0]
    used = b < nu
    slot = b % 2

    def gather(blk, buf):
        def body(r, carry):
            s = src_ref[blk * MOE_BM + r]
            tok = jnp.where(s < N_SLOTS, _slot_token(s), 0)
            _row_copy(h_ref, tok, xbuf.at[buf], r, gsem.at[buf]).start()
            return carry

        lax.fori_loop(0, MOE_BM, body, 0, unroll=8)

    def scatter(blk, buf):
        def body(r, carry):
            _row_copy(ybuf.at[buf], r, y_ref, src_ref[blk * MOE_BM + r], ssem.at[buf]).start()
            return carry

        lax.fori_loop(0, MOE_BM, body, 0, unroll=8)

    def wait_block(buf_ref, sem):
        pltpu.make_async_copy(buf_ref, buf_ref, sem).wait()

    @pl.when(b == 0)
    def _():
        ybuf[0] = jnp.zeros((MOE_BM, D_MODEL), F32)
        fills = [pltpu.make_async_copy(ybuf.at[0], y_ref.at[pl.ds(N_SLOTS + j * MOE_BM, MOE_BM), :], ssem.at[0])
                 for j in range(MOE_DUMP // MOE_BM)]
        for f in fills:
            f.start()
        for f in fills:
            f.wait()
        gather(0, 0)

    @pl.when(jnp.logical_and(used, new_ref[b] == 1))
    def _():
        r = lax.broadcasted_iota(I32, (UP_GROUP, UP_GROUP), 0)
        c = lax.broadcasted_iota(I32, (UP_GROUP, UP_GROUP), 1)
        src = jnp.where(c < UP_GROUP // 2, 2 * c, 2 * (c - UP_GROUP // 2) + 1)
        perm = (r == src).astype(BF16)
        for g in range(2 * D_FF // UP_GROUP):
            cols = slice(g * UP_GROUP, (g + 1) * UP_GROUP)
            wu_s[:, cols] = _dot(wu_ref[0, 0, :, cols].astype(BF16), perm).astype(BF16)
        wd_s[...] = wd_ref[0, 0].astype(BF16)

    @pl.when(used)
    def _():
        wait_block(xbuf.at[slot], gsem.at[slot])

        @pl.when(b + 1 < nu)
        def _():
            gather(b + 1, 1 - slot)

        @pl.when(b >= 2)
        def _():
            wait_block(ybuf.at[slot], ssem.at[slot])

        up = _dot(xbuf[slot].astype(BF16), wu_s[...]) + bu_ref[0]
        half = UP_GROUP // 2
        acts = []
        for g in range(2 * D_FF // UP_GROUP):
            gate = jnp.minimum(up[:, g * UP_GROUP:g * UP_GROUP + half], SWIGLU_LIMIT)
            lin = jnp.clip(up[:, g * UP_GROUP + half:(g + 1) * UP_GROUP], -SWIGLU_LIMIT, SWIGLU_LIMIT)
            acts.append((gate * _sigmoid(SWIGLU_ALPHA * gate) * (lin + 1.0)).astype(BF16))
        ybuf[slot] = _dot(jnp.concatenate(acts, axis=-1), wd_s[...]) + bd_ref[0]
        scatter(b, slot)

        @pl.when(b + 1 == nu)
        def _():
            wait_block(ybuf.at[slot], ssem.at[slot])

            @pl.when(b >= 1)
            def _():
                wait_block(ybuf.at[1 - slot], ssem.at[1 - slot])


def _experts(layer, blk_expert, n_used, blk_new, src, h, w_up, b_up, w_down, b_down):
    blk = lambda b, nu: jnp.maximum(jnp.minimum(b, nu[0] - 1), 0)
    return pl.pallas_call(
        _expert_kernel,
        grid_spec=pltpu.PrefetchScalarGridSpec(
            num_scalar_prefetch=4,
            grid=(MOE_NB,),
            in_specs=[pl.BlockSpec(memory_space=pl.ANY),
                      pl.BlockSpec((1, 1, D_MODEL, 2 * D_FF), lambda b, be, nu, nw, sr: (layer, be[blk(b, nu)], 0, 0)),
                      pl.BlockSpec((1, 1, 2 * D_FF), lambda b, be, nu, nw, sr: (be[blk(b, nu)], 0, 0)),
                      pl.BlockSpec((1, 1, D_FF, D_MODEL), lambda b, be, nu, nw, sr: (layer, be[blk(b, nu)], 0, 0)),
                      pl.BlockSpec((1, 1, D_MODEL), lambda b, be, nu, nw, sr: (be[blk(b, nu)], 0, 0))],
            out_specs=pl.BlockSpec(memory_space=pl.ANY),
            scratch_shapes=[pltpu.VMEM((D_MODEL, 2 * D_FF), BF16),
                            pltpu.VMEM((D_FF, D_MODEL), BF16),
                            pltpu.VMEM((2, MOE_BM, D_MODEL), F32),
                            pltpu.VMEM((2, MOE_BM, D_MODEL), F32),
                            pltpu.SemaphoreType.DMA((2,)),
                            pltpu.SemaphoreType.DMA((2,))]),
        out_shape=jax.ShapeDtypeStruct((N_SLOTS + MOE_DUMP, D_MODEL), F32),
        compiler_params=_cparams("arbitrary"),
        name="moe_experts",
    )(blk_expert, n_used, blk_new, src, h, w_up, b_up, w_down, b_down)


def _combine_kernel(x_ref, gate_ref, m_ref, fn_ref, y_ref, o_ref, *, final):
    gate = gate_ref[...]
    acc = y_ref[0:ROW_TILE, :] * gate[:, 0:1]
    for k in range(1, TOP_K):
        acc = acc + y_ref[k * ROW_TILE:(k + 1) * ROW_TILE, :] * gate[:, k:k + 1]
    x = x_ref[...] + m_ref[0][5:6] * acc
    o_ref[...] = _rms(x, fn_ref[...]) if final else x


def _combine(x, gates, mods, final_norm, y_slots, final):
    return pl.pallas_call(
        functools.partial(_combine_kernel, final=final),
        grid=(N_ROW_TILES,),
        in_specs=[pl.BlockSpec((ROW_TILE, D_MODEL), lambda i: (i, 0)),
                  pl.BlockSpec((ROW_TILE, LANES), lambda i: (i, 0)),
                  pl.BlockSpec((1, 6, D_MODEL), lambda i: (_cond_of_tile(i), 0, 0)),
                  pl.BlockSpec((1, D_MODEL), lambda i: (0, 0)),
                  pl.BlockSpec((ROW_TILE * TOP_K, D_MODEL), lambda i: (i, 0))],
        out_specs=pl.BlockSpec((ROW_TILE, D_MODEL), lambda i: (i, 0)),
        out_shape=jax.ShapeDtypeStruct((N_TOK, D_MODEL), F32),
        compiler_params=_cparams("parallel"),
        name="moe_combine",
    )(x, gates, mods, final_norm, y_slots)


def _moe_plan(rank, cnt, top_idx):
    sizes = cnt[0, :N_EXPERTS].astype(I32)
    padded = (sizes + MOE_BM - 1) // MOE_BM * MOE_BM
    pad_ends = jnp.cumsum(padded)
    pad_starts = pad_ends - padded
    row = rank[:, :N_EXPERTS].astype(I32) + pad_starts[None, :]
    dest = jnp.take_along_axis(row, top_idx, axis=1).reshape(-1)
    blk_start = jnp.arange(MOE_NB, dtype=I32) * MOE_BM
    blk_expert = jnp.minimum(jnp.sum(pad_ends[None, :] <= blk_start[:, None], axis=1), N_EXPERTS - 1).astype(I32)
    blk_new = jnp.concatenate([jnp.ones((1,), I32), (blk_expert[1:] != blk_expert[:-1]).astype(I32)])
    n_used = (pad_ends[-1:] // MOE_BM).astype(I32)
    starts = jnp.concatenate([pad_starts, pad_ends[-1:]]).astype(I32)
    return _invert(dest, sizes, starts), blk_expert, n_used, blk_new


def _pad_lanes(v, fill=0.0):
    return jnp.pad(v, ((0, 0), (0, LANES - v.shape[-1])), constant_values=fill)


def kernel(x_prompt, x_sample, cache_na_k, cache_na_v, cache_gqa_k, cache_gqa_v, state_ssm, c, c_ctx, w_ada, b_ada, norm_mix, norm_ffn, w_in, na_rpb, gqa_q_norm, gqa_k_norm, ssm_conv_w, ssm_conv_b, ssm_dt_bias, ssm_a_log, ssm_d, ssm_norm, w_out, w_router, b_router, w_up, b_up, w_down, b_down, final_norm):
    x = jnp.concatenate([x_prompt.reshape(N_CTX_TOK, D_MODEL), x_sample.reshape(N_LAT_TOK, D_MODEL)], axis=0)
    conds = jnp.concatenate([c_ctx[None], c, jnp.zeros((COND_ROWS - N_COND, D_MODEL), F32)], axis=0)
    mods = _adaln(conds, w_ada, b_ada).reshape(DEPTH, COND_ROWS, 6, D_MODEL)
    cos, sin = _rope_tables()

    w_in_b = jnp.pad(w_in, ((0, 0), (0, 0), (0, IN_PAD - IN_DIM))).astype(BF16)
    w_out_b = w_out.astype(BF16)
    b_up_s = b_up.reshape(DEPTH, N_EXPERTS, 2 * D_FF // UP_GROUP, UP_GROUP // 2, 2)
    b_up_s = jnp.swapaxes(b_up_s, -1, -2).reshape(DEPTH, N_EXPERTS, 1, 2 * D_FF)
    b_down_s = b_down.reshape(DEPTH, N_EXPERTS, 1, D_MODEL)

    ctx_out = []
    for l in range(DEPTH):
        qkv, gqa, z, xbc, dt_raw = _inproj(x, mods[l], norm_mix[l][None], w_in_b[l])
        qn, kn = gqa_q_norm[l][None], gqa_k_norm[l][None]

        nao_c, go_c, gk_c = _ctx_attn(qkv, gqa, qn, kn)
        go_l = _lat_gqa(gqa, cache_gqa_k[:, l].reshape(DEC_BATCH, PAST_LEN, GQA_KV_DIM),
                        cache_gqa_v[:, l].reshape(DEC_BATCH, PAST_LEN, GQA_KV_DIM), cos, sin, qn, kn)
        nao_l = _lat_na(qkv, cache_na_k[:, l].reshape(DEC_BATCH, PAST_LEN, NA_DIM),
                        cache_na_v[:, l].reshape(DEC_BATCH, PAST_LEN, NA_DIM), _na_bias_tables(na_rpb[l]))

        xa = _conv_act(xbc, ssm_conv_w[l], ssm_conv_b[l][None])
        zeros = jnp.zeros((BATCH, SSM_INNER, SSM_STATE), F32)
        h0 = [jnp.concatenate([zeros, state_ssm[:, l, d].reshape(DEC_BATCH, SSM_INNER, SSM_STATE)], axis=0)
              for d in range(2)]
        y_f, st_f, y_b, st_b = _ssd(xa, dt_raw, h0[0], h0[1], _pad_lanes(ssm_dt_bias[l]), _pad_lanes(ssm_a_log[l]))
        ys = [y_f, y_b]
        sts = [st[:BATCH].reshape(BATCH, SSM_HEADS, SSM_HEAD_DIM, SSM_STATE) for st in (st_f, st_b)]

        x, h, top_idx, gates, sel = _outproj(
            x, jnp.concatenate([nao_c, nao_l], axis=0), jnp.concatenate([go_c, go_l], axis=0),
            ys[0], ys[1], xa, z, jnp.repeat(ssm_d[l], SSM_HEAD_DIM)[None], ssm_norm[l][None], w_out_b[l],
            mods[l], norm_ffn[l][None], _pad_lanes(w_router[l]), _pad_lanes(b_router[l][None], NEG_INF))

        rank, cnt = _ranks(sel)
        src, blk_expert, n_used, blk_new = _moe_plan(rank, cnt, top_idx[:, :TOP_K])
        y_slots = _experts(l, blk_expert, n_used, blk_new, src, h, w_up, b_up_s[l], w_down, b_down_s[l])
        x = _combine(x, gates, mods[l], final_norm[None], y_slots, final=(l == DEPTH - 1))

        ctx_out.append((
            qkv[:N_CTX_TOK, NA_DIM:2 * NA_DIM].reshape(BATCH, SEQ, NA_HEADS, HEAD_DIM),
            qkv[:N_CTX_TOK, 2 * NA_DIM:].reshape(BATCH, SEQ, NA_HEADS, HEAD_DIM),
            gk_c.reshape(BATCH, SEQ, GQA_KV_HEADS, HEAD_DIM),
            gqa[:N_CTX_TOK, GQA_Q_DIM + GQA_KV_DIM:].reshape(BATCH, SEQ, GQA_KV_HEADS, HEAD_DIM),
            jnp.stack(sts, axis=1)))

    y_prompt = x[:N_CTX_TOK].reshape(BATCH, SEQ, D_MODEL)
    y_sample = x[N_CTX_TOK:].reshape(DEC_BATCH, DEC_SEQ, D_MODEL)
    return (y_prompt, y_sample) + tuple(jnp.stack([e[i] for e in ctx_out], axis=1) for i in range(5))
```

```python
import functools

import numpy as np
import jax
import jax.numpy as jnp
from jax import lax
from jax.experimental import pallas as pl
from jax.experimental.pallas import tpu as pltpu

F32 = jnp.float32
BF16 = jnp.bfloat16
I32 = jnp.int32

D_MODEL = 1024
BATCH = 16
SEQ = 256
DEPTH = 2
DEC_BATCH = 2
DEC_SEQ = 2048
PAST_LEN = 512
GRID_W = 64
HEAD_DIM = 64
NA_HEADS = 4
NA_WIN_ROWS = 8
NA_WIN_COLS = 16
GQA_HEADS = 4
GQA_KV_HEADS = 2
ROPE_THETA = 10000.0
SSM_HEADS = 8
SSM_HEAD_DIM = 64
SSM_STATE = 64
SSM_GROUPS = 2
SSM_INNER = SSM_HEADS * SSM_HEAD_DIM
SSM_BC_DIM = SSM_GROUPS * SSM_STATE
CONV_DIM = SSM_INNER + 2 * SSM_BC_DIM
CONV_W = 5
CHUNK = 128
NA_DIM = NA_HEADS * HEAD_DIM
GQA_Q_DIM = GQA_HEADS * HEAD_DIM
GQA_KV_DIM = GQA_KV_HEADS * HEAD_DIM
D_MIX = NA_DIM + GQA_Q_DIM + SSM_INNER
IN_DIM = 3 * NA_DIM + GQA_Q_DIM + 2 * GQA_KV_DIM + SSM_INNER + CONV_DIM + SSM_HEADS
N_EXPERTS = 32
TOP_K = 4
D_FF = D_MODEL
SWIGLU_LIMIT = 7.0
SWIGLU_ALPHA = 1.702
EPS = 1e-6
NEG_INF = -1e30

LANES = 128
N_CTX_TOK = BATCH * SEQ
N_LAT_TOK = DEC_BATCH * DEC_SEQ
N_TOK = N_CTX_TOK + N_LAT_TOK
N_COND = 1 + DEC_BATCH
COND_ROWS = 16
IN_PAD = 3 * NA_DIM + GQA_Q_DIM + 2 * GQA_KV_DIM + SSM_INNER + CONV_DIM + LANES
ROW_TILE = 256
N_ROW_TILES = N_TOK // ROW_TILE
MOE_BM = 256
N_SLOTS = N_TOK * TOP_K
RUN_ALIGN = 8
N_RUNS = N_ROW_TILES * N_EXPERTS
LOCAL_ROWS = 1280
MOE_NB = -(-(N_SLOTS + N_RUNS * (RUN_ALIGN - 1)) // MOE_BM) + N_EXPERTS
UP_GROUP = 256
N_SEQ = BATCH + DEC_BATCH
N_CHUNKS = N_TOK // CHUNK
N_CTX_CHUNKS = N_CTX_TOK // CHUNK
VMEM_LIMIT = 56 * 1024 * 1024


def _cparams(*sem):
    return pltpu.CompilerParams(dimension_semantics=sem, vmem_limit_bytes=VMEM_LIMIT)


def _sigmoid(x):
    return 1.0 / (1.0 + jnp.exp(-x))


def _dot(a, b):
    return jnp.dot(a, b, preferred_element_type=F32)


def _dot_nt(a, b):
    return lax.dot_general(a, b, (((1,), (1,)), ((), ())), preferred_element_type=F32)


def _dot_exact(a, b):
    return jnp.dot(a, b, preferred_element_type=F32, precision=lax.Precision.HIGHEST)


def _rms(x, g):
    return x * lax.rsqrt(jnp.mean(x * x, axis=-1, keepdims=True) + EPS) * g


def _cond_of_tile(i):
    ctx_tiles = N_CTX_TOK // ROW_TILE
    return jnp.where(i < ctx_tiles, 0, 1 + (i - ctx_tiles) // (DEC_SEQ // ROW_TILE))


def _adaln_kernel(c_ref, w_ref, b_ref, o_ref):
    c = c_ref[...]
    s = (c * _sigmoid(c)).astype(BF16)
    o_ref[0] = _dot(s, w_ref[0].astype(BF16)) + b_ref[0]


def _adaln(conds, w_ada, b_ada):
    tn = 1536
    return pl.pallas_call(
        _adaln_kernel,
        grid=(DEPTH, 6 * D_MODEL // tn),
        in_specs=[pl.BlockSpec((COND_ROWS, D_MODEL), lambda l, j: (0, 0)),
                  pl.BlockSpec((1, D_MODEL, tn), lambda l, j: (l, 0, j)),
                  pl.BlockSpec((1, 1, tn), lambda l, j: (l, 0, j))],
        out_specs=pl.BlockSpec((1, COND_ROWS, tn), lambda l, j: (l, 0, j)),
        out_shape=jax.ShapeDtypeStruct((DEPTH, COND_ROWS, 6 * D_MODEL), F32),
        compiler_params=_cparams("parallel", "parallel"),
        name="adaln",
    )(conds, w_ada, b_ada.reshape(DEPTH, 1, 6 * D_MODEL))


_IN_SPLITS = (3 * NA_DIM, GQA_Q_DIM + 2 * GQA_KV_DIM, SSM_INNER, CONV_DIM, LANES)


def _inproj_kernel(x_ref, m_ref, g_ref, w_ref, qkv_ref, gqa_ref, z_ref, xbc_ref, dt_ref):
    m = m_ref[0]
    h = _rms(x_ref[...], g_ref[...]) * (1.0 + m[1:2]) + m[0:1]
    p = _dot(h.astype(BF16), w_ref[...])
    off = 0
    for ref, width in zip((qkv_ref, gqa_ref, z_ref, xbc_ref, dt_ref), _IN_SPLITS):
        ref[...] = p[:, off:off + width]
        off += width


def _inproj(x, mods, gain, w_in):
    row = lambda w: pl.BlockSpec((ROW_TILE, w), lambda i: (i, 0))
    return pl.pallas_call(
        _inproj_kernel,
        grid=(N_ROW_TILES,),
        in_specs=[row(D_MODEL),
                  pl.BlockSpec((1, 6, D_MODEL), lambda i: (_cond_of_tile(i), 0, 0)),
                  pl.BlockSpec((1, D_MODEL), lambda i: (0, 0)),
                  pl.BlockSpec((D_MODEL, IN_PAD), lambda i: (0, 0))],
        out_specs=[row(w) for w in _IN_SPLITS],
        out_shape=[jax.ShapeDtypeStruct((N_TOK, w), F32) for w in _IN_SPLITS],
        compiler_params=_cparams("parallel"),
        name="inproj",
    )(x, mods, gain, w_in)


def _softmax_pv(scores, values):
    m = scores[0].max(axis=-1, keepdims=True)
    for s in scores[1:]:
        m = jnp.maximum(m, s.max(axis=-1, keepdims=True))
    den = 0.0
    acc = 0.0
    for s, v in zip(scores, values):
        e = jnp.exp(s - m)
        den = den + e.sum(axis=-1, keepdims=True)
        acc = acc + _dot(e.astype(BF16), v)
    return acc / den


def _heads_rms(x, n_heads, g):
    return jnp.concatenate(
        [_rms(x[:, h * HEAD_DIM:(h + 1) * HEAD_DIM], g) for h in range(n_heads)], axis=-1)


def _rope(x, cos, sin_signed):
    w = x.shape[-1]
    lane = lax.broadcasted_iota(I32, x.shape, 1)
    partner = jnp.where((lane & 1) == 0, pltpu.roll(x, w - 1, 1), pltpu.roll(x, 1, 1))
    return x * cos + partner * sin_signed


_ATT_SCALE = HEAD_DIM ** -0.5


def _ctx_attn_kernel(qkv_ref, gqa_ref, qn_ref, kn_ref, nao_ref, go_ref, gk_ref):
    outs = []
    for h in range(NA_HEADS):
        sl = slice(h * HEAD_DIM, (h + 1) * HEAD_DIM)
        q = qkv_ref[:, sl].astype(BF16)
        k = qkv_ref[:, NA_DIM + h * HEAD_DIM:NA_DIM + (h + 1) * HEAD_DIM].astype(BF16)
        v = qkv_ref[:, 2 * NA_DIM + h * HEAD_DIM:2 * NA_DIM + (h + 1) * HEAD_DIM].astype(BF16)
        outs.append(_softmax_pv([_dot_nt(q, k) * _ATT_SCALE], [v]))
    nao_ref[...] = jnp.concatenate(outs, axis=-1).astype(BF16)

    gq = _heads_rms(gqa_ref[:, 0:GQA_Q_DIM], GQA_HEADS, qn_ref[...])
    gk = _heads_rms(gqa_ref[:, GQA_Q_DIM:GQA_Q_DIM + GQA_KV_DIM], GQA_KV_HEADS, kn_ref[...])
    gk_ref[...] = gk
    rep = GQA_HEADS // GQA_KV_HEADS
    outs = []
    for h in range(GQA_HEADS):
        g = h // rep
        q = gq[:, h * HEAD_DIM:(h + 1) * HEAD_DIM].astype(BF16)
        k = gk[:, g * HEAD_DIM:(g + 1) * HEAD_DIM].astype(BF16)
        v0 = GQA_Q_DIM + GQA_KV_DIM + g * HEAD_DIM
        v = gqa_ref[:, v0:v0 + HEAD_DIM].astype(BF16)
        outs.append(_softmax_pv([_dot_nt(q, k) * _ATT_SCALE], [v]))
    go_ref[...] = jnp.concatenate(outs, axis=-1).astype(BF16)


def _ctx_attn(qkv, gqa, q_norm, k_norm):
    return pl.pallas_call(
        _ctx_attn_kernel,
        grid=(BATCH,),
        in_specs=[pl.BlockSpec((SEQ, 3 * NA_DIM), lambda b: (b, 0)),
                  pl.BlockSpec((SEQ, GQA_Q_DIM + 2 * GQA_KV_DIM), lambda b: (b, 0)),
                  pl.BlockSpec((1, HEAD_DIM), lambda b: (0, 0)),
                  pl.BlockSpec((1, HEAD_DIM), lambda b: (0, 0))],
        out_specs=[pl.BlockSpec((SEQ, NA_DIM), lambda b: (b, 0)),
                   pl.BlockSpec((SEQ, GQA_Q_DIM), lambda b: (b, 0)),
                   pl.BlockSpec((SEQ, GQA_KV_DIM), lambda b: (b, 0))],
        out_shape=[jax.ShapeDtypeStruct((N_CTX_TOK, NA_DIM), BF16),
                   jax.ShapeDtypeStruct((N_CTX_TOK, GQA_Q_DIM), BF16),
                   jax.ShapeDtypeStruct((N_CTX_TOK, GQA_KV_DIM), F32)],
        compiler_params=_cparams("parallel"),
        name="ctx_attn",
    )(qkv, gqa, q_norm, k_norm)


GQA_TQ = 256
GQA_KEYS = PAST_LEN + DEC_SEQ


def _lat_gqa_kernel(gqa_ref, ck_ref, cv_ref, cos_ref, sin_ref, qn_ref, kn_ref, o_ref, kbuf, vbuf):
    qb = pl.program_id(1)

    @pl.when(qb == 0)
    def _():
        kbuf[0:PAST_LEN, :] = ck_ref[0].astype(BF16)
        vbuf[0:PAST_LEN, :] = cv_ref[0].astype(BF16)
        k = _heads_rms(gqa_ref[:, GQA_Q_DIM:GQA_Q_DIM + GQA_KV_DIM], GQA_KV_HEADS, kn_ref[...])
        k = _rope(k, cos_ref[:, 0:GQA_KV_DIM], sin_ref[:, 0:GQA_KV_DIM])
        kbuf[PAST_LEN:GQA_KEYS, :] = k.astype(BF16)
        vbuf[PAST_LEN:GQA_KEYS, :] = gqa_ref[:, GQA_Q_DIM + GQA_KV_DIM:].astype(BF16)

    r0 = pl.multiple_of(qb * GQA_TQ, GQA_TQ)
    q = _heads_rms(gqa_ref[pl.ds(r0, GQA_TQ), 0:GQA_Q_DIM], GQA_HEADS, qn_ref[...])
    q = _rope(q, cos_ref[pl.ds(r0, GQA_TQ), :], sin_ref[pl.ds(r0, GQA_TQ), :]).astype(BF16)
    rep = GQA_HEADS // GQA_KV_HEADS
    outs = []
    for h in range(GQA_HEADS):
        g = h // rep
        k = kbuf[:, g * HEAD_DIM:(g + 1) * HEAD_DIM]
        v = vbuf[:, g * HEAD_DIM:(g + 1) * HEAD_DIM]
        s = _dot_nt(q[:, h * HEAD_DIM:(h + 1) * HEAD_DIM], k) * _ATT_SCALE
        outs.append(_softmax_pv([s], [v]))
    o_ref[...] = jnp.concatenate(outs, axis=-1).astype(BF16)


def _lat_gqa(gqa, cache_k, cache_v, cos, sin, q_norm, k_norm):
    lat_blk = N_CTX_TOK // DEC_SEQ
    return pl.pallas_call(
        _lat_gqa_kernel,
        grid=(DEC_BATCH, DEC_SEQ // GQA_TQ),
        in_specs=[pl.BlockSpec((DEC_SEQ, GQA_Q_DIM + 2 * GQA_KV_DIM), lambda b, q: (lat_blk + b, 0)),
                  pl.BlockSpec((1, PAST_LEN, GQA_KV_DIM), lambda b, q: (b, 0, 0)),
                  pl.BlockSpec((1, PAST_LEN, GQA_KV_DIM), lambda b, q: (b, 0, 0)),
                  pl.BlockSpec((DEC_SEQ, GQA_Q_DIM), lambda b, q: (0, 0)),
                  pl.BlockSpec((DEC_SEQ, GQA_Q_DIM), lambda b, q: (0, 0)),
                  pl.BlockSpec((1, HEAD_DIM), lambda b, q: (0, 0)),
                  pl.BlockSpec((1, HEAD_DIM), lambda b, q: (0, 0))],
        out_specs=pl.BlockSpec((GQA_TQ, GQA_Q_DIM), lambda b, q: (b * (DEC_SEQ // GQA_TQ) + q, 0)),
        out_shape=jax.ShapeDtypeStruct((N_LAT_TOK, GQA_Q_DIM), BF16),
        scratch_shapes=[pltpu.VMEM((GQA_KEYS, GQA_KV_DIM), BF16),
                        pltpu.VMEM((GQA_KEYS, GQA_KV_DIM), BF16)],
        compiler_params=_cparams("arbitrary", "arbitrary"),
        name="lat_gqa",
    )(gqa, cache_k, cache_v, cos, sin, q_norm, k_norm)


def _rope_tables():
    t = jnp.arange(DEC_SEQ)
    row = (t // GRID_W).astype(F32)
    col = (t % GRID_W).astype(F32)
    axis_dim = HEAD_DIM // 2
    inv_freq = ROPE_THETA ** (-jnp.arange(0, axis_dim, 2, dtype=F32) / axis_dim)
    ang = jnp.concatenate([row[:, None] * inv_freq, col[:, None] * inv_freq], axis=-1)
    cos = jnp.repeat(jnp.cos(ang), 2, axis=-1)
    sin = jnp.repeat(jnp.sin(ang), 2, axis=-1) * jnp.tile(jnp.array([-1.0, 1.0], F32), HEAD_DIM // 2)
    return jnp.tile(cos, (1, GQA_HEADS)), jnp.tile(sin, (1, GQA_HEADS))


NA_ROWS = DEC_SEQ // GRID_W
NA_KEYS = NA_WIN_ROWS * GRID_W


def _lat_na_kernel(qkv_ref, ck_ref, cv_ref, bias_ref, o_ref):
    r = pl.program_id(1)
    r0 = jnp.clip(r - NA_WIN_ROWS // 2, 0, NA_ROWS - NA_WIN_ROWS)
    q0 = pl.multiple_of(r * GRID_W, GRID_W)
    k0 = pl.multiple_of(r0 * GRID_W, GRID_W)
    outs = []
    for h in range(NA_HEADS):
        c0 = h * HEAD_DIM
        q = qkv_ref[pl.ds(q0, GRID_W), c0:c0 + HEAD_DIM].astype(BF16)
        k = qkv_ref[pl.ds(k0, NA_KEYS), NA_DIM + c0:NA_DIM + c0 + HEAD_DIM].astype(BF16)
        v = qkv_ref[pl.ds(k0, NA_KEYS), 2 * NA_DIM + c0:2 * NA_DIM + c0 + HEAD_DIM].astype(BF16)
        kc = ck_ref[0, :, c0:c0 + HEAD_DIM].astype(BF16)
        vc = cv_ref[0, :, c0:c0 + HEAD_DIM].astype(BF16)
        s_nb = _dot_nt(q, k) * _ATT_SCALE + bias_ref[0, h]
        s_ctx = _dot_nt(q, kc) * _ATT_SCALE
        outs.append(_softmax_pv([s_nb, s_ctx], [v, vc]))
    o_ref[...] = jnp.concatenate(outs, axis=-1).astype(BF16)


def _na_row_offset(r):
    return r - jnp.clip(r - NA_WIN_ROWS // 2, 0, NA_ROWS - NA_WIN_ROWS)


def _lat_na(qkv, cache_k, cache_v, bias):
    lat_blk = N_CTX_TOK // DEC_SEQ
    return pl.pallas_call(
        _lat_na_kernel,
        grid=(DEC_BATCH, NA_ROWS),
        in_specs=[pl.BlockSpec((DEC_SEQ, 3 * NA_DIM), lambda b, r: (lat_blk + b, 0)),
                  pl.BlockSpec((1, PAST_LEN, NA_DIM), lambda b, r: (b, 0, 0)),
                  pl.BlockSpec((1, PAST_LEN, NA_DIM), lambda b, r: (b, 0, 0)),
                  pl.BlockSpec((1, NA_HEADS, GRID_W, NA_KEYS), lambda b, r: (_na_row_offset(r), 0, 0, 0))],
        out_specs=pl.BlockSpec((GRID_W, NA_DIM), lambda b, r: (b * NA_ROWS + r, 0)),
        out_shape=jax.ShapeDtypeStruct((N_LAT_TOK, NA_DIM), BF16),
        compiler_params=_cparams("parallel", "arbitrary"),
        name="lat_na",
    )(qkv, cache_k, cache_v, bias)


def _na_bias_tables(rpb):
    d = np.arange(NA_WIN_ROWS)[:, None]
    kr = np.arange(NA_WIN_ROWS)[None, :]
    dr = kr - d + NA_WIN_ROWS - 1
    qc = np.arange(GRID_W)[:, None]
    kc = np.arange(GRID_W)[None, :]
    col0 = np.clip(qc - NA_WIN_COLS // 2, 0, GRID_W - NA_WIN_COLS)
    in_win = (kc >= col0) & (kc < col0 + NA_WIN_COLS)
    dc = np.clip(kc - qc + NA_WIN_COLS - 1, 0, 2 * NA_WIN_COLS - 2)
    row_hot = (dr[:, :, None] == np.arange(2 * NA_WIN_ROWS - 1)).astype(np.float32)
    col_hot = (dc[:, :, None] == np.arange(2 * NA_WIN_COLS - 1)).astype(np.float32)
    b = jnp.einsum('hac,dka,qxc->dhqkx', rpb.astype(F32), row_hot, col_hot, precision=lax.Precision.HIGHEST)
    b = jnp.where(in_win[None, None, :, None, :], b, NEG_INF)
    return b.reshape(NA_WIN_ROWS, NA_HEADS, GRID_W, NA_KEYS)


CONV_TB = 1024
CONV_HALO = 8
CONV_HALO_BLOCKS = CONV_TB // CONV_HALO


def _conv_kernel(prev_ref, x_ref, next_ref, w_ref, b_ref, o_ref):
    i = pl.program_id(0)
    seq = jnp.where(i < N_CTX_TOK // CONV_TB, SEQ, DEC_SEQ)
    x = x_ref[...]
    ext = jnp.concatenate([prev_ref[...], x, next_ref[...]], axis=0)
    n_ext = CONV_TB + 2 * CONV_HALO
    pos = (lax.broadcasted_iota(I32, (CONV_TB, 1), 0) + i * CONV_TB) & (seq - 1)
    half = CONV_W // 2
    acc = x * w_ref[half:half + 1, :]
    for s in range(-half, half + 1):
        if s == 0:
            continue
        shifted = pltpu.roll(ext, (-s) % n_ext, 0)[CONV_HALO:CONV_HALO + CONV_TB]
        valid = (pos + s >= 0) & (pos + s < seq)
        acc = acc + jnp.where(valid, shifted, 0.0) * w_ref[half + s:half + s + 1, :]
    acc = acc + b_ref[...]
    o_ref[...] = acc * _sigmoid(acc)


def _conv_act(xbc, conv_w, conv_b):
    return pl.pallas_call(
        _conv_kernel,
        grid=(N_TOK // CONV_TB,),
        in_specs=[pl.BlockSpec((CONV_HALO, CONV_DIM),
                               lambda i: (jnp.maximum(i * CONV_HALO_BLOCKS - 1, 0), 0)),
                  pl.BlockSpec((CONV_TB, CONV_DIM), lambda i: (i, 0)),
                  pl.BlockSpec((CONV_HALO, CONV_DIM),
                               lambda i: (jnp.minimum((i + 1) * CONV_HALO_BLOCKS, N_TOK // CONV_HALO - 1), 0)),
                  pl.BlockSpec((CONV_W, CONV_DIM), lambda i: (0, 0)),
                  pl.BlockSpec((1, CONV_DIM), lambda i: (0, 0))],
        out_specs=pl.BlockSpec((CONV_TB, CONV_DIM), lambda i: (i, 0)),
        out_shape=jax.ShapeDtypeStruct((N_TOK, CONV_DIM), F32),
        compiler_params=_cparams("parallel"),
        name="conv_act",
    )(xbc, xbc, xbc, conv_w, conv_b)


def _chunk_seq(g):
    ctx_n = SEQ // CHUNK
    lat_n = DEC_SEQ // CHUNK
    is_ctx = g < N_CTX_CHUNKS
    gl = g - N_CTX_CHUNKS
    sid = jnp.where(is_ctx, g // ctx_n, BATCH + gl // lat_n)
    cin = jnp.where(is_ctx, g % ctx_n, gl % lat_n)
    n = jnp.where(is_ctx, ctx_n, lat_n)
    return sid, cin, n


def _ssd_init(h0_ref, st_ref, gg, reverse):
    _, cin, n = _chunk_seq(gg)

    @pl.when(cin == (n - 1 if reverse else 0))
    def _():
        st_ref[0] = h0_ref[0]


def _ssd_chunk(xa_ref, dt_ref, dtb, alog, y_ref, st_ref, reverse):
    x = dt_ref[...] + dtb
    dt = jnp.maximum(x, 0.0) + jnp.log1p(jnp.exp(-jnp.abs(x)))
    dta = dt * -jnp.exp(alog)
    ii = lax.broadcasted_iota(I32, (CHUNK, CHUNK), 0)
    jj = lax.broadcasted_iota(I32, (CHUNK, CHUNK), 1)
    tri = (jj >= ii) if reverse else (jj <= ii)
    cum = _dot_exact(tri.astype(F32), dta)
    cum_t = cum.T
    edge = 0 if reverse else CHUNK - 1
    tot = cum[edge:edge + 1, :]
    rep = SSM_HEADS // SSM_GROUPS
    for grp in range(SSM_GROUPS):
        bg = xa_ref[:, SSM_INNER + grp * SSM_STATE:SSM_INNER + (grp + 1) * SSM_STATE].astype(BF16)
        c0 = SSM_INNER + SSM_BC_DIM + grp * SSM_STATE
        cg = xa_ref[:, c0:c0 + SSM_STATE].astype(BF16)
        cb = _dot_nt(cg, bg)
        for h in range(grp * rep, (grp + 1) * rep):
            hs = slice(h * SSM_HEAD_DIM, (h + 1) * SSM_HEAD_DIM)
            col = cum[:, h:h + 1]
            row = cum_t[h:h + 1, :]
            decay = jnp.where(tri, jnp.exp(jnp.minimum(col - row, 0.0)), 0.0)
            xdt = xa_ref[:, hs] * dt[:, h:h + 1]
            state = st_ref[0, hs, :]
            y = _dot((cb * decay).astype(BF16), xdt.astype(BF16))
            y = y + _dot_nt(cg, state.astype(BF16)) * jnp.exp(col)
            y_ref[:, hs] = y
            toth = tot[:, h:h + 1]
            w = (xdt * jnp.exp(toth - col)).astype(BF16)
            upd = lax.dot_general(w, bg, (((0,), (0,)), ((), ())), preferred_element_type=F32)
            st_ref[0, hs, :] = state * jnp.exp(toth) + upd


def _ssd_kernel(xaf_ref, dtf_ref, h0f_ref, xab_ref, dtb_ref, h0b_ref, bias_ref, alog_ref,
                yf_ref, stf_ref, yb_ref, stb_ref):
    g = pl.program_id(0)
    _ssd_init(h0f_ref, stf_ref, g, False)
    _ssd_chunk(xaf_ref, dtf_ref, bias_ref[0:1, :], alog_ref[0:1, :], yf_ref, stf_ref, False)
    _ssd_init(h0b_ref, stb_ref, N_CHUNKS - 1 - g, True)
    _ssd_chunk(xab_ref, dtb_ref, bias_ref[1:2, :], alog_ref[1:2, :], yb_ref, stb_ref, True)


def _ssd(xa, dt_raw, h0_fwd, h0_bwd, dt_bias, a_log):
    rev = lambda g: N_CHUNKS - 1 - g
    chunk = lambda w, order: pl.BlockSpec((CHUNK, w), lambda g: (order(g), 0))
    state = lambda order: pl.BlockSpec((1, SSM_INNER, SSM_STATE), lambda g: (_chunk_seq(order(g))[0], 0, 0))
    same = lambda g: g
    y_shape = jax.ShapeDtypeStruct((N_TOK, SSM_INNER), F32)
    st_shape = jax.ShapeDtypeStruct((N_SEQ, SSM_INNER, SSM_STATE), F32)
    return pl.pallas_call(
        _ssd_kernel,
        grid=(N_CHUNKS,),
        in_specs=[chunk(CONV_DIM, same), chunk(LANES, same), state(same),
                  chunk(CONV_DIM, rev), chunk(LANES, rev), state(rev),
                  pl.BlockSpec((2, LANES), lambda g: (0, 0)),
                  pl.BlockSpec((2, LANES), lambda g: (0, 0))],
        out_specs=[chunk(SSM_INNER, same), state(same), chunk(SSM_INNER, rev), state(rev)],
        out_shape=[y_shape, st_shape, y_shape, st_shape],
        compiler_params=_cparams("arbitrary"),
        name="ssd",
    )(xa, dt_raw, h0_fwd, xa, dt_raw, h0_bwd, dt_bias, a_log)


def _outproj_kernel(x_ref, nao_ref, go_ref, yf_ref, yb_ref, xs_ref, z_ref, dsk_ref, sn_ref, wo_ref,
                    m_ref, g2_ref, wr_ref, br_ref, xo_ref, h_ref, idx_ref, gate_ref, sel_ref):
    m = m_ref[0]
    z = z_ref[...]
    y = (yf_ref[...] + yb_ref[...] + xs_ref[...] * dsk_ref[...]) * (z * _sigmoid(z))
    s_o = _rms(y, sn_ref[...]).astype(BF16)
    mix = (_dot(nao_ref[...], wo_ref[0:NA_DIM, :])
           + _dot(go_ref[...], wo_ref[NA_DIM:NA_DIM + GQA_Q_DIM, :])
           + _dot(s_o, wo_ref[NA_DIM + GQA_Q_DIM:, :]))
    x = x_ref[...] + m[2:3] * mix
    xo_ref[...] = x
    h = _rms(x, g2_ref[...]) * (1.0 + m[4:5]) + m[3:4]
    h_ref[...] = h

    logits = _dot_exact(h, wr_ref[...]) + br_ref[...]
    lane = lax.broadcasted_iota(I32, logits.shape, 1).astype(F32)
    vals, idxs = [], []
    for _ in range(TOP_K):
        v = logits.max(axis=-1, keepdims=True)
        i = jnp.where(logits == v, lane, float(LANES)).min(axis=-1, keepdims=True)
        vals.append(v)
        idxs.append(i)
        logits = jnp.where(lane == i, -jnp.inf, logits)
    es = [jnp.exp(v - vals[0]) for v in vals]
    den = es[0] + es[1] + es[2] + es[3]
    idx_out = jnp.zeros(lane.shape, F32)
    gate_out = jnp.zeros(lane.shape, F32)
    sel = jnp.zeros(lane.shape, F32)
    for k in range(TOP_K):
        idx_out = jnp.where(lane == float(k), idxs[k], idx_out)
        gate_out = jnp.where(lane == float(k), es[k] / den, gate_out)
        sel = jnp.where(lane == idxs[k], 1.0, sel)
    idx_ref[...] = idx_out.astype(I32)
    gate_ref[...] = gate_out
    sel_ref[...] = sel.astype(BF16)


def _outproj(x, nao, go, yf, yb, xa, z, d_skip, ssm_norm, w_out, mods, gain2, w_router, b_router):
    row = lambda w: pl.BlockSpec((ROW_TILE, w), lambda i: (i, 0))
    full = lambda a, b: pl.BlockSpec((a, b), lambda i: (0, 0))
    return pl.pallas_call(
        _outproj_kernel,
        grid=(N_ROW_TILES,),
        in_specs=[row(D_MODEL), row(NA_DIM), row(GQA_Q_DIM), row(SSM_INNER), row(SSM_INNER),
                  row(SSM_INNER), row(SSM_INNER), full(1, SSM_INNER), full(1, SSM_INNER),
                  full(D_MIX, D_MODEL),
                  pl.BlockSpec((1, 6, D_MODEL), lambda i: (_cond_of_tile(i), 0, 0)),
                  full(1, D_MODEL), full(D_MODEL, LANES), full(1, LANES)],
        out_specs=[row(D_MODEL), row(D_MODEL), row(LANES), row(LANES), row(LANES)],
        out_shape=[jax.ShapeDtypeStruct((N_TOK, D_MODEL), F32),
                   jax.ShapeDtypeStruct((N_TOK, D_MODEL), F32),
                   jax.ShapeDtypeStruct((N_TOK, LANES), I32),
                   jax.ShapeDtypeStruct((N_TOK, LANES), F32),
                   jax.ShapeDtypeStruct((N_TOK, LANES), BF16)],
        compiler_params=_cparams("parallel"),
        name="outproj_router",
    )(x, nao, go, yf, yb, xa, z, d_skip, ssm_norm, w_out, mods, gain2, w_router, b_router)


RANK_TB = 512


def _rank_kernel(sel_ref, rank_ref, cnt_ref, carry):
    @pl.when(pl.program_id(0) == 0)
    def _():
        carry[...] = jnp.zeros_like(carry)

    sel = sel_ref[...]
    ii = lax.broadcasted_iota(I32, (RANK_TB, RANK_TB), 0)
    jj = lax.broadcasted_iota(I32, (RANK_TB, RANK_TB), 1)
    before = (jj < ii).astype(BF16)
    rank_ref[...] = _dot(before, sel) + carry[0:1, :]
    carry[...] = carry[...] + _dot(jnp.ones((8, RANK_TB), BF16), sel)
    cnt_ref[...] = carry[...]


def _ranks(sel):
    return pl.pallas_call(
        _rank_kernel,
        grid=(N_TOK // RANK_TB,),
        in_specs=[pl.BlockSpec((RANK_TB, LANES), lambda i: (i, 0))],
        out_specs=[pl.BlockSpec((RANK_TB, LANES), lambda i: (i, 0)),
                   pl.BlockSpec((8, LANES), lambda i: (0, 0))],
        out_shape=[jax.ShapeDtypeStruct((N_TOK, LANES), F32),
                   jax.ShapeDtypeStruct((8, LANES), F32)],
        scratch_shapes=[pltpu.VMEM((8, LANES), F32)],
        compiler_params=_cparams("arbitrary"),
        name="moe_ranks",
    )(sel)


_RUN_PIECES = (256, 128, 64, 32, 16, 8)


def _run_dma(src, s0, dst, d0, n, sem, *, wait, fixed_src=False):
    for size in _RUN_PIECES:
        @pl.when((n & size) != 0)
        def _(size=size):
            done = n & ~(2 * size - 1)
            s = 0 if fixed_src else pl.multiple_of(s0 + done, RUN_ALIGN)
            d = pl.multiple_of(d0 + done, RUN_ALIGN)
            copy = pltpu.make_async_copy(src.at[pl.ds(s, size), :], dst.at[pl.ds(d, size), :], sem)
            if wait:
                copy.wait()
            else:
                copy.start()


def _local_rows(idx_ref, rank_ref, base_ref):
    pos = rank_ref[...] + base_ref[0]
    lane = lax.broadcasted_iota(I32, pos.shape, 1)
    idx = idx_ref[...]
    return [jnp.sum(jnp.where(lane == idx[:, k:k + 1], pos, 0.0), axis=-1, keepdims=True) for k in range(TOP_K)]


def _dispatch_kernel(run_ref, loc_ref, glb_ref, fs_ref, fl_ref, nu_ref,
                     h_ref, idx_ref, rank_ref, base_ref, out_ref, xl, zbuf, sem):
    i = pl.program_id(0)

    @pl.when(i == 0)
    def _():
        zbuf[...] = jnp.zeros_like(zbuf)
        for e in range(N_EXPERTS):
            _run_dma(zbuf, 0, out_ref, fs_ref[e], fl_ref[e], sem, wait=False, fixed_src=True)
        for e in range(N_EXPERTS):
            _run_dma(zbuf, 0, out_ref, fs_ref[e], fl_ref[e], sem, wait=True, fixed_src=True)

        def tail(b, carry):
            copy = pltpu.make_async_copy(zbuf, out_ref.at[pl.ds(pl.multiple_of(b * MOE_BM, MOE_BM), MOE_BM), :], sem)
            copy.start()
            copy.wait()
            return carry

        lax.fori_loop(nu_ref[0], MOE_NB, tail, 0)

    rows = _local_rows(idx_ref, rank_ref, base_ref)
    p = lax.broadcasted_iota(I32, (ROW_TILE, LOCAL_ROWS), 1).astype(F32)
    hot = (p == rows[0])
    for k in range(1, TOP_K):
        hot = hot | (p == rows[k])
    xl[...] = lax.dot_general(hot.astype(BF16), h_ref[...].astype(BF16), (((0,), (0,)), ((), ())),
                              preferred_element_type=F32)
    for wait in (False, True):
        for e in range(N_EXPERTS):
            j = i * N_EXPERTS + e
            _run_dma(xl, loc_ref[j], out_ref, glb_ref[j], run_ref[j], sem, wait=wait)


def _dispatch(plan, h, idx, rank):
    tile = lambda w: pl.BlockSpec((ROW_TILE, w), lambda i, *_: (i, 0))
    return pl.pallas_call(
        _dispatch_kernel,
        grid_spec=pltpu.PrefetchScalarGridSpec(
            num_scalar_prefetch=6,
            grid=(N_ROW_TILES,),
            in_specs=[tile(D_MODEL), tile(LANES), tile(LANES),
                      pl.BlockSpec((1, 1, LANES), lambda i, *_: (i, 0, 0))],
            out_specs=pl.BlockSpec(memory_space=pl.ANY),
            scratch_shapes=[pltpu.VMEM((LOCAL_ROWS, D_MODEL), F32),
                            pltpu.VMEM((MOE_BM, D_MODEL), F32),
                            pltpu.SemaphoreType.DMA(())]),
        out_shape=jax.ShapeDtypeStruct((MOE_NB * MOE_BM, D_MODEL), F32),
        compiler_params=_cparams("arbitrary"),
        name="moe_dispatch",
    )(plan["run"], plan["local"], plan["global"], plan["fill_start"], plan["fill_len"], plan["n_used"],
      h, idx, rank, plan["base"])


def _expert_kernel(be_ref, nu_ref, new_ref, x_ref, wu_ref, bu_ref, wd_ref, bd_ref, y_ref, wu_s, wd_s):
    b = pl.program_id(0)
    used = b < nu_ref[0]

    @pl.when(jnp.logical_not(used))
    def _():
        y_ref[...] = jnp.zeros_like(y_ref)

    @pl.when(jnp.logical_and(used, new_ref[b] == 1))
    def _():
        r = lax.broadcasted_iota(I32, (UP_GROUP, UP_GROUP), 0)
        c = lax.broadcasted_iota(I32, (UP_GROUP, UP_GROUP), 1)
        src = jnp.where(c < UP_GROUP // 2, 2 * c, 2 * (c - UP_GROUP // 2) + 1)
        perm = (r == src).astype(BF16)
        for g in range(2 * D_FF // UP_GROUP):
            cols = slice(g * UP_GROUP, (g + 1) * UP_GROUP)
            wu_s[:, cols] = _dot(wu_ref[0, 0, :, cols].astype(BF16), perm).astype(BF16)
        wd_s[...] = wd_ref[0, 0].astype(BF16)

    @pl.when(used)
    def _():
        up = _dot(x_ref[...].astype(BF16), wu_s[...]) + bu_ref[0]
        half = UP_GROUP // 2
        acts = []
        for g in range(2 * D_FF // UP_GROUP):
            gate = jnp.minimum(up[:, g * UP_GROUP:g * UP_GROUP + half], SWIGLU_LIMIT)
            lin = jnp.clip(up[:, g * UP_GROUP + half:(g + 1) * UP_GROUP], -SWIGLU_LIMIT, SWIGLU_LIMIT)
            acts.append((gate * _sigmoid(SWIGLU_ALPHA * gate) * (lin + 1.0)).astype(BF16))
        y_ref[...] = _dot(jnp.concatenate(acts, axis=-1), wd_s[...]) + bd_ref[0]


def _experts(layer, plan, xs, w_up, b_up, w_down, b_down):
    blk = lambda b, nu: jnp.maximum(jnp.minimum(b, nu[0] - 1), 0)
    return pl.pallas_call(
        _expert_kernel,
        grid_spec=pltpu.PrefetchScalarGridSpec(
            num_scalar_prefetch=3,
            grid=(MOE_NB,),
            in_specs=[pl.BlockSpec((MOE_BM, D_MODEL), lambda b, be, nu, nw: (blk(b, nu), 0)),
                      pl.BlockSpec((1, 1, D_MODEL, 2 * D_FF), lambda b, be, nu, nw: (layer, be[blk(b, nu)], 0, 0)),
                      pl.BlockSpec((1, 1, 2 * D_FF), lambda b, be, nu, nw: (be[blk(b, nu)], 0, 0)),
                      pl.BlockSpec((1, 1, D_FF, D_MODEL), lambda b, be, nu, nw: (layer, be[blk(b, nu)], 0, 0)),
                      pl.BlockSpec((1, 1, D_MODEL), lambda b, be, nu, nw: (be[blk(b, nu)], 0, 0))],
            out_specs=pl.BlockSpec((MOE_BM, D_MODEL), lambda b, be, nu, nw: (b, 0)),
            scratch_shapes=[pltpu.VMEM((D_MODEL, 2 * D_FF), BF16),
                            pltpu.VMEM((D_FF, D_MODEL), BF16)]),
        out_shape=jax.ShapeDtypeStruct((MOE_NB * MOE_BM, D_MODEL), F32),
        compiler_params=_cparams("arbitrary"),
        name="moe_experts",
    )(plan["blk_expert"], plan["n_used"], plan["blk_new"], xs, w_up, b_up, w_down, b_down)


def _combine_kernel(run_ref, loc_ref, glb_ref, x_ref, gate_ref, idx_ref, rank_ref, base_ref, m_ref, fn_ref,
                    ys_ref, o_ref, yl, sem, *, final):
    i = pl.program_id(0)
    for e in range(N_EXPERTS):
        j = i * N_EXPERTS + e
        _run_dma(ys_ref, glb_ref[j], yl, loc_ref[j], run_ref[j], sem, wait=False)

    last = i * N_EXPERTS + N_EXPERTS - 1

    def clear(r, carry):
        yl[pl.ds(pl.multiple_of(r * RUN_ALIGN, RUN_ALIGN), RUN_ALIGN), :] = jnp.zeros((RUN_ALIGN, D_MODEL), F32)
        return carry

    lax.fori_loop((loc_ref[last] + run_ref[last]) // RUN_ALIGN, LOCAL_ROWS // RUN_ALIGN, clear, 0)

    rows = _local_rows(idx_ref, rank_ref, base_ref)
    gate = gate_ref[...]
    p = lax.broadcasted_iota(I32, (ROW_TILE, LOCAL_ROWS), 1).astype(F32)
    w = jnp.zeros((ROW_TILE, LOCAL_ROWS), F32)
    for k in range(TOP_K):
        w = jnp.where(p == rows[k], gate[:, k:k + 1], w)

    for e in range(N_EXPERTS):
        j = i * N_EXPERTS + e
        _run_dma(ys_ref, glb_ref[j], yl, loc_ref[j], run_ref[j], sem, wait=True)
    acc = _dot(w.astype(BF16), yl[...].astype(BF16))
    x = x_ref[...] + m_ref[0][5:6] * acc
    o_ref[...] = _rms(x, fn_ref[...]) if final else x


def _combine(plan, x, gates, idx, rank, mods, final_norm, ys, final):
    tile = lambda w: pl.BlockSpec((ROW_TILE, w), lambda i, *_: (i, 0))
    return pl.pallas_call(
        functools.partial(_combine_kernel, final=final),
        grid_spec=pltpu.PrefetchScalarGridSpec(
            num_scalar_prefetch=3,
            grid=(N_ROW_TILES,),
            in_specs=[tile(D_MODEL), tile(LANES), tile(LANES), tile(LANES),
                      pl.BlockSpec((1, 1, LANES), lambda i, *_: (i, 0, 0)),
                      pl.BlockSpec((1, 6, D_MODEL), lambda i, *_: (_cond_of_tile(i), 0, 0)),
                      pl.BlockSpec((1, D_MODEL), lambda i, *_: (0, 0)),
                      pl.BlockSpec(memory_space=pl.ANY)],
            out_specs=tile(D_MODEL),
            scratch_shapes=[pltpu.VMEM((LOCAL_ROWS, D_MODEL), F32),
                            pltpu.SemaphoreType.DMA(())]),
        out_shape=jax.ShapeDtypeStruct((N_TOK, D_MODEL), F32),
        compiler_params=_cparams("arbitrary"),
        name="moe_combine",
    )(plan["run"], plan["local"], plan["global"], x, gates, idx, rank, plan["base"], mods, final_norm, ys)


def _moe_plan(rank, cnt):
    first = rank[::ROW_TILE, :N_EXPERTS].astype(I32)
    total = cnt[0:1, :N_EXPERTS].astype(I32)
    run = jnp.concatenate([first[1:], total], axis=0) - first
    run = (run + RUN_ALIGN - 1) // RUN_ALIGN * RUN_ALIGN
    local = jnp.cumsum(run, axis=1) - run
    sizes = run.sum(axis=0)
    padded = (sizes + MOE_BM - 1) // MOE_BM * MOE_BM
    ends = jnp.cumsum(padded)
    starts = ends - padded
    glob = starts[None, :] + jnp.cumsum(run, axis=0) - run
    blk_start = jnp.arange(MOE_NB, dtype=I32) * MOE_BM
    blk_expert = jnp.minimum(jnp.sum(ends[None, :] <= blk_start[:, None], axis=1), N_EXPERTS - 1).astype(I32)
    blk_new = jnp.concatenate([jnp.ones((1,), I32), (blk_expert[1:] != blk_expert[:-1]).astype(I32)])
    base = _pad_lanes((local - first).astype(F32)).reshape(N_ROW_TILES, 1, LANES)
    return {"run": run.reshape(-1), "local": local.reshape(-1), "global": glob.reshape(-1),
            "fill_start": starts + sizes, "fill_len": padded - sizes, "n_used": ends[-1:] // MOE_BM,
            "blk_expert": blk_expert, "blk_new": blk_new, "base": base}


def _pad_lanes(v, fill=0.0):
    return jnp.pad(v, ((0, 0), (0, LANES - v.shape[-1])), constant_values=fill)


def kernel(x_prompt, x_sample, cache_na_k, cache_na_v, cache_gqa_k, cache_gqa_v, state_ssm, c, c_ctx, w_ada, b_ada, norm_mix, norm_ffn, w_in, na_rpb, gqa_q_norm, gqa_k_norm, ssm_conv_w, ssm_conv_b, ssm_dt_bias, ssm_a_log, ssm_d, ssm_norm, w_out, w_router, b_router, w_up, b_up, w_down, b_down, final_norm):
    x = jnp.concatenate([x_prompt.reshape(N_CTX_TOK, D_MODEL), x_sample.reshape(N_LAT_TOK, D_MODEL)], axis=0)
    conds = jnp.concatenate([c_ctx[None], c, jnp.zeros((COND_ROWS - N_COND, D_MODEL), F32)], axis=0)
    mods = _adaln(conds, w_ada, b_ada).reshape(DEPTH, COND_ROWS, 6, D_MODEL)
    cos, sin = _rope_tables()

    w_in_b = jnp.pad(w_in, ((0, 0), (0, 0), (0, IN_PAD - IN_DIM))).astype(BF16)
    w_out_b = w_out.astype(BF16)
    b_up_s = b_up.reshape(DEPTH, N_EXPERTS, 2 * D_FF // UP_GROUP, UP_GROUP // 2, 2)
    b_up_s = jnp.swapaxes(b_up_s, -1, -2).reshape(DEPTH, N_EXPERTS, 1, 2 * D_FF)
    b_down_s = b_down.reshape(DEPTH, N_EXPERTS, 1, D_MODEL)

    ctx_out = []
    for l in range(DEPTH):
        qkv, gqa, z, xbc, dt_raw = _inproj(x, mods[l], norm_mix[l][None], w_in_b[l])
        qn, kn = gqa_q_norm[l][None], gqa_k_norm[l][None]

        nao_c, go_c, gk_c = _ctx_attn(qkv, gqa, qn, kn)
        go_l = _lat_gqa(gqa, cache_gqa_k[:, l].reshape(DEC_BATCH, PAST_LEN, GQA_KV_DIM),
                        cache_gqa_v[:, l].reshape(DEC_BATCH, PAST_LEN, GQA_KV_DIM), cos, sin, qn, kn)
        nao_l = _lat_na(qkv, cache_na_k[:, l].reshape(DEC_BATCH, PAST_LEN, NA_DIM),
                        cache_na_v[:, l].reshape(DEC_BATCH, PAST_LEN, NA_DIM), _na_bias_tables(na_rpb[l]))

        xa = _conv_act(xbc, ssm_conv_w[l], ssm_conv_b[l][None])
        zeros = jnp.zeros((BATCH, SSM_INNER, SSM_STATE), F32)
        h0 = [jnp.concatenate([zeros, state_ssm[:, l, d].reshape(DEC_BATCH, SSM_INNER, SSM_STATE)], axis=0)
              for d in range(2)]
        y_f, st_f, y_b, st_b = _ssd(xa, dt_raw, h0[0], h0[1], _pad_lanes(ssm_dt_bias[l]), _pad_lanes(ssm_a_log[l]))
        ys = [y_f, y_b]
        sts = [st[:BATCH].reshape(BATCH, SSM_HEADS, SSM_HEAD_DIM, SSM_STATE) for st in (st_f, st_b)]

        x, h, top_idx, gates, sel = _outproj(
            x, jnp.concatenate([nao_c, nao_l], axis=0), jnp.concatenate([go_c, go_l], axis=0),
            ys[0], ys[1], xa, z, jnp.repeat(ssm_d[l], SSM_HEAD_DIM)[None], ssm_norm[l][None], w_out_b[l],
            mods[l], norm_ffn[l][None], _pad_lanes(w_router[l]), _pad_lanes(b_router[l][None], NEG_INF))

        rank, cnt = _ranks(sel)
        plan = _moe_plan(rank, cnt)
        y_sorted = _experts(l, plan, _dispatch(plan, h, top_idx, rank), w_up, b_up_s[l], w_down, b_down_s[l])
        x = _combine(plan, x, gates, top_idx, rank, mods[l], final_norm[None], y_sorted, final=(l == DEPTH - 1))

        ctx_out.append((
            qkv[:N_CTX_TOK, NA_DIM:2 * NA_DIM].reshape(BATCH, SEQ, NA_HEADS, HEAD_DIM),
            qkv[:N_CTX_TOK, 2 * NA_DIM:].reshape(BATCH, SEQ, NA_HEADS, HEAD_DIM),
            gk_c.reshape(BATCH, SEQ, GQA_KV_HEADS, HEAD_DIM),
            gqa[:N_CTX_TOK, GQA_Q_DIM + GQA_KV_DIM:].reshape(BATCH, SEQ, GQA_KV_HEADS, HEAD_DIM),
            jnp.stack(sts, axis=1)))

    y_prompt = x[:N_CTX_TOK].reshape(BATCH, SEQ, D_MODEL)
    y_sample = x[N_CTX_TOK:].reshape(DEC_BATCH, DEC_SEQ, D_MODEL)
    return (y_prompt, y_sample) + tuple(jnp.stack([e[i] for e in ctx_out], axis=1) for i in range(5))
```

```python
import functools

import numpy as np
import jax
import jax.numpy as jnp
from jax import lax
from jax.experimental import pallas as pl
from jax.experimental.pallas import tpu as pltpu

F32 = jnp.float32
BF16 = jnp.bfloat16
I32 = jnp.int32

D_MODEL = 1024
BATCH = 16
SEQ = 256
DEPTH = 2
DEC_BATCH = 2
DEC_SEQ = 2048
PAST_LEN = 512
GRID_W = 64
HEAD_DIM = 64
NA_HEADS = 4
NA_WIN_ROWS = 8
NA_WIN_COLS = 16
GQA_HEADS = 4
GQA_KV_HEADS = 2
ROPE_THETA = 10000.0
SSM_HEADS = 8
SSM_HEAD_DIM = 64
SSM_STATE = 64
SSM_GROUPS = 2
SSM_INNER = SSM_HEADS * SSM_HEAD_DIM
SSM_BC_DIM = SSM_GROUPS * SSM_STATE
CONV_DIM = SSM_INNER + 2 * SSM_BC_DIM
CONV_W = 5
CHUNK = 128
NA_DIM = NA_HEADS * HEAD_DIM
GQA_Q_DIM = GQA_HEADS * HEAD_DIM
GQA_KV_DIM = GQA_KV_HEADS * HEAD_DIM
D_MIX = NA_DIM + GQA_Q_DIM + SSM_INNER
IN_DIM = 3 * NA_DIM + GQA_Q_DIM + 2 * GQA_KV_DIM + SSM_INNER + CONV_DIM + SSM_HEADS
N_EXPERTS = 32
TOP_K = 4
D_FF = D_MODEL
SWIGLU_LIMIT = 7.0
SWIGLU_ALPHA = 1.702
EPS = 1e-6
NEG_INF = -1e30

LANES = 128
N_CTX_TOK = BATCH * SEQ
N_LAT_TOK = DEC_BATCH * DEC_SEQ
N_TOK = N_CTX_TOK + N_LAT_TOK
N_COND = 1 + DEC_BATCH
COND_ROWS = 16
IN_PAD = 3 * NA_DIM + GQA_Q_DIM + 2 * GQA_KV_DIM + SSM_INNER + CONV_DIM + LANES
ROW_TILE = 256
N_ROW_TILES = N_TOK // ROW_TILE
MOE_BM = 256
N_SLOTS = N_TOK * TOP_K
RUN_ALIGN = 8
N_RUNS = N_ROW_TILES * N_EXPERTS
LOCAL_ROWS = 1280
MOE_NB = -(-(N_SLOTS + N_RUNS * (RUN_ALIGN - 1)) // MOE_BM) + N_EXPERTS
UP_GROUP = 256
N_SEQ = BATCH + DEC_BATCH
N_CHUNKS = N_TOK // CHUNK
N_CTX_CHUNKS = N_CTX_TOK // CHUNK
VMEM_LIMIT = 56 * 1024 * 1024


def _cparams(*sem):
    return pltpu.CompilerParams(dimension_semantics=sem, vmem_limit_bytes=VMEM_LIMIT)


def _sigmoid(x):
    return 1.0 / (1.0 + jnp.exp(-x))


def _dot(a, b):
    return jnp.dot(a, b, preferred_element_type=F32)


def _dot_nt(a, b):
    return lax.dot_general(a, b, (((1,), (1,)), ((), ())), preferred_element_type=F32)


def _dot_exact(a, b):
    return jnp.dot(a, b, preferred_element_type=F32, precision=lax.Precision.HIGHEST)


def _rms(x, g):
    return x * lax.rsqrt(jnp.mean(x * x, axis=-1, keepdims=True) + EPS) * g


def _cond_of_tile(i):
    ctx_tiles = N_CTX_TOK // ROW_TILE
    return jnp.where(i < ctx_tiles, 0, 1 + (i - ctx_tiles) // (DEC_SEQ // ROW_TILE))


def _adaln_kernel(c_ref, w_ref, b_ref, o_ref):
    c = c_ref[...]
    s = (c * _sigmoid(c)).astype(BF16)
    o_ref[0] = _dot(s, w_ref[0].astype(BF16)) + b_ref[0]


def _adaln(conds, w_ada, b_ada):
    tn = 1536
    return pl.pallas_call(
        _adaln_kernel,
        grid=(DEPTH, 6 * D_MODEL // tn),
        in_specs=[pl.BlockSpec((COND_ROWS, D_MODEL), lambda l, j: (0, 0)),
                  pl.BlockSpec((1, D_MODEL, tn), lambda l, j: (l, 0, j)),
                  pl.BlockSpec((1, 1, tn), lambda l, j: (l, 0, j))],
        out_specs=pl.BlockSpec((1, COND_ROWS, tn), lambda l, j: (l, 0, j)),
        out_shape=jax.ShapeDtypeStruct((DEPTH, COND_ROWS, 6 * D_MODEL), F32),
        compiler_params=_cparams("parallel", "parallel"),
        name="adaln",
    )(conds, w_ada, b_ada.reshape(DEPTH, 1, 6 * D_MODEL))


_IN_SPLITS = (3 * NA_DIM, GQA_Q_DIM + 2 * GQA_KV_DIM, SSM_INNER, CONV_DIM, LANES)


def _inproj_kernel(x_ref, m_ref, g_ref, w_ref, qkv_ref, gqa_ref, z_ref, xbc_ref, dt_ref):
    m = m_ref[0]
    h = _rms(x_ref[...], g_ref[...]) * (1.0 + m[1:2]) + m[0:1]
    p = _dot(h.astype(BF16), w_ref[...])
    off = 0
    for ref, width in zip((qkv_ref, gqa_ref, z_ref, xbc_ref, dt_ref), _IN_SPLITS):
        ref[...] = p[:, off:off + width]
        off += width


def _inproj(x, mods, gain, w_in):
    row = lambda w: pl.BlockSpec((ROW_TILE, w), lambda i: (i, 0))
    return pl.pallas_call(
        _inproj_kernel,
        grid=(N_ROW_TILES,),
        in_specs=[row(D_MODEL),
                  pl.BlockSpec((1, 6, D_MODEL), lambda i: (_cond_of_tile(i), 0, 0)),
                  pl.BlockSpec((1, D_MODEL), lambda i: (0, 0)),
                  pl.BlockSpec((D_MODEL, IN_PAD), lambda i: (0, 0))],
        out_specs=[row(w) for w in _IN_SPLITS],
        out_shape=[jax.ShapeDtypeStruct((N_TOK, w), F32) for w in _IN_SPLITS],
        compiler_params=_cparams("parallel"),
        name="inproj",
    )(x, mods, gain, w_in)


def _softmax_pv(scores, values):
    m = scores[0].max(axis=-1, keepdims=True)
    for s in scores[1:]:
        m = jnp.maximum(m, s.max(axis=-1, keepdims=True))
    den = 0.0
    acc = 0.0
    for s, v in zip(scores, values):
        e = jnp.exp(s - m)
        den = den + e.sum(axis=-1, keepdims=True)
        acc = acc + _dot(e.astype(BF16), v)
    return acc / den


def _heads_rms(x, n_heads, g):
    return jnp.concatenate(
        [_rms(x[:, h * HEAD_DIM:(h + 1) * HEAD_DIM], g) for h in range(n_heads)], axis=-1)


def _rope(x, cos, sin_signed):
    w = x.shape[-1]
    lane = lax.broadcasted_iota(I32, x.shape, 1)
    partner = jnp.where((lane & 1) == 0, pltpu.roll(x, w - 1, 1), pltpu.roll(x, 1, 1))
    return x * cos + partner * sin_signed


_ATT_SCALE = HEAD_DIM ** -0.5


def _ctx_attn_kernel(qkv_ref, gqa_ref, qn_ref, kn_ref, nao_ref, go_ref, gk_ref):
    outs = []
    for h in range(NA_HEADS):
        sl = slice(h * HEAD_DIM, (h + 1) * HEAD_DIM)
        q = qkv_ref[:, sl].astype(BF16)
        k = qkv_ref[:, NA_DIM + h * HEAD_DIM:NA_DIM + (h + 1) * HEAD_DIM].astype(BF16)
        v = qkv_ref[:, 2 * NA_DIM + h * HEAD_DIM:2 * NA_DIM + (h + 1) * HEAD_DIM].astype(BF16)
        outs.append(_softmax_pv([_dot_nt(q, k) * _ATT_SCALE], [v]))
    nao_ref[...] = jnp.concatenate(outs, axis=-1).astype(BF16)

    gq = _heads_rms(gqa_ref[:, 0:GQA_Q_DIM], GQA_HEADS, qn_ref[...])
    gk = _heads_rms(gqa_ref[:, GQA_Q_DIM:GQA_Q_DIM + GQA_KV_DIM], GQA_KV_HEADS, kn_ref[...])
    gk_ref[...] = gk
    rep = GQA_HEADS // GQA_KV_HEADS
    outs = []
    for h in range(GQA_HEADS):
        g = h // rep
        q = gq[:, h * HEAD_DIM:(h + 1) * HEAD_DIM].astype(BF16)
        k = gk[:, g * HEAD_DIM:(g + 1) * HEAD_DIM].astype(BF16)
        v0 = GQA_Q_DIM + GQA_KV_DIM + g * HEAD_DIM
        v = gqa_ref[:, v0:v0 + HEAD_DIM].astype(BF16)
        outs.append(_softmax_pv([_dot_nt(q, k) * _ATT_SCALE], [v]))
    go_ref[...] = jnp.concatenate(outs, axis=-1).astype(BF16)


def _ctx_attn(qkv, gqa, q_norm, k_norm):
    return pl.pallas_call(
        _ctx_attn_kernel,
        grid=(BATCH,),
        in_specs=[pl.BlockSpec((SEQ, 3 * NA_DIM), lambda b: (b, 0)),
                  pl.BlockSpec((SEQ, GQA_Q_DIM + 2 * GQA_KV_DIM), lambda b: (b, 0)),
                  pl.BlockSpec((1, HEAD_DIM), lambda b: (0, 0)),
                  pl.BlockSpec((1, HEAD_DIM), lambda b: (0, 0))],
        out_specs=[pl.BlockSpec((SEQ, NA_DIM), lambda b: (b, 0)),
                   pl.BlockSpec((SEQ, GQA_Q_DIM), lambda b: (b, 0)),
                   pl.BlockSpec((SEQ, GQA_KV_DIM), lambda b: (b, 0))],
        out_shape=[jax.ShapeDtypeStruct((N_CTX_TOK, NA_DIM), BF16),
                   jax.ShapeDtypeStruct((N_CTX_TOK, GQA_Q_DIM), BF16),
                   jax.ShapeDtypeStruct((N_CTX_TOK, GQA_KV_DIM), F32)],
        compiler_params=_cparams("parallel"),
        name="ctx_attn",
    )(qkv, gqa, q_norm, k_norm)


GQA_TQ = 256
GQA_KEYS = PAST_LEN + DEC_SEQ


def _lat_gqa_kernel(gqa_ref, ck_ref, cv_ref, cos_ref, sin_ref, qn_ref, kn_ref, o_ref, kbuf, vbuf):
    qb = pl.program_id(1)

    @pl.when(qb == 0)
    def _():
        kbuf[0:PAST_LEN, :] = ck_ref[0].astype(BF16)
        vbuf[0:PAST_LEN, :] = cv_ref[0].astype(BF16)
        k = _heads_rms(gqa_ref[:, GQA_Q_DIM:GQA_Q_DIM + GQA_KV_DIM], GQA_KV_HEADS, kn_ref[...])
        k = _rope(k, cos_ref[:, 0:GQA_KV_DIM], sin_ref[:, 0:GQA_KV_DIM])
        kbuf[PAST_LEN:GQA_KEYS, :] = k.astype(BF16)
        vbuf[PAST_LEN:GQA_KEYS, :] = gqa_ref[:, GQA_Q_DIM + GQA_KV_DIM:].astype(BF16)

    r0 = pl.multiple_of(qb * GQA_TQ, GQA_TQ)
    q = _heads_rms(gqa_ref[pl.ds(r0, GQA_TQ), 0:GQA_Q_DIM], GQA_HEADS, qn_ref[...])
    q = _rope(q, cos_ref[pl.ds(r0, GQA_TQ), :], sin_ref[pl.ds(r0, GQA_TQ), :]).astype(BF16)
    rep = GQA_HEADS // GQA_KV_HEADS
    outs = []
    for h in range(GQA_HEADS):
        g = h // rep
        k = kbuf[:, g * HEAD_DIM:(g + 1) * HEAD_DIM]
        v = vbuf[:, g * HEAD_DIM:(g + 1) * HEAD_DIM]
        s = _dot_nt(q[:, h * HEAD_DIM:(h + 1) * HEAD_DIM], k) * _ATT_SCALE
        outs.append(_softmax_pv([s], [v]))
    o_ref[...] = jnp.concatenate(outs, axis=-1).astype(BF16)


def _lat_gqa(gqa, cache_k, cache_v, cos, sin, q_norm, k_norm):
    lat_blk = N_CTX_TOK // DEC_SEQ
    return pl.pallas_call(
        _lat_gqa_kernel,
        grid=(DEC_BATCH, DEC_SEQ // GQA_TQ),
        in_specs=[pl.BlockSpec((DEC_SEQ, GQA_Q_DIM + 2 * GQA_KV_DIM), lambda b, q: (lat_blk + b, 0)),
                  pl.BlockSpec((1, PAST_LEN, GQA_KV_DIM), lambda b, q: (b, 0, 0)),
                  pl.BlockSpec((1, PAST_LEN, GQA_KV_DIM), lambda b, q: (b, 0, 0)),
                  pl.BlockSpec((DEC_SEQ, GQA_Q_DIM), lambda b, q: (0, 0)),
                  pl.BlockSpec((DEC_SEQ, GQA_Q_DIM), lambda b, q: (0, 0)),
                  pl.BlockSpec((1, HEAD_DIM), lambda b, q: (0, 0)),
                  pl.BlockSpec((1, HEAD_DIM), lambda b, q: (0, 0))],
        out_specs=pl.BlockSpec((GQA_TQ, GQA_Q_DIM), lambda b, q: (b * (DEC_SEQ // GQA_TQ) + q, 0)),
        out_shape=jax.ShapeDtypeStruct((N_LAT_TOK, GQA_Q_DIM), BF16),
        scratch_shapes=[pltpu.VMEM((GQA_KEYS, GQA_KV_DIM), BF16),
                        pltpu.VMEM((GQA_KEYS, GQA_KV_DIM), BF16)],
        compiler_params=_cparams("arbitrary", "arbitrary"),
        name="lat_gqa",
    )(gqa, cache_k, cache_v, cos, sin, q_norm, k_norm)


def _rope_tables():
    t = jnp.arange(DEC_SEQ)
    row = (t // GRID_W).astype(F32)
    col = (t % GRID_W).astype(F32)
    axis_dim = HEAD_DIM // 2
    inv_freq = ROPE_THETA ** (-jnp.arange(0, axis_dim, 2, dtype=F32) / axis_dim)
    ang = jnp.concatenate([row[:, None] * inv_freq, col[:, None] * inv_freq], axis=-1)
    cos = jnp.repeat(jnp.cos(ang), 2, axis=-1)
    sin = jnp.repeat(jnp.sin(ang), 2, axis=-1) * jnp.tile(jnp.array([-1.0, 1.0], F32), HEAD_DIM // 2)
    return jnp.tile(cos, (1, GQA_HEADS)), jnp.tile(sin, (1, GQA_HEADS))


NA_ROWS = DEC_SEQ // GRID_W
NA_KEYS = NA_WIN_ROWS * GRID_W


def _lat_na_kernel(qkv_ref, ck_ref, cv_ref, bias_ref, o_ref):
    r = pl.program_id(1)
    r0 = jnp.clip(r - NA_WIN_ROWS // 2, 0, NA_ROWS - NA_WIN_ROWS)
    q0 = pl.multiple_of(r * GRID_W, GRID_W)
    k0 = pl.multiple_of(r0 * GRID_W, GRID_W)
    outs = []
    for h in range(NA_HEADS):
        c0 = h * HEAD_DIM
        q = qkv_ref[pl.ds(q0, GRID_W), c0:c0 + HEAD_DIM].astype(BF16)
        k = qkv_ref[pl.ds(k0, NA_KEYS), NA_DIM + c0:NA_DIM + c0 + HEAD_DIM].astype(BF16)
        v = qkv_ref[pl.ds(k0, NA_KEYS), 2 * NA_DIM + c0:2 * NA_DIM + c0 + HEAD_DIM].astype(BF16)
        kc = ck_ref[0, :, c0:c0 + HEAD_DIM].astype(BF16)
        vc = cv_ref[0, :, c0:c0 + HEAD_DIM].astype(BF16)
        s_nb = _dot_nt(q, k) * _ATT_SCALE + bias_ref[0, h]
        s_ctx = _dot_nt(q, kc) * _ATT_SCALE
        outs.append(_softmax_pv([s_nb, s_ctx], [v, vc]))
    o_ref[...] = jnp.concatenate(outs, axis=-1).astype(BF16)


def _na_row_offset(r):
    return r - jnp.clip(r - NA_WIN_ROWS // 2, 0, NA_ROWS - NA_WIN_ROWS)


def _lat_na(qkv, cache_k, cache_v, bias):
    lat_blk = N_CTX_TOK // DEC_SEQ
    return pl.pallas_call(
        _lat_na_kernel,
        grid=(DEC_BATCH, NA_ROWS),
        in_specs=[pl.BlockSpec((DEC_SEQ, 3 * NA_DIM), lambda b, r: (lat_blk + b, 0)),
                  pl.BlockSpec((1, PAST_LEN, NA_DIM), lambda b, r: (b, 0, 0)),
                  pl.BlockSpec((1, PAST_LEN, NA_DIM), lambda b, r: (b, 0, 0)),
                  pl.BlockSpec((1, NA_HEADS, GRID_W, NA_KEYS), lambda b, r: (_na_row_offset(r), 0, 0, 0))],
        out_specs=pl.BlockSpec((GRID_W, NA_DIM), lambda b, r: (b * NA_ROWS + r, 0)),
        out_shape=jax.ShapeDtypeStruct((N_LAT_TOK, NA_DIM), BF16),
        compiler_params=_cparams("parallel", "arbitrary"),
        name="lat_na",
    )(qkv, cache_k, cache_v, bias)


def _na_bias_tables(rpb):
    d = np.arange(NA_WIN_ROWS)[:, None]
    kr = np.arange(NA_WIN_ROWS)[None, :]
    dr = kr - d + NA_WIN_ROWS - 1
    qc = np.arange(GRID_W)[:, None]
    kc = np.arange(GRID_W)[None, :]
    col0 = np.clip(qc - NA_WIN_COLS // 2, 0, GRID_W - NA_WIN_COLS)
    in_win = (kc >= col0) & (kc < col0 + NA_WIN_COLS)
    dc = np.clip(kc - qc + NA_WIN_COLS - 1, 0, 2 * NA_WIN_COLS - 2)
    row_hot = (dr[:, :, None] == np.arange(2 * NA_WIN_ROWS - 1)).astype(np.float32)
    col_hot = (dc[:, :, None] == np.arange(2 * NA_WIN_COLS - 1)).astype(np.float32)
    b = jnp.einsum('hac,dka,qxc->dhqkx', rpb.astype(F32), row_hot, col_hot, precision=lax.Precision.HIGHEST)
    b = jnp.where(in_win[None, None, :, None, :], b, NEG_INF)
    return b.reshape(NA_WIN_ROWS, NA_HEADS, GRID_W, NA_KEYS)


CONV_TB = 1024
CONV_HALO = 8
CONV_HALO_BLOCKS = CONV_TB // CONV_HALO


def _conv_kernel(prev_ref, x_ref, next_ref, w_ref, b_ref, o_ref):
    i = pl.program_id(0)
    seq = jnp.where(i < N_CTX_TOK // CONV_TB, SEQ, DEC_SEQ)
    x = x_ref[...]
    ext = jnp.concatenate([prev_ref[...], x, next_ref[...]], axis=0)
    n_ext = CONV_TB + 2 * CONV_HALO
    pos = (lax.broadcasted_iota(I32, (CONV_TB, 1), 0) + i * CONV_TB) & (seq - 1)
    half = CONV_W // 2
    acc = x * w_ref[half:half + 1, :]
    for s in range(-half, half + 1):
        if s == 0:
            continue
        shifted = pltpu.roll(ext, (-s) % n_ext, 0)[CONV_HALO:CONV_HALO + CONV_TB]
        valid = (pos + s >= 0) & (pos + s < seq)
        acc = acc + jnp.where(valid, shifted, 0.0) * w_ref[half + s:half + s + 1, :]
    acc = acc + b_ref[...]
    o_ref[...] = acc * _sigmoid(acc)


def _conv_act(xbc, conv_w, conv_b):
    return pl.pallas_call(
        _conv_kernel,
        grid=(N_TOK // CONV_TB,),
        in_specs=[pl.BlockSpec((CONV_HALO, CONV_DIM),
                               lambda i: (jnp.maximum(i * CONV_HALO_BLOCKS - 1, 0), 0)),
                  pl.BlockSpec((CONV_TB, CONV_DIM), lambda i: (i, 0)),
                  pl.BlockSpec((CONV_HALO, CONV_DIM),
                               lambda i: (jnp.minimum((i + 1) * CONV_HALO_BLOCKS, N_TOK // CONV_HALO - 1), 0)),
                  pl.BlockSpec((CONV_W, CONV_DIM), lambda i: (0, 0)),
                  pl.BlockSpec((1, CONV_DIM), lambda i: (0, 0))],
        out_specs=pl.BlockSpec((CONV_TB, CONV_DIM), lambda i: (i, 0)),
        out_shape=jax.ShapeDtypeStruct((N_TOK, CONV_DIM), F32),
        compiler_params=_cparams("parallel"),
        name="conv_act",
    )(xbc, xbc, xbc, conv_w, conv_b)


def _chunk_seq(g):
    ctx_n = SEQ // CHUNK
    lat_n = DEC_SEQ // CHUNK
    is_ctx = g < N_CTX_CHUNKS
    gl = g - N_CTX_CHUNKS
    sid = jnp.where(is_ctx, g // ctx_n, BATCH + gl // lat_n)
    cin = jnp.where(is_ctx, g % ctx_n, gl % lat_n)
    n = jnp.where(is_ctx, ctx_n, lat_n)
    return sid, cin, n


def _ssd_init(h0_ref, st_ref, gg, reverse):
    _, cin, n = _chunk_seq(gg)

    @pl.when(cin == (n - 1 if reverse else 0))
    def _():
        st_ref[0] = h0_ref[0]


def _ssd_chunk(xa_ref, dt_ref, dtb, alog, y_ref, st_ref, reverse):
    x = dt_ref[...] + dtb
    dt = jnp.maximum(x, 0.0) + jnp.log1p(jnp.exp(-jnp.abs(x)))
    dta = dt * -jnp.exp(alog)
    ii = lax.broadcasted_iota(I32, (CHUNK, CHUNK), 0)
    jj = lax.broadcasted_iota(I32, (CHUNK, CHUNK), 1)
    tri = (jj >= ii) if reverse else (jj <= ii)
    cum = _dot_exact(tri.astype(F32), dta)
    cum_t = cum.T
    edge = 0 if reverse else CHUNK - 1
    tot = cum[edge:edge + 1, :]
    rep = SSM_HEADS // SSM_GROUPS
    for grp in range(SSM_GROUPS):
        bg = xa_ref[:, SSM_INNER + grp * SSM_STATE:SSM_INNER + (grp + 1) * SSM_STATE].astype(BF16)
        c0 = SSM_INNER + SSM_BC_DIM + grp * SSM_STATE
        cg = xa_ref[:, c0:c0 + SSM_STATE].astype(BF16)
        cb = _dot_nt(cg, bg)
        for h in range(grp * rep, (grp + 1) * rep):
            hs = slice(h * SSM_HEAD_DIM, (h + 1) * SSM_HEAD_DIM)
            col = cum[:, h:h + 1]
            row = cum_t[h:h + 1, :]
            decay = jnp.where(tri, jnp.exp(jnp.minimum(col - row, 0.0)), 0.0)
            xdt = xa_ref[:, hs] * dt[:, h:h + 1]
            state = st_ref[0, hs, :]
            y = _dot((cb * decay).astype(BF16), xdt.astype(BF16))
            y = y + _dot_nt(cg, state.astype(BF16)) * jnp.exp(col)
            y_ref[:, hs] = y
            toth = tot[:, h:h + 1]
            w = (xdt * jnp.exp(toth - col)).astype(BF16)
            upd = lax.dot_general(w, bg, (((0,), (0,)), ((), ())), preferred_element_type=F32)
            st_ref[0, hs, :] = state * jnp.exp(toth) + upd


def _ssd_kernel(xaf_ref, dtf_ref, h0f_ref, xab_ref, dtb_ref, h0b_ref, bias_ref, alog_ref,
                yf_ref, stf_ref, yb_ref, stb_ref):
    g = pl.program_id(0)
    _ssd_init(h0f_ref, stf_ref, g, False)
    _ssd_chunk(xaf_ref, dtf_ref, bias_ref[0:1, :], alog_ref[0:1, :], yf_ref, stf_ref, False)
    _ssd_init(h0b_ref, stb_ref, N_CHUNKS - 1 - g, True)
    _ssd_chunk(xab_ref, dtb_ref, bias_ref[1:2, :], alog_ref[1:2, :], yb_ref, stb_ref, True)


def _ssd(xa, dt_raw, h0_fwd, h0_bwd, dt_bias, a_log):
    rev = lambda g: N_CHUNKS - 1 - g
    chunk = lambda w, order: pl.BlockSpec((CHUNK, w), lambda g: (order(g), 0))
    state = lambda order: pl.BlockSpec((1, SSM_INNER, SSM_STATE), lambda g: (_chunk_seq(order(g))[0], 0, 0))
    same = lambda g: g
    y_shape = jax.ShapeDtypeStruct((N_TOK, SSM_INNER), F32)
    st_shape = jax.ShapeDtypeStruct((N_SEQ, SSM_INNER, SSM_STATE), F32)
    return pl.pallas_call(
        _ssd_kernel,
        grid=(N_CHUNKS,),
        in_specs=[chunk(CONV_DIM, same), chunk(LANES, same), state(same),
                  chunk(CONV_DIM, rev), chunk(LANES, rev), state(rev),
                  pl.BlockSpec((2, LANES), lambda g: (0, 0)),
                  pl.BlockSpec((2, LANES), lambda g: (0, 0))],
        out_specs=[chunk(SSM_INNER, same), state(same), chunk(SSM_INNER, rev), state(rev)],
        out_shape=[y_shape, st_shape, y_shape, st_shape],
        compiler_params=_cparams("arbitrary"),
        name="ssd",
    )(xa, dt_raw, h0_fwd, xa, dt_raw, h0_bwd, dt_bias, a_log)


def _outproj_kernel(x_ref, nao_ref, go_ref, yf_ref, yb_ref, xs_ref, z_ref, dsk_ref, sn_ref, wo_ref,
                    m_ref, g2_ref, wr_ref, br_ref, xo_ref, h_ref, idx_ref, gate_ref, sel_ref):
    m = m_ref[0]
    z = z_ref[...]
    y = (yf_ref[...] + yb_ref[...] + xs_ref[...] * dsk_ref[...]) * (z * _sigmoid(z))
    s_o = _rms(y, sn_ref[...]).astype(BF16)
    mix = (_dot(nao_ref[...], wo_ref[0:NA_DIM, :])
           + _dot(go_ref[...], wo_ref[NA_DIM:NA_DIM + GQA_Q_DIM, :])
           + _dot(s_o, wo_ref[NA_DIM + GQA_Q_DIM:, :]))
    x = x_ref[...] + m[2:3] * mix
    xo_ref[...] = x
    h = _rms(x, g2_ref[...]) * (1.0 + m[4:5]) + m[3:4]
    h_ref[...] = h.astype(BF16)

    logits = _dot_exact(h, wr_ref[...]) + br_ref[...]
    lane = lax.broadcasted_iota(I32, logits.shape, 1).astype(F32)
    vals, idxs = [], []
    for _ in range(TOP_K):
        v = logits.max(axis=-1, keepdims=True)
        i = jnp.where(logits == v, lane, float(LANES)).min(axis=-1, keepdims=True)
        vals.append(v)
        idxs.append(i)
        logits = jnp.where(lane == i, -jnp.inf, logits)
    es = [jnp.exp(v - vals[0]) for v in vals]
    den = es[0] + es[1] + es[2] + es[3]
    idx_out = jnp.zeros(lane.shape, F32)
    gate_out = jnp.zeros(lane.shape, F32)
    sel = jnp.zeros(lane.shape, F32)
    for k in range(TOP_K):
        idx_out = jnp.where(lane == float(k), idxs[k], idx_out)
        gate_out = jnp.where(lane == float(k), es[k] / den, gate_out)
        sel = jnp.where(lane == idxs[k], 1.0, sel)
    idx_ref[...] = idx_out.astype(I32)
    gate_ref[...] = gate_out
    sel_ref[...] = sel.astype(BF16)


def _outproj(x, nao, go, yf, yb, xa, z, d_skip, ssm_norm, w_out, mods, gain2, w_router, b_router):
    row = lambda w: pl.BlockSpec((ROW_TILE, w), lambda i: (i, 0))
    full = lambda a, b: pl.BlockSpec((a, b), lambda i: (0, 0))
    return pl.pallas_call(
        _outproj_kernel,
        grid=(N_ROW_TILES,),
        in_specs=[row(D_MODEL), row(NA_DIM), row(GQA_Q_DIM), row(SSM_INNER), row(SSM_INNER),
                  row(SSM_INNER), row(SSM_INNER), full(1, SSM_INNER), full(1, SSM_INNER),
                  full(D_MIX, D_MODEL),
                  pl.BlockSpec((1, 6, D_MODEL), lambda i: (_cond_of_tile(i), 0, 0)),
                  full(1, D_MODEL), full(D_MODEL, LANES), full(1, LANES)],
        out_specs=[row(D_MODEL), row(D_MODEL), row(LANES), row(LANES), row(LANES)],
        out_shape=[jax.ShapeDtypeStruct((N_TOK, D_MODEL), F32),
                   jax.ShapeDtypeStruct((N_TOK, D_MODEL), BF16),
                   jax.ShapeDtypeStruct((N_TOK, LANES), I32),
                   jax.ShapeDtypeStruct((N_TOK, LANES), F32),
                   jax.ShapeDtypeStruct((N_TOK, LANES), BF16)],
        compiler_params=_cparams("parallel"),
        name="outproj_router",
    )(x, nao, go, yf, yb, xa, z, d_skip, ssm_norm, w_out, mods, gain2, w_router, b_router)


RANK_TB = 512


def _rank_kernel(sel_ref, rank_ref, cnt_ref, carry):
    @pl.when(pl.program_id(0) == 0)
    def _():
        carry[...] = jnp.zeros_like(carry)

    sel = sel_ref[...]
    ii = lax.broadcasted_iota(I32, (RANK_TB, RANK_TB), 0)
    jj = lax.broadcasted_iota(I32, (RANK_TB, RANK_TB), 1)
    before = (jj < ii).astype(BF16)
    rank_ref[...] = _dot(before, sel) + carry[0:1, :]
    carry[...] = carry[...] + _dot(jnp.ones((8, RANK_TB), BF16), sel)
    cnt_ref[...] = carry[...]


def _ranks(sel):
    return pl.pallas_call(
        _rank_kernel,
        grid=(N_TOK // RANK_TB,),
        in_specs=[pl.BlockSpec((RANK_TB, LANES), lambda i: (i, 0))],
        out_specs=[pl.BlockSpec((RANK_TB, LANES), lambda i: (i, 0)),
                   pl.BlockSpec((8, LANES), lambda i: (0, 0))],
        out_shape=[jax.ShapeDtypeStruct((N_TOK, LANES), F32),
                   jax.ShapeDtypeStruct((8, LANES), F32)],
        scratch_shapes=[pltpu.VMEM((8, LANES), F32)],
        compiler_params=_cparams("arbitrary"),
        name="moe_ranks",
    )(sel)


_RUN_PIECES = (256, 128, 64, 32, 16, 8)


def _run_dma(src, s0, dst, d0, n, sem, *, wait, fixed_src=False):
    for size in _RUN_PIECES:
        @pl.when((n & size) != 0)
        def _(size=size):
            done = n & ~(2 * size - 1)
            s = 0 if fixed_src else pl.multiple_of(s0 + done, RUN_ALIGN)
            d = pl.multiple_of(d0 + done, RUN_ALIGN)
            copy = pltpu.make_async_copy(src.at[pl.ds(s, size), :], dst.at[pl.ds(d, size), :], sem)
            if wait:
                copy.wait()
            else:
                copy.start()


_TILE_PIECES = (1024, 512, 256, 128, 64, 32, 16, 8)


def _wait_rows(buf, n, sem):
    for size in _TILE_PIECES:
        @pl.when((n & size) != 0)
        def _(size=size):
            pltpu.make_async_copy(buf.at[pl.ds(0, size), :], buf.at[pl.ds(0, size), :], sem).wait()


def _local_rows(idx_ref, rank_ref, base_ref):
    pos = rank_ref[...] + base_ref[0]
    lane = lax.broadcasted_iota(I32, pos.shape, 1)
    idx = idx_ref[...]
    return [jnp.sum(jnp.where(lane == idx[:, k:k + 1], pos, 0.0), axis=-1, keepdims=True) for k in range(TOP_K)]


def _dispatch_kernel(run_ref, loc_ref, glb_ref, tot_ref, fs_ref, fl_ref, nu_ref,
                     h_ref, idx_ref, rank_ref, base_ref, out_ref, xl_ref, zbuf, sems):
    i = pl.program_id(0)
    slot = i % 2
    xl = xl_ref.at[slot]
    sem = sems.at[slot]

    @pl.when(i == 0)
    def _():
        zbuf[...] = jnp.zeros_like(zbuf)
        for e in range(N_EXPERTS):
            _run_dma(zbuf, 0, out_ref, fs_ref[e], fl_ref[e], sem, wait=False, fixed_src=True)
        for e in range(N_EXPERTS):
            _run_dma(zbuf, 0, out_ref, fs_ref[e], fl_ref[e], sem, wait=True, fixed_src=True)

        def tail(b, carry):
            copy = pltpu.make_async_copy(zbuf, out_ref.at[pl.ds(pl.multiple_of(b * MOE_BM, MOE_BM), MOE_BM), :], sem)
            copy.start()
            copy.wait()
            return carry

        lax.fori_loop(nu_ref[0], MOE_NB, tail, 0)

    @pl.when(i >= 2)
    def _():
        _wait_rows(xl, tot_ref[i - 2], sem)

    rows = _local_rows(idx_ref, rank_ref, base_ref)
    p = lax.broadcasted_iota(I32, (ROW_TILE, LOCAL_ROWS), 1).astype(F32)
    hot = (p == rows[0])
    for k in range(1, TOP_K):
        hot = hot | (p == rows[k])
    xl[...] = lax.dot_general(hot.astype(BF16), h_ref[...].astype(BF16), (((0,), (0,)), ((), ())),
                              preferred_element_type=F32)
    for e in range(N_EXPERTS):
        j = i * N_EXPERTS + e
        _run_dma(xl, loc_ref[j], out_ref, glb_ref[j], run_ref[j], sem, wait=False)

    @pl.when(i == N_ROW_TILES - 1)
    def _():
        _wait_rows(xl, tot_ref[i], sem)
        _wait_rows(xl_ref.at[1 - slot], tot_ref[i - 1], sems.at[1 - slot])


def _dispatch(plan, h, idx, rank):
    tile = lambda w: pl.BlockSpec((ROW_TILE, w), lambda i, *_: (i, 0))
    return pl.pallas_call(
        _dispatch_kernel,
        grid_spec=pltpu.PrefetchScalarGridSpec(
            num_scalar_prefetch=7,
            grid=(N_ROW_TILES,),
            in_specs=[tile(D_MODEL), tile(LANES), tile(LANES),
                      pl.BlockSpec((1, 1, LANES), lambda i, *_: (i, 0, 0))],
            out_specs=pl.BlockSpec(memory_space=pl.ANY),
            scratch_shapes=[pltpu.VMEM((2, LOCAL_ROWS, D_MODEL), F32),
                            pltpu.VMEM((MOE_BM, D_MODEL), F32),
                            pltpu.SemaphoreType.DMA((2,))]),
        out_shape=jax.ShapeDtypeStruct((MOE_NB * MOE_BM, D_MODEL), F32),
        compiler_params=_cparams("arbitrary"),
        name="moe_dispatch",
    )(plan["run"], plan["local"], plan["global"], plan["total"], plan["fill_start"], plan["fill_len"],
      plan["n_used"], h, idx, rank, plan["base"])


def _expert_kernel(be_ref, nu_ref, new_ref, x_ref, wu_ref, bu_ref, wd_ref, bd_ref, y_ref, wu_s, wd_s):
    b = pl.program_id(0)
    used = b < nu_ref[0]

    @pl.when(jnp.logical_not(used))
    def _():
        y_ref[...] = jnp.zeros_like(y_ref)

    @pl.when(jnp.logical_and(used, new_ref[b] == 1))
    def _():
        r = lax.broadcasted_iota(I32, (UP_GROUP, UP_GROUP), 0)
        c = lax.broadcasted_iota(I32, (UP_GROUP, UP_GROUP), 1)
        src = jnp.where(c < UP_GROUP // 2, 2 * c, 2 * (c - UP_GROUP // 2) + 1)
        perm = (r == src).astype(BF16)
        for g in range(2 * D_FF // UP_GROUP):
            cols = slice(g * UP_GROUP, (g + 1) * UP_GROUP)
            wu_s[:, cols] = _dot(wu_ref[0, 0, :, cols].astype(BF16), perm).astype(BF16)
        wd_s[...] = wd_ref[0, 0].astype(BF16)

    @pl.when(used)
    def _():
        up = _dot(x_ref[...].astype(BF16), wu_s[...]) + bu_ref[0]
        half = UP_GROUP // 2
        acts = []
        for g in range(2 * D_FF // UP_GROUP):
            gate = jnp.minimum(up[:, g * UP_GROUP:g * UP_GROUP + half], SWIGLU_LIMIT)
            lin = jnp.clip(up[:, g * UP_GROUP + half:(g + 1) * UP_GROUP], -SWIGLU_LIMIT, SWIGLU_LIMIT)
            acts.append((gate * _sigmoid(SWIGLU_ALPHA * gate) * (lin + 1.0)).astype(BF16))
        y_ref[...] = _dot(jnp.concatenate(acts, axis=-1), wd_s[...]) + bd_ref[0]


def _experts(layer, plan, xs, w_up, b_up, w_down, b_down):
    blk = lambda b, nu: jnp.maximum(jnp.minimum(b, nu[0] - 1), 0)
    return pl.pallas_call(
        _expert_kernel,
        grid_spec=pltpu.PrefetchScalarGridSpec(
            num_scalar_prefetch=3,
            grid=(MOE_NB,),
            in_specs=[pl.BlockSpec((MOE_BM, D_MODEL), lambda b, be, nu, nw: (blk(b, nu), 0)),
                      pl.BlockSpec((1, 1, D_MODEL, 2 * D_FF), lambda b, be, nu, nw: (layer, be[blk(b, nu)], 0, 0)),
                      pl.BlockSpec((1, 1, 2 * D_FF), lambda b, be, nu, nw: (be[blk(b, nu)], 0, 0)),
                      pl.BlockSpec((1, 1, D_FF, D_MODEL), lambda b, be, nu, nw: (layer, be[blk(b, nu)], 0, 0)),
                      pl.BlockSpec((1, 1, D_MODEL), lambda b, be, nu, nw: (be[blk(b, nu)], 0, 0))],
            out_specs=pl.BlockSpec((MOE_BM, D_MODEL), lambda b, be, nu, nw: (b, 0)),
            scratch_shapes=[pltpu.VMEM((D_MODEL, 2 * D_FF), BF16),
                            pltpu.VMEM((D_FF, D_MODEL), BF16)]),
        out_shape=jax.ShapeDtypeStruct((MOE_NB * MOE_BM, D_MODEL), F32),
        compiler_params=_cparams("arbitrary"),
        name="moe_experts",
    )(plan["blk_expert"], plan["n_used"], plan["blk_new"], xs, w_up, b_up, w_down, b_down)


def _combine_kernel(run_ref, loc_ref, glb_ref, tot_ref, x_ref, gate_ref, idx_ref, rank_ref, base_ref, m_ref, fn_ref,
                    ys_ref, o_ref, yl_ref, sems, *, final):
    i = pl.program_id(0)
    slot = i % 2

    def fetch(tile, buf):
        yl = yl_ref.at[buf]
        for e in range(N_EXPERTS):
            j = tile * N_EXPERTS + e
            _run_dma(ys_ref, glb_ref[j], yl, loc_ref[j], run_ref[j], sems.at[buf], wait=False)

        def clear(r, carry):
            yl[pl.ds(pl.multiple_of(r * RUN_ALIGN, RUN_ALIGN), RUN_ALIGN), :] = jnp.zeros((RUN_ALIGN, D_MODEL), F32)
            return carry

        lax.fori_loop(tot_ref[tile] // RUN_ALIGN, LOCAL_ROWS // RUN_ALIGN, clear, 0)

    @pl.when(i == 0)
    def _():
        fetch(0, 0)

    @pl.when(i + 1 < N_ROW_TILES)
    def _():
        fetch(i + 1, 1 - slot)

    rows = _local_rows(idx_ref, rank_ref, base_ref)
    gate = gate_ref[...]
    p = lax.broadcasted_iota(I32, (ROW_TILE, LOCAL_ROWS), 1).astype(F32)
    w = jnp.zeros((ROW_TILE, LOCAL_ROWS), F32)
    for k in range(TOP_K):
        w = jnp.where(p == rows[k], gate[:, k:k + 1], w)

    _wait_rows(yl_ref.at[slot], tot_ref[i], sems.at[slot])
    acc = _dot(w.astype(BF16), yl_ref[slot].astype(BF16))
    x = x_ref[...] + m_ref[0][5:6] * acc
    o_ref[...] = _rms(x, fn_ref[...]) if final else x


def _combine(plan, x, gates, idx, rank, mods, final_norm, ys, final):
    tile = lambda w: pl.BlockSpec((ROW_TILE, w), lambda i, *_: (i, 0))
    return pl.pallas_call(
        functools.partial(_combine_kernel, final=final),
        grid_spec=pltpu.PrefetchScalarGridSpec(
            num_scalar_prefetch=4,
            grid=(N_ROW_TILES,),
            in_specs=[tile(D_MODEL), tile(LANES), tile(LANES), tile(LANES),
                      pl.BlockSpec((1, 1, LANES), lambda i, *_: (i, 0, 0)),
                      pl.BlockSpec((1, 6, D_MODEL), lambda i, *_: (_cond_of_tile(i), 0, 0)),
                      pl.BlockSpec((1, D_MODEL), lambda i, *_: (0, 0)),
                      pl.BlockSpec(memory_space=pl.ANY)],
            out_specs=tile(D_MODEL),
            scratch_shapes=[pltpu.VMEM((2, LOCAL_ROWS, D_MODEL), F32),
                            pltpu.SemaphoreType.DMA((2,))]),
        out_shape=jax.ShapeDtypeStruct((N_TOK, D_MODEL), F32),
        compiler_params=_cparams("arbitrary"),
        name="moe_combine",
    )(plan["run"], plan["local"], plan["global"], plan["total"], x, gates, idx, rank, plan["base"], mods,
      final_norm, ys)


def _moe_plan(rank, cnt):
    first = rank[::ROW_TILE, :N_EXPERTS].astype(I32)
    total = cnt[0:1, :N_EXPERTS].astype(I32)
    run = jnp.concatenate([first[1:], total], axis=0) - first
    run = (run + RUN_ALIGN - 1) // RUN_ALIGN * RUN_ALIGN
    local = jnp.cumsum(run, axis=1) - run
    sizes = run.sum(axis=0)
    padded = (sizes + MOE_BM - 1) // MOE_BM * MOE_BM
    ends = jnp.cumsum(padded)
    starts = ends - padded
    glob = starts[None, :] + jnp.cumsum(run, axis=0) - run
    blk_start = jnp.arange(MOE_NB, dtype=I32) * MOE_BM
    blk_expert = jnp.minimum(jnp.sum(ends[None, :] <= blk_start[:, None], axis=1), N_EXPERTS - 1).astype(I32)
    blk_new = jnp.concatenate([jnp.ones((1,), I32), (blk_expert[1:] != blk_expert[:-1]).astype(I32)])
    base = _pad_lanes((local - first).astype(F32)).reshape(N_ROW_TILES, 1, LANES)
    return {"run": run.reshape(-1), "local": local.reshape(-1), "global": glob.reshape(-1), "total": run.sum(axis=1),
            "fill_start": starts + sizes, "fill_len": padded - sizes, "n_used": ends[-1:] // MOE_BM,
            "blk_expert": blk_expert, "blk_new": blk_new, "base": base}


def _pad_lanes(v, fill=0.0):
    return jnp.pad(v, ((0, 0), (0, LANES - v.shape[-1])), constant_values=fill)


def kernel(x_prompt, x_sample, cache_na_k, cache_na_v, cache_gqa_k, cache_gqa_v, state_ssm, c, c_ctx, w_ada, b_ada, norm_mix, norm_ffn, w_in, na_rpb, gqa_q_norm, gqa_k_norm, ssm_conv_w, ssm_conv_b, ssm_dt_bias, ssm_a_log, ssm_d, ssm_norm, w_out, w_router, b_router, w_up, b_up, w_down, b_down, final_norm):
    x = jnp.concatenate([x_prompt.reshape(N_CTX_TOK, D_MODEL), x_sample.reshape(N_LAT_TOK, D_MODEL)], axis=0)
    conds = jnp.concatenate([c_ctx[None], c, jnp.zeros((COND_ROWS - N_COND, D_MODEL), F32)], axis=0)
    mods = _adaln(conds, w_ada, b_ada).reshape(DEPTH, COND_ROWS, 6, D_MODEL)
    cos, sin = _rope_tables()

    w_in_b = jnp.pad(w_in, ((0, 0), (0, 0), (0, IN_PAD - IN_DIM))).astype(BF16)
    w_out_b = w_out.astype(BF16)
    b_up_s = b_up.reshape(DEPTH, N_EXPERTS, 2 * D_FF // UP_GROUP, UP_GROUP // 2, 2)
    b_up_s = jnp.swapaxes(b_up_s, -1, -2).reshape(DEPTH, N_EXPERTS, 1, 2 * D_FF)
    b_down_s = b_down.reshape(DEPTH, N_EXPERTS, 1, D_MODEL)

    ctx_out = []
    for l in range(DEPTH):
        qkv, gqa, z, xbc, dt_raw = _inproj(x, mods[l], norm_mix[l][None], w_in_b[l])
        qn, kn = gqa_q_norm[l][None], gqa_k_norm[l][None]

        nao_c, go_c, gk_c = _ctx_attn(qkv, gqa, qn, kn)
        go_l = _lat_gqa(gqa, cache_gqa_k[:, l].reshape(DEC_BATCH, PAST_LEN, GQA_KV_DIM),
                        cache_gqa_v[:, l].reshape(DEC_BATCH, PAST_LEN, GQA_KV_DIM), cos, sin, qn, kn)
        nao_l = _lat_na(qkv, cache_na_k[:, l].reshape(DEC_BATCH, PAST_LEN, NA_DIM),
                        cache_na_v[:, l].reshape(DEC_BATCH, PAST_LEN, NA_DIM), _na_bias_tables(na_rpb[l]))

        xa = _conv_act(xbc, ssm_conv_w[l], ssm_conv_b[l][None])
        zeros = jnp.zeros((BATCH, SSM_INNER, SSM_STATE), F32)
        h0 = [jnp.concatenate([zeros, state_ssm[:, l, d].reshape(DEC_BATCH, SSM_INNER, SSM_STATE)], axis=0)
              for d in range(2)]
        y_f, st_f, y_b, st_b = _ssd(xa, dt_raw, h0[0], h0[1], _pad_lanes(ssm_dt_bias[l]), _pad_lanes(ssm_a_log[l]))
        ys = [y_f, y_b]
        sts = [st[:BATCH].reshape(BATCH, SSM_HEADS, SSM_HEAD_DIM, SSM_STATE) for st in (st_f, st_b)]

        x, h, top_idx, gates, sel = _outproj(
            x, jnp.concatenate([nao_c, nao_l], axis=0), jnp.concatenate([go_c, go_l], axis=0),
            ys[0], ys[1], xa, z, jnp.repeat(ssm_d[l], SSM_HEAD_DIM)[None], ssm_norm[l][None], w_out_b[l],
            mods[l], norm_ffn[l][None], _pad_lanes(w_router[l]), _pad_lanes(b_router[l][None], NEG_INF))

        rank, cnt = _ranks(sel)
        plan = _moe_plan(rank, cnt)
        y_sorted = _experts(l, plan, _dispatch(plan, h, top_idx, rank), w_up, b_up_s[l], w_down, b_down_s[l])
        x = _combine(plan, x, gates, top_idx, rank, mods[l], final_norm[None], y_sorted, final=(l == DEPTH - 1))

        ctx_out.append((
            qkv[:N_CTX_TOK, NA_DIM:2 * NA_DIM].reshape(BATCH, SEQ, NA_HEADS, HEAD_DIM),
            qkv[:N_CTX_TOK, 2 * NA_DIM:].reshape(BATCH, SEQ, NA_HEADS, HEAD_DIM),
            gk_c.reshape(BATCH, SEQ, GQA_KV_HEADS, HEAD_DIM),
            gqa[:N_CTX_TOK, GQA_Q_DIM + GQA_KV_DIM:].reshape(BATCH, SEQ, GQA_KV_HEADS, HEAD_DIM),
            jnp.stack(sts, axis=1)))

    y_prompt = x[:N_CTX_TOK].reshape(BATCH, SEQ, D_MODEL)
    y_sample = x[N_CTX_TOK:].reshape(DEC_BATCH, DEC_SEQ, D_MODEL)
    return (y_prompt, y_sample) + tuple(jnp.stack([e[i] for e in ctx_out], axis=1) for i in range(5))
```

```python
import functools

import numpy as np
import jax
import jax.numpy as jnp
from jax import lax
from jax.experimental import pallas as pl
from jax.experimental.pallas import tpu as pltpu

F32 = jnp.float32
BF16 = jnp.bfloat16
I32 = jnp.int32

D_MODEL = 1024
BATCH = 16
SEQ = 256
DEPTH = 2
DEC_BATCH = 2
DEC_SEQ = 2048
PAST_LEN = 512
GRID_W = 64
HEAD_DIM = 64
NA_HEADS = 4
NA_WIN_ROWS = 8
NA_WIN_COLS = 16
GQA_HEADS = 4
GQA_KV_HEADS = 2
ROPE_THETA = 10000.0
SSM_HEADS = 8
SSM_HEAD_DIM = 64
SSM_STATE = 64
SSM_GROUPS = 2
SSM_INNER = SSM_HEADS * SSM_HEAD_DIM
SSM_BC_DIM = SSM_GROUPS * SSM_STATE
CONV_DIM = SSM_INNER + 2 * SSM_BC_DIM
CONV_W = 5
CHUNK = 128
NA_DIM = NA_HEADS * HEAD_DIM
GQA_Q_DIM = GQA_HEADS * HEAD_DIM
GQA_KV_DIM = GQA_KV_HEADS * HEAD_DIM
D_MIX = NA_DIM + GQA_Q_DIM + SSM_INNER
IN_DIM = 3 * NA_DIM + GQA_Q_DIM + 2 * GQA_KV_DIM + SSM_INNER + CONV_DIM + SSM_HEADS
N_EXPERTS = 32
TOP_K = 4
D_FF = D_MODEL
SWIGLU_LIMIT = 7.0
SWIGLU_ALPHA = 1.702
EPS = 1e-6
NEG_INF = -1e30

LANES = 128
N_CTX_TOK = BATCH * SEQ
N_LAT_TOK = DEC_BATCH * DEC_SEQ
N_TOK = N_CTX_TOK + N_LAT_TOK
N_COND = 1 + DEC_BATCH
COND_ROWS = 16
IN_PAD = 3 * NA_DIM + GQA_Q_DIM + 2 * GQA_KV_DIM + SSM_INNER + CONV_DIM + LANES
ROW_TILE = 256
N_ROW_TILES = N_TOK // ROW_TILE
MOE_BM = 256
N_SLOTS = N_TOK * TOP_K
RUN_ALIGN = 8
N_RUNS = N_ROW_TILES * N_EXPERTS
LOCAL_ROWS = 1280
MOE_NB = -(-(N_SLOTS + N_RUNS * (RUN_ALIGN - 1)) // MOE_BM) + N_EXPERTS
UP_GROUP = 256
N_SEQ = BATCH + DEC_BATCH
N_CHUNKS = N_TOK // CHUNK
N_CTX_CHUNKS = N_CTX_TOK // CHUNK
VMEM_LIMIT = 56 * 1024 * 1024


def _cparams(*sem):
    return pltpu.CompilerParams(dimension_semantics=sem, vmem_limit_bytes=VMEM_LIMIT)


def _sigmoid(x):
    return 1.0 / (1.0 + jnp.exp(-x))


def _dot(a, b):
    return jnp.dot(a, b, preferred_element_type=F32)


def _dot_nt(a, b):
    return lax.dot_general(a, b, (((1,), (1,)), ((), ())), preferred_element_type=F32)


def _dot_exact(a, b):
    return jnp.dot(a, b, preferred_element_type=F32, precision=lax.Precision.HIGHEST)


def _rms(x, g):
    return x * lax.rsqrt(jnp.mean(x * x, axis=-1, keepdims=True) + EPS) * g


def _cond_of_tile(i):
    ctx_tiles = N_CTX_TOK // ROW_TILE
    return jnp.where(i < ctx_tiles, 0, 1 + (i - ctx_tiles) // (DEC_SEQ // ROW_TILE))


def _adaln_kernel(c_ref, w_ref, b_ref, o_ref):
    c = c_ref[...]
    s = (c * _sigmoid(c)).astype(BF16)
    o_ref[0] = _dot(s, w_ref[0].astype(BF16)) + b_ref[0]


def _adaln(conds, w_ada, b_ada):
    tn = 1536
    return pl.pallas_call(
        _adaln_kernel,
        grid=(DEPTH, 6 * D_MODEL // tn),
        in_specs=[pl.BlockSpec((COND_ROWS, D_MODEL), lambda l, j: (0, 0)),
                  pl.BlockSpec((1, D_MODEL, tn), lambda l, j: (l, 0, j)),
                  pl.BlockSpec((1, 1, tn), lambda l, j: (l, 0, j))],
        out_specs=pl.BlockSpec((1, COND_ROWS, tn), lambda l, j: (l, 0, j)),
        out_shape=jax.ShapeDtypeStruct((DEPTH, COND_ROWS, 6 * D_MODEL), F32),
        compiler_params=_cparams("parallel", "parallel"),
        name="adaln",
    )(conds, w_ada, b_ada.reshape(DEPTH, 1, 6 * D_MODEL))


_IN_SPLITS = (3 * NA_DIM, GQA_Q_DIM + 2 * GQA_KV_DIM, SSM_INNER, CONV_DIM, LANES)


def _inproj_kernel(x_ref, m_ref, g_ref, w_ref, qkv_ref, gqa_ref, z_ref, xbc_ref, dt_ref):
    m = m_ref[0]
    h = _rms(x_ref[...], g_ref[...]) * (1.0 + m[1:2]) + m[0:1]
    p = _dot(h.astype(BF16), w_ref[...])
    off = 0
    for ref, width in zip((qkv_ref, gqa_ref, z_ref, xbc_ref, dt_ref), _IN_SPLITS):
        ref[...] = p[:, off:off + width]
        off += width


def _inproj(x, mods, gain, w_in):
    row = lambda w: pl.BlockSpec((ROW_TILE, w), lambda i: (i, 0))
    return pl.pallas_call(
        _inproj_kernel,
        grid=(N_ROW_TILES,),
        in_specs=[row(D_MODEL),
                  pl.BlockSpec((1, 6, D_MODEL), lambda i: (_cond_of_tile(i), 0, 0)),
                  pl.BlockSpec((1, D_MODEL), lambda i: (0, 0)),
                  pl.BlockSpec((D_MODEL, IN_PAD), lambda i: (0, 0))],
        out_specs=[row(w) for w in _IN_SPLITS],
        out_shape=[jax.ShapeDtypeStruct((N_TOK, w), F32) for w in _IN_SPLITS],
        compiler_params=_cparams("parallel"),
        name="inproj",
    )(x, mods, gain, w_in)


def _softmax_pv(scores, values):
    m = scores[0].max(axis=-1, keepdims=True)
    for s in scores[1:]:
        m = jnp.maximum(m, s.max(axis=-1, keepdims=True))
    den = 0.0
    acc = 0.0
    for s, v in zip(scores, values):
        e = jnp.exp(s - m)
        den = den + e.sum(axis=-1, keepdims=True)
        acc = acc + _dot(e.astype(BF16), v)
    return acc / den


def _heads_rms(x, n_heads, g):
    return jnp.concatenate(
        [_rms(x[:, h * HEAD_DIM:(h + 1) * HEAD_DIM], g) for h in range(n_heads)], axis=-1)


def _rope(x, cos, sin_signed):
    w = x.shape[-1]
    lane = lax.broadcasted_iota(I32, x.shape, 1)
    partner = jnp.where((lane & 1) == 0, pltpu.roll(x, w - 1, 1), pltpu.roll(x, 1, 1))
    return x * cos + partner * sin_signed


_ATT_SCALE = HEAD_DIM ** -0.5


def _ctx_attn_kernel(qkv_ref, gqa_ref, qn_ref, kn_ref, nao_ref, go_ref, gk_ref):
    outs = []
    for h in range(NA_HEADS):
        sl = slice(h * HEAD_DIM, (h + 1) * HEAD_DIM)
        q = qkv_ref[:, sl].astype(BF16)
        k = qkv_ref[:, NA_DIM + h * HEAD_DIM:NA_DIM + (h + 1) * HEAD_DIM].astype(BF16)
        v = qkv_ref[:, 2 * NA_DIM + h * HEAD_DIM:2 * NA_DIM + (h + 1) * HEAD_DIM].astype(BF16)
        outs.append(_softmax_pv([_dot_nt(q, k) * _ATT_SCALE], [v]))
    nao_ref[...] = jnp.concatenate(outs, axis=-1).astype(BF16)

    gq = _heads_rms(gqa_ref[:, 0:GQA_Q_DIM], GQA_HEADS, qn_ref[...])
    gk = _heads_rms(gqa_ref[:, GQA_Q_DIM:GQA_Q_DIM + GQA_KV_DIM], GQA_KV_HEADS, kn_ref[...])
    gk_ref[...] = gk
    rep = GQA_HEADS // GQA_KV_HEADS
    outs = []
    for h in range(GQA_HEADS):
        g = h // rep
        q = gq[:, h * HEAD_DIM:(h + 1) * HEAD_DIM].astype(BF16)
        k = gk[:, g * HEAD_DIM:(g + 1) * HEAD_DIM].astype(BF16)
        v0 = GQA_Q_DIM + GQA_KV_DIM + g * HEAD_DIM
        v = gqa_ref[:, v0:v0 + HEAD_DIM].astype(BF16)
        outs.append(_softmax_pv([_dot_nt(q, k) * _ATT_SCALE], [v]))
    go_ref[...] = jnp.concatenate(outs, axis=-1).astype(BF16)


def _ctx_attn(qkv, gqa, q_norm, k_norm):
    return pl.pallas_call(
        _ctx_attn_kernel,
        grid=(BATCH,),
        in_specs=[pl.BlockSpec((SEQ, 3 * NA_DIM), lambda b: (b, 0)),
                  pl.BlockSpec((SEQ, GQA_Q_DIM + 2 * GQA_KV_DIM), lambda b: (b, 0)),
                  pl.BlockSpec((1, HEAD_DIM), lambda b: (0, 0)),
                  pl.BlockSpec((1, HEAD_DIM), lambda b: (0, 0))],
        out_specs=[pl.BlockSpec((SEQ, NA_DIM), lambda b: (b, 0)),
                   pl.BlockSpec((SEQ, GQA_Q_DIM), lambda b: (b, 0)),
                   pl.BlockSpec((SEQ, GQA_KV_DIM), lambda b: (b, 0))],
        out_shape=[jax.ShapeDtypeStruct((N_CTX_TOK, NA_DIM), BF16),
                   jax.ShapeDtypeStruct((N_CTX_TOK, GQA_Q_DIM), BF16),
                   jax.ShapeDtypeStruct((N_CTX_TOK, GQA_KV_DIM), F32)],
        compiler_params=_cparams("parallel"),
        name="ctx_attn",
    )(qkv, gqa, q_norm, k_norm)


GQA_TQ = 256
GQA_KEYS = PAST_LEN + DEC_SEQ


def _lat_gqa_kernel(gqa_ref, ck_ref, cv_ref, cos_ref, sin_ref, qn_ref, kn_ref, o_ref, kbuf, vbuf):
    qb = pl.program_id(1)

    @pl.when(qb == 0)
    def _():
        kbuf[0:PAST_LEN, :] = ck_ref[0].astype(BF16)
        vbuf[0:PAST_LEN, :] = cv_ref[0].astype(BF16)
        k = _heads_rms(gqa_ref[:, GQA_Q_DIM:GQA_Q_DIM + GQA_KV_DIM], GQA_KV_HEADS, kn_ref[...])
        k = _rope(k, cos_ref[:, 0:GQA_KV_DIM], sin_ref[:, 0:GQA_KV_DIM])
        kbuf[PAST_LEN:GQA_KEYS, :] = k.astype(BF16)
        vbuf[PAST_LEN:GQA_KEYS, :] = gqa_ref[:, GQA_Q_DIM + GQA_KV_DIM:].astype(BF16)

    r0 = pl.multiple_of(qb * GQA_TQ, GQA_TQ)
    q = _heads_rms(gqa_ref[pl.ds(r0, GQA_TQ), 0:GQA_Q_DIM], GQA_HEADS, qn_ref[...])
    q = _rope(q, cos_ref[pl.ds(r0, GQA_TQ), :], sin_ref[pl.ds(r0, GQA_TQ), :]).astype(BF16)
    rep = GQA_HEADS // GQA_KV_HEADS
    outs = []
    for h in range(GQA_HEADS):
        g = h // rep
        k = kbuf[:, g * HEAD_DIM:(g + 1) * HEAD_DIM]
        v = vbuf[:, g * HEAD_DIM:(g + 1) * HEAD_DIM]
        s = _dot_nt(q[:, h * HEAD_DIM:(h + 1) * HEAD_DIM], k) * _ATT_SCALE
        outs.append(_softmax_pv([s], [v]))
    o_ref[...] = jnp.concatenate(outs, axis=-1).astype(BF16)


def _lat_gqa(gqa, cache_k, cache_v, cos, sin, q_norm, k_norm):
    lat_blk = N_CTX_TOK // DEC_SEQ
    return pl.pallas_call(
        _lat_gqa_kernel,
        grid=(DEC_BATCH, DEC_SEQ // GQA_TQ),
        in_specs=[pl.BlockSpec((DEC_SEQ, GQA_Q_DIM + 2 * GQA_KV_DIM), lambda b, q: (lat_blk + b, 0)),
                  pl.BlockSpec((1, PAST_LEN, GQA_KV_DIM), lambda b, q: (b, 0, 0)),
                  pl.BlockSpec((1, PAST_LEN, GQA_KV_DIM), lambda b, q: (b, 0, 0)),
                  pl.BlockSpec((DEC_SEQ, GQA_Q_DIM), lambda b, q: (0, 0)),
                  pl.BlockSpec((DEC_SEQ, GQA_Q_DIM), lambda b, q: (0, 0)),
                  pl.BlockSpec((1, HEAD_DIM), lambda b, q: (0, 0)),
                  pl.BlockSpec((1, HEAD_DIM), lambda b, q: (0, 0))],
        out_specs=pl.BlockSpec((GQA_TQ, GQA_Q_DIM), lambda b, q: (b * (DEC_SEQ // GQA_TQ) + q, 0)),
        out_shape=jax.ShapeDtypeStruct((N_LAT_TOK, GQA_Q_DIM), BF16),
        scratch_shapes=[pltpu.VMEM((GQA_KEYS, GQA_KV_DIM), BF16),
                        pltpu.VMEM((GQA_KEYS, GQA_KV_DIM), BF16)],
        compiler_params=_cparams("arbitrary", "arbitrary"),
        name="lat_gqa",
    )(gqa, cache_k, cache_v, cos, sin, q_norm, k_norm)


def _rope_tables():
    t = jnp.arange(DEC_SEQ)
    row = (t // GRID_W).astype(F32)
    col = (t % GRID_W).astype(F32)
    axis_dim = HEAD_DIM // 2
    inv_freq = ROPE_THETA ** (-jnp.arange(0, axis_dim, 2, dtype=F32) / axis_dim)
    ang = jnp.concatenate([row[:, None] * inv_freq, col[:, None] * inv_freq], axis=-1)
    cos = jnp.repeat(jnp.cos(ang), 2, axis=-1)
    sin = jnp.repeat(jnp.sin(ang), 2, axis=-1) * jnp.tile(jnp.array([-1.0, 1.0], F32), HEAD_DIM // 2)
    return jnp.tile(cos, (1, GQA_HEADS)), jnp.tile(sin, (1, GQA_HEADS))


NA_ROWS = DEC_SEQ // GRID_W
NA_KEYS = NA_WIN_ROWS * GRID_W


NA_ROWS_PER_STEP = 2


def _lat_na_kernel(qkv_ref, ck_ref, cv_ref, *rest):
    bias_refs, o_ref = rest[:NA_ROWS_PER_STEP], rest[NA_ROWS_PER_STEP]
    for j in range(NA_ROWS_PER_STEP):
        r = pl.program_id(1) * NA_ROWS_PER_STEP + j
        r0 = jnp.clip(r - NA_WIN_ROWS // 2, 0, NA_ROWS - NA_WIN_ROWS)
        q0 = pl.multiple_of(r * GRID_W, GRID_W)
        k0 = pl.multiple_of(r0 * GRID_W, GRID_W)
        outs = []
        for h in range(NA_HEADS):
            c0 = h * HEAD_DIM
            q = qkv_ref[pl.ds(q0, GRID_W), c0:c0 + HEAD_DIM].astype(BF16)
            k = qkv_ref[pl.ds(k0, NA_KEYS), NA_DIM + c0:NA_DIM + c0 + HEAD_DIM].astype(BF16)
            v = qkv_ref[pl.ds(k0, NA_KEYS), 2 * NA_DIM + c0:2 * NA_DIM + c0 + HEAD_DIM].astype(BF16)
            kc = ck_ref[0, :, c0:c0 + HEAD_DIM].astype(BF16)
            vc = cv_ref[0, :, c0:c0 + HEAD_DIM].astype(BF16)
            s_nb = _dot_nt(q, k) * _ATT_SCALE + bias_refs[j][0, h]
            s_ctx = _dot_nt(q, kc) * _ATT_SCALE
            outs.append(_softmax_pv([s_nb, s_ctx], [v, vc]))
        o_ref[j * GRID_W:(j + 1) * GRID_W, :] = jnp.concatenate(outs, axis=-1).astype(BF16)


def _na_row_offset(r):
    return r - jnp.clip(r - NA_WIN_ROWS // 2, 0, NA_ROWS - NA_WIN_ROWS)


def _lat_na(qkv, cache_k, cache_v, bias):
    lat_blk = N_CTX_TOK // DEC_SEQ
    steps = NA_ROWS // NA_ROWS_PER_STEP
    bias_spec = lambda j: pl.BlockSpec((1, NA_HEADS, GRID_W, NA_KEYS),
                                       lambda b, s: (_na_row_offset(s * NA_ROWS_PER_STEP + j), 0, 0, 0))
    return pl.pallas_call(
        _lat_na_kernel,
        grid=(DEC_BATCH, steps),
        in_specs=[pl.BlockSpec((DEC_SEQ, 3 * NA_DIM), lambda b, s: (lat_blk + b, 0)),
                  pl.BlockSpec((1, PAST_LEN, NA_DIM), lambda b, s: (b, 0, 0)),
                  pl.BlockSpec((1, PAST_LEN, NA_DIM), lambda b, s: (b, 0, 0))]
                 + [bias_spec(j) for j in range(NA_ROWS_PER_STEP)],
        out_specs=pl.BlockSpec((NA_ROWS_PER_STEP * GRID_W, NA_DIM), lambda b, s: (b * steps + s, 0)),
        out_shape=jax.ShapeDtypeStruct((N_LAT_TOK, NA_DIM), BF16),
        compiler_params=_cparams("parallel", "arbitrary"),
        name="lat_na",
    )(qkv, cache_k, cache_v, *([bias] * NA_ROWS_PER_STEP))


def _na_bias_tables(rpb):
    d = np.arange(NA_WIN_ROWS)[:, None]
    kr = np.arange(NA_WIN_ROWS)[None, :]
    dr = kr - d + NA_WIN_ROWS - 1
    qc = np.arange(GRID_W)[:, None]
    kc = np.arange(GRID_W)[None, :]
    col0 = np.clip(qc - NA_WIN_COLS // 2, 0, GRID_W - NA_WIN_COLS)
    in_win = (kc >= col0) & (kc < col0 + NA_WIN_COLS)
    dc = np.clip(kc - qc + NA_WIN_COLS - 1, 0, 2 * NA_WIN_COLS - 2)
    row_hot = (dr[:, :, None] == np.arange(2 * NA_WIN_ROWS - 1)).astype(np.float32)
    col_hot = (dc[:, :, None] == np.arange(2 * NA_WIN_COLS - 1)).astype(np.float32)
    b = jnp.einsum('hac,dka,qxc->dhqkx', rpb.astype(F32), row_hot, col_hot, precision=lax.Precision.HIGHEST)
    b = jnp.where(in_win[None, None, :, None, :], b, NEG_INF)
    return b.reshape(NA_WIN_ROWS, NA_HEADS, GRID_W, NA_KEYS)


CONV_TB = 1024
CONV_HALO = 8
CONV_HALO_BLOCKS = CONV_TB // CONV_HALO


def _conv_kernel(prev_ref, x_ref, next_ref, w_ref, b_ref, o_ref):
    i = pl.program_id(0)
    seq = jnp.where(i < N_CTX_TOK // CONV_TB, SEQ, DEC_SEQ)
    x = x_ref[...]
    ext = jnp.concatenate([prev_ref[...], x, next_ref[...]], axis=0)
    n_ext = CONV_TB + 2 * CONV_HALO
    pos = (lax.broadcasted_iota(I32, (CONV_TB, 1), 0) + i * CONV_TB) & (seq - 1)
    half = CONV_W // 2
    acc = x * w_ref[half:half + 1, :]
    for s in range(-half, half + 1):
        if s == 0:
            continue
        shifted = pltpu.roll(ext, (-s) % n_ext, 0)[CONV_HALO:CONV_HALO + CONV_TB]
        valid = (pos + s >= 0) & (pos + s < seq)
        acc = acc + jnp.where(valid, shifted, 0.0) * w_ref[half + s:half + s + 1, :]
    acc = acc + b_ref[...]
    o_ref[...] = acc * _sigmoid(acc)


def _conv_act(xbc, conv_w, conv_b):
    return pl.pallas_call(
        _conv_kernel,
        grid=(N_TOK // CONV_TB,),
        in_specs=[pl.BlockSpec((CONV_HALO, CONV_DIM),
                               lambda i: (jnp.maximum(i * CONV_HALO_BLOCKS - 1, 0), 0)),
                  pl.BlockSpec((CONV_TB, CONV_DIM), lambda i: (i, 0)),
                  pl.BlockSpec((CONV_HALO, CONV_DIM),
                               lambda i: (jnp.minimum((i + 1) * CONV_HALO_BLOCKS, N_TOK // CONV_HALO - 1), 0)),
                  pl.BlockSpec((CONV_W, CONV_DIM), lambda i: (0, 0)),
                  pl.BlockSpec((1, CONV_DIM), lambda i: (0, 0))],
        out_specs=pl.BlockSpec((CONV_TB, CONV_DIM), lambda i: (i, 0)),
        out_shape=jax.ShapeDtypeStruct((N_TOK, CONV_DIM), F32),
        compiler_params=_cparams("parallel"),
        name="conv_act",
    )(xbc, xbc, xbc, conv_w, conv_b)


def _chunk_seq(g):
    ctx_n = SEQ // CHUNK
    lat_n = DEC_SEQ // CHUNK
    is_ctx = g < N_CTX_CHUNKS
    gl = g - N_CTX_CHUNKS
    sid = jnp.where(is_ctx, g // ctx_n, BATCH + gl // lat_n)
    cin = jnp.where(is_ctx, g % ctx_n, gl % lat_n)
    n = jnp.where(is_ctx, ctx_n, lat_n)
    return sid, cin, n


def _ssd_init(h0_ref, st_ref, gg, reverse):
    _, cin, n = _chunk_seq(gg)

    @pl.when(cin == (n - 1 if reverse else 0))
    def _():
        st_ref[0] = h0_ref[0]


def _ssd_chunk(xa_ref, dt_ref, dtb, alog, y_ref, st_ref, reverse):
    x = dt_ref[...] + dtb
    dt = jnp.maximum(x, 0.0) + jnp.log1p(jnp.exp(-jnp.abs(x)))
    dta = dt * -jnp.exp(alog)
    ii = lax.broadcasted_iota(I32, (CHUNK, CHUNK), 0)
    jj = lax.broadcasted_iota(I32, (CHUNK, CHUNK), 1)
    tri = (jj >= ii) if reverse else (jj <= ii)
    cum = _dot_exact(tri.astype(F32), dta)
    cum_t = cum.T
    edge = 0 if reverse else CHUNK - 1
    tot = cum[edge:edge + 1, :]
    rep = SSM_HEADS // SSM_GROUPS
    for grp in range(SSM_GROUPS):
        bg = xa_ref[:, SSM_INNER + grp * SSM_STATE:SSM_INNER + (grp + 1) * SSM_STATE].astype(BF16)
        c0 = SSM_INNER + SSM_BC_DIM + grp * SSM_STATE
        cg = xa_ref[:, c0:c0 + SSM_STATE].astype(BF16)
        cb = _dot_nt(cg, bg)
        for h in range(grp * rep, (grp + 1) * rep):
            hs = slice(h * SSM_HEAD_DIM, (h + 1) * SSM_HEAD_DIM)
            col = cum[:, h:h + 1]
            row = cum_t[h:h + 1, :]
            decay = jnp.where(tri, jnp.exp(jnp.minimum(col - row, 0.0)), 0.0)
            xdt = xa_ref[:, hs] * dt[:, h:h + 1]
            state = st_ref[0, hs, :]
            y = _dot((cb * decay).astype(BF16), xdt.astype(BF16))
            y = y + _dot_nt(cg, state.astype(BF16)) * jnp.exp(col)
            y_ref[:, hs] = y
            toth = tot[:, h:h + 1]
            w = (xdt * jnp.exp(toth - col)).astype(BF16)
            upd = lax.dot_general(w, bg, (((0,), (0,)), ((), ())), preferred_element_type=F32)
            st_ref[0, hs, :] = state * jnp.exp(toth) + upd


def _ssd_kernel(xaf_ref, dtf_ref, h0f_ref, xab_ref, dtb_ref, h0b_ref, bias_ref, alog_ref,
                yf_ref, stf_ref, yb_ref, stb_ref):
    g = pl.program_id(0)
    _ssd_init(h0f_ref, stf_ref, g, False)
    _ssd_chunk(xaf_ref, dtf_ref, bias_ref[0:1, :], alog_ref[0:1, :], yf_ref, stf_ref, False)
    _ssd_init(h0b_ref, stb_ref, N_CHUNKS - 1 - g, True)
    _ssd_chunk(xab_ref, dtb_ref, bias_ref[1:2, :], alog_ref[1:2, :], yb_ref, stb_ref, True)


def _ssd(xa, dt_raw, h0_fwd, h0_bwd, dt_bias, a_log):
    rev = lambda g: N_CHUNKS - 1 - g
    chunk = lambda w, order: pl.BlockSpec((CHUNK, w), lambda g: (order(g), 0))
    state = lambda order: pl.BlockSpec((1, SSM_INNER, SSM_STATE), lambda g: (_chunk_seq(order(g))[0], 0, 0))
    same = lambda g: g
    y_shape = jax.ShapeDtypeStruct((N_TOK, SSM_INNER), F32)
    st_shape = jax.ShapeDtypeStruct((N_SEQ, SSM_INNER, SSM_STATE), F32)
    return pl.pallas_call(
        _ssd_kernel,
        grid=(N_CHUNKS,),
        in_specs=[chunk(CONV_DIM, same), chunk(LANES, same), state(same),
                  chunk(CONV_DIM, rev), chunk(LANES, rev), state(rev),
                  pl.BlockSpec((2, LANES), lambda g: (0, 0)),
                  pl.BlockSpec((2, LANES), lambda g: (0, 0))],
        out_specs=[chunk(SSM_INNER, same), state(same), chunk(SSM_INNER, rev), state(rev)],
        out_shape=[y_shape, st_shape, y_shape, st_shape],
        compiler_params=_cparams("arbitrary"),
        name="ssd",
    )(xa, dt_raw, h0_fwd, xa, dt_raw, h0_bwd, dt_bias, a_log)


def _outproj_kernel(x_ref, nao_ref, go_ref, yf_ref, yb_ref, xs_ref, z_ref, dsk_ref, sn_ref, wo_ref,
                    m_ref, g2_ref, wr_ref, br_ref, xo_ref, h_ref, idx_ref, gate_ref, sel_ref):
    m = m_ref[0]
    z = z_ref[...]
    y = (yf_ref[...] + yb_ref[...] + xs_ref[...] * dsk_ref[...]) * (z * _sigmoid(z))
    s_o = _rms(y, sn_ref[...]).astype(BF16)
    mix = (_dot(nao_ref[...], wo_ref[0:NA_DIM, :])
           + _dot(go_ref[...], wo_ref[NA_DIM:NA_DIM + GQA_Q_DIM, :])
           + _dot(s_o, wo_ref[NA_DIM + GQA_Q_DIM:, :]))
    x = x_ref[...] + m[2:3] * mix
    xo_ref[...] = x
    h = _rms(x, g2_ref[...]) * (1.0 + m[4:5]) + m[3:4]
    h_ref[...] = h.astype(BF16)

    logits = _dot_exact(h, wr_ref[...]) + br_ref[...]
    lane = lax.broadcasted_iota(I32, logits.shape, 1).astype(F32)
    vals, idxs = [], []
    for _ in range(TOP_K):
        v = logits.max(axis=-1, keepdims=True)
        i = jnp.where(logits == v, lane, float(LANES)).min(axis=-1, keepdims=True)
        vals.append(v)
        idxs.append(i)
        logits = jnp.where(lane == i, -jnp.inf, logits)
    es = [jnp.exp(v - vals[0]) for v in vals]
    den = es[0] + es[1] + es[2] + es[3]
    idx_out = jnp.zeros(lane.shape, F32)
    gate_out = jnp.zeros(lane.shape, F32)
    sel = jnp.zeros(lane.shape, F32)
    for k in range(TOP_K):
        idx_out = jnp.where(lane == float(k), idxs[k], idx_out)
        gate_out = jnp.where(lane == float(k), es[k] / den, gate_out)
        sel = jnp.where(lane == idxs[k], 1.0, sel)
    idx_ref[...] = idx_out.astype(I32)
    gate_ref[...] = gate_out
    sel_ref[...] = sel.astype(BF16)


def _outproj(x, nao, go, yf, yb, xa, z, d_skip, ssm_norm, w_out, mods, gain2, w_router, b_router):
    row = lambda w: pl.BlockSpec((ROW_TILE, w), lambda i: (i, 0))
    full = lambda a, b: pl.BlockSpec((a, b), lambda i: (0, 0))
    return pl.pallas_call(
        _outproj_kernel,
        grid=(N_ROW_TILES,),
        in_specs=[row(D_MODEL), row(NA_DIM), row(GQA_Q_DIM), row(SSM_INNER), row(SSM_INNER),
                  row(SSM_INNER), row(SSM_INNER), full(1, SSM_INNER), full(1, SSM_INNER),
                  full(D_MIX, D_MODEL),
                  pl.BlockSpec((1, 6, D_MODEL), lambda i: (_cond_of_tile(i), 0, 0)),
                  full(1, D_MODEL), full(D_MODEL, LANES), full(1, LANES)],
        out_specs=[row(D_MODEL), row(D_MODEL), row(LANES), row(LANES), row(LANES)],
        out_shape=[jax.ShapeDtypeStruct((N_TOK, D_MODEL), F32),
                   jax.ShapeDtypeStruct((N_TOK, D_MODEL), BF16),
                   jax.ShapeDtypeStruct((N_TOK, LANES), I32),
                   jax.ShapeDtypeStruct((N_TOK, LANES), F32),
                   jax.ShapeDtypeStruct((N_TOK, LANES), BF16)],
        compiler_params=_cparams("parallel"),
        name="outproj_router",
    )(x, nao, go, yf, yb, xa, z, d_skip, ssm_norm, w_out, mods, gain2, w_router, b_router)


RANK_TB = 512


def _rank_kernel(sel_ref, rank_ref, cnt_ref, carry):
    @pl.when(pl.program_id(0) == 0)
    def _():
        carry[...] = jnp.zeros_like(carry)

    sel = sel_ref[...]
    ii = lax.broadcasted_iota(I32, (RANK_TB, RANK_TB), 0)
    jj = lax.broadcasted_iota(I32, (RANK_TB, RANK_TB), 1)
    before = (jj < ii).astype(BF16)
    rank_ref[...] = _dot(before, sel) + carry[0:1, :]
    carry[...] = carry[...] + _dot(jnp.ones((8, RANK_TB), BF16), sel)
    cnt_ref[...] = carry[...]


def _ranks(sel):
    return pl.pallas_call(
        _rank_kernel,
        grid=(N_TOK // RANK_TB,),
        in_specs=[pl.BlockSpec((RANK_TB, LANES), lambda i: (i, 0))],
        out_specs=[pl.BlockSpec((RANK_TB, LANES), lambda i: (i, 0)),
                   pl.BlockSpec((8, LANES), lambda i: (0, 0))],
        out_shape=[jax.ShapeDtypeStruct((N_TOK, LANES), F32),
                   jax.ShapeDtypeStruct((8, LANES), F32)],
        scratch_shapes=[pltpu.VMEM((8, LANES), F32)],
        compiler_params=_cparams("arbitrary"),
        name="moe_ranks",
    )(sel)


_RUN_PIECES = (256, 128, 64, 32, 16, 8)


def _run_dma(src, s0, dst, d0, n, sem, *, wait, fixed_src=False):
    for size in _RUN_PIECES:
        @pl.when((n & size) != 0)
        def _(size=size):
            done = n & ~(2 * size - 1)
            s = 0 if fixed_src else pl.multiple_of(s0 + done, RUN_ALIGN)
            d = pl.multiple_of(d0 + done, RUN_ALIGN)
            copy = pltpu.make_async_copy(src.at[pl.ds(s, size), :], dst.at[pl.ds(d, size), :], sem)
            if wait:
                copy.wait()
            else:
                copy.start()


_TILE_PIECES = (1024, 512, 256, 128, 64, 32, 16, 8)


def _wait_rows(buf, n, sem):
    for size in _TILE_PIECES:
        @pl.when((n & size) != 0)
        def _(size=size):
            pltpu.make_async_copy(buf.at[pl.ds(0, size), :], buf.at[pl.ds(0, size), :], sem).wait()


def _local_rows(idx_ref, rank_ref, base_ref):
    pos = rank_ref[...] + base_ref[0]
    lane = lax.broadcasted_iota(I32, pos.shape, 1)
    idx = idx_ref[...]
    return [jnp.sum(jnp.where(lane == idx[:, k:k + 1], pos, 0.0), axis=-1, keepdims=True) for k in range(TOP_K)]


def _dispatch_kernel(run_ref, loc_ref, glb_ref, tot_ref, fs_ref, fl_ref, nu_ref,
                     h_ref, idx_ref, rank_ref, base_ref, out_ref, xl_ref, zbuf, sems):
    i = pl.program_id(0)
    slot = i % 2
    xl = xl_ref.at[slot]
    sem = sems.at[slot]

    @pl.when(i == 0)
    def _():
        zbuf[...] = jnp.zeros_like(zbuf)
        for e in range(N_EXPERTS):
            _run_dma(zbuf, 0, out_ref, fs_ref[e], fl_ref[e], sem, wait=False, fixed_src=True)
        for e in range(N_EXPERTS):
            _run_dma(zbuf, 0, out_ref, fs_ref[e], fl_ref[e], sem, wait=True, fixed_src=True)

        def tail(b, carry):
            copy = pltpu.make_async_copy(zbuf, out_ref.at[pl.ds(pl.multiple_of(b * MOE_BM, MOE_BM), MOE_BM), :], sem)
            copy.start()
            copy.wait()
            return carry

        lax.fori_loop(nu_ref[0], MOE_NB, tail, 0)

    @pl.when(i >= 2)
    def _():
        _wait_rows(xl, tot_ref[i - 2], sem)

    rows = _local_rows(idx_ref, rank_ref, base_ref)
    p = lax.broadcasted_iota(I32, (ROW_TILE, LOCAL_ROWS), 1).astype(F32)
    hot = (p == rows[0])
    for k in range(1, TOP_K):
        hot = hot | (p == rows[k])
    xl[...] = lax.dot_general(hot.astype(BF16), h_ref[...].astype(BF16), (((0,), (0,)), ((), ())),
                              preferred_element_type=F32)
    for e in range(N_EXPERTS):
        j = i * N_EXPERTS + e
        _run_dma(xl, loc_ref[j], out_ref, glb_ref[j], run_ref[j], sem, wait=False)

    @pl.when(i == N_ROW_TILES - 1)
    def _():
        _wait_rows(xl, tot_ref[i], sem)
        _wait_rows(xl_ref.at[1 - slot], tot_ref[i - 1], sems.at[1 - slot])


def _dispatch(plan, h, idx, rank):
    tile = lambda w: pl.BlockSpec((ROW_TILE, w), lambda i, *_: (i, 0))
    return pl.pallas_call(
        _dispatch_kernel,
        grid_spec=pltpu.PrefetchScalarGridSpec(
            num_scalar_prefetch=7,
            grid=(N_ROW_TILES,),
            in_specs=[tile(D_MODEL), tile(LANES), tile(LANES),
                      pl.BlockSpec((1, 1, LANES), lambda i, *_: (i, 0, 0))],
            out_specs=pl.BlockSpec(memory_space=pl.ANY),
            scratch_shapes=[pltpu.VMEM((2, LOCAL_ROWS, D_MODEL), F32),
                            pltpu.VMEM((MOE_BM, D_MODEL), F32),
                            pltpu.SemaphoreType.DMA((2,))]),
        out_shape=jax.ShapeDtypeStruct((MOE_NB * MOE_BM, D_MODEL), F32),
        compiler_params=_cparams("arbitrary"),
        name="moe_dispatch",
    )(plan["run"], plan["local"], plan["global"], plan["total"], plan["fill_start"], plan["fill_len"],
      plan["n_used"], h, idx, rank, plan["base"])


def _expert_kernel(be_ref, nu_ref, new_ref, fe_ref, x_ref, wu_ref, bu_ref, wd_ref, bd_ref, y_ref, wu_s, wd_s):
    del fe_ref
    b = pl.program_id(0)
    used = b < nu_ref[0]

    @pl.when(jnp.logical_not(used))
    def _():
        y_ref[...] = jnp.zeros_like(y_ref)

    @pl.when(jnp.logical_and(used, new_ref[b] == 1))
    def _():
        r = lax.broadcasted_iota(I32, (UP_GROUP, UP_GROUP), 0)
        c = lax.broadcasted_iota(I32, (UP_GROUP, UP_GROUP), 1)
        src = jnp.where(c < UP_GROUP // 2, 2 * c, 2 * (c - UP_GROUP // 2) + 1)
        perm = (r == src).astype(BF16)
        for g in range(2 * D_FF // UP_GROUP):
            cols = slice(g * UP_GROUP, (g + 1) * UP_GROUP)
            wu_s[:, cols] = _dot(wu_ref[0, 0, :, cols].astype(BF16), perm).astype(BF16)
        wd_s[...] = wd_ref[0, 0].astype(BF16)

    @pl.when(used)
    def _():
        up = _dot(x_ref[...].astype(BF16), wu_s[...]) + bu_ref[0]
        half = UP_GROUP // 2
        acts = []
        for g in range(2 * D_FF // UP_GROUP):
            gate = jnp.minimum(up[:, g * UP_GROUP:g * UP_GROUP + half], SWIGLU_LIMIT)
            lin = jnp.clip(up[:, g * UP_GROUP + half:(g + 1) * UP_GROUP], -SWIGLU_LIMIT, SWIGLU_LIMIT)
            acts.append((gate * _sigmoid(SWIGLU_ALPHA * gate) * (lin + 1.0)).astype(BF16))
        y_ref[...] = _dot(jnp.concatenate(acts, axis=-1), wd_s[...]) + bd_ref[0]


def _experts(layer, plan, xs, w_up, b_up, w_down, b_down):
    blk = lambda b, nu: jnp.maximum(jnp.minimum(b, nu[0] - 1), 0)
    return pl.pallas_call(
        _expert_kernel,
        grid_spec=pltpu.PrefetchScalarGridSpec(
            num_scalar_prefetch=4,
            grid=(MOE_NB,),
            in_specs=[pl.BlockSpec((MOE_BM, D_MODEL), lambda b, be, nu, nw, fe: (blk(b, nu), 0)),
                      pl.BlockSpec((1, 1, D_MODEL, 2 * D_FF), lambda b, be, nu, nw, fe: (layer, fe[blk(b, nu)], 0, 0)),
                      pl.BlockSpec((1, 1, 2 * D_FF), lambda b, be, nu, nw, fe: (be[blk(b, nu)], 0, 0)),
                      pl.BlockSpec((1, 1, D_FF, D_MODEL), lambda b, be, nu, nw, fe: (layer, fe[blk(b, nu)], 0, 0)),
                      pl.BlockSpec((1, 1, D_MODEL), lambda b, be, nu, nw, fe: (be[blk(b, nu)], 0, 0))],
            out_specs=pl.BlockSpec((MOE_BM, D_MODEL), lambda b, be, nu, nw, fe: (b, 0)),
            scratch_shapes=[pltpu.VMEM((D_MODEL, 2 * D_FF), BF16),
                            pltpu.VMEM((D_FF, D_MODEL), BF16)]),
        out_shape=jax.ShapeDtypeStruct((MOE_NB * MOE_BM, D_MODEL), F32),
        compiler_params=_cparams("arbitrary"),
        name="moe_experts",
    )(plan["blk_expert"], plan["n_used"], plan["blk_new"], plan["blk_fetch"], xs, w_up, b_up, w_down, b_down)


def _combine_kernel(run_ref, loc_ref, glb_ref, tot_ref, x_ref, gate_ref, idx_ref, rank_ref, base_ref, m_ref, fn_ref,
                    ys_ref, o_ref, yl_ref, sems, *, final):
    i = pl.program_id(0)
    slot = i % 2

    def fetch(tile, buf):
        yl = yl_ref.at[buf]
        for e in range(N_EXPERTS):
            j = tile * N_EXPERTS + e
            _run_dma(ys_ref, glb_ref[j], yl, loc_ref[j], run_ref[j], sems.at[buf], wait=False)

        def clear(r, carry):
            yl[pl.ds(pl.multiple_of(r * RUN_ALIGN, RUN_ALIGN), RUN_ALIGN), :] = jnp.zeros((RUN_ALIGN, D_MODEL), F32)
            return carry

        lax.fori_loop(tot_ref[tile] // RUN_ALIGN, LOCAL_ROWS // RUN_ALIGN, clear, 0)

    @pl.when(i == 0)
    def _():
        fetch(0, 0)

    @pl.when(i + 1 < N_ROW_TILES)
    def _():
        fetch(i + 1, 1 - slot)

    rows = _local_rows(idx_ref, rank_ref, base_ref)
    gate = gate_ref[...]
    p = lax.broadcasted_iota(I32, (ROW_TILE, LOCAL_ROWS), 1).astype(F32)
    w = jnp.zeros((ROW_TILE, LOCAL_ROWS), F32)
    for k in range(TOP_K):
        w = jnp.where(p == rows[k], gate[:, k:k + 1], w)

    _wait_rows(yl_ref.at[slot], tot_ref[i], sems.at[slot])
    acc = _dot(w.astype(BF16), yl_ref[slot].astype(BF16))
    x = x_ref[...] + m_ref[0][5:6] * acc
    o_ref[...] = _rms(x, fn_ref[...]) if final else x


def _combine(plan, x, gates, idx, rank, mods, final_norm, ys, final):
    tile = lambda w: pl.BlockSpec((ROW_TILE, w), lambda i, *_: (i, 0))
    return pl.pallas_call(
        functools.partial(_combine_kernel, final=final),
        grid_spec=pltpu.PrefetchScalarGridSpec(
            num_scalar_prefetch=4,
            grid=(N_ROW_TILES,),
            in_specs=[tile(D_MODEL), tile(LANES), tile(LANES), tile(LANES),
                      pl.BlockSpec((1, 1, LANES), lambda i, *_: (i, 0, 0)),
                      pl.BlockSpec((1, 6, D_MODEL), lambda i, *_: (_cond_of_tile(i), 0, 0)),
                      pl.BlockSpec((1, D_MODEL), lambda i, *_: (0, 0)),
                      pl.BlockSpec(memory_space=pl.ANY)],
            out_specs=tile(D_MODEL),
            scratch_shapes=[pltpu.VMEM((2, LOCAL_ROWS, D_MODEL), F32),
                            pltpu.SemaphoreType.DMA((2,))]),
        out_shape=jax.ShapeDtypeStruct((N_TOK, D_MODEL), F32),
        compiler_params=_cparams("arbitrary"),
        name="moe_combine",
    )(plan["run"], plan["local"], plan["global"], plan["total"], x, gates, idx, rank, plan["base"], mods,
      final_norm, ys)


def _moe_plan(rank, cnt):
    first = rank[::ROW_TILE, :N_EXPERTS].astype(I32)
    total = cnt[0:1, :N_EXPERTS].astype(I32)
    run = jnp.concatenate([first[1:], total], axis=0) - first
    run = (run + RUN_ALIGN - 1) // RUN_ALIGN * RUN_ALIGN
    local = jnp.cumsum(run, axis=1) - run
    sizes = run.sum(axis=0)
    padded = (sizes + MOE_BM - 1) // MOE_BM * MOE_BM
    ends = jnp.cumsum(padded)
    starts = ends - padded
    glob = starts[None, :] + jnp.cumsum(run, axis=0) - run
    blk_start = jnp.arange(MOE_NB, dtype=I32) * MOE_BM
    blk_expert = jnp.minimum(jnp.sum(ends[None, :] <= blk_start[:, None], axis=1), N_EXPERTS - 1).astype(I32)
    blk_new = jnp.concatenate([jnp.ones((1,), I32), (blk_expert[1:] != blk_expert[:-1]).astype(I32)])
    blk = jnp.arange(MOE_NB, dtype=I32)
    later_start = (blk_new[None, :] == 1) & (blk[None, :] > blk[:, None])
    next_start = jnp.min(jnp.where(later_start, blk[None, :], MOE_NB), axis=1)
    next_hot = (blk[None, :] == next_start[:, None]).astype(I32)
    next_expert = jnp.where(next_start < MOE_NB, jnp.sum(next_hot * blk_expert[None, :], axis=1), blk_expert)
    blk_fetch = jnp.where(blk_new == 1, blk_expert, next_expert)
    base = _pad_lanes((local - first).astype(F32)).reshape(N_ROW_TILES, 1, LANES)
    return {"run": run.reshape(-1), "local": local.reshape(-1), "global": glob.reshape(-1), "total": run.sum(axis=1),
            "fill_start": starts + sizes, "fill_len": padded - sizes, "n_used": ends[-1:] // MOE_BM,
            "blk_expert": blk_expert, "blk_new": blk_new, "blk_fetch": blk_fetch, "base": base}


def _pad_lanes(v, fill=0.0):
    return jnp.pad(v, ((0, 0), (0, LANES - v.shape[-1])), constant_values=fill)


def kernel(x_prompt, x_sample, cache_na_k, cache_na_v, cache_gqa_k, cache_gqa_v, state_ssm, c, c_ctx, w_ada, b_ada, norm_mix, norm_ffn, w_in, na_rpb, gqa_q_norm, gqa_k_norm, ssm_conv_w, ssm_conv_b, ssm_dt_bias, ssm_a_log, ssm_d, ssm_norm, w_out, w_router, b_router, w_up, b_up, w_down, b_down, final_norm):
    x = jnp.concatenate([x_prompt.reshape(N_CTX_TOK, D_MODEL), x_sample.reshape(N_LAT_TOK, D_MODEL)], axis=0)
    conds = jnp.concatenate([c_ctx[None], c, jnp.zeros((COND_ROWS - N_COND, D_MODEL), F32)], axis=0)
    mods = _adaln(conds, w_ada, b_ada).reshape(DEPTH, COND_ROWS, 6, D_MODEL)
    cos, sin = _rope_tables()

    w_in_b = jnp.pad(w_in, ((0, 0), (0, 0), (0, IN_PAD - IN_DIM))).astype(BF16)
    w_out_b = w_out.astype(BF16)
    b_up_s = b_up.reshape(DEPTH, N_EXPERTS, 2 * D_FF // UP_GROUP, UP_GROUP // 2, 2)
    b_up_s = jnp.swapaxes(b_up_s, -1, -2).reshape(DEPTH, N_EXPERTS, 1, 2 * D_FF)
    b_down_s = b_down.reshape(DEPTH, N_EXPERTS, 1, D_MODEL)

    ctx_out = []
    for l in range(DEPTH):
        qkv, gqa, z, xbc, dt_raw = _inproj(x, mods[l], norm_mix[l][None], w_in_b[l])
        qn, kn = gqa_q_norm[l][None], gqa_k_norm[l][None]

        nao_c, go_c, gk_c = _ctx_attn(qkv, gqa, qn, kn)
        go_l = _lat_gqa(gqa, cache_gqa_k[:, l].reshape(DEC_BATCH, PAST_LEN, GQA_KV_DIM),
                        cache_gqa_v[:, l].reshape(DEC_BATCH, PAST_LEN, GQA_KV_DIM), cos, sin, qn, kn)
        nao_l = _lat_na(qkv, cache_na_k[:, l].reshape(DEC_BATCH, PAST_LEN, NA_DIM),
                        cache_na_v[:, l].reshape(DEC_BATCH, PAST_LEN, NA_DIM), _na_bias_tables(na_rpb[l]))

        xa = _conv_act(xbc, ssm_conv_w[l], ssm_conv_b[l][None])
        zeros = jnp.zeros((BATCH, SSM_INNER, SSM_STATE), F32)
        h0 = [jnp.concatenate([zeros, state_ssm[:, l, d].reshape(DEC_BATCH, SSM_INNER, SSM_STATE)], axis=0)
              for d in range(2)]
        y_f, st_f, y_b, st_b = _ssd(xa, dt_raw, h0[0], h0[1], _pad_lanes(ssm_dt_bias[l]), _pad_lanes(ssm_a_log[l]))
        ys = [y_f, y_b]
        sts = [st[:BATCH].reshape(BATCH, SSM_HEADS, SSM_HEAD_DIM, SSM_STATE) for st in (st_f, st_b)]

        x, h, top_idx, gates, sel = _outproj(
            x, jnp.concatenate([nao_c, nao_l], axis=0), jnp.concatenate([go_c, go_l], axis=0),
            ys[0], ys[1], xa, z, jnp.repeat(ssm_d[l], SSM_HEAD_DIM)[None], ssm_norm[l][None], w_out_b[l],
            mods[l], norm_ffn[l][None], _pad_lanes(w_router[l]), _pad_lanes(b_router[l][None], NEG_INF))

        rank, cnt = _ranks(sel)
        plan = _moe_plan(rank, cnt)
        y_sorted = _experts(l, plan, _dispatch(plan, h, top_idx, rank), w_up, b_up_s[l], w_down, b_down_s[l])
        x = _combine(plan, x, gates, top_idx, rank, mods[l], final_norm[None], y_sorted, final=(l == DEPTH - 1))

        ctx_out.append((
            qkv[:N_CTX_TOK, NA_DIM:2 * NA_DIM].reshape(BATCH, SEQ, NA_HEADS, HEAD_DIM),
            qkv[:N_CTX_TOK, 2 * NA_DIM:].reshape(BATCH, SEQ, NA_HEADS, HEAD_DIM),
            gk_c.reshape(BATCH, SEQ, GQA_KV_HEADS, HEAD_DIM),
            gqa[:N_CTX_TOK, GQA_Q_DIM + GQA_KV_DIM:].reshape(BATCH, SEQ, GQA_KV_HEADS, HEAD_DIM),
            jnp.stack(sts, axis=1)))

    y_prompt = x[:N_CTX_TOK].reshape(BATCH, SEQ, D_MODEL)
    y_sample = x[N_CTX_TOK:].reshape(DEC_BATCH, DEC_SEQ, D_MODEL)
    return (y_prompt, y_sample) + tuple(jnp.stack([e[i] for e in ctx_out], axis=1) for i in range(5))
```

```python
import functools

import numpy as np
import jax
import jax.numpy as jnp
from jax import lax
from jax.experimental import pallas as pl
from jax.experimental.pallas import tpu as pltpu

F32 = jnp.float32
BF16 = jnp.bfloat16
I32 = jnp.int32

D_MODEL = 1024
BATCH = 16
SEQ = 256
DEPTH = 2
DEC_BATCH = 2
DEC_SEQ = 2048
PAST_LEN = 512
GRID_W = 64
HEAD_DIM = 64
NA_HEADS = 4
NA_WIN_ROWS = 8
NA_WIN_COLS = 16
GQA_HEADS = 4
GQA_KV_HEADS = 2
ROPE_THETA = 10000.0
SSM_HEADS = 8
SSM_HEAD_DIM = 64
SSM_STATE = 64
SSM_GROUPS = 2
SSM_INNER = SSM_HEADS * SSM_HEAD_DIM
SSM_BC_DIM = SSM_GROUPS * SSM_STATE
CONV_DIM = SSM_INNER + 2 * SSM_BC_DIM
CONV_W = 5
CHUNK = 128
NA_DIM = NA_HEADS * HEAD_DIM
GQA_Q_DIM = GQA_HEADS * HEAD_DIM
GQA_KV_DIM = GQA_KV_HEADS * HEAD_DIM
D_MIX = NA_DIM + GQA_Q_DIM + SSM_INNER
IN_DIM = 3 * NA_DIM + GQA_Q_DIM + 2 * GQA_KV_DIM + SSM_INNER + CONV_DIM + SSM_HEADS
N_EXPERTS = 32
TOP_K = 4
D_FF = D_MODEL
SWIGLU_LIMIT = 7.0
SWIGLU_ALPHA = 1.702
EPS = 1e-6
NEG_INF = -1e30

LANES = 128
N_CTX_TOK = BATCH * SEQ
N_LAT_TOK = DEC_BATCH * DEC_SEQ
N_TOK = N_CTX_TOK + N_LAT_TOK
N_COND = 1 + DEC_BATCH
COND_ROWS = 16
IN_PAD = 3 * NA_DIM + GQA_Q_DIM + 2 * GQA_KV_DIM + SSM_INNER + CONV_DIM + LANES
ROW_TILE = 256
N_ROW_TILES = N_TOK // ROW_TILE
MOE_BM = 256
N_SLOTS = N_TOK * TOP_K
RUN_ALIGN = 8
MOE_TILE = 512
N_MOE_TILES = N_TOK // MOE_TILE
N_RUNS = N_MOE_TILES * N_EXPERTS
LOCAL_ROWS = -(-(MOE_TILE * TOP_K + N_EXPERTS * (RUN_ALIGN - 1)) // MOE_BM) * MOE_BM
MOE_NB = -(-(N_SLOTS + N_RUNS * (RUN_ALIGN - 1)) // MOE_BM) + N_EXPERTS
UP_GROUP = 256
N_SEQ = BATCH + DEC_BATCH
N_CHUNKS = N_TOK // CHUNK
N_CTX_CHUNKS = N_CTX_TOK // CHUNK
VMEM_LIMIT = 56 * 1024 * 1024


def _cparams(*sem):
    return pltpu.CompilerParams(dimension_semantics=sem, vmem_limit_bytes=VMEM_LIMIT)


def _sigmoid(x):
    return 1.0 / (1.0 + jnp.exp(-x))


def _dot(a, b):
    return jnp.dot(a, b, preferred_element_type=F32)


def _dot_nt(a, b):
    return lax.dot_general(a, b, (((1,), (1,)), ((), ())), preferred_element_type=F32)


def _dot_exact(a, b):
    return jnp.dot(a, b, preferred_element_type=F32, precision=lax.Precision.HIGHEST)


def _rms(x, g):
    return x * lax.rsqrt(jnp.mean(x * x, axis=-1, keepdims=True) + EPS) * g


def _cond_of_tile(i, rows=ROW_TILE):
    ctx_tiles = N_CTX_TOK // rows
    return jnp.where(i < ctx_tiles, 0, 1 + (i - ctx_tiles) // (DEC_SEQ // rows))


def _adaln_kernel(c_ref, w_ref, b_ref, o_ref):
    c = c_ref[...]
    s = (c * _sigmoid(c)).astype(BF16)
    o_ref[0] = _dot(s, w_ref[0].astype(BF16)) + b_ref[0]


def _adaln(conds, w_ada, b_ada):
    tn = 1536
    return pl.pallas_call(
        _adaln_kernel,
        grid=(DEPTH, 6 * D_MODEL // tn),
        in_specs=[pl.BlockSpec((COND_ROWS, D_MODEL), lambda l, j: (0, 0)),
                  pl.BlockSpec((1, D_MODEL, tn), lambda l, j: (l, 0, j)),
                  pl.BlockSpec((1, 1, tn), lambda l, j: (l, 0, j))],
        out_specs=pl.BlockSpec((1, COND_ROWS, tn), lambda l, j: (l, 0, j)),
        out_shape=jax.ShapeDtypeStruct((DEPTH, COND_ROWS, 6 * D_MODEL), F32),
        compiler_params=_cparams("parallel", "parallel"),
        name="adaln",
    )(conds, w_ada, b_ada.reshape(DEPTH, 1, 6 * D_MODEL))


_IN_SPLITS = (3 * NA_DIM, GQA_Q_DIM + 2 * GQA_KV_DIM, SSM_INNER, CONV_DIM, LANES)


def _inproj_kernel(x_ref, m_ref, g_ref, w_ref, qkv_ref, gqa_ref, z_ref, xbc_ref, dt_ref):
    m = m_ref[0]
    h = _rms(x_ref[...], g_ref[...]) * (1.0 + m[1:2]) + m[0:1]
    p = _dot(h.astype(BF16), w_ref[...])
    off = 0
    for ref, width in zip((qkv_ref, gqa_ref, z_ref, xbc_ref, dt_ref), _IN_SPLITS):
        ref[...] = p[:, off:off + width]
        off += width


def _inproj(x, mods, gain, w_in):
    row = lambda w: pl.BlockSpec((ROW_TILE, w), lambda i: (i, 0))
    return pl.pallas_call(
        _inproj_kernel,
        grid=(N_ROW_TILES,),
        in_specs=[row(D_MODEL),
                  pl.BlockSpec((1, 6, D_MODEL), lambda i: (_cond_of_tile(i), 0, 0)),
                  pl.BlockSpec((1, D_MODEL), lambda i: (0, 0)),
                  pl.BlockSpec((D_MODEL, IN_PAD), lambda i: (0, 0))],
        out_specs=[row(w) for w in _IN_SPLITS],
        out_shape=[jax.ShapeDtypeStruct((N_TOK, w), F32) for w in _IN_SPLITS],
        compiler_params=_cparams("parallel"),
        name="inproj",
    )(x, mods, gain, w_in)


def _softmax_pv(scores, values):
    m = scores[0].max(axis=-1, keepdims=True)
    for s in scores[1:]:
        m = jnp.maximum(m, s.max(axis=-1, keepdims=True))
    den = 0.0
    acc = 0.0
    for s, v in zip(scores, values):
        e = jnp.exp(s - m)
        den = den + e.sum(axis=-1, keepdims=True)
        acc = acc + _dot(e.astype(BF16), v)
    return acc / den


def _heads_rms(x, n_heads, g):
    return jnp.concatenate(
        [_rms(x[:, h * HEAD_DIM:(h + 1) * HEAD_DIM], g) for h in range(n_heads)], axis=-1)


def _rope(x, cos, sin_signed):
    w = x.shape[-1]
    lane = lax.broadcasted_iota(I32, x.shape, 1)
    partner = jnp.where((lane & 1) == 0, pltpu.roll(x, w - 1, 1), pltpu.roll(x, 1, 1))
    return x * cos + partner * sin_signed


_ATT_SCALE = HEAD_DIM ** -0.5


def _ctx_attn_kernel(qkv_ref, gqa_ref, qn_ref, kn_ref, nao_ref, go_ref, gk_ref):
    outs = []
    for h in range(NA_HEADS):
        sl = slice(h * HEAD_DIM, (h + 1) * HEAD_DIM)
        q = qkv_ref[:, sl].astype(BF16)
        k = qkv_ref[:, NA_DIM + h * HEAD_DIM:NA_DIM + (h + 1) * HEAD_DIM].astype(BF16)
        v = qkv_ref[:, 2 * NA_DIM + h * HEAD_DIM:2 * NA_DIM + (h + 1) * HEAD_DIM].astype(BF16)
        outs.append(_softmax_pv([_dot_nt(q, k) * _ATT_SCALE], [v]))
    nao_ref[...] = jnp.concatenate(outs, axis=-1).astype(BF16)

    gq = _heads_rms(gqa_ref[:, 0:GQA_Q_DIM], GQA_HEADS, qn_ref[...])
    gk = _heads_rms(gqa_ref[:, GQA_Q_DIM:GQA_Q_DIM + GQA_KV_DIM], GQA_KV_HEADS, kn_ref[...])
    gk_ref[...] = gk
    rep = GQA_HEADS // GQA_KV_HEADS
    outs = []
    for h in range(GQA_HEADS):
        g = h // rep
        q = gq[:, h * HEAD_DIM:(h + 1) * HEAD_DIM].astype(BF16)
        k = gk[:, g * HEAD_DIM:(g + 1) * HEAD_DIM].astype(BF16)
        v0 = GQA_Q_DIM + GQA_KV_DIM + g * HEAD_DIM
        v = gqa_ref[:, v0:v0 + HEAD_DIM].astype(BF16)
        outs.append(_softmax_pv([_dot_nt(q, k) * _ATT_SCALE], [v]))
    go_ref[...] = jnp.concatenate(outs, axis=-1).astype(BF16)


def _ctx_attn(qkv, gqa, q_norm, k_norm):
    return pl.pallas_call(
        _ctx_attn_kernel,
        grid=(BATCH,),
        in_specs=[pl.BlockSpec((SEQ, 3 * NA_DIM), lambda b: (b, 0)),
                  pl.BlockSpec((SEQ, GQA_Q_DIM + 2 * GQA_KV_DIM), lambda b: (b, 0)),
                  pl.BlockSpec((1, HEAD_DIM), lambda b: (0, 0)),
                  pl.BlockSpec((1, HEAD_DIM), lambda b: (0, 0))],
        out_specs=[pl.BlockSpec((SEQ, NA_DIM), lambda b: (b, 0)),
                   pl.BlockSpec((SEQ, GQA_Q_DIM), lambda b: (b, 0)),
                   pl.BlockSpec((SEQ, GQA_KV_DIM), lambda b: (b, 0))],
        out_shape=[jax.ShapeDtypeStruct((N_CTX_TOK, NA_DIM), BF16),
                   jax.ShapeDtypeStruct((N_CTX_TOK, GQA_Q_DIM), BF16),
                   jax.ShapeDtypeStruct((N_CTX_TOK, GQA_KV_DIM), F32)],
        compiler_params=_cparams("parallel"),
        name="ctx_attn",
    )(qkv, gqa, q_norm, k_norm)


GQA_TQ = 256
GQA_KEYS = PAST_LEN + DEC_SEQ


def _lat_gqa_kernel(gqa_ref, ck_ref, cv_ref, cos_ref, sin_ref, qn_ref, kn_ref, o_ref, kbuf, vbuf):
    qb = pl.program_id(1)

    @pl.when(qb == 0)
    def _():
        kbuf[0:PAST_LEN, :] = ck_ref[0].astype(BF16)
        vbuf[0:PAST_LEN, :] = cv_ref[0].astype(BF16)
        k = _heads_rms(gqa_ref[:, GQA_Q_DIM:GQA_Q_DIM + GQA_KV_DIM], GQA_KV_HEADS, kn_ref[...])
        k = _rope(k, cos_ref[:, 0:GQA_KV_DIM], sin_ref[:, 0:GQA_KV_DIM])
        kbuf[PAST_LEN:GQA_KEYS, :] = k.astype(BF16)
        vbuf[PAST_LEN:GQA_KEYS, :] = gqa_ref[:, GQA_Q_DIM + GQA_KV_DIM:].astype(BF16)

    r0 = pl.multiple_of(qb * GQA_TQ, GQA_TQ)
    q = _heads_rms(gqa_ref[pl.ds(r0, GQA_TQ), 0:GQA_Q_DIM], GQA_HEADS, qn_ref[...])
    q = _rope(q, cos_ref[pl.ds(r0, GQA_TQ), :], sin_ref[pl.ds(r0, GQA_TQ), :]).astype(BF16)
    rep = GQA_HEADS // GQA_KV_HEADS
    outs = []
    for h in range(GQA_HEADS):
        g = h // rep
        k = kbuf[:, g * HEAD_DIM:(g + 1) * HEAD_DIM]
        v = vbuf[:, g * HEAD_DIM:(g + 1) * HEAD_DIM]
        s = _dot_nt(q[:, h * HEAD_DIM:(h + 1) * HEAD_DIM], k) * _ATT_SCALE
        outs.append(_softmax_pv([s], [v]))
    o_ref[...] = jnp.concatenate(outs, axis=-1).astype(BF16)


def _lat_gqa(gqa, cache_k, cache_v, cos, sin, q_norm, k_norm):
    lat_blk = N_CTX_TOK // DEC_SEQ
    return pl.pallas_call(
        _lat_gqa_kernel,
        grid=(DEC_BATCH, DEC_SEQ // GQA_TQ),
        in_specs=[pl.BlockSpec((DEC_SEQ, GQA_Q_DIM + 2 * GQA_KV_DIM), lambda b, q: (lat_blk + b, 0)),
                  pl.BlockSpec((1, PAST_LEN, GQA_KV_DIM), lambda b, q: (b, 0, 0)),
                  pl.BlockSpec((1, PAST_LEN, GQA_KV_DIM), lambda b, q: (b, 0, 0)),
                  pl.BlockSpec((DEC_SEQ, GQA_Q_DIM), lambda b, q: (0, 0)),
                  pl.BlockSpec((DEC_SEQ, GQA_Q_DIM), lambda b, q: (0, 0)),
                  pl.BlockSpec((1, HEAD_DIM), lambda b, q: (0, 0)),
                  pl.BlockSpec((1, HEAD_DIM), lambda b, q: (0, 0))],
        out_specs=pl.BlockSpec((GQA_TQ, GQA_Q_DIM), lambda b, q: (b * (DEC_SEQ // GQA_TQ) + q, 0)),
        out_shape=jax.ShapeDtypeStruct((N_LAT_TOK, GQA_Q_DIM), BF16),
        scratch_shapes=[pltpu.VMEM((GQA_KEYS, GQA_KV_DIM), BF16),
                        pltpu.VMEM((GQA_KEYS, GQA_KV_DIM), BF16)],
        compiler_params=_cparams("arbitrary", "arbitrary"),
        name="lat_gqa",
    )(gqa, cache_k, cache_v, cos, sin, q_norm, k_norm)


def _rope_tables():
    t = jnp.arange(DEC_SEQ)
    row = (t // GRID_W).astype(F32)
    col = (t % GRID_W).astype(F32)
    axis_dim = HEAD_DIM // 2
    inv_freq = ROPE_THETA ** (-jnp.arange(0, axis_dim, 2, dtype=F32) / axis_dim)
    ang = jnp.concatenate([row[:, None] * inv_freq, col[:, None] * inv_freq], axis=-1)
    cos = jnp.repeat(jnp.cos(ang), 2, axis=-1)
    sin = jnp.repeat(jnp.sin(ang), 2, axis=-1) * jnp.tile(jnp.array([-1.0, 1.0], F32), HEAD_DIM // 2)
    return jnp.tile(cos, (1, GQA_HEADS)), jnp.tile(sin, (1, GQA_HEADS))


NA_ROWS = DEC_SEQ // GRID_W
NA_KEYS = NA_WIN_ROWS * GRID_W


NA_ROWS_PER_STEP = 2


def _lat_na_kernel(qkv_ref, ck_ref, cv_ref, *rest):
    bias_refs, o_ref = rest[:NA_ROWS_PER_STEP], rest[NA_ROWS_PER_STEP]
    for j in range(NA_ROWS_PER_STEP):
        r = pl.program_id(1) * NA_ROWS_PER_STEP + j
        r0 = jnp.clip(r - NA_WIN_ROWS // 2, 0, NA_ROWS - NA_WIN_ROWS)
        q0 = pl.multiple_of(r * GRID_W, GRID_W)
        k0 = pl.multiple_of(r0 * GRID_W, GRID_W)
        outs = []
        for h in range(NA_HEADS):
            c0 = h * HEAD_DIM
            q = qkv_ref[pl.ds(q0, GRID_W), c0:c0 + HEAD_DIM].astype(BF16)
            k = qkv_ref[pl.ds(k0, NA_KEYS), NA_DIM + c0:NA_DIM + c0 + HEAD_DIM].astype(BF16)
            v = qkv_ref[pl.ds(k0, NA_KEYS), 2 * NA_DIM + c0:2 * NA_DIM + c0 + HEAD_DIM].astype(BF16)
            kc = ck_ref[0, :, c0:c0 + HEAD_DIM].astype(BF16)
            vc = cv_ref[0, :, c0:c0 + HEAD_DIM].astype(BF16)
            s_nb = _dot_nt(q, k) * _ATT_SCALE + bias_refs[j][0, h]
            s_ctx = _dot_nt(q, kc) * _ATT_SCALE
            outs.append(_softmax_pv([s_nb, s_ctx], [v, vc]))
        o_ref[j * GRID_W:(j + 1) * GRID_W, :] = jnp.concatenate(outs, axis=-1).astype(BF16)


def _na_row_offset(r):
    return r - jnp.clip(r - NA_WIN_ROWS // 2, 0, NA_ROWS - NA_WIN_ROWS)


def _lat_na(qkv, cache_k, cache_v, bias):
    lat_blk = N_CTX_TOK // DEC_SEQ
    steps = NA_ROWS // NA_ROWS_PER_STEP
    bias_spec = lambda j: pl.BlockSpec((1, NA_HEADS, GRID_W, NA_KEYS),
                                       lambda b, s: (_na_row_offset(s * NA_ROWS_PER_STEP + j), 0, 0, 0))
    return pl.pallas_call(
        _lat_na_kernel,
        grid=(DEC_BATCH, steps),
        in_specs=[pl.BlockSpec((DEC_SEQ, 3 * NA_DIM), lambda b, s: (lat_blk + b, 0)),
                  pl.BlockSpec((1, PAST_LEN, NA_DIM), lambda b, s: (b, 0, 0)),
                  pl.BlockSpec((1, PAST_LEN, NA_DIM), lambda b, s: (b, 0, 0))]
                 + [bias_spec(j) for j in range(NA_ROWS_PER_STEP)],
        out_specs=pl.BlockSpec((NA_ROWS_PER_STEP * GRID_W, NA_DIM), lambda b, s: (b * steps + s, 0)),
        out_shape=jax.ShapeDtypeStruct((N_LAT_TOK, NA_DIM), BF16),
        compiler_params=_cparams("parallel", "arbitrary"),
        name="lat_na",
    )(qkv, cache_k, cache_v, *([bias] * NA_ROWS_PER_STEP))


def _na_bias_tables(rpb):
    d = np.arange(NA_WIN_ROWS)[:, None]
    kr = np.arange(NA_WIN_ROWS)[None, :]
    dr = kr - d + NA_WIN_ROWS - 1
    qc = np.arange(GRID_W)[:, None]
    kc = np.arange(GRID_W)[None, :]
    col0 = np.clip(qc - NA_WIN_COLS // 2, 0, GRID_W - NA_WIN_COLS)
    in_win = (kc >= col0) & (kc < col0 + NA_WIN_COLS)
    dc = np.clip(kc - qc + NA_WIN_COLS - 1, 0, 2 * NA_WIN_COLS - 2)
    row_hot = (dr[:, :, None] == np.arange(2 * NA_WIN_ROWS - 1)).astype(np.float32)
    col_hot = (dc[:, :, None] == np.arange(2 * NA_WIN_COLS - 1)).astype(np.float32)
    b = jnp.einsum('hac,dka,qxc->dhqkx', rpb.astype(F32), row_hot, col_hot, precision=lax.Precision.HIGHEST)
    b = jnp.where(in_win[None, None, :, None, :], b, NEG_INF)
    return b.reshape(NA_WIN_ROWS, NA_HEADS, GRID_W, NA_KEYS)


CONV_TB = 1024
CONV_HALO = 8
CONV_HALO_BLOCKS = CONV_TB // CONV_HALO


def _conv_kernel(prev_ref, x_ref, next_ref, w_ref, b_ref, o_ref):
    i = pl.program_id(0)
    seq = jnp.where(i < N_CTX_TOK // CONV_TB, SEQ, DEC_SEQ)
    x = x_ref[...]
    ext = jnp.concatenate([prev_ref[...], x, next_ref[...]], axis=0)
    n_ext = CONV_TB + 2 * CONV_HALO
    pos = (lax.broadcasted_iota(I32, (CONV_TB, 1), 0) + i * CONV_TB) & (seq - 1)
    half = CONV_W // 2
    acc = x * w_ref[half:half + 1, :]
    for s in range(-half, half + 1):
        if s == 0:
            continue
        shifted = pltpu.roll(ext, (-s) % n_ext, 0)[CONV_HALO:CONV_HALO + CONV_TB]
        valid = (pos + s >= 0) & (pos + s < seq)
        acc = acc + jnp.where(valid, shifted, 0.0) * w_ref[half + s:half + s + 1, :]
    acc = acc + b_ref[...]
    o_ref[...] = acc * _sigmoid(acc)


def _conv_act(xbc, conv_w, conv_b):
    return pl.pallas_call(
        _conv_kernel,
        grid=(N_TOK // CONV_TB,),
        in_specs=[pl.BlockSpec((CONV_HALO, CONV_DIM),
                               lambda i: (jnp.maximum(i * CONV_HALO_BLOCKS - 1, 0), 0)),
                  pl.BlockSpec((CONV_TB, CONV_DIM), lambda i: (i, 0)),
                  pl.BlockSpec((CONV_HALO, CONV_DIM),
                               lambda i: (jnp.minimum((i + 1) * CONV_HALO_BLOCKS, N_TOK // CONV_HALO - 1), 0)),
                  pl.BlockSpec((CONV_W, CONV_DIM), lambda i: (0, 0)),
                  pl.BlockSpec((1, CONV_DIM), lambda i: (0, 0))],
        out_specs=pl.BlockSpec((CONV_TB, CONV_DIM), lambda i: (i, 0)),
        out_shape=jax.ShapeDtypeStruct((N_TOK, CONV_DIM), F32),
        compiler_params=_cparams("parallel"),
        name="conv_act",
    )(xbc, xbc, xbc, conv_w, conv_b)


def _chunk_seq(g):
    ctx_n = SEQ // CHUNK
    lat_n = DEC_SEQ // CHUNK
    is_ctx = g < N_CTX_CHUNKS
    gl = g - N_CTX_CHUNKS
    sid = jnp.where(is_ctx, g // ctx_n, BATCH + gl // lat_n)
    cin = jnp.where(is_ctx, g % ctx_n, gl % lat_n)
    n = jnp.where(is_ctx, ctx_n, lat_n)
    return sid, cin, n


def _ssd_init(h0_ref, st_ref, gg, reverse):
    _, cin, n = _chunk_seq(gg)

    @pl.when(cin == (n - 1 if reverse else 0))
    def _():
        st_ref[0] = h0_ref[0]


def _ssd_chunk(xa_ref, dt_ref, dtb, alog, y_ref, st_ref, reverse):
    x = dt_ref[...] + dtb
    dt = jnp.maximum(x, 0.0) + jnp.log1p(jnp.exp(-jnp.abs(x)))
    dta = dt * -jnp.exp(alog)
    ii = lax.broadcasted_iota(I32, (CHUNK, CHUNK), 0)
    jj = lax.broadcasted_iota(I32, (CHUNK, CHUNK), 1)
    tri = (jj >= ii) if reverse else (jj <= ii)
    cum = _dot_exact(tri.astype(F32), dta)
    cum_t = cum.T
    edge = 0 if reverse else CHUNK - 1
    tot = cum[edge:edge + 1, :]
    first = jj < SSM_HEAD_DIM
    first_rows = lax.broadcasted_iota(I32, (2 * SSM_HEAD_DIM, SSM_STATE), 0) < SSM_HEAD_DIM
    rep = SSM_HEADS // SSM_GROUPS
    for grp in range(SSM_GROUPS):
        bg = xa_ref[:, SSM_INNER + grp * SSM_STATE:SSM_INNER + (grp + 1) * SSM_STATE].astype(BF16)
        c0 = SSM_INNER + SSM_BC_DIM + grp * SSM_STATE
        cg = xa_ref[:, c0:c0 + SSM_STATE].astype(BF16)
        cb = _dot_nt(cg, bg)
        for ha in range(grp * rep, (grp + 1) * rep, 2):
            hb = ha + 1
            slab = slice(ha * SSM_HEAD_DIM, (hb + 1) * SSM_HEAD_DIM)
            col_a, col_b = cum[:, ha:ha + 1], cum[:, hb:hb + 1]
            tot_a, tot_b = tot[:, ha:ha + 1], tot[:, hb:hb + 1]
            decay_a = jnp.where(tri, jnp.exp(jnp.minimum(col_a - cum_t[ha:ha + 1, :], 0.0)), 0.0)
            decay_b = jnp.where(tri, jnp.exp(jnp.minimum(col_b - cum_t[hb:hb + 1, :], 0.0)), 0.0)
            xdt = xa_ref[:, slab] * jnp.where(first, dt[:, ha:ha + 1], dt[:, hb:hb + 1])
            xdt_b = xdt.astype(BF16)
            y = jnp.where(first, _dot((cb * decay_a).astype(BF16), xdt_b), _dot((cb * decay_b).astype(BF16), xdt_b))
            state = st_ref[0, slab, :]
            y = y + _dot_nt(cg, state.astype(BF16)) * jnp.where(first, jnp.exp(col_a), jnp.exp(col_b))
            y_ref[:, slab] = y
            to_end = jnp.where(first, jnp.exp(tot_a - col_a), jnp.exp(tot_b - col_b))
            upd = lax.dot_general((xdt * to_end).astype(BF16), bg, (((0,), (0,)), ((), ())),
                                  preferred_element_type=F32)
            st_ref[0, slab, :] = state * jnp.where(first_rows, jnp.exp(tot_a), jnp.exp(tot_b)) + upd


def _ssd_kernel(xaf_ref, dtf_ref, h0f_ref, xab_ref, dtb_ref, h0b_ref, bias_ref, alog_ref,
                yf_ref, stf_ref, yb_ref, stb_ref):
    g = pl.program_id(0)
    _ssd_init(h0f_ref, stf_ref, g, False)
    _ssd_chunk(xaf_ref, dtf_ref, bias_ref[0:1, :], alog_ref[0:1, :], yf_ref, stf_ref, False)
    _ssd_init(h0b_ref, stb_ref, N_CHUNKS - 1 - g, True)
    _ssd_chunk(xab_ref, dtb_ref, bias_ref[1:2, :], alog_ref[1:2, :], yb_ref, stb_ref, True)


def _ssd(xa, dt_raw, h0_fwd, h0_bwd, dt_bias, a_log):
    rev = lambda g: N_CHUNKS - 1 - g
    chunk = lambda w, order: pl.BlockSpec((CHUNK, w), lambda g: (order(g), 0))
    state = lambda order: pl.BlockSpec((1, SSM_INNER, SSM_STATE), lambda g: (_chunk_seq(order(g))[0], 0, 0))
    same = lambda g: g
    y_shape = jax.ShapeDtypeStruct((N_TOK, SSM_INNER), F32)
    st_shape = jax.ShapeDtypeStruct((N_SEQ, SSM_INNER, SSM_STATE), F32)
    return pl.pallas_call(
        _ssd_kernel,
        grid=(N_CHUNKS,),
        in_specs=[chunk(CONV_DIM, same), chunk(LANES, same), state(same),
                  chunk(CONV_DIM, rev), chunk(LANES, rev), state(rev),
                  pl.BlockSpec((2, LANES), lambda g: (0, 0)),
                  pl.BlockSpec((2, LANES), lambda g: (0, 0))],
        out_specs=[chunk(SSM_INNER, same), state(same), chunk(SSM_INNER, rev), state(rev)],
        out_shape=[y_shape, st_shape, y_shape, st_shape],
        compiler_params=_cparams("arbitrary"),
        name="ssd",
    )(xa, dt_raw, h0_fwd, xa, dt_raw, h0_bwd, dt_bias, a_log)


def _outproj_kernel(x_ref, nao_ref, go_ref, yf_ref, yb_ref, xs_ref, z_ref, dsk_ref, sn_ref, wo_ref,
                    m_ref, g2_ref, wr_ref, br_ref, xo_ref, h_ref, idx_ref, gate_ref, sel_ref):
    m = m_ref[0]
    z = z_ref[...]
    y = (yf_ref[...] + yb_ref[...] + xs_ref[...] * dsk_ref[...]) * (z * _sigmoid(z))
    s_o = _rms(y, sn_ref[...]).astype(BF16)
    mix = (_dot(nao_ref[...], wo_ref[0:NA_DIM, :])
           + _dot(go_ref[...], wo_ref[NA_DIM:NA_DIM + GQA_Q_DIM, :])
           + _dot(s_o, wo_ref[NA_DIM + GQA_Q_DIM:, :]))
    x = x_ref[...] + m[2:3] * mix
    xo_ref[...] = x
    h = _rms(x, g2_ref[...]) * (1.0 + m[4:5]) + m[3:4]
    h_ref[...] = h.astype(BF16)

    logits = _dot_exact(h, wr_ref[...]) + br_ref[...]
    lane = lax.broadcasted_iota(I32, logits.shape, 1).astype(F32)
    vals, idxs = [], []
    for _ in range(TOP_K):
        v = logits.max(axis=-1, keepdims=True)
        i = jnp.where(logits == v, lane, float(LANES)).min(axis=-1, keepdims=True)
        vals.append(v)
        idxs.append(i)
        logits = jnp.where(lane == i, -jnp.inf, logits)
    es = [jnp.exp(v - vals[0]) for v in vals]
    den = es[0] + es[1] + es[2] + es[3]
    idx_out = jnp.zeros(lane.shape, F32)
    gate_out = jnp.zeros(lane.shape, F32)
    sel = jnp.zeros(lane.shape, F32)
    for k in range(TOP_K):
        idx_out = jnp.where(lane == float(k), idxs[k], idx_out)
        gate_out = jnp.where(lane == float(k), es[k] / den, gate_out)
        sel = jnp.where(lane == idxs[k], 1.0, sel)
    idx_ref[...] = idx_out.astype(I32)
    gate_ref[...] = gate_out
    sel_ref[...] = sel.astype(BF16)


def _outproj(x, nao, go, yf, yb, xa, z, d_skip, ssm_norm, w_out, mods, gain2, w_router, b_router):
    row = lambda w: pl.BlockSpec((ROW_TILE, w), lambda i: (i, 0))
    full = lambda a, b: pl.BlockSpec((a, b), lambda i: (0, 0))
    return pl.pallas_call(
        _outproj_kernel,
        grid=(N_ROW_TILES,),
        in_specs=[row(D_MODEL), row(NA_DIM), row(GQA_Q_DIM), row(SSM_INNER), row(SSM_INNER),
                  row(SSM_INNER), row(SSM_INNER), full(1, SSM_INNER), full(1, SSM_INNER),
                  full(D_MIX, D_MODEL),
                  pl.BlockSpec((1, 6, D_MODEL), lambda i: (_cond_of_tile(i), 0, 0)),
                  full(1, D_MODEL), full(D_MODEL, LANES), full(1, LANES)],
        out_specs=[row(D_MODEL), row(D_MODEL), row(LANES), row(LANES), row(LANES)],
        out_shape=[jax.ShapeDtypeStruct((N_TOK, D_MODEL), F32),
                   jax.ShapeDtypeStruct((N_TOK, D_MODEL), BF16),
                   jax.ShapeDtypeStruct((N_TOK, LANES), I32),
                   jax.ShapeDtypeStruct((N_TOK, LANES), F32),
                   jax.ShapeDtypeStruct((N_TOK, LANES), BF16)],
        compiler_params=_cparams("parallel"),
        name="outproj_router",
    )(x, nao, go, yf, yb, xa, z, d_skip, ssm_norm, w_out, mods, gain2, w_router, b_router)


RANK_TB = 512


def _rank_kernel(sel_ref, rank_ref, cnt_ref, carry):
    @pl.when(pl.program_id(0) == 0)
    def _():
        carry[...] = jnp.zeros_like(carry)

    sel = sel_ref[...]
    ii = lax.broadcasted_iota(I32, (RANK_TB, RANK_TB), 0)
    jj = lax.broadcasted_iota(I32, (RANK_TB, RANK_TB), 1)
    before = (jj < ii).astype(BF16)
    rank_ref[...] = _dot(before, sel) + carry[0:1, :]
    carry[...] = carry[...] + _dot(jnp.ones((8, RANK_TB), BF16), sel)
    cnt_ref[...] = carry[...]


def _ranks(sel):
    return pl.pallas_call(
        _rank_kernel,
        grid=(N_TOK // RANK_TB,),
        in_specs=[pl.BlockSpec((RANK_TB, LANES), lambda i: (i, 0))],
        out_specs=[pl.BlockSpec((RANK_TB, LANES), lambda i: (i, 0)),
                   pl.BlockSpec((8, LANES), lambda i: (0, 0))],
        out_shape=[jax.ShapeDtypeStruct((N_TOK, LANES), F32),
                   jax.ShapeDtypeStruct((8, LANES), F32)],
        scratch_shapes=[pltpu.VMEM((8, LANES), F32)],
        compiler_params=_cparams("arbitrary"),
        name="moe_ranks",
    )(sel)


_RUN_PIECES = (512, 256, 128, 64, 32, 16, 8)


def _run_dma(src, s0, dst, d0, n, sem, *, wait, fixed_src=False, pieces=_RUN_PIECES):
    for size in pieces:
        @pl.when((n & size) != 0)
        def _(size=size):
            done = n & ~(2 * size - 1)
            s = 0 if fixed_src else pl.multiple_of(s0 + done, RUN_ALIGN)
            d = pl.multiple_of(d0 + done, RUN_ALIGN)
            copy = pltpu.make_async_copy(src.at[pl.ds(s, size), :], dst.at[pl.ds(d, size), :], sem)
            if wait:
                copy.wait()
            else:
                copy.start()


_TILE_PIECES = (2048, 1024, 512, 256, 128, 64, 32, 16, 8)


def _wait_rows(buf, n, sem):
    for size in _TILE_PIECES:
        @pl.when((n & size) != 0)
        def _(size=size):
            pltpu.make_async_copy(buf.at[pl.ds(0, size), :], buf.at[pl.ds(0, size), :], sem).wait()


def _local_rows(idx_ref, rank_ref, base_ref):
    pos = rank_ref[...] + base_ref[0]
    lane = lax.broadcasted_iota(I32, pos.shape, 1)
    idx = idx_ref[...]
    return [jnp.sum(jnp.where(lane == idx[:, k:k + 1], pos, 0.0), axis=-1, keepdims=True) for k in range(TOP_K)]


def _dispatch_kernel(run_ref, loc_ref, glb_ref, tot_ref, fs_ref, fl_ref, nu_ref,
                     h_ref, idx_ref, rank_ref, base_ref, out_ref, xl_ref, zbuf, sems):
    i = pl.program_id(0)
    slot = i % 2
    xl = xl_ref.at[slot]
    sem = sems.at[slot]

    @pl.when(i == 0)
    def _():
        zbuf[...] = jnp.zeros_like(zbuf)
        fill_pieces = tuple(s for s in _RUN_PIECES if s < MOE_BM)
        for wait in (False, True):
            for e in range(N_EXPERTS):
                _run_dma(zbuf, 0, out_ref, fs_ref[e], fl_ref[e], sem, wait=wait, fixed_src=True, pieces=fill_pieces)

        def tail(b, carry):
            copy = pltpu.make_async_copy(zbuf, out_ref.at[pl.ds(pl.multiple_of(b * MOE_BM, MOE_BM), MOE_BM), :], sem)
            copy.start()
            copy.wait()
            return carry

        lax.fori_loop(nu_ref[0], MOE_NB, tail, 0)

    @pl.when(i >= 2)
    def _():
        _wait_rows(xl, tot_ref[i - 2], sem)

    rows = _local_rows(idx_ref, rank_ref, base_ref)
    p = lax.broadcasted_iota(I32, (MOE_TILE, LOCAL_ROWS), 1).astype(F32)
    hot = (p == rows[0])
    for k in range(1, TOP_K):
        hot = hot | (p == rows[k])
    xl[...] = lax.dot_general(hot.astype(BF16), h_ref[...].astype(BF16), (((0,), (0,)), ((), ())),
                              preferred_element_type=F32)
    for e in range(N_EXPERTS):
        j = i * N_EXPERTS + e
        _run_dma(xl, loc_ref[j], out_ref, glb_ref[j], run_ref[j], sem, wait=False)

    @pl.when(i == N_MOE_TILES - 1)
    def _():
        _wait_rows(xl, tot_ref[i], sem)
        _wait_rows(xl_ref.at[1 - slot], tot_ref[i - 1], sems.at[1 - slot])


def _dispatch(plan, h, idx, rank):
    tile = lambda w: pl.BlockSpec((MOE_TILE, w), lambda i, *_: (i, 0))
    return pl.pallas_call(
        _dispatch_kernel,
        grid_spec=pltpu.PrefetchScalarGridSpec(
            num_scalar_prefetch=7,
            grid=(N_MOE_TILES,),
            in_specs=[tile(D_MODEL), tile(LANES), tile(LANES),
                      pl.BlockSpec((1, 1, LANES), lambda i, *_: (i, 0, 0))],
            out_specs=pl.BlockSpec(memory_space=pl.ANY),
            scratch_shapes=[pltpu.VMEM((2, LOCAL_ROWS, D_MODEL), F32),
                            pltpu.VMEM((MOE_BM, D_MODEL), F32),
                            pltpu.SemaphoreType.DMA((2,))]),
        out_shape=jax.ShapeDtypeStruct((MOE_NB * MOE_BM, D_MODEL), F32),
        compiler_params=_cparams("arbitrary"),
        name="moe_dispatch",
    )(plan["run"], plan["local"], plan["global"], plan["total"], plan["fill_start"], plan["fill_len"],
      plan["n_used"], h, idx, rank, plan["base"])


def _expert_kernel(be_ref, nu_ref, new_ref, fe_ref, x_ref, wu_ref, bu_ref, wd_ref, bd_ref, y_ref, wu_s, wd_s):
    del fe_ref
    b = pl.program_id(0)
    used = b < nu_ref[0]

    @pl.when(jnp.logical_not(used))
    def _():
        y_ref[...] = jnp.zeros_like(y_ref)

    @pl.when(jnp.logical_and(used, new_ref[b] == 1))
    def _():
        r = lax.broadcasted_iota(I32, (UP_GROUP, UP_GROUP), 0)
        c = lax.broadcasted_iota(I32, (UP_GROUP, UP_GROUP), 1)
        src = jnp.where(c < UP_GROUP // 2, 2 * c, 2 * (c - UP_GROUP // 2) + 1)
        perm = (r == src).astype(BF16)
        for g in range(2 * D_FF // UP_GROUP):
            cols = slice(g * UP_GROUP, (g + 1) * UP_GROUP)
            wu_s[:, cols] = _dot(wu_ref[0, 0, :, cols].astype(BF16), perm).astype(BF16)
        wd_s[...] = wd_ref[0, 0].astype(BF16)

    @pl.when(used)
    def _():
        up = _dot(x_ref[...].astype(BF16), wu_s[...]) + bu_ref[0]
        half = UP_GROUP // 2
        acts = []
        for g in range(2 * D_FF // UP_GROUP):
            gate = jnp.minimum(up[:, g * UP_GROUP:g * UP_GROUP + half], SWIGLU_LIMIT)
            lin = jnp.clip(up[:, g * UP_GROUP + half:(g + 1) * UP_GROUP], -SWIGLU_LIMIT, SWIGLU_LIMIT)
            acts.append((gate * _sigmoid(SWIGLU_ALPHA * gate) * (lin + 1.0)).astype(BF16))
        y_ref[...] = _dot(jnp.concatenate(acts, axis=-1), wd_s[...]) + bd_ref[0]


def _experts(layer, plan, xs, w_up, b_up, w_down, b_down):
    blk = lambda b, nu: jnp.maximum(jnp.minimum(b, nu[0] - 1), 0)
    return pl.pallas_call(
        _expert_kernel,
        grid_spec=pltpu.PrefetchScalarGridSpec(
            num_scalar_prefetch=4,
            grid=(MOE_NB,),
            in_specs=[pl.BlockSpec((MOE_BM, D_MODEL), lambda b, be, nu, nw, fe: (blk(b, nu), 0)),
                      pl.BlockSpec((1, 1, D_MODEL, 2 * D_FF), lambda b, be, nu, nw, fe: (layer, fe[blk(b, nu)], 0, 0)),
                      pl.BlockSpec((1, 1, 2 * D_FF), lambda b, be, nu, nw, fe: (be[blk(b, nu)], 0, 0)),
                      pl.BlockSpec((1, 1, D_FF, D_MODEL), lambda b, be, nu, nw, fe: (layer, fe[blk(b, nu)], 0, 0)),
                      pl.BlockSpec((1, 1, D_MODEL), lambda b, be, nu, nw, fe: (be[blk(b, nu)], 0, 0))],
            out_specs=pl.BlockSpec((MOE_BM, D_MODEL), lambda b, be, nu, nw, fe: (b, 0)),
            scratch_shapes=[pltpu.VMEM((D_MODEL, 2 * D_FF), BF16),
                            pltpu.VMEM((D_FF, D_MODEL), BF16)]),
        out_shape=jax.ShapeDtypeStruct((MOE_NB * MOE_BM, D_MODEL), F32),
        compiler_params=_cparams("arbitrary"),
        name="moe_experts",
    )(plan["blk_expert"], plan["n_used"], plan["blk_new"], plan["blk_fetch"], xs, w_up, b_up, w_down, b_down)


def _combine_kernel(run_ref, loc_ref, glb_ref, tot_ref, x_ref, gate_ref, idx_ref, rank_ref, base_ref, m_ref, fn_ref,
                    ys_ref, o_ref, yl_ref, sems, *, final):
    i = pl.program_id(0)
    slot = i % 2

    def fetch(tile, buf):
        yl = yl_ref.at[buf]
        for e in range(N_EXPERTS):
            j = tile * N_EXPERTS + e
            _run_dma(ys_ref, glb_ref[j], yl, loc_ref[j], run_ref[j], sems.at[buf], wait=False)

        def clear(r, carry):
            yl[pl.ds(pl.multiple_of(r * RUN_ALIGN, RUN_ALIGN), RUN_ALIGN), :] = jnp.zeros((RUN_ALIGN, D_MODEL), F32)
            return carry

        lax.fori_loop(tot_ref[tile] // RUN_ALIGN, LOCAL_ROWS // RUN_ALIGN, clear, 0)

    @pl.when(i == 0)
    def _():
        fetch(0, 0)

    @pl.when(i + 1 < N_MOE_TILES)
    def _():
        fetch(i + 1, 1 - slot)

    rows = _local_rows(idx_ref, rank_ref, base_ref)
    gate = gate_ref[...]
    p = lax.broadcasted_iota(I32, (MOE_TILE, LOCAL_ROWS), 1).astype(F32)
    w = jnp.zeros((MOE_TILE, LOCAL_ROWS), F32)
    for k in range(TOP_K):
        w = jnp.where(p == rows[k], gate[:, k:k + 1], w)

    _wait_rows(yl_ref.at[slot], tot_ref[i], sems.at[slot])
    acc = _dot(w.astype(BF16), yl_ref[slot].astype(BF16))
    x = x_ref[...] + m_ref[0][5:6] * acc
    o_ref[...] = _rms(x, fn_ref[...]) if final else x


def _combine(plan, x, gates, idx, rank, mods, final_norm, ys, final):
    tile = lambda w: pl.BlockSpec((MOE_TILE, w), lambda i, *_: (i, 0))
    return pl.pallas_call(
        functools.partial(_combine_kernel, final=final),
        grid_spec=pltpu.PrefetchScalarGridSpec(
            num_scalar_prefetch=4,
            grid=(N_MOE_TILES,),
            in_specs=[tile(D_MODEL), tile(LANES), tile(LANES), tile(LANES),
                      pl.BlockSpec((1, 1, LANES), lambda i, *_: (i, 0, 0)),
                      pl.BlockSpec((1, 6, D_MODEL), lambda i, *_: (_cond_of_tile(i, MOE_TILE), 0, 0)),
                      pl.BlockSpec((1, D_MODEL), lambda i, *_: (0, 0)),
                      pl.BlockSpec(memory_space=pl.ANY)],
            out_specs=tile(D_MODEL),
            scratch_shapes=[pltpu.VMEM((2, LOCAL_ROWS, D_MODEL), F32),
                            pltpu.SemaphoreType.DMA((2,))]),
        out_shape=jax.ShapeDtypeStruct((N_TOK, D_MODEL), F32),
        compiler_params=_cparams("arbitrary"),
        name="moe_combine",
    )(plan["run"], plan["local"], plan["global"], plan["total"], x, gates, idx, rank, plan["base"], mods,
      final_norm, ys)


def _moe_plan(rank, cnt):
    first = rank[::MOE_TILE, :N_EXPERTS].astype(I32)
    total = cnt[0:1, :N_EXPERTS].astype(I32)
    run = jnp.concatenate([first[1:], total], axis=0) - first
    run = (run + RUN_ALIGN - 1) // RUN_ALIGN * RUN_ALIGN
    local = jnp.cumsum(run, axis=1) - run
    sizes = run.sum(axis=0)
    padded = (sizes + MOE_BM - 1) // MOE_BM * MOE_BM
    ends = jnp.cumsum(padded)
    starts = ends - padded
    glob = starts[None, :] + jnp.cumsum(run, axis=0) - run
    blk_start = jnp.arange(MOE_NB, dtype=I32) * MOE_BM
    blk_expert = jnp.minimum(jnp.sum(ends[None, :] <= blk_start[:, None], axis=1), N_EXPERTS - 1).astype(I32)
    blk_new = jnp.concatenate([jnp.ones((1,), I32), (blk_expert[1:] != blk_expert[:-1]).astype(I32)])
    blk = jnp.arange(MOE_NB, dtype=I32)
    later_start = (blk_new[None, :] == 1) & (blk[None, :] > blk[:, None])
    next_start = jnp.min(jnp.where(later_start, blk[None, :], MOE_NB), axis=1)
    next_hot = (blk[None, :] == next_start[:, None]).astype(I32)
    next_expert = jnp.where(next_start < MOE_NB, jnp.sum(next_hot * blk_expert[None, :], axis=1), blk_expert)
    blk_fetch = jnp.where(blk_new == 1, blk_expert, next_expert)
    base = _pad_lanes((local - first).astype(F32)).reshape(N_MOE_TILES, 1, LANES)
    return {"run": run.reshape(-1), "local": local.reshape(-1), "global": glob.reshape(-1), "total": run.sum(axis=1),
            "fill_start": starts + sizes, "fill_len": padded - sizes, "n_used": ends[-1:] // MOE_BM,
            "blk_expert": blk_expert, "blk_new": blk_new, "blk_fetch": blk_fetch, "base": base}


def _pad_lanes(v, fill=0.0):
    return jnp.pad(v, ((0, 0), (0, LANES - v.shape[-1])), constant_values=fill)


def kernel(x_prompt, x_sample, cache_na_k, cache_na_v, cache_gqa_k, cache_gqa_v, state_ssm, c, c_ctx, w_ada, b_ada, norm_mix, norm_ffn, w_in, na_rpb, gqa_q_norm, gqa_k_norm, ssm_conv_w, ssm_conv_b, ssm_dt_bias, ssm_a_log, ssm_d, ssm_norm, w_out, w_router, b_router, w_up, b_up, w_down, b_down, final_norm):
    x = jnp.concatenate([x_prompt.reshape(N_CTX_TOK, D_MODEL), x_sample.reshape(N_LAT_TOK, D_MODEL)], axis=0)
    conds = jnp.concatenate([c_ctx[None], c, jnp.zeros((COND_ROWS - N_COND, D_MODEL), F32)], axis=0)
    mods = _adaln(conds, w_ada, b_ada).reshape(DEPTH, COND_ROWS, 6, D_MODEL)
    cos, sin = _rope_tables()

    w_in_b = jnp.pad(w_in, ((0, 0), (0, 0), (0, IN_PAD - IN_DIM))).astype(BF16)
    w_out_b = w_out.astype(BF16)
    b_up_s = b_up.reshape(DEPTH, N_EXPERTS, 2 * D_FF // UP_GROUP, UP_GROUP // 2, 2)
    b_up_s = jnp.swapaxes(b_up_s, -1, -2).reshape(DEPTH, N_EXPERTS, 1, 2 * D_FF)
    b_down_s = b_down.reshape(DEPTH, N_EXPERTS, 1, D_MODEL)

    ctx_out = []
    for l in range(DEPTH):
        qkv, gqa, z, xbc, dt_raw = _inproj(x, mods[l], norm_mix[l][None], w_in_b[l])
        qn, kn = gqa_q_norm[l][None], gqa_k_norm[l][None]

        nao_c, go_c, gk_c = _ctx_attn(qkv, gqa, qn, kn)
        go_l = _lat_gqa(gqa, cache_gqa_k[:, l].reshape(DEC_BATCH, PAST_LEN, GQA_KV_DIM),
                        cache_gqa_v[:, l].reshape(DEC_BATCH, PAST_LEN, GQA_KV_DIM), cos, sin, qn, kn)
        nao_l = _lat_na(qkv, cache_na_k[:, l].reshape(DEC_BATCH, PAST_LEN, NA_DIM),
                        cache_na_v[:, l].reshape(DEC_BATCH, PAST_LEN, NA_DIM), _na_bias_tables(na_rpb[l]))

        xa = _conv_act(xbc, ssm_conv_w[l], ssm_conv_b[l][None])
        zeros = jnp.zeros((BATCH, SSM_INNER, SSM_STATE), F32)
        h0 = [jnp.concatenate([zeros, state_ssm[:, l, d].reshape(DEC_BATCH, SSM_INNER, SSM_STATE)], axis=0)
              for d in range(2)]
        y_f, st_f, y_b, st_b = _ssd(xa, dt_raw, h0[0], h0[1], _pad_lanes(ssm_dt_bias[l]), _pad_lanes(ssm_a_log[l]))
        ys = [y_f, y_b]
        sts = [st[:BATCH].reshape(BATCH, SSM_HEADS, SSM_HEAD_DIM, SSM_STATE) for st in (st_f, st_b)]

        x, h, top_idx, gates, sel = _outproj(
            x, jnp.concatenate([nao_c, nao_l], axis=0), jnp.concatenate([go_c, go_l], axis=0),
            ys[0], ys[1], xa, z, jnp.repeat(ssm_d[l], SSM_HEAD_DIM)[None], ssm_norm[l][None], w_out_b[l],
            mods[l], norm_ffn[l][None], _pad_lanes(w_router[l]), _pad_lanes(b_router[l][None], NEG_INF))

        rank, cnt = _ranks(sel)
        plan = _moe_plan(rank, cnt)
        y_sorted = _experts(l, plan, _dispatch(plan, h, top_idx, rank), w_up, b_up_s[l], w_down, b_down_s[l])
        x = _combine(plan, x, gates, top_idx, rank, mods[l], final_norm[None], y_sorted, final=(l == DEPTH - 1))

        ctx_out.append((
            qkv[:N_CTX_TOK, NA_DIM:2 * NA_DIM].reshape(BATCH, SEQ, NA_HEADS, HEAD_DIM),
            qkv[:N_CTX_TOK, 2 * NA_DIM:].reshape(BATCH, SEQ, NA_HEADS, HEAD_DIM),
            gk_c.reshape(BATCH, SEQ, GQA_KV_HEADS, HEAD_DIM),
            gqa[:N_CTX_TOK, GQA_Q_DIM + GQA_KV_DIM:].reshape(BATCH, SEQ, GQA_KV_HEADS, HEAD_DIM),
            jnp.stack(sts, axis=1)))

    y_prompt = x[:N_CTX_TOK].reshape(BATCH, SEQ, D_MODEL)
    y_sample = x[N_CTX_TOK:].reshape(DEC_BATCH, DEC_SEQ, D_MODEL)
    return (y_prompt, y_sample) + tuple(jnp.stack([e[i] for e in ctx_out], axis=1) for i in range(5))
```

```python
import functools

import numpy as np
import jax
import jax.numpy as jnp
from jax import lax
from jax.experimental import pallas as pl
from jax.experimental.pallas import tpu as pltpu

F32 = jnp.float32
BF16 = jnp.bfloat16
I32 = jnp.int32

D_MODEL = 1024
BATCH = 16
SEQ = 256
DEPTH = 2
DEC_BATCH = 2
DEC_SEQ = 2048
PAST_LEN = 512
GRID_W = 64
HEAD_DIM = 64
NA_HEADS = 4
NA_WIN_ROWS = 8
NA_WIN_COLS = 16
GQA_HEADS = 4
GQA_KV_HEADS = 2
ROPE_THETA = 10000.0
SSM_HEADS = 8
SSM_HEAD_DIM = 64
SSM_STATE = 64
SSM_GROUPS = 2
SSM_INNER = SSM_HEADS * SSM_HEAD_DIM
SSM_BC_DIM = SSM_GROUPS * SSM_STATE
CONV_DIM = SSM_INNER + 2 * SSM_BC_DIM
CONV_W = 5
CHUNK = 128
NA_DIM = NA_HEADS * HEAD_DIM
GQA_Q_DIM = GQA_HEADS * HEAD_DIM
GQA_KV_DIM = GQA_KV_HEADS * HEAD_DIM
D_MIX = NA_DIM + GQA_Q_DIM + SSM_INNER
IN_DIM = 3 * NA_DIM + GQA_Q_DIM + 2 * GQA_KV_DIM + SSM_INNER + CONV_DIM + SSM_HEADS
N_EXPERTS = 32
TOP_K = 4
D_FF = D_MODEL
SWIGLU_LIMIT = 7.0
SWIGLU_ALPHA = 1.702
EPS = 1e-6
NEG_INF = -1e30

LANES = 128
N_CTX_TOK = BATCH * SEQ
N_LAT_TOK = DEC_BATCH * DEC_SEQ
N_TOK = N_CTX_TOK + N_LAT_TOK
N_COND = 1 + DEC_BATCH
COND_ROWS = 16
IN_PAD = 3 * NA_DIM + GQA_Q_DIM + 2 * GQA_KV_DIM + SSM_INNER + CONV_DIM + LANES
ROW_TILE = 256
N_ROW_TILES = N_TOK // ROW_TILE
MOE_BM = 256
N_SLOTS = N_TOK * TOP_K
RUN_ALIGN = 8
MOE_TILE = 512
N_MOE_TILES = N_TOK // MOE_TILE
N_RUNS = N_MOE_TILES * N_EXPERTS
LOCAL_ROWS = -(-(MOE_TILE * TOP_K + N_EXPERTS * (RUN_ALIGN - 1)) // MOE_BM) * MOE_BM
MOE_NB = -(-(N_SLOTS + N_RUNS * (RUN_ALIGN - 1)) // MOE_BM) + N_EXPERTS
UP_GROUP = 256
N_SEQ = BATCH + DEC_BATCH
N_CHUNKS = N_TOK // CHUNK
N_CTX_CHUNKS = N_CTX_TOK // CHUNK
VMEM_LIMIT = 56 * 1024 * 1024


def _cparams(*sem):
    return pltpu.CompilerParams(dimension_semantics=sem, vmem_limit_bytes=VMEM_LIMIT)


def _sigmoid(x):
    return 1.0 / (1.0 + jnp.exp(-x))


def _dot(a, b):
    return jnp.dot(a, b, preferred_element_type=F32)


def _dot_nt(a, b):
    return lax.dot_general(a, b, (((1,), (1,)), ((), ())), preferred_element_type=F32)


def _dot_exact(a, b):
    return jnp.dot(a, b, preferred_element_type=F32, precision=lax.Precision.HIGHEST)


def _rms(x, g):
    return x * lax.rsqrt(jnp.mean(x * x, axis=-1, keepdims=True) + EPS) * g


def _cond_of_tile(i, rows=ROW_TILE):
    ctx_tiles = N_CTX_TOK // rows
    return jnp.where(i < ctx_tiles, 0, 1 + (i - ctx_tiles) // (DEC_SEQ // rows))


def _adaln_kernel(c_ref, w_ref, b_ref, o_ref):
    c = c_ref[...]
    s = (c * _sigmoid(c)).astype(BF16)
    o_ref[0] = _dot(s, w_ref[0].astype(BF16)) + b_ref[0]


def _adaln(conds, w_ada, b_ada):
    tn = 1536
    return pl.pallas_call(
        _adaln_kernel,
        grid=(DEPTH, 6 * D_MODEL // tn),
        in_specs=[pl.BlockSpec((COND_ROWS, D_MODEL), lambda l, j: (0, 0)),
                  pl.BlockSpec((1, D_MODEL, tn), lambda l, j: (l, 0, j)),
                  pl.BlockSpec((1, 1, tn), lambda l, j: (l, 0, j))],
        out_specs=pl.BlockSpec((1, COND_ROWS, tn), lambda l, j: (l, 0, j)),
        out_shape=jax.ShapeDtypeStruct((DEPTH, COND_ROWS, 6 * D_MODEL), F32),
        compiler_params=_cparams("parallel", "parallel"),
        name="adaln",
    )(conds, w_ada, b_ada.reshape(DEPTH, 1, 6 * D_MODEL))


_IN_SPLITS = (3 * NA_DIM, GQA_Q_DIM + 2 * GQA_KV_DIM, SSM_INNER, CONV_DIM, LANES)


def _inproj_kernel(x_ref, m_ref, g_ref, w_ref, qkv_ref, gqa_ref, z_ref, xbc_ref, dt_ref, w_s):
    @pl.when(pl.program_id(0) == 0)
    def _():
        w_s[...] = jnp.zeros_like(w_s)
        w_s[:, 0:IN_DIM] = w_ref[0].astype(BF16)

    m = m_ref[0]
    h = _rms(x_ref[...], g_ref[...]) * (1.0 + m[1:2]) + m[0:1]
    p = _dot(h.astype(BF16), w_s[...])
    off = 0
    for ref, width in zip((qkv_ref, gqa_ref, z_ref, xbc_ref, dt_ref), _IN_SPLITS):
        ref[...] = p[:, off:off + width]
        off += width


def _inproj(layer, x, mods, gain, w_in):
    row = lambda w: pl.BlockSpec((ROW_TILE, w), lambda i: (i, 0))
    return pl.pallas_call(
        _inproj_kernel,
        grid=(N_ROW_TILES,),
        in_specs=[row(D_MODEL),
                  pl.BlockSpec((1, 6, D_MODEL), lambda i: (_cond_of_tile(i), 0, 0)),
                  pl.BlockSpec((1, D_MODEL), lambda i: (0, 0)),
                  pl.BlockSpec((1, D_MODEL, IN_DIM), lambda i: (layer, 0, 0))],
        out_specs=[row(w) for w in _IN_SPLITS],
        out_shape=[jax.ShapeDtypeStruct((N_TOK, w), F32) for w in _IN_SPLITS],
        scratch_shapes=[pltpu.VMEM((D_MODEL, IN_PAD), BF16)],
        compiler_params=_cparams("arbitrary"),
        name="inproj",
    )(x, mods, gain, w_in)


def _softmax_pv(scores, values):
    m = scores[0].max(axis=-1, keepdims=True)
    for s in scores[1:]:
        m = jnp.maximum(m, s.max(axis=-1, keepdims=True))
    den = 0.0
    acc = 0.0
    for s, v in zip(scores, values):
        e = jnp.exp(s - m)
        den = den + e.sum(axis=-1, keepdims=True)
        acc = acc + _dot(e.astype(BF16), v)
    return acc / den


def _heads_rms(x, n_heads, g):
    return jnp.concatenate(
        [_rms(x[:, h * HEAD_DIM:(h + 1) * HEAD_DIM], g) for h in range(n_heads)], axis=-1)


def _rope(x, cos, sin_signed):
    w = x.shape[-1]
    lane = lax.broadcasted_iota(I32, x.shape, 1)
    partner = jnp.where((lane & 1) == 0, pltpu.roll(x, w - 1, 1), pltpu.roll(x, 1, 1))
    return x * cos + partner * sin_signed


_ATT_SCALE = HEAD_DIM ** -0.5


def _ctx_attn_kernel(qkv_ref, gqa_ref, qn_ref, kn_ref, nao_ref, go_ref, gk_ref):
    outs = []
    for h in range(NA_HEADS):
        sl = slice(h * HEAD_DIM, (h + 1) * HEAD_DIM)
        q = qkv_ref[:, sl].astype(BF16)
        k = qkv_ref[:, NA_DIM + h * HEAD_DIM:NA_DIM + (h + 1) * HEAD_DIM].astype(BF16)
        v = qkv_ref[:, 2 * NA_DIM + h * HEAD_DIM:2 * NA_DIM + (h + 1) * HEAD_DIM].astype(BF16)
        outs.append(_softmax_pv([_dot_nt(q, k) * _ATT_SCALE], [v]))
    nao_ref[...] = jnp.concatenate(outs, axis=-1).astype(BF16)

    gq = _heads_rms(gqa_ref[:, 0:GQA_Q_DIM], GQA_HEADS, qn_ref[...])
    gk = _heads_rms(gqa_ref[:, GQA_Q_DIM:GQA_Q_DIM + GQA_KV_DIM], GQA_KV_HEADS, kn_ref[...])
    gk_ref[...] = gk
    rep = GQA_HEADS // GQA_KV_HEADS
    outs = []
    for h in range(GQA_HEADS):
        g = h // rep
        q = gq[:, h * HEAD_DIM:(h + 1) * HEAD_DIM].astype(BF16)
        k = gk[:, g * HEAD_DIM:(g + 1) * HEAD_DIM].astype(BF16)
        v0 = GQA_Q_DIM + GQA_KV_DIM + g * HEAD_DIM
        v = gqa_ref[:, v0:v0 + HEAD_DIM].astype(BF16)
        outs.append(_softmax_pv([_dot_nt(q, k) * _ATT_SCALE], [v]))
    go_ref[...] = jnp.concatenate(outs, axis=-1).astype(BF16)


def _ctx_attn(qkv, gqa, q_norm, k_norm):
    return pl.pallas_call(
        _ctx_attn_kernel,
        grid=(BATCH,),
        in_specs=[pl.BlockSpec((SEQ, 3 * NA_DIM), lambda b: (b, 0)),
                  pl.BlockSpec((SEQ, GQA_Q_DIM + 2 * GQA_KV_DIM), lambda b: (b, 0)),
                  pl.BlockSpec((1, HEAD_DIM), lambda b: (0, 0)),
                  pl.BlockSpec((1, HEAD_DIM), lambda b: (0, 0))],
        out_specs=[pl.BlockSpec((SEQ, NA_DIM), lambda b: (b, 0)),
                   pl.BlockSpec((SEQ, GQA_Q_DIM), lambda b: (b, 0)),
                   pl.BlockSpec((SEQ, GQA_KV_DIM), lambda b: (b, 0))],
        out_shape=[jax.ShapeDtypeStruct((N_CTX_TOK, NA_DIM), BF16),
                   jax.ShapeDtypeStruct((N_CTX_TOK, GQA_Q_DIM), BF16),
                   jax.ShapeDtypeStruct((N_CTX_TOK, GQA_KV_DIM), F32)],
        compiler_params=_cparams("parallel"),
        name="ctx_attn",
    )(qkv, gqa, q_norm, k_norm)


GQA_TQ = 256
GQA_KEYS = PAST_LEN + DEC_SEQ


def _lat_gqa_kernel(gqa_ref, ck_ref, cv_ref, cos_ref, sin_ref, qn_ref, kn_ref, o_ref, kbuf, vbuf):
    qb = pl.program_id(1)

    @pl.when(qb == 0)
    def _():
        kbuf[0:PAST_LEN, :] = ck_ref[0].astype(BF16)
        vbuf[0:PAST_LEN, :] = cv_ref[0].astype(BF16)
        k = _heads_rms(gqa_ref[:, GQA_Q_DIM:GQA_Q_DIM + GQA_KV_DIM], GQA_KV_HEADS, kn_ref[...])
        k = _rope(k, cos_ref[:, 0:GQA_KV_DIM], sin_ref[:, 0:GQA_KV_DIM])
        kbuf[PAST_LEN:GQA_KEYS, :] = k.astype(BF16)
        vbuf[PAST_LEN:GQA_KEYS, :] = gqa_ref[:, GQA_Q_DIM + GQA_KV_DIM:].astype(BF16)

    r0 = pl.multiple_of(qb * GQA_TQ, GQA_TQ)
    q = _heads_rms(gqa_ref[pl.ds(r0, GQA_TQ), 0:GQA_Q_DIM], GQA_HEADS, qn_ref[...])
    q = _rope(q, cos_ref[pl.ds(r0, GQA_TQ), :], sin_ref[pl.ds(r0, GQA_TQ), :]).astype(BF16)
    rep = GQA_HEADS // GQA_KV_HEADS
    outs = []
    for h in range(GQA_HEADS):
        g = h // rep
        k = kbuf[:, g * HEAD_DIM:(g + 1) * HEAD_DIM]
        v = vbuf[:, g * HEAD_DIM:(g + 1) * HEAD_DIM]
        s = _dot_nt(q[:, h * HEAD_DIM:(h + 1) * HEAD_DIM], k) * _ATT_SCALE
        outs.append(_softmax_pv([s], [v]))
    o_ref[...] = jnp.concatenate(outs, axis=-1).astype(BF16)


def _lat_gqa(gqa, cache_k, cache_v, cos, sin, q_norm, k_norm):
    lat_blk = N_CTX_TOK // DEC_SEQ
    return pl.pallas_call(
        _lat_gqa_kernel,
        grid=(DEC_BATCH, DEC_SEQ // GQA_TQ),
        in_specs=[pl.BlockSpec((DEC_SEQ, GQA_Q_DIM + 2 * GQA_KV_DIM), lambda b, q: (lat_blk + b, 0)),
                  pl.BlockSpec((1, PAST_LEN, GQA_KV_DIM), lambda b, q: (b, 0, 0)),
                  pl.BlockSpec((1, PAST_LEN, GQA_KV_DIM), lambda b, q: (b, 0, 0)),
                  pl.BlockSpec((DEC_SEQ, GQA_Q_DIM), lambda b, q: (0, 0)),
                  pl.BlockSpec((DEC_SEQ, GQA_Q_DIM), lambda b, q: (0, 0)),
                  pl.BlockSpec((1, HEAD_DIM), lambda b, q: (0, 0)),
                  pl.BlockSpec((1, HEAD_DIM), lambda b, q: (0, 0))],
        out_specs=pl.BlockSpec((GQA_TQ, GQA_Q_DIM), lambda b, q: (b * (DEC_SEQ // GQA_TQ) + q, 0)),
        out_shape=jax.ShapeDtypeStruct((N_LAT_TOK, GQA_Q_DIM), BF16),
        scratch_shapes=[pltpu.VMEM((GQA_KEYS, GQA_KV_DIM), BF16),
                        pltpu.VMEM((GQA_KEYS, GQA_KV_DIM), BF16)],
        compiler_params=_cparams("arbitrary", "arbitrary"),
        name="lat_gqa",
    )(gqa, cache_k, cache_v, cos, sin, q_norm, k_norm)


def _rope_tables():
    t = jnp.arange(DEC_SEQ)
    row = (t // GRID_W).astype(F32)
    col = (t % GRID_W).astype(F32)
    axis_dim = HEAD_DIM // 2
    inv_freq = ROPE_THETA ** (-jnp.arange(0, axis_dim, 2, dtype=F32) / axis_dim)
    ang = jnp.concatenate([row[:, None] * inv_freq, col[:, None] * inv_freq], axis=-1)
    cos = jnp.repeat(jnp.cos(ang), 2, axis=-1)
    sin = jnp.repeat(jnp.sin(ang), 2, axis=-1) * jnp.tile(jnp.array([-1.0, 1.0], F32), HEAD_DIM // 2)
    return jnp.tile(cos, (1, GQA_HEADS)), jnp.tile(sin, (1, GQA_HEADS))


NA_ROWS = DEC_SEQ // GRID_W
NA_KEYS = NA_WIN_ROWS * GRID_W


NA_ROWS_PER_STEP = 2


def _lat_na_kernel(qkv_ref, ck_ref, cv_ref, *rest):
    bias_refs, o_ref = rest[:NA_ROWS_PER_STEP], rest[NA_ROWS_PER_STEP]
    for j in range(NA_ROWS_PER_STEP):
        r = pl.program_id(1) * NA_ROWS_PER_STEP + j
        r0 = jnp.clip(r - NA_WIN_ROWS // 2, 0, NA_ROWS - NA_WIN_ROWS)
        q0 = pl.multiple_of(r * GRID_W, GRID_W)
        k0 = pl.multiple_of(r0 * GRID_W, GRID_W)
        outs = []
        for h in range(NA_HEADS):
            c0 = h * HEAD_DIM
            q = qkv_ref[pl.ds(q0, GRID_W), c0:c0 + HEAD_DIM].astype(BF16)
            k = qkv_ref[pl.ds(k0, NA_KEYS), NA_DIM + c0:NA_DIM + c0 + HEAD_DIM].astype(BF16)
            v = qkv_ref[pl.ds(k0, NA_KEYS), 2 * NA_DIM + c0:2 * NA_DIM + c0 + HEAD_DIM].astype(BF16)
            kc = ck_ref[0, :, c0:c0 + HEAD_DIM].astype(BF16)
            vc = cv_ref[0, :, c0:c0 + HEAD_DIM].astype(BF16)
            s_nb = _dot_nt(q, k) * _ATT_SCALE + bias_refs[j][0, h]
            s_ctx = _dot_nt(q, kc) * _ATT_SCALE
            outs.append(_softmax_pv([s_nb, s_ctx], [v, vc]))
        o_ref[j * GRID_W:(j + 1) * GRID_W, :] = jnp.concatenate(outs, axis=-1).astype(BF16)


def _na_row_offset(r):
    return r - jnp.clip(r - NA_WIN_ROWS // 2, 0, NA_ROWS - NA_WIN_ROWS)


def _lat_na(qkv, cache_k, cache_v, bias):
    lat_blk = N_CTX_TOK // DEC_SEQ
    steps = NA_ROWS // NA_ROWS_PER_STEP
    bias_spec = lambda j: pl.BlockSpec((1, NA_HEADS, GRID_W, NA_KEYS),
                                       lambda b, s: (_na_row_offset(s * NA_ROWS_PER_STEP + j), 0, 0, 0))
    return pl.pallas_call(
        _lat_na_kernel,
        grid=(DEC_BATCH, steps),
        in_specs=[pl.BlockSpec((DEC_SEQ, 3 * NA_DIM), lambda b, s: (lat_blk + b, 0)),
                  pl.BlockSpec((1, PAST_LEN, NA_DIM), lambda b, s: (b, 0, 0)),
                  pl.BlockSpec((1, PAST_LEN, NA_DIM), lambda b, s: (b, 0, 0))]
                 + [bias_spec(j) for j in range(NA_ROWS_PER_STEP)],
        out_specs=pl.BlockSpec((NA_ROWS_PER_STEP * GRID_W, NA_DIM), lambda b, s: (b * steps + s, 0)),
        out_shape=jax.ShapeDtypeStruct((N_LAT_TOK, NA_DIM), BF16),
        compiler_params=_cparams("parallel", "arbitrary"),
        name="lat_na",
    )(qkv, cache_k, cache_v, *([bias] * NA_ROWS_PER_STEP))


def _na_bias_tables(rpb):
    d = np.arange(NA_WIN_ROWS)[:, None]
    kr = np.arange(NA_WIN_ROWS)[None, :]
    dr = kr - d + NA_WIN_ROWS - 1
    qc = np.arange(GRID_W)[:, None]
    kc = np.arange(GRID_W)[None, :]
    col0 = np.clip(qc - NA_WIN_COLS // 2, 0, GRID_W - NA_WIN_COLS)
    in_win = (kc >= col0) & (kc < col0 + NA_WIN_COLS)
    dc = np.clip(kc - qc + NA_WIN_COLS - 1, 0, 2 * NA_WIN_COLS - 2)
    row_hot = (dr[:, :, None] == np.arange(2 * NA_WIN_ROWS - 1)).astype(np.float32)
    col_hot = (dc[:, :, None] == np.arange(2 * NA_WIN_COLS - 1)).astype(np.float32)
    b = jnp.einsum('hac,dka,qxc->dhqkx', rpb.astype(F32), row_hot, col_hot, precision=lax.Precision.HIGHEST)
    b = jnp.where(in_win[None, None, :, None, :], b, NEG_INF)
    return b.reshape(NA_WIN_ROWS, NA_HEADS, GRID_W, NA_KEYS)


CONV_TB = 1024
CONV_HALO = 8
CONV_HALO_BLOCKS = CONV_TB // CONV_HALO


def _conv_kernel(prev_ref, x_ref, next_ref, w_ref, b_ref, o_ref):
    i = pl.program_id(0)
    seq = jnp.where(i < N_CTX_TOK // CONV_TB, SEQ, DEC_SEQ)
    x = x_ref[...]
    ext = jnp.concatenate([prev_ref[...], x, next_ref[...]], axis=0)
    n_ext = CONV_TB + 2 * CONV_HALO
    pos = (lax.broadcasted_iota(I32, (CONV_TB, 1), 0) + i * CONV_TB) & (seq - 1)
    half = CONV_W // 2
    acc = x * w_ref[half:half + 1, :]
    for s in range(-half, half + 1):
        if s == 0:
            continue
        shifted = pltpu.roll(ext, (-s) % n_ext, 0)[CONV_HALO:CONV_HALO + CONV_TB]
        valid = (pos + s >= 0) & (pos + s < seq)
        acc = acc + jnp.where(valid, shifted, 0.0) * w_ref[half + s:half + s + 1, :]
    acc = acc + b_ref[...]
    o_ref[...] = acc * _sigmoid(acc)


def _conv_act(xbc, conv_w, conv_b):
    return pl.pallas_call(
        _conv_kernel,
        grid=(N_TOK // CONV_TB,),
        in_specs=[pl.BlockSpec((CONV_HALO, CONV_DIM),
                               lambda i: (jnp.maximum(i * CONV_HALO_BLOCKS - 1, 0), 0)),
                  pl.BlockSpec((CONV_TB, CONV_DIM), lambda i: (i, 0)),
                  pl.BlockSpec((CONV_HALO, CONV_DIM),
                               lambda i: (jnp.minimum((i + 1) * CONV_HALO_BLOCKS, N_TOK // CONV_HALO - 1), 0)),
                  pl.BlockSpec((CONV_W, CONV_DIM), lambda i: (0, 0)),
                  pl.BlockSpec((1, CONV_DIM), lambda i: (0, 0))],
        out_specs=pl.BlockSpec((CONV_TB, CONV_DIM), lambda i: (i, 0)),
        out_shape=jax.ShapeDtypeStruct((N_TOK, CONV_DIM), F32),
        compiler_params=_cparams("parallel"),
        name="conv_act",
    )(xbc, xbc, xbc, conv_w, conv_b)


def _chunk_seq(g):
    ctx_n = SEQ // CHUNK
    lat_n = DEC_SEQ // CHUNK
    is_ctx = g < N_CTX_CHUNKS
    gl = g - N_CTX_CHUNKS
    sid = jnp.where(is_ctx, g // ctx_n, BATCH + gl // lat_n)
    cin = jnp.where(is_ctx, g % ctx_n, gl % lat_n)
    n = jnp.where(is_ctx, ctx_n, lat_n)
    return sid, cin, n


def _ssd_init(h0_ref, st_ref, gg, reverse):
    _, cin, n = _chunk_seq(gg)

    @pl.when(cin == (n - 1 if reverse else 0))
    def _():
        st_ref[0] = h0_ref[0]


def _ssd_chunk(xa_ref, dt_ref, dtb, alog, y_ref, st_ref, reverse):
    x = dt_ref[...] + dtb
    dt = jnp.maximum(x, 0.0) + jnp.log1p(jnp.exp(-jnp.abs(x)))
    dta = dt * -jnp.exp(alog)
    ii = lax.broadcasted_iota(I32, (CHUNK, CHUNK), 0)
    jj = lax.broadcasted_iota(I32, (CHUNK, CHUNK), 1)
    tri = (jj >= ii) if reverse else (jj <= ii)
    cum = _dot_exact(tri.astype(F32), dta)
    cum_t = cum.T
    edge = 0 if reverse else CHUNK - 1
    tot = cum[edge:edge + 1, :]
    first = jj < SSM_HEAD_DIM
    first_rows = lax.broadcasted_iota(I32, (2 * SSM_HEAD_DIM, SSM_STATE), 0) < SSM_HEAD_DIM
    rep = SSM_HEADS // SSM_GROUPS
    for grp in range(SSM_GROUPS):
        bg = xa_ref[:, SSM_INNER + grp * SSM_STATE:SSM_INNER + (grp + 1) * SSM_STATE].astype(BF16)
        c0 = SSM_INNER + SSM_BC_DIM + grp * SSM_STATE
        cg = xa_ref[:, c0:c0 + SSM_STATE].astype(BF16)
        cb = _dot_nt(cg, bg)
        for ha in range(grp * rep, (grp + 1) * rep, 2):
            hb = ha + 1
            slab = slice(ha * SSM_HEAD_DIM, (hb + 1) * SSM_HEAD_DIM)
            col_a, col_b = cum[:, ha:ha + 1], cum[:, hb:hb + 1]
            tot_a, tot_b = tot[:, ha:ha + 1], tot[:, hb:hb + 1]
            decay_a = jnp.where(tri, jnp.exp(jnp.minimum(col_a - cum_t[ha:ha + 1, :], 0.0)), 0.0)
            decay_b = jnp.where(tri, jnp.exp(jnp.minimum(col_b - cum_t[hb:hb + 1, :], 0.0)), 0.0)
            xdt = xa_ref[:, slab] * jnp.where(first, dt[:, ha:ha + 1], dt[:, hb:hb + 1])
            xdt_b = xdt.astype(BF16)
            y = jnp.where(first, _dot((cb * decay_a).astype(BF16), xdt_b), _dot((cb * decay_b).astype(BF16), xdt_b))
            state = st_ref[0, slab, :]
            y = y + _dot_nt(cg, state.astype(BF16)) * jnp.where(first, jnp.exp(col_a), jnp.exp(col_b))
            y_ref[:, slab] = y
            to_end = jnp.where(first, jnp.exp(tot_a - col_a), jnp.exp(tot_b - col_b))
            upd = lax.dot_general((xdt * to_end).astype(BF16), bg, (((0,), (0,)), ((), ())),
                                  preferred_element_type=F32)
            st_ref[0, slab, :] = state * jnp.where(first_rows, jnp.exp(tot_a), jnp.exp(tot_b)) + upd


def _ssd_kernel(xaf_ref, dtf_ref, h0f_ref, xab_ref, dtb_ref, h0b_ref, bias_ref, alog_ref,
                yf_ref, stf_ref, yb_ref, stb_ref):
    g = pl.program_id(0)
    _ssd_init(h0f_ref, stf_ref, g, False)
    _ssd_chunk(xaf_ref, dtf_ref, bias_ref[0:1, :], alog_ref[0:1, :], yf_ref, stf_ref, False)
    _ssd_init(h0b_ref, stb_ref, N_CHUNKS - 1 - g, True)
    _ssd_chunk(xab_ref, dtb_ref, bias_ref[1:2, :], alog_ref[1:2, :], yb_ref, stb_ref, True)


def _ssd(xa, dt_raw, h0_fwd, h0_bwd, dt_bias, a_log):
    rev = lambda g: N_CHUNKS - 1 - g
    chunk = lambda w, order: pl.BlockSpec((CHUNK, w), lambda g: (order(g), 0))
    state = lambda order: pl.BlockSpec((1, SSM_INNER, SSM_STATE), lambda g: (_chunk_seq(order(g))[0], 0, 0))
    same = lambda g: g
    y_shape = jax.ShapeDtypeStruct((N_TOK, SSM_INNER), F32)
    st_shape = jax.ShapeDtypeStruct((N_SEQ, SSM_INNER, SSM_STATE), F32)
    return pl.pallas_call(
        _ssd_kernel,
        grid=(N_CHUNKS,),
        in_specs=[chunk(CONV_DIM, same), chunk(LANES, same), state(same),
                  chunk(CONV_DIM, rev), chunk(LANES, rev), state(rev),
                  pl.BlockSpec((2, LANES), lambda g: (0, 0)),
                  pl.BlockSpec((2, LANES), lambda g: (0, 0))],
        out_specs=[chunk(SSM_INNER, same), state(same), chunk(SSM_INNER, rev), state(rev)],
        out_shape=[y_shape, st_shape, y_shape, st_shape],
        compiler_params=_cparams("arbitrary"),
        name="ssd",
    )(xa, dt_raw, h0_fwd, xa, dt_raw, h0_bwd, dt_bias, a_log)


def _outproj_kernel(x_ref, nao_ref, go_ref, yf_ref, yb_ref, xs_ref, z_ref, dsk_ref, sn_ref, wo_ref,
                    m_ref, g2_ref, wrh_ref, wrl_ref, br_ref, xo_ref, h_ref, idx_ref, gate_ref, sel_ref, wo_s):
    @pl.when(pl.program_id(0) == 0)
    def _():
        wo_s[...] = wo_ref[0].astype(BF16)

    m = m_ref[0]
    z = z_ref[...]
    y = (yf_ref[...] + yb_ref[...] + xs_ref[...] * dsk_ref[...]) * (z * _sigmoid(z))
    s_o = _rms(y, sn_ref[...]).astype(BF16)
    mix = (_dot(nao_ref[...], wo_s[0:NA_DIM, :])
           + _dot(go_ref[...], wo_s[NA_DIM:NA_DIM + GQA_Q_DIM, :])
           + _dot(s_o, wo_s[NA_DIM + GQA_Q_DIM:, :]))
    x = x_ref[...] + m[2:3] * mix
    xo_ref[...] = x
    h = _rms(x, g2_ref[...]) * (1.0 + m[4:5]) + m[3:4]
    h_hi = h.astype(BF16)
    h_ref[...] = h_hi

    h_lo = (h - h_hi.astype(F32)).astype(BF16)
    w_hi = wrh_ref[...]
    logits = _dot(h_hi, w_hi) + _dot(h_lo, w_hi) + _dot(h_hi, wrl_ref[...]) + br_ref[...]
    lane = lax.broadcasted_iota(I32, logits.shape, 1).astype(F32)
    vals, idxs = [], []
    for _ in range(TOP_K):
        v = logits.max(axis=-1, keepdims=True)
        i = jnp.where(logits == v, lane, float(LANES)).min(axis=-1, keepdims=True)
        vals.append(v)
        idxs.append(i)
        logits = jnp.where(lane == i, -jnp.inf, logits)
    es = [jnp.exp(v - vals[0]) for v in vals]
    den = es[0] + es[1] + es[2] + es[3]
    idx_out = jnp.zeros(lane.shape, F32)
    gate_out = jnp.zeros(lane.shape, F32)
    sel = jnp.zeros(lane.shape, F32)
    for k in range(TOP_K):
        idx_out = jnp.where(lane == float(k), idxs[k], idx_out)
        gate_out = jnp.where(lane == float(k), es[k] / den, gate_out)
        sel = jnp.where(lane == idxs[k], 1.0, sel)
    idx_ref[...] = idx_out.astype(I32)
    gate_ref[...] = gate_out
    sel_ref[...] = sel.astype(BF16)


def _outproj(layer, x, nao, go, yf, yb, xa, z, d_skip, ssm_norm, w_out, mods, gain2, w_router, b_router):
    row = lambda w: pl.BlockSpec((ROW_TILE, w), lambda i: (i, 0))
    full = lambda a, b: pl.BlockSpec((a, b), lambda i: (0, 0))
    wr_hi = w_router.astype(BF16)
    wr_lo = (w_router - wr_hi.astype(F32)).astype(BF16)
    return pl.pallas_call(
        _outproj_kernel,
        grid=(N_ROW_TILES,),
        in_specs=[row(D_MODEL), row(NA_DIM), row(GQA_Q_DIM), row(SSM_INNER), row(SSM_INNER),
                  row(SSM_INNER), row(SSM_INNER), full(1, SSM_INNER), full(1, SSM_INNER),
                  pl.BlockSpec((1, D_MIX, D_MODEL), lambda i: (layer, 0, 0)),
                  pl.BlockSpec((1, 6, D_MODEL), lambda i: (_cond_of_tile(i), 0, 0)),
                  full(1, D_MODEL), full(D_MODEL, LANES), full(D_MODEL, LANES), full(1, LANES)],
        out_specs=[row(D_MODEL), row(D_MODEL), row(LANES), row(LANES), row(LANES)],
        out_shape=[jax.ShapeDtypeStruct((N_TOK, D_MODEL), F32),
                   jax.ShapeDtypeStruct((N_TOK, D_MODEL), BF16),
                   jax.ShapeDtypeStruct((N_TOK, LANES), I32),
                   jax.ShapeDtypeStruct((N_TOK, LANES), F32),
                   jax.ShapeDtypeStruct((N_TOK, LANES), BF16)],
        scratch_shapes=[pltpu.VMEM((D_MIX, D_MODEL), BF16)],
        compiler_params=_cparams("arbitrary"),
        name="outproj_router",
    )(x, nao, go, yf, yb, xa, z, d_skip, ssm_norm, w_out, mods, gain2, wr_hi, wr_lo, b_router)


RANK_TB = 512


def _rank_kernel(sel_ref, rank_ref, cnt_ref, carry):
    @pl.when(pl.program_id(0) == 0)
    def _():
        carry[...] = jnp.zeros_like(carry)

    sel = sel_ref[...]
    ii = lax.broadcasted_iota(I32, (RANK_TB, RANK_TB), 0)
    jj = lax.broadcasted_iota(I32, (RANK_TB, RANK_TB), 1)
    before = (jj < ii).astype(BF16)
    rank_ref[...] = _dot(before, sel) + carry[0:1, :]
    carry[...] = carry[...] + _dot(jnp.ones((8, RANK_TB), BF16), sel)
    cnt_ref[...] = carry[...]


def _ranks(sel):
    return pl.pallas_call(
        _rank_kernel,
        grid=(N_TOK // RANK_TB,),
        in_specs=[pl.BlockSpec((RANK_TB, LANES), lambda i: (i, 0))],
        out_specs=[pl.BlockSpec((RANK_TB, LANES), lambda i: (i, 0)),
                   pl.BlockSpec((8, LANES), lambda i: (0, 0))],
        out_shape=[jax.ShapeDtypeStruct((N_TOK, LANES), F32),
                   jax.ShapeDtypeStruct((8, LANES), F32)],
        scratch_shapes=[pltpu.VMEM((8, LANES), F32)],
        compiler_params=_cparams("arbitrary"),
        name="moe_ranks",
    )(sel)


_RUN_PIECES = (512, 256, 128, 64, 32, 16, 8)


def _run_dma(src, s0, dst, d0, n, sem, *, wait, fixed_src=False, pieces=_RUN_PIECES):
    for size in pieces:
        @pl.when((n & size) != 0)
        def _(size=size):
            done = n & ~(2 * size - 1)
            s = 0 if fixed_src else pl.multiple_of(s0 + done, RUN_ALIGN)
            d = pl.multiple_of(d0 + done, RUN_ALIGN)
            copy = pltpu.make_async_copy(src.at[pl.ds(s, size), :], dst.at[pl.ds(d, size), :], sem)
            if wait:
                copy.wait()
            else:
                copy.start()


_TILE_PIECES = (2048, 1024, 512, 256, 128, 64, 32, 16, 8)


def _wait_rows(buf, n, sem):
    for size in _TILE_PIECES:
        @pl.when((n & size) != 0)
        def _(size=size):
            pltpu.make_async_copy(buf.at[pl.ds(0, size), :], buf.at[pl.ds(0, size), :], sem).wait()


def _local_rows(idx_ref, rank_ref, base_ref):
    pos = rank_ref[...] + base_ref[0]
    lane = lax.broadcasted_iota(I32, pos.shape, 1)
    idx = idx_ref[...]
    return [jnp.sum(jnp.where(lane == idx[:, k:k + 1], pos, 0.0), axis=-1, keepdims=True) for k in range(TOP_K)]


def _dispatch_kernel(run_ref, loc_ref, glb_ref, tot_ref, fs_ref, fl_ref, nu_ref,
                     h_ref, idx_ref, rank_ref, base_ref, out_ref, xl_ref, zbuf, sems):
    i = pl.program_id(0)
    slot = i % 2
    xl = xl_ref.at[slot]
    sem = sems.at[slot]

    @pl.when(i == 0)
    def _():
        zbuf[...] = jnp.zeros_like(zbuf)
        fill_pieces = tuple(s for s in _RUN_PIECES if s < MOE_BM)
        for wait in (False, True):
            for e in range(N_EXPERTS):
                _run_dma(zbuf, 0, out_ref, fs_ref[e], fl_ref[e], sem, wait=wait, fixed_src=True, pieces=fill_pieces)

        def tail(b, carry):
            copy = pltpu.make_async_copy(zbuf, out_ref.at[pl.ds(pl.multiple_of(b * MOE_BM, MOE_BM), MOE_BM), :], sem)
            copy.start()
            copy.wait()
            return carry

        lax.fori_loop(nu_ref[0], MOE_NB, tail, 0)

    @pl.when(i >= 2)
    def _():
        _wait_rows(xl, tot_ref[i - 2], sem)

    rows = _local_rows(idx_ref, rank_ref, base_ref)
    p = lax.broadcasted_iota(I32, (MOE_TILE, LOCAL_ROWS), 1).astype(F32)
    hot = (p == rows[0])
    for k in range(1, TOP_K):
        hot = hot | (p == rows[k])
    xl[...] = lax.dot_general(hot.astype(BF16), h_ref[...].astype(BF16), (((0,), (0,)), ((), ())),
                              preferred_element_type=F32)
    for e in range(N_EXPERTS):
        j = i * N_EXPERTS + e
        _run_dma(xl, loc_ref[j], out_ref, glb_ref[j], run_ref[j], sem, wait=False)

    @pl.when(i == N_MOE_TILES - 1)
    def _():
        _wait_rows(xl, tot_ref[i], sem)
        _wait_rows(xl_ref.at[1 - slot], tot_ref[i - 1], sems.at[1 - slot])


def _dispatch(plan, h, idx, rank):
    tile = lambda w: pl.BlockSpec((MOE_TILE, w), lambda i, *_: (i, 0))
    return pl.pallas_call(
        _dispatch_kernel,
        grid_spec=pltpu.PrefetchScalarGridSpec(
            num_scalar_prefetch=7,
            grid=(N_MOE_TILES,),
            in_specs=[tile(D_MODEL), tile(LANES), tile(LANES),
                      pl.BlockSpec((1, 1, LANES), lambda i, *_: (i, 0, 0))],
            out_specs=pl.BlockSpec(memory_space=pl.ANY),
            scratch_shapes=[pltpu.VMEM((2, LOCAL_ROWS, D_MODEL), F32),
                            pltpu.VMEM((MOE_BM, D_MODEL), F32),
                            pltpu.SemaphoreType.DMA((2,))]),
        out_shape=jax.ShapeDtypeStruct((MOE_NB * MOE_BM, D_MODEL), F32),
        compiler_params=_cparams("arbitrary"),
        name="moe_dispatch",
    )(plan["run"], plan["local"], plan["global"], plan["total"], plan["fill_start"], plan["fill_len"],
      plan["n_used"], h, idx, rank, plan["base"])


def _expert_kernel(be_ref, nu_ref, new_ref, fe_ref, x_ref, wu_ref, bu_ref, wd_ref, bd_ref, y_ref, wu_s, wd_s):
    del fe_ref
    b = pl.program_id(0)
    used = b < nu_ref[0]

    @pl.when(jnp.logical_not(used))
    def _():
        y_ref[...] = jnp.zeros_like(y_ref)

    @pl.when(jnp.logical_and(used, new_ref[b] == 1))
    def _():
        r = lax.broadcasted_iota(I32, (UP_GROUP, UP_GROUP), 0)
        c = lax.broadcasted_iota(I32, (UP_GROUP, UP_GROUP), 1)
        src = jnp.where(c < UP_GROUP // 2, 2 * c, 2 * (c - UP_GROUP // 2) + 1)
        perm = (r == src).astype(BF16)
        for g in range(2 * D_FF // UP_GROUP):
            cols = slice(g * UP_GROUP, (g + 1) * UP_GROUP)
            wu_s[:, cols] = _dot(wu_ref[0, 0, :, cols].astype(BF16), perm).astype(BF16)
        wd_s[...] = wd_ref[0, 0].astype(BF16)

    @pl.when(used)
    def _():
        up = _dot(x_ref[...].astype(BF16), wu_s[...]) + bu_ref[0]
        half = UP_GROUP // 2
        acts = []
        for g in range(2 * D_FF // UP_GROUP):
            gate = jnp.minimum(up[:, g * UP_GROUP:g * UP_GROUP + half], SWIGLU_LIMIT)
            lin = jnp.clip(up[:, g * UP_GROUP + half:(g + 1) * UP_GROUP], -SWIGLU_LIMIT, SWIGLU_LIMIT)
            acts.append((gate * _sigmoid(SWIGLU_ALPHA * gate) * (lin + 1.0)).astype(BF16))
        y_ref[...] = _dot(jnp.concatenate(acts, axis=-1), wd_s[...]) + bd_ref[0]


def _experts(layer, plan, xs, w_up, b_up, w_down, b_down):
    blk = lambda b, nu: jnp.maximum(jnp.minimum(b, nu[0] - 1), 0)
    return pl.pallas_call(
        _expert_kernel,
        grid_spec=pltpu.PrefetchScalarGridSpec(
            num_scalar_prefetch=4,
            grid=(MOE_NB,),
            in_specs=[pl.BlockSpec((MOE_BM, D_MODEL), lambda b, be, nu, nw, fe: (blk(b, nu), 0)),
                      pl.BlockSpec((1, 1, D_MODEL, 2 * D_FF), lambda b, be, nu, nw, fe: (layer, fe[blk(b, nu)], 0, 0)),
                      pl.BlockSpec((1, 1, 2 * D_FF), lambda b, be, nu, nw, fe: (be[blk(b, nu)], 0, 0)),
                      pl.BlockSpec((1, 1, D_FF, D_MODEL), lambda b, be, nu, nw, fe: (layer, fe[blk(b, nu)], 0, 0)),
                      pl.BlockSpec((1, 1, D_MODEL), lambda b, be, nu, nw, fe: (be[blk(b, nu)], 0, 0))],
            out_specs=pl.BlockSpec((MOE_BM, D_MODEL), lambda b, be, nu, nw, fe: (b, 0)),
            scratch_shapes=[pltpu.VMEM((D_MODEL, 2 * D_FF), BF16),
                            pltpu.VMEM((D_FF, D_MODEL), BF16)]),
        out_shape=jax.ShapeDtypeStruct((MOE_NB * MOE_BM, D_MODEL), F32),
        compiler_params=_cparams("arbitrary"),
        name="moe_experts",
    )(plan["blk_expert"], plan["n_used"], plan["blk_new"], plan["blk_fetch"], xs, w_up, b_up, w_down, b_down)


def _combine_kernel(run_ref, loc_ref, glb_ref, tot_ref, x_ref, gate_ref, idx_ref, rank_ref, base_ref, m_ref, fn_ref,
                    ys_ref, *rest, final):
    n_out = 2 if final else 1
    o_refs, (yl_ref, sems) = rest[:n_out], rest[n_out:]
    i = pl.program_id(0)
    slot = i % 2

    def fetch(tile, buf):
        yl = yl_ref.at[buf]
        for e in range(N_EXPERTS):
            j = tile * N_EXPERTS + e
            _run_dma(ys_ref, glb_ref[j], yl, loc_ref[j], run_ref[j], sems.at[buf], wait=False)

        def clear(r, carry):
            yl[pl.ds(pl.multiple_of(r * RUN_ALIGN, RUN_ALIGN), RUN_ALIGN), :] = jnp.zeros((RUN_ALIGN, D_MODEL), F32)
            return carry

        lax.fori_loop(tot_ref[tile] // RUN_ALIGN, LOCAL_ROWS // RUN_ALIGN, clear, 0)

    @pl.when(i == 0)
    def _():
        fetch(0, 0)

    @pl.when(i + 1 < N_MOE_TILES)
    def _():
        fetch(i + 1, 1 - slot)

    rows = _local_rows(idx_ref, rank_ref, base_ref)
    gate = gate_ref[...]
    p = lax.broadcasted_iota(I32, (MOE_TILE, LOCAL_ROWS), 1).astype(F32)
    w = jnp.zeros((MOE_TILE, LOCAL_ROWS), F32)
    for k in range(TOP_K):
        w = jnp.where(p == rows[k], gate[:, k:k + 1], w)

    _wait_rows(yl_ref.at[slot], tot_ref[i], sems.at[slot])
    acc = _dot(w.astype(BF16), yl_ref[slot].astype(BF16))
    x = x_ref[...] + m_ref[0][5:6] * acc
    if not final:
        o_refs[0][...] = x
        return
    y = _rms(x, fn_ref[...])
    ctx_ref, lat_ref = o_refs

    @pl.when(i < N_CTX_TOK // MOE_TILE)
    def _():
        ctx_ref[...] = y

    lat_ref[...] = y


def _combine(plan, x, gates, idx, rank, mods, final_norm, ys, final):
    tile = lambda w: pl.BlockSpec((MOE_TILE, w), lambda i, *_: (i, 0))
    ctx_tiles = N_CTX_TOK // MOE_TILE
    if final:
        out_specs = [pl.BlockSpec((MOE_TILE, D_MODEL), lambda i, *_: (jnp.minimum(i, ctx_tiles - 1), 0)),
                     pl.BlockSpec((MOE_TILE, D_MODEL), lambda i, *_: (jnp.maximum(i - ctx_tiles, 0), 0))]
        out_shape = [jax.ShapeDtypeStruct((N_CTX_TOK, D_MODEL), F32), jax.ShapeDtypeStruct((N_LAT_TOK, D_MODEL), F32)]
    else:
        out_specs = [tile(D_MODEL)]
        out_shape = [jax.ShapeDtypeStruct((N_TOK, D_MODEL), F32)]
    return pl.pallas_call(
        functools.partial(_combine_kernel, final=final),
        grid_spec=pltpu.PrefetchScalarGridSpec(
            num_scalar_prefetch=4,
            grid=(N_MOE_TILES,),
            in_specs=[tile(D_MODEL), tile(LANES), tile(LANES), tile(LANES),
                      pl.BlockSpec((1, 1, LANES), lambda i, *_: (i, 0, 0)),
                      pl.BlockSpec((1, 6, D_MODEL), lambda i, *_: (_cond_of_tile(i, MOE_TILE), 0, 0)),
                      pl.BlockSpec((1, D_MODEL), lambda i, *_: (0, 0)),
                      pl.BlockSpec(memory_space=pl.ANY)],
            out_specs=out_specs,
            scratch_shapes=[pltpu.VMEM((2, LOCAL_ROWS, D_MODEL), F32),
                            pltpu.SemaphoreType.DMA((2,))]),
        out_shape=out_shape,
        compiler_params=_cparams("arbitrary"),
        name="moe_combine",
    )(plan["run"], plan["local"], plan["global"], plan["total"], x, gates, idx, rank, plan["base"], mods,
      final_norm, ys)


def _moe_plan(rank, cnt):
    first = rank[::MOE_TILE, :N_EXPERTS].astype(I32)
    total = cnt[0:1, :N_EXPERTS].astype(I32)
    run = jnp.concatenate([first[1:], total], axis=0) - first
    run = (run + RUN_ALIGN - 1) // RUN_ALIGN * RUN_ALIGN
    local = jnp.cumsum(run, axis=1) - run
    sizes = run.sum(axis=0)
    padded = (sizes + MOE_BM - 1) // MOE_BM * MOE_BM
    ends = jnp.cumsum(padded)
    starts = ends - padded
    glob = starts[None, :] + jnp.cumsum(run, axis=0) - run
    blk_start = jnp.arange(MOE_NB, dtype=I32) * MOE_BM
    blk_expert = jnp.minimum(jnp.sum(ends[None, :] <= blk_start[:, None], axis=1), N_EXPERTS - 1).astype(I32)
    blk_new = jnp.concatenate([jnp.ones((1,), I32), (blk_expert[1:] != blk_expert[:-1]).astype(I32)])
    blk = jnp.arange(MOE_NB, dtype=I32)
    later_start = (blk_new[None, :] == 1) & (blk[None, :] > blk[:, None])
    next_start = jnp.min(jnp.where(later_start, blk[None, :], MOE_NB), axis=1)
    next_hot = (blk[None, :] == next_start[:, None]).astype(I32)
    next_expert = jnp.where(next_start < MOE_NB, jnp.sum(next_hot * blk_expert[None, :], axis=1), blk_expert)
    blk_fetch = jnp.where(blk_new == 1, blk_expert, next_expert)
    base = _pad_lanes((local - first).astype(F32)).reshape(N_MOE_TILES, 1, LANES)
    return {"run": run.reshape(-1), "local": local.reshape(-1), "global": glob.reshape(-1), "total": run.sum(axis=1),
            "fill_start": starts + sizes, "fill_len": padded - sizes, "n_used": ends[-1:] // MOE_BM,
            "blk_expert": blk_expert, "blk_new": blk_new, "blk_fetch": blk_fetch, "base": base}


def _pad_lanes(v, fill=0.0):
    return jnp.pad(v, ((0, 0), (0, LANES - v.shape[-1])), constant_values=fill)


def kernel(x_prompt, x_sample, cache_na_k, cache_na_v, cache_gqa_k, cache_gqa_v, state_ssm, c, c_ctx, w_ada, b_ada, norm_mix, norm_ffn, w_in, na_rpb, gqa_q_norm, gqa_k_norm, ssm_conv_w, ssm_conv_b, ssm_dt_bias, ssm_a_log, ssm_d, ssm_norm, w_out, w_router, b_router, w_up, b_up, w_down, b_down, final_norm):
    x = jnp.concatenate([x_prompt.reshape(N_CTX_TOK, D_MODEL), x_sample.reshape(N_LAT_TOK, D_MODEL)], axis=0)
    conds = jnp.concatenate([c_ctx[None], c, jnp.zeros((COND_ROWS - N_COND, D_MODEL), F32)], axis=0)
    mods = _adaln(conds, w_ada, b_ada).reshape(DEPTH, COND_ROWS, 6, D_MODEL)
    cos, sin = _rope_tables()

    b_up_s = b_up.reshape(DEPTH, N_EXPERTS, 2 * D_FF // UP_GROUP, UP_GROUP // 2, 2)
    b_up_s = jnp.swapaxes(b_up_s, -1, -2).reshape(DEPTH, N_EXPERTS, 1, 2 * D_FF)
    b_down_s = b_down.reshape(DEPTH, N_EXPERTS, 1, D_MODEL)

    ctx_out = []
    for l in range(DEPTH):
        qkv, gqa, z, xbc, dt_raw = _inproj(l, x, mods[l], norm_mix[l][None], w_in)
        qn, kn = gqa_q_norm[l][None], gqa_k_norm[l][None]

        nao_c, go_c, gk_c = _ctx_attn(qkv, gqa, qn, kn)
        go_l = _lat_gqa(gqa, cache_gqa_k[:, l].reshape(DEC_BATCH, PAST_LEN, GQA_KV_DIM),
                        cache_gqa_v[:, l].reshape(DEC_BATCH, PAST_LEN, GQA_KV_DIM), cos, sin, qn, kn)
        nao_l = _lat_na(qkv, cache_na_k[:, l].reshape(DEC_BATCH, PAST_LEN, NA_DIM),
                        cache_na_v[:, l].reshape(DEC_BATCH, PAST_LEN, NA_DIM), _na_bias_tables(na_rpb[l]))

        xa = _conv_act(xbc, ssm_conv_w[l], ssm_conv_b[l][None])
        zeros = jnp.zeros((BATCH, SSM_INNER, SSM_STATE), F32)
        h0 = [jnp.concatenate([zeros, state_ssm[:, l, d].reshape(DEC_BATCH, SSM_INNER, SSM_STATE)], axis=0)
              for d in range(2)]
        y_f, st_f, y_b, st_b = _ssd(xa, dt_raw, h0[0], h0[1], _pad_lanes(ssm_dt_bias[l]), _pad_lanes(ssm_a_log[l]))
        ys = [y_f, y_b]
        sts = [st[:BATCH].reshape(BATCH, SSM_HEADS, SSM_HEAD_DIM, SSM_STATE) for st in (st_f, st_b)]

        x, h, top_idx, gates, sel = _outproj(
            l, x, jnp.concatenate([nao_c, nao_l], axis=0), jnp.concatenate([go_c, go_l], axis=0),
            ys[0], ys[1], xa, z, jnp.repeat(ssm_d[l], SSM_HEAD_DIM)[None], ssm_norm[l][None], w_out,
            mods[l], norm_ffn[l][None], _pad_lanes(w_router[l]), _pad_lanes(b_router[l][None], NEG_INF))

        rank, cnt = _ranks(sel)
        plan = _moe_plan(rank, cnt)
        y_sorted = _experts(l, plan, _dispatch(plan, h, top_idx, rank), w_up, b_up_s[l], w_down, b_down_s[l])
        outs = _combine(plan, x, gates, top_idx, rank, mods[l], final_norm[None], y_sorted, final=(l == DEPTH - 1))
        x = outs[0]

        ctx_out.append((
            qkv[:N_CTX_TOK, NA_DIM:2 * NA_DIM].reshape(BATCH, SEQ, NA_HEADS, HEAD_DIM),
            qkv[:N_CTX_TOK, 2 * NA_DIM:].reshape(BATCH, SEQ, NA_HEADS, HEAD_DIM),
            gk_c.reshape(BATCH, SEQ, GQA_KV_HEADS, HEAD_DIM),
            gqa[:N_CTX_TOK, GQA_Q_DIM + GQA_KV_DIM:].reshape(BATCH, SEQ, GQA_KV_HEADS, HEAD_DIM),
            jnp.stack(sts, axis=1)))

    y_prompt = outs[0].reshape(BATCH, SEQ, D_MODEL)
    y_sample = outs[1].reshape(DEC_BATCH, DEC_SEQ, D_MODEL)
    return (y_prompt, y_sample) + tuple(jnp.stack([e[i] for e in ctx_out], axis=1) for i in range(5))
```

```python
import functools

import numpy as np
import jax
import jax.numpy as jnp
from jax import lax
from jax.experimental import pallas as pl
from jax.experimental.pallas import tpu as pltpu

F32 = jnp.float32
BF16 = jnp.bfloat16
I32 = jnp.int32

D_MODEL = 1024
BATCH = 16
SEQ = 256
DEPTH = 2
DEC_BATCH = 2
DEC_SEQ = 2048
PAST_LEN = 512
GRID_W = 64
HEAD_DIM = 64
NA_HEADS = 4
NA_WIN_ROWS = 8
NA_WIN_COLS = 16
GQA_HEADS = 4
GQA_KV_HEADS = 2
ROPE_THETA = 10000.0
SSM_HEADS = 8
SSM_HEAD_DIM = 64
SSM_STATE = 64
SSM_GROUPS = 2
SSM_INNER = SSM_HEADS * SSM_HEAD_DIM
SSM_BC_DIM = SSM_GROUPS * SSM_STATE
CONV_DIM = SSM_INNER + 2 * SSM_BC_DIM
CONV_W = 5
CHUNK = 128
NA_DIM = NA_HEADS * HEAD_DIM
GQA_Q_DIM = GQA_HEADS * HEAD_DIM
GQA_KV_DIM = GQA_KV_HEADS * HEAD_DIM
D_MIX = NA_DIM + GQA_Q_DIM + SSM_INNER
IN_DIM = 3 * NA_DIM + GQA_Q_DIM + 2 * GQA_KV_DIM + SSM_INNER + CONV_DIM + SSM_HEADS
N_EXPERTS = 32
TOP_K = 4
D_FF = D_MODEL
SWIGLU_LIMIT = 7.0
SWIGLU_ALPHA = 1.702
EPS = 1e-6
NEG_INF = -1e30

LANES = 128
N_CTX_TOK = BATCH * SEQ
N_LAT_TOK = DEC_BATCH * DEC_SEQ
N_TOK = N_CTX_TOK + N_LAT_TOK
N_COND = 1 + DEC_BATCH
COND_ROWS = 16
IN_PAD = 3 * NA_DIM + GQA_Q_DIM + 2 * GQA_KV_DIM + SSM_INNER + CONV_DIM + LANES
ROW_TILE = 256
N_ROW_TILES = N_TOK // ROW_TILE
MOE_BM = 256
N_SLOTS = N_TOK * TOP_K
RUN_ALIGN = 8
MOE_TILE = 512
N_MOE_TILES = N_TOK // MOE_TILE
N_RUNS = N_MOE_TILES * N_EXPERTS
LOCAL_ROWS = -(-(MOE_TILE * TOP_K + N_EXPERTS * (RUN_ALIGN - 1)) // MOE_BM) * MOE_BM
MOE_NB = -(-(N_SLOTS + N_RUNS * (RUN_ALIGN - 1)) // MOE_BM) + N_EXPERTS
UP_GROUP = 256
N_SEQ = BATCH + DEC_BATCH
N_CHUNKS = N_TOK // CHUNK
N_CTX_CHUNKS = N_CTX_TOK // CHUNK
VMEM_LIMIT = 56 * 1024 * 1024


def _cparams(*sem):
    return pltpu.CompilerParams(dimension_semantics=sem, vmem_limit_bytes=VMEM_LIMIT)


def _sigmoid(x):
    return 1.0 / (1.0 + jnp.exp(-x))


def _dot(a, b):
    return jnp.dot(a, b, preferred_element_type=F32)


def _dot_nt(a, b):
    return lax.dot_general(a, b, (((1,), (1,)), ((), ())), preferred_element_type=F32)


def _dot_exact(a, b):
    return jnp.dot(a, b, preferred_element_type=F32, precision=lax.Precision.HIGHEST)


def _rms(x, g):
    return x * lax.rsqrt(jnp.mean(x * x, axis=-1, keepdims=True) + EPS) * g


def _cond_of_tile(i, rows=ROW_TILE):
    ctx_tiles = N_CTX_TOK // rows
    return jnp.where(i < ctx_tiles, 0, 1 + (i - ctx_tiles) // (DEC_SEQ // rows))


def _adaln_kernel(c_ref, w_ref, b_ref, o_ref):
    c = c_ref[...]
    s = (c * _sigmoid(c)).astype(BF16)
    o_ref[0] = _dot(s, w_ref[0].astype(BF16)) + b_ref[0]


def _adaln(conds, w_ada, b_ada):
    tn = 1536
    return pl.pallas_call(
        _adaln_kernel,
        grid=(DEPTH, 6 * D_MODEL // tn),
        in_specs=[pl.BlockSpec((COND_ROWS, D_MODEL), lambda l, j: (0, 0)),
                  pl.BlockSpec((1, D_MODEL, tn), lambda l, j: (l, 0, j)),
                  pl.BlockSpec((1, 1, tn), lambda l, j: (l, 0, j))],
        out_specs=pl.BlockSpec((1, COND_ROWS, tn), lambda l, j: (l, 0, j)),
        out_shape=jax.ShapeDtypeStruct((DEPTH, COND_ROWS, 6 * D_MODEL), F32),
        compiler_params=_cparams("parallel", "parallel"),
        name="adaln",
    )(conds, w_ada, b_ada.reshape(DEPTH, 1, 6 * D_MODEL))


_IN_SPLITS = (3 * NA_DIM, GQA_Q_DIM + 2 * GQA_KV_DIM, SSM_INNER, CONV_DIM, LANES)


def _inproj_kernel(x_ref, m_ref, g_ref, w_ref, qkv_ref, gqa_ref, z_ref, xbc_ref, dt_ref, w_s):
    @pl.when(pl.program_id(0) == 0)
    def _():
        w_s[...] = jnp.zeros_like(w_s)
        w_s[:, 0:IN_DIM] = w_ref[0].astype(BF16)

    m = m_ref[0]
    h = _rms(x_ref[...], g_ref[...]) * (1.0 + m[1:2]) + m[0:1]
    p = _dot(h.astype(BF16), w_s[...])
    off = 0
    for ref, width in zip((qkv_ref, gqa_ref, z_ref, xbc_ref, dt_ref), _IN_SPLITS):
        ref[...] = p[:, off:off + width]
        off += width


def _inproj(layer, x, mods, gain, w_in):
    row = lambda w: pl.BlockSpec((ROW_TILE, w), lambda i: (i, 0))
    return pl.pallas_call(
        _inproj_kernel,
        grid=(N_ROW_TILES,),
        in_specs=[row(D_MODEL),
                  pl.BlockSpec((1, 6, D_MODEL), lambda i: (_cond_of_tile(i), 0, 0)),
                  pl.BlockSpec((1, D_MODEL), lambda i: (0, 0)),
                  pl.BlockSpec((1, D_MODEL, IN_DIM), lambda i: (layer, 0, 0))],
        out_specs=[row(w) for w in _IN_SPLITS],
        out_shape=[jax.ShapeDtypeStruct((N_TOK, w), F32) for w in _IN_SPLITS],
        scratch_shapes=[pltpu.VMEM((D_MODEL, IN_PAD), BF16)],
        compiler_params=_cparams("arbitrary"),
        name="inproj",
    )(x, mods, gain, w_in)


def _softmax_pv(scores, values):
    m = scores[0].max(axis=-1, keepdims=True)
    for s in scores[1:]:
        m = jnp.maximum(m, s.max(axis=-1, keepdims=True))
    den = 0.0
    acc = 0.0
    for s, v in zip(scores, values):
        e = jnp.exp(s - m)
        den = den + e.sum(axis=-1, keepdims=True)
        acc = acc + _dot(e.astype(BF16), v)
    return acc / den


def _heads_rms(x, n_heads, g):
    return jnp.concatenate(
        [_rms(x[:, h * HEAD_DIM:(h + 1) * HEAD_DIM], g) for h in range(n_heads)], axis=-1)


def _rope(x, cos, sin_signed):
    w = x.shape[-1]
    lane = lax.broadcasted_iota(I32, x.shape, 1)
    partner = jnp.where((lane & 1) == 0, pltpu.roll(x, w - 1, 1), pltpu.roll(x, 1, 1))
    return x * cos + partner * sin_signed


_ATT_SCALE = HEAD_DIM ** -0.5


def _ctx_attn_kernel(qkv_ref, gqa_ref, qn_ref, kn_ref, nao_ref, go_ref, gk_ref):
    outs = []
    for h in range(NA_HEADS):
        sl = slice(h * HEAD_DIM, (h + 1) * HEAD_DIM)
        q = qkv_ref[:, sl].astype(BF16)
        k = qkv_ref[:, NA_DIM + h * HEAD_DIM:NA_DIM + (h + 1) * HEAD_DIM].astype(BF16)
        v = qkv_ref[:, 2 * NA_DIM + h * HEAD_DIM:2 * NA_DIM + (h + 1) * HEAD_DIM].astype(BF16)
        outs.append(_softmax_pv([_dot_nt(q, k) * _ATT_SCALE], [v]))
    nao_ref[...] = jnp.concatenate(outs, axis=-1).astype(BF16)

    gq = _heads_rms(gqa_ref[:, 0:GQA_Q_DIM], GQA_HEADS, qn_ref[...])
    gk = _heads_rms(gqa_ref[:, GQA_Q_DIM:GQA_Q_DIM + GQA_KV_DIM], GQA_KV_HEADS, kn_ref[...])
    gk_ref[...] = gk
    rep = GQA_HEADS // GQA_KV_HEADS
    outs = []
    for h in range(GQA_HEADS):
        g = h // rep
        q = gq[:, h * HEAD_DIM:(h + 1) * HEAD_DIM].astype(BF16)
        k = gk[:, g * HEAD_DIM:(g + 1) * HEAD_DIM].astype(BF16)
        v0 = GQA_Q_DIM + GQA_KV_DIM + g * HEAD_DIM
        v = gqa_ref[:, v0:v0 + HEAD_DIM].astype(BF16)
        outs.append(_softmax_pv([_dot_nt(q, k) * _ATT_SCALE], [v]))
    go_ref[...] = jnp.concatenate(outs, axis=-1).astype(BF16)


def _ctx_attn(qkv, gqa, q_norm, k_norm):
    return pl.pallas_call(
        _ctx_attn_kernel,
        grid=(BATCH,),
        in_specs=[pl.BlockSpec((SEQ, 3 * NA_DIM), lambda b: (b, 0)),
                  pl.BlockSpec((SEQ, GQA_Q_DIM + 2 * GQA_KV_DIM), lambda b: (b, 0)),
                  pl.BlockSpec((1, HEAD_DIM), lambda b: (0, 0)),
                  pl.BlockSpec((1, HEAD_DIM), lambda b: (0, 0))],
        out_specs=[pl.BlockSpec((SEQ, NA_DIM), lambda b: (b, 0)),
                   pl.BlockSpec((SEQ, GQA_Q_DIM), lambda b: (b, 0)),
                   pl.BlockSpec((SEQ, GQA_KV_DIM), lambda b: (b, 0))],
        out_shape=[jax.ShapeDtypeStruct((N_CTX_TOK, NA_DIM), BF16),
                   jax.ShapeDtypeStruct((N_CTX_TOK, GQA_Q_DIM), BF16),
                   jax.ShapeDtypeStruct((N_CTX_TOK, GQA_KV_DIM), F32)],
        compiler_params=_cparams("parallel"),
        name="ctx_attn",
    )(qkv, gqa, q_norm, k_norm)


GQA_TQ = 256
GQA_KEYS = PAST_LEN + DEC_SEQ


def _lat_gqa_kernel(gqa_ref, ck_ref, cv_ref, cos_ref, sin_ref, qn_ref, kn_ref, o_ref, kbuf, vbuf):
    qb = pl.program_id(1)

    @pl.when(qb == 0)
    def _():
        kbuf[0:PAST_LEN, :] = ck_ref[0, 0].astype(BF16)
        vbuf[0:PAST_LEN, :] = cv_ref[0, 0].astype(BF16)
        k = _heads_rms(gqa_ref[:, GQA_Q_DIM:GQA_Q_DIM + GQA_KV_DIM], GQA_KV_HEADS, kn_ref[...])
        k = _rope(k, cos_ref[:, 0:GQA_KV_DIM], sin_ref[:, 0:GQA_KV_DIM])
        kbuf[PAST_LEN:GQA_KEYS, :] = k.astype(BF16)
        vbuf[PAST_LEN:GQA_KEYS, :] = gqa_ref[:, GQA_Q_DIM + GQA_KV_DIM:].astype(BF16)

    r0 = pl.multiple_of(qb * GQA_TQ, GQA_TQ)
    q = _heads_rms(gqa_ref[pl.ds(r0, GQA_TQ), 0:GQA_Q_DIM], GQA_HEADS, qn_ref[...])
    q = _rope(q, cos_ref[pl.ds(r0, GQA_TQ), :], sin_ref[pl.ds(r0, GQA_TQ), :]).astype(BF16)
    rep = GQA_HEADS // GQA_KV_HEADS
    outs = []
    for h in range(GQA_HEADS):
        g = h // rep
        k = kbuf[:, g * HEAD_DIM:(g + 1) * HEAD_DIM]
        v = vbuf[:, g * HEAD_DIM:(g + 1) * HEAD_DIM]
        s = _dot_nt(q[:, h * HEAD_DIM:(h + 1) * HEAD_DIM], k) * _ATT_SCALE
        outs.append(_softmax_pv([s], [v]))
    o_ref[...] = jnp.concatenate(outs, axis=-1).astype(BF16)


def _lat_gqa(layer, gqa, cache_k, cache_v, cos, sin, q_norm, k_norm):
    lat_blk = N_CTX_TOK // DEC_SEQ
    return pl.pallas_call(
        _lat_gqa_kernel,
        grid=(DEC_BATCH, DEC_SEQ // GQA_TQ),
        in_specs=[pl.BlockSpec((DEC_SEQ, GQA_Q_DIM + 2 * GQA_KV_DIM), lambda b, q: (lat_blk + b, 0)),
                  pl.BlockSpec((1, 1, PAST_LEN, GQA_KV_DIM), lambda b, q: (b, layer, 0, 0)),
                  pl.BlockSpec((1, 1, PAST_LEN, GQA_KV_DIM), lambda b, q: (b, layer, 0, 0)),
                  pl.BlockSpec((DEC_SEQ, GQA_Q_DIM), lambda b, q: (0, 0)),
                  pl.BlockSpec((DEC_SEQ, GQA_Q_DIM), lambda b, q: (0, 0)),
                  pl.BlockSpec((1, HEAD_DIM), lambda b, q: (0, 0)),
                  pl.BlockSpec((1, HEAD_DIM), lambda b, q: (0, 0))],
        out_specs=pl.BlockSpec((GQA_TQ, GQA_Q_DIM), lambda b, q: (b * (DEC_SEQ // GQA_TQ) + q, 0)),
        out_shape=jax.ShapeDtypeStruct((N_LAT_TOK, GQA_Q_DIM), BF16),
        scratch_shapes=[pltpu.VMEM((GQA_KEYS, GQA_KV_DIM), BF16),
                        pltpu.VMEM((GQA_KEYS, GQA_KV_DIM), BF16)],
        compiler_params=_cparams("arbitrary", "arbitrary"),
        name="lat_gqa",
    )(gqa, cache_k, cache_v, cos, sin, q_norm, k_norm)


def _rope_tables():
    t = jnp.arange(DEC_SEQ)
    row = (t // GRID_W).astype(F32)
    col = (t % GRID_W).astype(F32)
    axis_dim = HEAD_DIM // 2
    inv_freq = ROPE_THETA ** (-jnp.arange(0, axis_dim, 2, dtype=F32) / axis_dim)
    ang = jnp.concatenate([row[:, None] * inv_freq, col[:, None] * inv_freq], axis=-1)
    cos = jnp.repeat(jnp.cos(ang), 2, axis=-1)
    sin = jnp.repeat(jnp.sin(ang), 2, axis=-1) * jnp.tile(jnp.array([-1.0, 1.0], F32), HEAD_DIM // 2)
    return jnp.tile(cos, (1, GQA_HEADS)), jnp.tile(sin, (1, GQA_HEADS))


NA_ROWS = DEC_SEQ // GRID_W
NA_KEYS = NA_WIN_ROWS * GRID_W


NA_ROWS_PER_STEP = 2


def _lat_na_kernel(qkv_ref, ck_ref, cv_ref, *rest):
    bias_refs, o_ref = rest[:NA_ROWS_PER_STEP], rest[NA_ROWS_PER_STEP]
    for j in range(NA_ROWS_PER_STEP):
        r = pl.program_id(1) * NA_ROWS_PER_STEP + j
        r0 = jnp.clip(r - NA_WIN_ROWS // 2, 0, NA_ROWS - NA_WIN_ROWS)
        q0 = pl.multiple_of(r * GRID_W, GRID_W)
        k0 = pl.multiple_of(r0 * GRID_W, GRID_W)
        outs = []
        for h in range(NA_HEADS):
            c0 = h * HEAD_DIM
            q = qkv_ref[pl.ds(q0, GRID_W), c0:c0 + HEAD_DIM].astype(BF16)
            k = qkv_ref[pl.ds(k0, NA_KEYS), NA_DIM + c0:NA_DIM + c0 + HEAD_DIM].astype(BF16)
            v = qkv_ref[pl.ds(k0, NA_KEYS), 2 * NA_DIM + c0:2 * NA_DIM + c0 + HEAD_DIM].astype(BF16)
            kc = ck_ref[0, 0, :, c0:c0 + HEAD_DIM].astype(BF16)
            vc = cv_ref[0, 0, :, c0:c0 + HEAD_DIM].astype(BF16)
            s_nb = _dot_nt(q, k) * _ATT_SCALE + bias_refs[j][0, h]
            s_ctx = _dot_nt(q, kc) * _ATT_SCALE
            outs.append(_softmax_pv([s_nb, s_ctx], [v, vc]))
        o_ref[j * GRID_W:(j + 1) * GRID_W, :] = jnp.concatenate(outs, axis=-1).astype(BF16)


def _na_row_offset(r):
    return r - jnp.clip(r - NA_WIN_ROWS // 2, 0, NA_ROWS - NA_WIN_ROWS)


def _lat_na(layer, qkv, cache_k, cache_v, bias):
    lat_blk = N_CTX_TOK // DEC_SEQ
    steps = NA_ROWS // NA_ROWS_PER_STEP
    bias_spec = lambda j: pl.BlockSpec((1, NA_HEADS, GRID_W, NA_KEYS),
                                       lambda b, s: (_na_row_offset(s * NA_ROWS_PER_STEP + j), 0, 0, 0))
    return pl.pallas_call(
        _lat_na_kernel,
        grid=(DEC_BATCH, steps),
        in_specs=[pl.BlockSpec((DEC_SEQ, 3 * NA_DIM), lambda b, s: (lat_blk + b, 0)),
                  pl.BlockSpec((1, 1, PAST_LEN, NA_DIM), lambda b, s: (b, layer, 0, 0)),
                  pl.BlockSpec((1, 1, PAST_LEN, NA_DIM), lambda b, s: (b, layer, 0, 0))]
                 + [bias_spec(j) for j in range(NA_ROWS_PER_STEP)],
        out_specs=pl.BlockSpec((NA_ROWS_PER_STEP * GRID_W, NA_DIM), lambda b, s: (b * steps + s, 0)),
        out_shape=jax.ShapeDtypeStruct((N_LAT_TOK, NA_DIM), BF16),
        compiler_params=_cparams("parallel", "arbitrary"),
        name="lat_na",
    )(qkv, cache_k, cache_v, *([bias] * NA_ROWS_PER_STEP))


def _na_bias_tables(rpb):
    d = np.arange(NA_WIN_ROWS)[:, None]
    kr = np.arange(NA_WIN_ROWS)[None, :]
    dr = kr - d + NA_WIN_ROWS - 1
    qc = np.arange(GRID_W)[:, None]
    kc = np.arange(GRID_W)[None, :]
    col0 = np.clip(qc - NA_WIN_COLS // 2, 0, GRID_W - NA_WIN_COLS)
    in_win = (kc >= col0) & (kc < col0 + NA_WIN_COLS)
    dc = np.clip(kc - qc + NA_WIN_COLS - 1, 0, 2 * NA_WIN_COLS - 2)
    row_hot = (dr[:, :, None] == np.arange(2 * NA_WIN_ROWS - 1)).astype(np.float32)
    col_hot = (dc[:, :, None] == np.arange(2 * NA_WIN_COLS - 1)).astype(np.float32)
    b = jnp.einsum('hac,dka,qxc->dhqkx', rpb.astype(F32), row_hot, col_hot, precision=lax.Precision.HIGHEST)
    b = jnp.where(in_win[None, None, :, None, :], b, NEG_INF)
    return b.reshape(NA_WIN_ROWS, NA_HEADS, GRID_W, NA_KEYS)


CONV_TB = 1024
CONV_HALO = 8
CONV_HALO_BLOCKS = CONV_TB // CONV_HALO


def _conv_kernel(prev_ref, x_ref, next_ref, w_ref, b_ref, o_ref):
    i = pl.program_id(0)
    seq = jnp.where(i < N_CTX_TOK // CONV_TB, SEQ, DEC_SEQ)
    x = x_ref[...]
    ext = jnp.concatenate([prev_ref[...], x, next_ref[...]], axis=0)
    n_ext = CONV_TB + 2 * CONV_HALO
    pos = (lax.broadcasted_iota(I32, (CONV_TB, 1), 0) + i * CONV_TB) & (seq - 1)
    half = CONV_W // 2
    acc = x * w_ref[half:half + 1, :]
    for s in range(-half, half + 1):
        if s == 0:
            continue
        shifted = pltpu.roll(ext, (-s) % n_ext, 0)[CONV_HALO:CONV_HALO + CONV_TB]
        valid = (pos + s >= 0) & (pos + s < seq)
        acc = acc + jnp.where(valid, shifted, 0.0) * w_ref[half + s:half + s + 1, :]
    acc = acc + b_ref[...]
    o_ref[...] = acc * _sigmoid(acc)


def _conv_act(xbc, conv_w, conv_b):
    return pl.pallas_call(
        _conv_kernel,
        grid=(N_TOK // CONV_TB,),
        in_specs=[pl.BlockSpec((CONV_HALO, CONV_DIM),
                               lambda i: (jnp.maximum(i * CONV_HALO_BLOCKS - 1, 0), 0)),
                  pl.BlockSpec((CONV_TB, CONV_DIM), lambda i: (i, 0)),
                  pl.BlockSpec((CONV_HALO, CONV_DIM),
                               lambda i: (jnp.minimum((i + 1) * CONV_HALO_BLOCKS, N_TOK // CONV_HALO - 1), 0)),
                  pl.BlockSpec((CONV_W, CONV_DIM), lambda i: (0, 0)),
                  pl.BlockSpec((1, CONV_DIM), lambda i: (0, 0))],
        out_specs=pl.BlockSpec((CONV_TB, CONV_DIM), lambda i: (i, 0)),
        out_shape=jax.ShapeDtypeStruct((N_TOK, CONV_DIM), F32),
        compiler_params=_cparams("parallel"),
        name="conv_act",
    )(xbc, xbc, xbc, conv_w, conv_b)


def _chunk_seq(g):
    ctx_n = SEQ // CHUNK
    lat_n = DEC_SEQ // CHUNK
    is_ctx = g < N_CTX_CHUNKS
    gl = g - N_CTX_CHUNKS
    sid = jnp.where(is_ctx, g // ctx_n, BATCH + gl // lat_n)
    cin = jnp.where(is_ctx, g % ctx_n, gl % lat_n)
    n = jnp.where(is_ctx, ctx_n, lat_n)
    return sid, cin, n


def _ssd_init(h0_ref, st_ref, gg, reverse):
    sid, cin, n = _chunk_seq(gg)

    @pl.when(cin == (n - 1 if reverse else 0))
    def _():
        st_ref[0] = jnp.where(sid >= BATCH, h0_ref[0, 0, 0], 0.0)


def _ssd_chunk(xa_ref, dt_ref, dtb, alog, y_ref, st_ref, reverse):
    x = dt_ref[...] + dtb
    dt = jnp.maximum(x, 0.0) + jnp.log1p(jnp.exp(-jnp.abs(x)))
    dta = dt * -jnp.exp(alog)
    ii = lax.broadcasted_iota(I32, (CHUNK, CHUNK), 0)
    jj = lax.broadcasted_iota(I32, (CHUNK, CHUNK), 1)
    tri = (jj >= ii) if reverse else (jj <= ii)
    cum = _dot_exact(tri.astype(F32), dta)
    cum_t = cum.T
    edge = 0 if reverse else CHUNK - 1
    tot = cum[edge:edge + 1, :]
    first = jj < SSM_HEAD_DIM
    first_rows = lax.broadcasted_iota(I32, (2 * SSM_HEAD_DIM, SSM_STATE), 0) < SSM_HEAD_DIM
    rep = SSM_HEADS // SSM_GROUPS
    for grp in range(SSM_GROUPS):
        bg = xa_ref[:, SSM_INNER + grp * SSM_STATE:SSM_INNER + (grp + 1) * SSM_STATE].astype(BF16)
        c0 = SSM_INNER + SSM_BC_DIM + grp * SSM_STATE
        cg = xa_ref[:, c0:c0 + SSM_STATE].astype(BF16)
        cb = _dot_nt(cg, bg)
        for ha in range(grp * rep, (grp + 1) * rep, 2):
            hb = ha + 1
            slab = slice(ha * SSM_HEAD_DIM, (hb + 1) * SSM_HEAD_DIM)
            col_a, col_b = cum[:, ha:ha + 1], cum[:, hb:hb + 1]
            tot_a, tot_b = tot[:, ha:ha + 1], tot[:, hb:hb + 1]
            decay_a = jnp.where(tri, jnp.exp(jnp.minimum(col_a - cum_t[ha:ha + 1, :], 0.0)), 0.0)
            decay_b = jnp.where(tri, jnp.exp(jnp.minimum(col_b - cum_t[hb:hb + 1, :], 0.0)), 0.0)
            xdt = xa_ref[:, slab] * jnp.where(first, dt[:, ha:ha + 1], dt[:, hb:hb + 1])
            xdt_b = xdt.astype(BF16)
            y = jnp.where(first, _dot((cb * decay_a).astype(BF16), xdt_b), _dot((cb * decay_b).astype(BF16), xdt_b))
            state = st_ref[0, slab, :]
            y = y + _dot_nt(cg, state.astype(BF16)) * jnp.where(first, jnp.exp(col_a), jnp.exp(col_b))
            y_ref[:, slab] = y
            to_end = jnp.where(first, jnp.exp(tot_a - col_a), jnp.exp(tot_b - col_b))
            upd = lax.dot_general((xdt * to_end).astype(BF16), bg, (((0,), (0,)), ((), ())),
                                  preferred_element_type=F32)
            st_ref[0, slab, :] = state * jnp.where(first_rows, jnp.exp(tot_a), jnp.exp(tot_b)) + upd


def _ssd_kernel(xaf_ref, dtf_ref, h0f_ref, xab_ref, dtb_ref, h0b_ref, bias_ref, alog_ref,
                yf_ref, stf_ref, yb_ref, stb_ref):
    g = pl.program_id(0)
    _ssd_init(h0f_ref, stf_ref, g, False)
    _ssd_chunk(xaf_ref, dtf_ref, bias_ref[0:1, :], alog_ref[0:1, :], yf_ref, stf_ref, False)
    _ssd_init(h0b_ref, stb_ref, N_CHUNKS - 1 - g, True)
    _ssd_chunk(xab_ref, dtb_ref, bias_ref[1:2, :], alog_ref[1:2, :], yb_ref, stb_ref, True)


def _ssd(layer, xa, dt_raw, h0, dt_bias, a_log):
    rev = lambda g: N_CHUNKS - 1 - g
    chunk = lambda w, order: pl.BlockSpec((CHUNK, w), lambda g: (order(g), 0))
    state = lambda order: pl.BlockSpec((1, SSM_INNER, SSM_STATE), lambda g: (_chunk_seq(order(g))[0], 0, 0))
    start = lambda order, d: pl.BlockSpec(
        (1, 1, 1, SSM_INNER, SSM_STATE),
        lambda g: (jnp.maximum(_chunk_seq(order(g))[0] - BATCH, 0), layer, d, 0, 0))
    same = lambda g: g
    y_shape = jax.ShapeDtypeStruct((N_TOK, SSM_INNER), F32)
    st_shape = jax.ShapeDtypeStruct((N_SEQ, SSM_INNER, SSM_STATE), F32)
    h0_fwd = h0_bwd = h0
    return pl.pallas_call(
        _ssd_kernel,
        grid=(N_CHUNKS,),
        in_specs=[chunk(CONV_DIM, same), chunk(LANES, same), start(same, 0),
                  chunk(CONV_DIM, rev), chunk(LANES, rev), start(rev, 1),
                  pl.BlockSpec((2, LANES), lambda g: (0, 0)),
                  pl.BlockSpec((2, LANES), lambda g: (0, 0))],
        out_specs=[chunk(SSM_INNER, same), state(same), chunk(SSM_INNER, rev), state(rev)],
        out_shape=[y_shape, st_shape, y_shape, st_shape],
        compiler_params=_cparams("arbitrary"),
        name="ssd",
    )(xa, dt_raw, h0_fwd, xa, dt_raw, h0_bwd, dt_bias, a_log)


def _outproj_kernel(x_ref, nao_c_ref, nao_l_ref, go_c_ref, go_l_ref, yf_ref, yb_ref, xs_ref, z_ref, dsk_ref, sn_ref,
                    wo_ref, m_ref, g2_ref, wrh_ref, wrl_ref, br_ref, xo_ref, h_ref, idx_ref, gate_ref, sel_ref, wo_s):
    @pl.when(pl.program_id(0) == 0)
    def _():
        wo_s[...] = wo_ref[0].astype(BF16)

    is_ctx = pl.program_id(0) < N_CTX_TOK // ROW_TILE
    nao = jnp.where(is_ctx, nao_c_ref[...], nao_l_ref[...])
    go = jnp.where(is_ctx, go_c_ref[...], go_l_ref[...])
    m = m_ref[0]
    z = z_ref[...]
    y = (yf_ref[...] + yb_ref[...] + xs_ref[...] * dsk_ref[...]) * (z * _sigmoid(z))
    s_o = _rms(y, sn_ref[...]).astype(BF16)
    mix = (_dot(nao, wo_s[0:NA_DIM, :])
           + _dot(go, wo_s[NA_DIM:NA_DIM + GQA_Q_DIM, :])
           + _dot(s_o, wo_s[NA_DIM + GQA_Q_DIM:, :]))
    x = x_ref[...] + m[2:3] * mix
    xo_ref[...] = x
    h = _rms(x, g2_ref[...]) * (1.0 + m[4:5]) + m[3:4]
    h_hi = h.astype(BF16)
    h_ref[...] = h_hi

    h_lo = (h - h_hi.astype(F32)).astype(BF16)
    w_hi = wrh_ref[...]
    logits = _dot(h_hi, w_hi) + _dot(h_lo, w_hi) + _dot(h_hi, wrl_ref[...]) + br_ref[...]
    lane = lax.broadcasted_iota(I32, logits.shape, 1).astype(F32)
    vals, idxs = [], []
    for _ in range(TOP_K):
        v = logits.max(axis=-1, keepdims=True)
        i = jnp.where(logits == v, lane, float(LANES)).min(axis=-1, keepdims=True)
        vals.append(v)
        idxs.append(i)
        logits = jnp.where(lane == i, -jnp.inf, logits)
    es = [jnp.exp(v - vals[0]) for v in vals]
    den = es[0] + es[1] + es[2] + es[3]
    idx_out = jnp.zeros(lane.shape, F32)
    gate_out = jnp.zeros(lane.shape, F32)
    sel = jnp.zeros(lane.shape, F32)
    for k in range(TOP_K):
        idx_out = jnp.where(lane == float(k), idxs[k], idx_out)
        gate_out = jnp.where(lane == float(k), es[k] / den, gate_out)
        sel = jnp.where(lane == idxs[k], 1.0, sel)
    idx_ref[...] = idx_out.astype(I32)
    gate_ref[...] = gate_out
    sel_ref[...] = sel.astype(BF16)


def _outproj(layer, x, nao_c, nao_l, go_c, go_l, yf, yb, xa, z, d_skip, ssm_norm, w_out, mods, gain2, w_router,
             b_router):
    row = lambda w: pl.BlockSpec((ROW_TILE, w), lambda i: (i, 0))
    full = lambda a, b: pl.BlockSpec((a, b), lambda i: (0, 0))
    ctx_tiles = N_CTX_TOK // ROW_TILE
    ctx_row = lambda w: pl.BlockSpec((ROW_TILE, w), lambda i: (jnp.minimum(i, ctx_tiles - 1), 0))
    lat_row = lambda w: pl.BlockSpec((ROW_TILE, w), lambda i: (jnp.maximum(i - ctx_tiles, 0), 0))
    wr_hi = w_router.astype(BF16)
    wr_lo = (w_router - wr_hi.astype(F32)).astype(BF16)
    return pl.pallas_call(
        _outproj_kernel,
        grid=(N_ROW_TILES,),
        in_specs=[row(D_MODEL), ctx_row(NA_DIM), lat_row(NA_DIM), ctx_row(GQA_Q_DIM), lat_row(GQA_Q_DIM),
                  row(SSM_INNER), row(SSM_INNER),
                  row(SSM_INNER), row(SSM_INNER), full(1, SSM_INNER), full(1, SSM_INNER),
                  pl.BlockSpec((1, D_MIX, D_MODEL), lambda i: (layer, 0, 0)),
                  pl.BlockSpec((1, 6, D_MODEL), lambda i: (_cond_of_tile(i), 0, 0)),
                  full(1, D_MODEL), full(D_MODEL, LANES), full(D_MODEL, LANES), full(1, LANES)],
        out_specs=[row(D_MODEL), row(D_MODEL), row(LANES), row(LANES), row(LANES)],
        out_shape=[jax.ShapeDtypeStruct((N_TOK, D_MODEL), F32),
                   jax.ShapeDtypeStruct((N_TOK, D_MODEL), BF16),
                   jax.ShapeDtypeStruct((N_TOK, LANES), I32),
                   jax.ShapeDtypeStruct((N_TOK, LANES), F32),
                   jax.ShapeDtypeStruct((N_TOK, LANES), BF16)],
        scratch_shapes=[pltpu.VMEM((D_MIX, D_MODEL), BF16)],
        compiler_params=_cparams("arbitrary"),
        name="outproj_router",
    )(x, nao_c, nao_l, go_c, go_l, yf, yb, xa, z, d_skip, ssm_norm, w_out, mods, gain2, wr_hi, wr_lo, b_router)


RANK_TB = 512


def _rank_kernel(sel_ref, rank_ref, cnt_ref, carry):
    @pl.when(pl.program_id(0) == 0)
    def _():
        carry[...] = jnp.zeros_like(carry)

    sel = sel_ref[...]
    ii = lax.broadcasted_iota(I32, (RANK_TB, RANK_TB), 0)
    jj = lax.broadcasted_iota(I32, (RANK_TB, RANK_TB), 1)
    before = (jj < ii).astype(BF16)
    rank_ref[...] = _dot(before, sel) + carry[0:1, :]
    carry[...] = carry[...] + _dot(jnp.ones((8, RANK_TB), BF16), sel)
    cnt_ref[...] = carry[...]


def _ranks(sel):
    return pl.pallas_call(
        _rank_kernel,
        grid=(N_TOK // RANK_TB,),
        in_specs=[pl.BlockSpec((RANK_TB, LANES), lambda i: (i, 0))],
        out_specs=[pl.BlockSpec((RANK_TB, LANES), lambda i: (i, 0)),
                   pl.BlockSpec((8, LANES), lambda i: (0, 0))],
        out_shape=[jax.ShapeDtypeStruct((N_TOK, LANES), F32),
                   jax.ShapeDtypeStruct((8, LANES), F32)],
        scratch_shapes=[pltpu.VMEM((8, LANES), F32)],
        compiler_params=_cparams("arbitrary"),
        name="moe_ranks",
    )(sel)


_RUN_PIECES = (512, 256, 128, 64, 32, 16, 8)


def _run_dma(src, s0, dst, d0, n, sem, *, wait, fixed_src=False, pieces=_RUN_PIECES):
    for size in pieces:
        @pl.when((n & size) != 0)
        def _(size=size):
            done = n & ~(2 * size - 1)
            s = 0 if fixed_src else pl.multiple_of(s0 + done, RUN_ALIGN)
            d = pl.multiple_of(d0 + done, RUN_ALIGN)
            copy = pltpu.make_async_copy(src.at[pl.ds(s, size), :], dst.at[pl.ds(d, size), :], sem)
            if wait:
                copy.wait()
            else:
                copy.start()


_TILE_PIECES = (2048, 1024, 512, 256, 128, 64, 32, 16, 8)


def _wait_rows(buf, n, sem):
    for size in _TILE_PIECES:
        @pl.when((n & size) != 0)
        def _(size=size):
            pltpu.make_async_copy(buf.at[pl.ds(0, size), :], buf.at[pl.ds(0, size), :], sem).wait()


def _local_rows(idx_ref, rank_ref, base_ref):
    pos = rank_ref[...] + base_ref[0]
    lane = lax.broadcasted_iota(I32, pos.shape, 1)
    idx = idx_ref[...]
    return [jnp.sum(jnp.where(lane == idx[:, k:k + 1], pos, 0.0), axis=-1, keepdims=True) for k in range(TOP_K)]


def _dispatch_kernel(run_ref, loc_ref, glb_ref, tot_ref, fs_ref, fl_ref, nu_ref,
                     h_ref, idx_ref, rank_ref, base_ref, out_ref, xl_ref, zbuf, sems):
    i = pl.program_id(0)
    slot = i % 2
    xl = xl_ref.at[slot]
    sem = sems.at[slot]

    @pl.when(i == 0)
    def _():
        zbuf[...] = jnp.zeros_like(zbuf)
        fill_pieces = tuple(s for s in _RUN_PIECES if s < MOE_BM)
        for wait in (False, True):
            for e in range(N_EXPERTS):
                _run_dma(zbuf, 0, out_ref, fs_ref[e], fl_ref[e], sem, wait=wait, fixed_src=True, pieces=fill_pieces)

        def tail(b, carry):
            copy = pltpu.make_async_copy(zbuf, out_ref.at[pl.ds(pl.multiple_of(b * MOE_BM, MOE_BM), MOE_BM), :], sem)
            copy.start()
            copy.wait()
            return carry

        lax.fori_loop(nu_ref[0], MOE_NB, tail, 0)

    @pl.when(i >= 2)
    def _():
        _wait_rows(xl, tot_ref[i - 2], sem)

    rows = _local_rows(idx_ref, rank_ref, base_ref)
    p = lax.broadcasted_iota(I32, (MOE_TILE, LOCAL_ROWS), 1).astype(F32)
    hot = (p == rows[0])
    for k in range(1, TOP_K):
        hot = hot | (p == rows[k])
    xl[...] = lax.dot_general(hot.astype(BF16), h_ref[...].astype(BF16), (((0,), (0,)), ((), ())),
                              preferred_element_type=F32)
    for e in range(N_EXPERTS):
        j = i * N_EXPERTS + e
        _run_dma(xl, loc_ref[j], out_ref, glb_ref[j], run_ref[j], sem, wait=False)

    @pl.when(i == N_MOE_TILES - 1)
    def _():
        _wait_rows(xl, tot_ref[i], sem)
        _wait_rows(xl_ref.at[1 - slot], tot_ref[i - 1], sems.at[1 - slot])


def _dispatch(plan, h, idx, rank):
    tile = lambda w: pl.BlockSpec((MOE_TILE, w), lambda i, *_: (i, 0))
    return pl.pallas_call(
        _dispatch_kernel,
        grid_spec=pltpu.PrefetchScalarGridSpec(
            num_scalar_prefetch=7,
            grid=(N_MOE_TILES,),
            in_specs=[tile(D_MODEL), tile(LANES), tile(LANES),
                      pl.BlockSpec((1, 1, LANES), lambda i, *_: (i, 0, 0))],
            out_specs=pl.BlockSpec(memory_space=pl.ANY),
            scratch_shapes=[pltpu.VMEM((2, LOCAL_ROWS, D_MODEL), F32),
                            pltpu.VMEM((MOE_BM, D_MODEL), F32),
                            pltpu.SemaphoreType.DMA((2,))]),
        out_shape=jax.ShapeDtypeStruct((MOE_NB * MOE_BM, D_MODEL), F32),
        compiler_params=_cparams("arbitrary"),
        name="moe_dispatch",
    )(plan["run"], plan["local"], plan["global"], plan["total"], plan["fill_start"], plan["fill_len"],
      plan["n_used"], h, idx, rank, plan["base"])


def _expert_kernel(be_ref, nu_ref, new_ref, fe_ref, x_ref, wu_ref, bu_ref, wd_ref, bd_ref, y_ref, wu_s, wd_s):
    del fe_ref
    b = pl.program_id(0)
    used = b < nu_ref[0]

    @pl.when(jnp.logical_not(used))
    def _():
        y_ref[...] = jnp.zeros_like(y_ref)

    @pl.when(jnp.logical_and(used, new_ref[b] == 1))
    def _():
        r = lax.broadcasted_iota(I32, (UP_GROUP, UP_GROUP), 0)
        c = lax.broadcasted_iota(I32, (UP_GROUP, UP_GROUP), 1)
        src = jnp.where(c < UP_GROUP // 2, 2 * c, 2 * (c - UP_GROUP // 2) + 1)
        perm = (r == src).astype(BF16)
        for g in range(2 * D_FF // UP_GROUP):
            cols = slice(g * UP_GROUP, (g + 1) * UP_GROUP)
            wu_s[:, cols] = _dot(wu_ref[0, 0, :, cols].astype(BF16), perm).astype(BF16)
        wd_s[...] = wd_ref[0, 0].astype(BF16)

    @pl.when(used)
    def _():
        up = _dot(x_ref[...].astype(BF16), wu_s[...]) + bu_ref[0]
        half = UP_GROUP // 2
        acts = []
        for g in range(2 * D_FF // UP_GROUP):
            gate = jnp.minimum(up[:, g * UP_GROUP:g * UP_GROUP + half], SWIGLU_LIMIT)
            lin = jnp.clip(up[:, g * UP_GROUP + half:(g + 1) * UP_GROUP], -SWIGLU_LIMIT, SWIGLU_LIMIT)
            acts.append((gate * _sigmoid(SWIGLU_ALPHA * gate) * (lin + 1.0)).astype(BF16))
        y_ref[...] = _dot(jnp.concatenate(acts, axis=-1), wd_s[...]) + bd_ref[0]


def _experts(layer, plan, xs, w_up, b_up, w_down, b_down):
    blk = lambda b, nu: jnp.maximum(jnp.minimum(b, nu[0] - 1), 0)
    return pl.pallas_call(
        _expert_kernel,
        grid_spec=pltpu.PrefetchScalarGridSpec(
            num_scalar_prefetch=4,
            grid=(MOE_NB,),
            in_specs=[pl.BlockSpec((MOE_BM, D_MODEL), lambda b, be, nu, nw, fe: (blk(b, nu), 0)),
                      pl.BlockSpec((1, 1, D_MODEL, 2 * D_FF), lambda b, be, nu, nw, fe: (layer, fe[blk(b, nu)], 0, 0)),
                      pl.BlockSpec((1, 1, 2 * D_FF), lambda b, be, nu, nw, fe: (be[blk(b, nu)], 0, 0)),
                      pl.BlockSpec((1, 1, D_FF, D_MODEL), lambda b, be, nu, nw, fe: (layer, fe[blk(b, nu)], 0, 0)),
                      pl.BlockSpec((1, 1, D_MODEL), lambda b, be, nu, nw, fe: (be[blk(b, nu)], 0, 0))],
            out_specs=pl.BlockSpec((MOE_BM, D_MODEL), lambda b, be, nu, nw, fe: (b, 0)),
            scratch_shapes=[pltpu.VMEM((D_MODEL, 2 * D_FF), BF16),
                            pltpu.VMEM((D_FF, D_MODEL), BF16)]),
        out_shape=jax.ShapeDtypeStruct((MOE_NB * MOE_BM, D_MODEL), F32),
        compiler_params=_cparams("arbitrary"),
        name="moe_experts",
    )(plan["blk_expert"], plan["n_used"], plan["blk_new"], plan["blk_fetch"], xs, w_up, b_up, w_down, b_down)


def _combine_kernel(run_ref, loc_ref, glb_ref, tot_ref, x_ref, gate_ref, idx_ref, rank_ref, base_ref, m_ref, fn_ref,
                    ys_ref, *rest, final):
    n_out = 2 if final else 1
    o_refs, (yl_ref, sems) = rest[:n_out], rest[n_out:]
    i = pl.program_id(0)
    slot = i % 2

    def fetch(tile, buf):
        yl = yl_ref.at[buf]
        for e in range(N_EXPERTS):
            j = tile * N_EXPERTS + e
            _run_dma(ys_ref, glb_ref[j], yl, loc_ref[j], run_ref[j], sems.at[buf], wait=False)

        def clear(r, carry):
            yl[pl.ds(pl.multiple_of(r * RUN_ALIGN, RUN_ALIGN), RUN_ALIGN), :] = jnp.zeros((RUN_ALIGN, D_MODEL), F32)
            return carry

        lax.fori_loop(tot_ref[tile] // RUN_ALIGN, LOCAL_ROWS // RUN_ALIGN, clear, 0)

    @pl.when(i == 0)
    def _():
        fetch(0, 0)

    @pl.when(i + 1 < N_MOE_TILES)
    def _():
        fetch(i + 1, 1 - slot)

    rows = _local_rows(idx_ref, rank_ref, base_ref)
    gate = gate_ref[...]
    p = lax.broadcasted_iota(I32, (MOE_TILE, LOCAL_ROWS), 1).astype(F32)
    w = jnp.zeros((MOE_TILE, LOCAL_ROWS), F32)
    for k in range(TOP_K):
        w = jnp.where(p == rows[k], gate[:, k:k + 1], w)

    _wait_rows(yl_ref.at[slot], tot_ref[i], sems.at[slot])
    acc = _dot(w.astype(BF16), yl_ref[slot].astype(BF16))
    x = x_ref[...] + m_ref[0][5:6] * acc
    if not final:
        o_refs[0][...] = x
        return
    y = _rms(x, fn_ref[...])
    ctx_ref, lat_ref = o_refs

    @pl.when(i < N_CTX_TOK // MOE_TILE)
    def _():
        ctx_ref[...] = y

    lat_ref[...] = y


def _combine(plan, x, gates, idx, rank, mods, final_norm, ys, final):
    tile = lambda w: pl.BlockSpec((MOE_TILE, w), lambda i, *_: (i, 0))
    ctx_tiles = N_CTX_TOK // MOE_TILE
    if final:
        out_specs = [pl.BlockSpec((MOE_TILE, D_MODEL), lambda i, *_: (jnp.minimum(i, ctx_tiles - 1), 0)),
                     pl.BlockSpec((MOE_TILE, D_MODEL), lambda i, *_: (jnp.maximum(i - ctx_tiles, 0), 0))]
        out_shape = [jax.ShapeDtypeStruct((N_CTX_TOK, D_MODEL), F32), jax.ShapeDtypeStruct((N_LAT_TOK, D_MODEL), F32)]
    else:
        out_specs = [tile(D_MODEL)]
        out_shape = [jax.ShapeDtypeStruct((N_TOK, D_MODEL), F32)]
    return pl.pallas_call(
        functools.partial(_combine_kernel, final=final),
        grid_spec=pltpu.PrefetchScalarGridSpec(
            num_scalar_prefetch=4,
            grid=(N_MOE_TILES,),
            in_specs=[tile(D_MODEL), tile(LANES), tile(LANES), tile(LANES),
                      pl.BlockSpec((1, 1, LANES), lambda i, *_: (i, 0, 0)),
                      pl.BlockSpec((1, 6, D_MODEL), lambda i, *_: (_cond_of_tile(i, MOE_TILE), 0, 0)),
                      pl.BlockSpec((1, D_MODEL), lambda i, *_: (0, 0)),
                      pl.BlockSpec(memory_space=pl.ANY)],
            out_specs=out_specs,
            scratch_shapes=[pltpu.VMEM((2, LOCAL_ROWS, D_MODEL), F32),
                            pltpu.SemaphoreType.DMA((2,))]),
        out_shape=out_shape,
        compiler_params=_cparams("arbitrary"),
        name="moe_combine",
    )(plan["run"], plan["local"], plan["global"], plan["total"], x, gates, idx, rank, plan["base"], mods,
      final_norm, ys)


def _moe_plan(rank, cnt):
    first = rank[::MOE_TILE, :N_EXPERTS].astype(I32)
    total = cnt[0:1, :N_EXPERTS].astype(I32)
    run = jnp.concatenate([first[1:], total], axis=0) - first
    run = (run + RUN_ALIGN - 1) // RUN_ALIGN * RUN_ALIGN
    local = jnp.cumsum(run, axis=1) - run
    sizes = run.sum(axis=0)
    padded = (sizes + MOE_BM - 1) // MOE_BM * MOE_BM
    ends = jnp.cumsum(padded)
    starts = ends - padded
    glob = starts[None, :] + jnp.cumsum(run, axis=0) - run
    blk_start = jnp.arange(MOE_NB, dtype=I32) * MOE_BM
    blk_expert = jnp.minimum(jnp.sum(ends[None, :] <= blk_start[:, None], axis=1), N_EXPERTS - 1).astype(I32)
    blk_new = jnp.concatenate([jnp.ones((1,), I32), (blk_expert[1:] != blk_expert[:-1]).astype(I32)])
    blk = jnp.arange(MOE_NB, dtype=I32)
    later_start = (blk_new[None, :] == 1) & (blk[None, :] > blk[:, None])
    next_start = jnp.min(jnp.where(later_start, blk[None, :], MOE_NB), axis=1)
    next_hot = (blk[None, :] == next_start[:, None]).astype(I32)
    next_expert = jnp.where(next_start < MOE_NB, jnp.sum(next_hot * blk_expert[None, :], axis=1), blk_expert)
    blk_fetch = jnp.where(blk_new == 1, blk_expert, next_expert)
    base = _pad_lanes((local - first).astype(F32)).reshape(N_MOE_TILES, 1, LANES)
    return {"run": run.reshape(-1), "local": local.reshape(-1), "global": glob.reshape(-1), "total": run.sum(axis=1),
            "fill_start": starts + sizes, "fill_len": padded - sizes, "n_used": ends[-1:] // MOE_BM,
            "blk_expert": blk_expert, "blk_new": blk_new, "blk_fetch": blk_fetch, "base": base}


def _pad_lanes(v, fill=0.0):
    return jnp.pad(v, ((0, 0), (0, LANES - v.shape[-1])), constant_values=fill)


def kernel(x_prompt, x_sample, cache_na_k, cache_na_v, cache_gqa_k, cache_gqa_v, state_ssm, c, c_ctx, w_ada, b_ada, norm_mix, norm_ffn, w_in, na_rpb, gqa_q_norm, gqa_k_norm, ssm_conv_w, ssm_conv_b, ssm_dt_bias, ssm_a_log, ssm_d, ssm_norm, w_out, w_router, b_router, w_up, b_up, w_down, b_down, final_norm):
    x = jnp.concatenate([x_prompt.reshape(N_CTX_TOK, D_MODEL), x_sample.reshape(N_LAT_TOK, D_MODEL)], axis=0)
    conds = jnp.concatenate([c_ctx[None], c, jnp.zeros((COND_ROWS - N_COND, D_MODEL), F32)], axis=0)
    mods = _adaln(conds, w_ada, b_ada).reshape(DEPTH, COND_ROWS, 6, D_MODEL)
    cos, sin = _rope_tables()

    b_up_s = b_up.reshape(DEPTH, N_EXPERTS, 2 * D_FF // UP_GROUP, UP_GROUP // 2, 2)
    b_up_s = jnp.swapaxes(b_up_s, -1, -2).reshape(DEPTH, N_EXPERTS, 1, 2 * D_FF)
    b_down_s = b_down.reshape(DEPTH, N_EXPERTS, 1, D_MODEL)
    na_k_ctx = cache_na_k.reshape(DEC_BATCH, DEPTH, PAST_LEN, NA_DIM)
    na_v_ctx = cache_na_v.reshape(DEC_BATCH, DEPTH, PAST_LEN, NA_DIM)
    gqa_k_ctx = cache_gqa_k.reshape(DEC_BATCH, DEPTH, PAST_LEN, GQA_KV_DIM)
    gqa_v_ctx = cache_gqa_v.reshape(DEC_BATCH, DEPTH, PAST_LEN, GQA_KV_DIM)
    ssm_ctx = state_ssm.reshape(DEC_BATCH, DEPTH, 2, SSM_INNER, SSM_STATE)

    ctx_out = []
    for l in range(DEPTH):
        qkv, gqa, z, xbc, dt_raw = _inproj(l, x, mods[l], norm_mix[l][None], w_in)
        qn, kn = gqa_q_norm[l][None], gqa_k_norm[l][None]

        nao_c, go_c, gk_c = _ctx_attn(qkv, gqa, qn, kn)
        go_l = _lat_gqa(l, gqa, gqa_k_ctx, gqa_v_ctx, cos, sin, qn, kn)
        nao_l = _lat_na(l, qkv, na_k_ctx, na_v_ctx, _na_bias_tables(na_rpb[l]))

        xa = _conv_act(xbc, ssm_conv_w[l], ssm_conv_b[l][None])
        y_f, st_f, y_b, st_b = _ssd(l, xa, dt_raw, ssm_ctx, _pad_lanes(ssm_dt_bias[l]), _pad_lanes(ssm_a_log[l]))
        sts = [st[:BATCH].reshape(BATCH, SSM_HEADS, SSM_HEAD_DIM, SSM_STATE) for st in (st_f, st_b)]

        x, h, top_idx, gates, sel = _outproj(
            l, x, nao_c, nao_l, go_c, go_l, y_f, y_b, xa, z, jnp.repeat(ssm_d[l], SSM_HEAD_DIM)[None],
            ssm_norm[l][None], w_out, mods[l], norm_ffn[l][None], _pad_lanes(w_router[l]),
            _pad_lanes(b_router[l][None], NEG_INF))

        rank, cnt = _ranks(sel)
        plan = _moe_plan(rank, cnt)
        y_sorted = _experts(l, plan, _dispatch(plan, h, top_idx, rank), w_up, b_up_s[l], w_down, b_down_s[l])
        outs = _combine(plan, x, gates, top_idx, rank, mods[l], final_norm[None], y_sorted, final=(l == DEPTH - 1))
        x = outs[0]

        ctx_out.append((
            qkv[:N_CTX_TOK, NA_DIM:2 * NA_DIM].reshape(BATCH, SEQ, NA_HEADS, HEAD_DIM),
            qkv[:N_CTX_TOK, 2 * NA_DIM:].reshape(BATCH, SEQ, NA_HEADS, HEAD_DIM),
            gk_c.reshape(BATCH, SEQ, GQA_KV_HEADS, HEAD_DIM),
            gqa[:N_CTX_TOK, GQA_Q_DIM + GQA_KV_DIM:].reshape(BATCH, SEQ, GQA_KV_HEADS, HEAD_DIM),
            jnp.stack(sts, axis=1)))

    y_prompt = outs[0].reshape(BATCH, SEQ, D_MODEL)
    y_sample = outs[1].reshape(DEC_BATCH, DEC_SEQ, D_MODEL)
    return (y_prompt, y_sample) + tuple(jnp.stack([e[i] for e in ctx_out], axis=1) for i in range(5))
```

```python
import functools

import numpy as np
import jax
import jax.numpy as jnp
from jax import lax
from jax.experimental import pallas as pl
from jax.experimental.pallas import tpu as pltpu

F32 = jnp.float32
BF16 = jnp.bfloat16
I32 = jnp.int32

D_MODEL = 1024
BATCH = 16
SEQ = 256
DEPTH = 2
DEC_BATCH = 2
DEC_SEQ = 2048
PAST_LEN = 512
GRID_W = 64
HEAD_DIM = 64
NA_HEADS = 4
NA_WIN_ROWS = 8
NA_WIN_COLS = 16
GQA_HEADS = 4
GQA_KV_HEADS = 2
ROPE_THETA = 10000.0
SSM_HEADS = 8
SSM_HEAD_DIM = 64
SSM_STATE = 64
SSM_GROUPS = 2
SSM_INNER = SSM_HEADS * SSM_HEAD_DIM
SSM_BC_DIM = SSM_GROUPS * SSM_STATE
CONV_DIM = SSM_INNER + 2 * SSM_BC_DIM
CONV_W = 5
CHUNK = 128
NA_DIM = NA_HEADS * HEAD_DIM
GQA_Q_DIM = GQA_HEADS * HEAD_DIM
GQA_KV_DIM = GQA_KV_HEADS * HEAD_DIM
D_MIX = NA_DIM + GQA_Q_DIM + SSM_INNER
IN_DIM = 3 * NA_DIM + GQA_Q_DIM + 2 * GQA_KV_DIM + SSM_INNER + CONV_DIM + SSM_HEADS
N_EXPERTS = 32
TOP_K = 4
D_FF = D_MODEL
SWIGLU_LIMIT = 7.0
SWIGLU_ALPHA = 1.702
EPS = 1e-6
NEG_INF = -1e30

LANES = 128
N_CTX_TOK = BATCH * SEQ
N_LAT_TOK = DEC_BATCH * DEC_SEQ
N_TOK = N_CTX_TOK + N_LAT_TOK
N_COND = 1 + DEC_BATCH
COND_ROWS = 16
IN_PAD = 3 * NA_DIM + GQA_Q_DIM + 2 * GQA_KV_DIM + SSM_INNER + CONV_DIM + LANES
ROW_TILE = 256
N_ROW_TILES = N_TOK // ROW_TILE
MOE_BM = 256
N_SLOTS = N_TOK * TOP_K
RUN_ALIGN = 8
MOE_TILE = 512
N_MOE_TILES = N_TOK // MOE_TILE
N_RUNS = N_MOE_TILES * N_EXPERTS
LOCAL_ROWS = -(-(MOE_TILE * TOP_K + N_EXPERTS * (RUN_ALIGN - 1)) // MOE_BM) * MOE_BM
MOE_NB = -(-(N_SLOTS + N_RUNS * (RUN_ALIGN - 1)) // MOE_BM) + N_EXPERTS
UP_GROUP = 256
N_SEQ = BATCH + DEC_BATCH
N_CHUNKS = N_TOK // CHUNK
N_CTX_CHUNKS = N_CTX_TOK // CHUNK
VMEM_LIMIT = 56 * 1024 * 1024


def _cparams(*sem):
    return pltpu.CompilerParams(dimension_semantics=sem, vmem_limit_bytes=VMEM_LIMIT)


def _sigmoid(x):
    return 1.0 / (1.0 + jnp.exp(-x))


def _dot(a, b):
    return jnp.dot(a, b, preferred_element_type=F32)


def _dot_nt(a, b):
    return lax.dot_general(a, b, (((1,), (1,)), ((), ())), preferred_element_type=F32)


def _dot_exact(a, b):
    return jnp.dot(a, b, preferred_element_type=F32, precision=lax.Precision.HIGHEST)


def _rms(x, g):
    return x * lax.rsqrt(jnp.mean(x * x, axis=-1, keepdims=True) + EPS) * g


def _cond_of_tile(i, rows=ROW_TILE):
    ctx_tiles = N_CTX_TOK // rows
    return jnp.where(i < ctx_tiles, 0, 1 + (i - ctx_tiles) // (DEC_SEQ // rows))


def _adaln_kernel(c_ref, w_ref, b_ref, o_ref):
    c = c_ref[...]
    s = (c * _sigmoid(c)).astype(BF16)
    o_ref[0] = _dot(s, w_ref[0].astype(BF16)) + b_ref[0]


def _adaln(conds, w_ada, b_ada):
    tn = 1536
    return pl.pallas_call(
        _adaln_kernel,
        grid=(DEPTH, 6 * D_MODEL // tn),
        in_specs=[pl.BlockSpec((COND_ROWS, D_MODEL), lambda l, j: (0, 0)),
                  pl.BlockSpec((1, D_MODEL, tn), lambda l, j: (l, 0, j)),
                  pl.BlockSpec((1, 1, tn), lambda l, j: (l, 0, j))],
        out_specs=pl.BlockSpec((1, COND_ROWS, tn), lambda l, j: (l, 0, j)),
        out_shape=jax.ShapeDtypeStruct((DEPTH, COND_ROWS, 6 * D_MODEL), F32),
        compiler_params=_cparams("parallel", "parallel"),
        name="adaln",
    )(conds, w_ada, b_ada.reshape(DEPTH, 1, 6 * D_MODEL))


_IN_SPLITS = (3 * NA_DIM, GQA_Q_DIM + 2 * GQA_KV_DIM, SSM_INNER, CONV_DIM, LANES)


def _inproj_kernel(x_ref, m_ref, g_ref, w_ref, qkv_ref, gqa_ref, z_ref, xbc_ref, dt_ref, w_s):
    @pl.when(pl.program_id(0) == 0)
    def _():
        w_s[...] = jnp.zeros_like(w_s)
        w_s[:, 0:IN_DIM] = w_ref[0].astype(BF16)

    m = m_ref[0]
    h = _rms(x_ref[...], g_ref[...]) * (1.0 + m[1:2]) + m[0:1]
    p = _dot(h.astype(BF16), w_s[...])
    off = 0
    for ref, width in zip((qkv_ref, gqa_ref, z_ref, xbc_ref, dt_ref), _IN_SPLITS):
        ref[...] = p[:, off:off + width]
        off += width


def _inproj(layer, x, mods, gain, w_in):
    row = lambda w: pl.BlockSpec((ROW_TILE, w), lambda i: (i, 0))
    return pl.pallas_call(
        _inproj_kernel,
        grid=(N_ROW_TILES,),
        in_specs=[row(D_MODEL),
                  pl.BlockSpec((1, 6, D_MODEL), lambda i: (_cond_of_tile(i), 0, 0)),
                  pl.BlockSpec((1, D_MODEL), lambda i: (0, 0)),
                  pl.BlockSpec((1, D_MODEL, IN_DIM), lambda i: (layer, 0, 0))],
        out_specs=[row(w) for w in _IN_SPLITS],
        out_shape=[jax.ShapeDtypeStruct((N_TOK, w), F32) for w in _IN_SPLITS],
        scratch_shapes=[pltpu.VMEM((D_MODEL, IN_PAD), BF16)],
        compiler_params=_cparams("arbitrary"),
        name="inproj",
    )(x, mods, gain, w_in)


def _softmax_pv(scores, values):
    m = scores[0].max(axis=-1, keepdims=True)
    for s in scores[1:]:
        m = jnp.maximum(m, s.max(axis=-1, keepdims=True))
    den = 0.0
    acc = 0.0
    for s, v in zip(scores, values):
        e = jnp.exp(s - m)
        den = den + e.sum(axis=-1, keepdims=True)
        acc = acc + _dot(e.astype(BF16), v)
    return acc / den


def _heads_rms(x, n_heads, g):
    return jnp.concatenate(
        [_rms(x[:, h * HEAD_DIM:(h + 1) * HEAD_DIM], g) for h in range(n_heads)], axis=-1)


def _rope(x, cos, sin_signed):
    w = x.shape[-1]
    lane = lax.broadcasted_iota(I32, x.shape, 1)
    partner = jnp.where((lane & 1) == 0, pltpu.roll(x, w - 1, 1), pltpu.roll(x, 1, 1))
    return x * cos + partner * sin_signed


_ATT_SCALE = HEAD_DIM ** -0.5


def _ctx_attn_kernel(qkv_ref, gqa_ref, qn_ref, kn_ref, nao_ref, go_ref, gk_ref):
    outs = []
    for h in range(NA_HEADS):
        sl = slice(h * HEAD_DIM, (h + 1) * HEAD_DIM)
        q = qkv_ref[:, sl].astype(BF16)
        k = qkv_ref[:, NA_DIM + h * HEAD_DIM:NA_DIM + (h + 1) * HEAD_DIM].astype(BF16)
        v = qkv_ref[:, 2 * NA_DIM + h * HEAD_DIM:2 * NA_DIM + (h + 1) * HEAD_DIM].astype(BF16)
        outs.append(_softmax_pv([_dot_nt(q, k) * _ATT_SCALE], [v]))
    nao_ref[...] = jnp.concatenate(outs, axis=-1).astype(BF16)

    gq = _heads_rms(gqa_ref[:, 0:GQA_Q_DIM], GQA_HEADS, qn_ref[...])
    gk = _heads_rms(gqa_ref[:, GQA_Q_DIM:GQA_Q_DIM + GQA_KV_DIM], GQA_KV_HEADS, kn_ref[...])
    gk_ref[...] = gk
    rep = GQA_HEADS // GQA_KV_HEADS
    outs = []
    for h in range(GQA_HEADS):
        g = h // rep
        q = gq[:, h * HEAD_DIM:(h + 1) * HEAD_DIM].astype(BF16)
        k = gk[:, g * HEAD_DIM:(g + 1) * HEAD_DIM].astype(BF16)
        v0 = GQA_Q_DIM + GQA_KV_DIM + g * HEAD_DIM
        v = gqa_ref[:, v0:v0 + HEAD_DIM].astype(BF16)
        outs.append(_softmax_pv([_dot_nt(q, k) * _ATT_SCALE], [v]))
    go_ref[...] = jnp.concatenate(outs, axis=-1).astype(BF16)


def _ctx_attn(qkv, gqa, q_norm, k_norm):
    return pl.pallas_call(
        _ctx_attn_kernel,
        grid=(BATCH,),
        in_specs=[pl.BlockSpec((SEQ, 3 * NA_DIM), lambda b: (b, 0)),
                  pl.BlockSpec((SEQ, GQA_Q_DIM + 2 * GQA_KV_DIM), lambda b: (b, 0)),
                  pl.BlockSpec((1, HEAD_DIM), lambda b: (0, 0)),
                  pl.BlockSpec((1, HEAD_DIM), lambda b: (0, 0))],
        out_specs=[pl.BlockSpec((SEQ, NA_DIM), lambda b: (b, 0)),
                   pl.BlockSpec((SEQ, GQA_Q_DIM), lambda b: (b, 0)),
                   pl.BlockSpec((SEQ, GQA_KV_DIM), lambda b: (b, 0))],
        out_shape=[jax.ShapeDtypeStruct((N_CTX_TOK, NA_DIM), BF16),
                   jax.ShapeDtypeStruct((N_CTX_TOK, GQA_Q_DIM), BF16),
                   jax.ShapeDtypeStruct((N_CTX_TOK, GQA_KV_DIM), F32)],
        compiler_params=_cparams("parallel"),
        name="ctx_attn",
    )(qkv, gqa, q_norm, k_norm)


GQA_TQ = 256
GQA_KEYS = PAST_LEN + DEC_SEQ


def _lat_gqa_kernel(gqa_ref, ck_ref, cv_ref, cos_ref, sin_ref, qn_ref, kn_ref, o_ref, kbuf, vbuf):
    qb = pl.program_id(1)

    @pl.when(qb == 0)
    def _():
        kbuf[0:PAST_LEN, :] = ck_ref[0, 0].astype(BF16)
        vbuf[0:PAST_LEN, :] = cv_ref[0, 0].astype(BF16)
        k = _heads_rms(gqa_ref[:, GQA_Q_DIM:GQA_Q_DIM + GQA_KV_DIM], GQA_KV_HEADS, kn_ref[...])
        k = _rope(k, cos_ref[:, 0:GQA_KV_DIM], sin_ref[:, 0:GQA_KV_DIM])
        kbuf[PAST_LEN:GQA_KEYS, :] = k.astype(BF16)
        vbuf[PAST_LEN:GQA_KEYS, :] = gqa_ref[:, GQA_Q_DIM + GQA_KV_DIM:].astype(BF16)

    r0 = pl.multiple_of(qb * GQA_TQ, GQA_TQ)
    q = _heads_rms(gqa_ref[pl.ds(r0, GQA_TQ), 0:GQA_Q_DIM], GQA_HEADS, qn_ref[...])
    q = _rope(q, cos_ref[pl.ds(r0, GQA_TQ), :], sin_ref[pl.ds(r0, GQA_TQ), :]).astype(BF16)
    rep = GQA_HEADS // GQA_KV_HEADS
    outs = []
    for h in range(GQA_HEADS):
        g = h // rep
        k = kbuf[:, g * HEAD_DIM:(g + 1) * HEAD_DIM]
        v = vbuf[:, g * HEAD_DIM:(g + 1) * HEAD_DIM]
        s = _dot_nt(q[:, h * HEAD_DIM:(h + 1) * HEAD_DIM], k) * _ATT_SCALE
        outs.append(_softmax_pv([s], [v]))
    o_ref[...] = jnp.concatenate(outs, axis=-1).astype(BF16)


def _lat_gqa(layer, gqa, cache_k, cache_v, cos, sin, q_norm, k_norm):
    lat_blk = N_CTX_TOK // DEC_SEQ
    return pl.pallas_call(
        _lat_gqa_kernel,
        grid=(DEC_BATCH, DEC_SEQ // GQA_TQ),
        in_specs=[pl.BlockSpec((DEC_SEQ, GQA_Q_DIM + 2 * GQA_KV_DIM), lambda b, q: (lat_blk + b, 0)),
                  pl.BlockSpec((1, 1, PAST_LEN, GQA_KV_DIM), lambda b, q: (b, layer, 0, 0)),
                  pl.BlockSpec((1, 1, PAST_LEN, GQA_KV_DIM), lambda b, q: (b, layer, 0, 0)),
                  pl.BlockSpec((DEC_SEQ, GQA_Q_DIM), lambda b, q: (0, 0)),
                  pl.BlockSpec((DEC_SEQ, GQA_Q_DIM), lambda b, q: (0, 0)),
                  pl.BlockSpec((1, HEAD_DIM), lambda b, q: (0, 0)),
                  pl.BlockSpec((1, HEAD_DIM), lambda b, q: (0, 0))],
        out_specs=pl.BlockSpec((GQA_TQ, GQA_Q_DIM), lambda b, q: (b * (DEC_SEQ // GQA_TQ) + q, 0)),
        out_shape=jax.ShapeDtypeStruct((N_LAT_TOK, GQA_Q_DIM), BF16),
        scratch_shapes=[pltpu.VMEM((GQA_KEYS, GQA_KV_DIM), BF16),
                        pltpu.VMEM((GQA_KEYS, GQA_KV_DIM), BF16)],
        compiler_params=_cparams("arbitrary", "arbitrary"),
        name="lat_gqa",
    )(gqa, cache_k, cache_v, cos, sin, q_norm, k_norm)


def _rope_tables():
    t = np.arange(DEC_SEQ)
    row = (t // GRID_W).astype(np.float64)
    col = (t % GRID_W).astype(np.float64)
    axis_dim = HEAD_DIM // 2
    inv_freq = ROPE_THETA ** (-np.arange(0, axis_dim, 2, dtype=np.float64) / axis_dim)
    ang = np.concatenate([row[:, None] * inv_freq, col[:, None] * inv_freq], axis=-1)
    cos = np.repeat(np.cos(ang), 2, axis=-1)
    sin = np.repeat(np.sin(ang), 2, axis=-1) * np.tile(np.array([-1.0, 1.0]), HEAD_DIM // 2)
    return (jnp.asarray(np.tile(cos, (1, GQA_HEADS)), F32), jnp.asarray(np.tile(sin, (1, GQA_HEADS)), F32))


NA_ROWS = DEC_SEQ // GRID_W
NA_KEYS = NA_WIN_ROWS * GRID_W


NA_ROWS_PER_STEP = 4


def _lat_na_kernel(qkv_ref, ck_ref, cv_ref, *rest):
    bias_refs, o_ref = rest[:NA_ROWS_PER_STEP], rest[NA_ROWS_PER_STEP]
    for j in range(NA_ROWS_PER_STEP):
        r = pl.program_id(1) * NA_ROWS_PER_STEP + j
        r0 = jnp.clip(r - NA_WIN_ROWS // 2, 0, NA_ROWS - NA_WIN_ROWS)
        q0 = pl.multiple_of(r * GRID_W, GRID_W)
        k0 = pl.multiple_of(r0 * GRID_W, GRID_W)
        first = lax.broadcasted_iota(I32, (GRID_W, 2 * HEAD_DIM), 1) < HEAD_DIM
        outs = []
        for ha in range(0, NA_HEADS, 2):
            c0 = ha * HEAD_DIM
            slab = slice(c0, c0 + 2 * HEAD_DIM)
            q = qkv_ref[pl.ds(q0, GRID_W), slab]
            k = qkv_ref[pl.ds(k0, NA_KEYS), NA_DIM + c0:NA_DIM + c0 + 2 * HEAD_DIM].astype(BF16)
            v = qkv_ref[pl.ds(k0, NA_KEYS), 2 * NA_DIM + c0:2 * NA_DIM + c0 + 2 * HEAD_DIM].astype(BF16)
            kc = ck_ref[0, 0, :, slab].astype(BF16)
            vc = cv_ref[0, 0, :, slab].astype(BF16)
            pair = []
            for h, own in ((ha, first), (ha + 1, jnp.logical_not(first))):
                qh = jnp.where(own, q, 0.0).astype(BF16)
                s_nb = _dot_nt(qh, k) * _ATT_SCALE + bias_refs[j][0, h]
                s_ctx = _dot_nt(qh, kc) * _ATT_SCALE
                pair.append(_softmax_pv([s_nb, s_ctx], [v, vc]))
            outs.append(jnp.where(first, pair[0], pair[1]))
        o_ref[j * GRID_W:(j + 1) * GRID_W, :] = jnp.concatenate(outs, axis=-1).astype(BF16)


def _na_row_offset(r):
    return r - jnp.clip(r - NA_WIN_ROWS // 2, 0, NA_ROWS - NA_WIN_ROWS)


def _lat_na(layer, qkv, cache_k, cache_v, bias):
    lat_blk = N_CTX_TOK // DEC_SEQ
    steps = NA_ROWS // NA_ROWS_PER_STEP
    bias_spec = lambda j: pl.BlockSpec((1, NA_HEADS, GRID_W, NA_KEYS),
                                       lambda b, s: (_na_row_offset(s * NA_ROWS_PER_STEP + j), 0, 0, 0))
    return pl.pallas_call(
        _lat_na_kernel,
        grid=(DEC_BATCH, steps),
        in_specs=[pl.BlockSpec((DEC_SEQ, 3 * NA_DIM), lambda b, s: (lat_blk + b, 0)),
                  pl.BlockSpec((1, 1, PAST_LEN, NA_DIM), lambda b, s: (b, layer, 0, 0)),
                  pl.BlockSpec((1, 1, PAST_LEN, NA_DIM), lambda b, s: (b, layer, 0, 0))]
                 + [bias_spec(j) for j in range(NA_ROWS_PER_STEP)],
        out_specs=pl.BlockSpec((NA_ROWS_PER_STEP * GRID_W, NA_DIM), lambda b, s: (b * steps + s, 0)),
        out_shape=jax.ShapeDtypeStruct((N_LAT_TOK, NA_DIM), BF16),
        compiler_params=_cparams("parallel", "arbitrary"),
        name="lat_na",
    )(qkv, cache_k, cache_v, *([bias] * NA_ROWS_PER_STEP))


def _na_bias_tables(rpb):
    d = np.arange(NA_WIN_ROWS)[:, None]
    kr = np.arange(NA_WIN_ROWS)[None, :]
    dr = kr - d + NA_WIN_ROWS - 1
    qc = np.arange(GRID_W)[:, None]
    kc = np.arange(GRID_W)[None, :]
    col0 = np.clip(qc - NA_WIN_COLS // 2, 0, GRID_W - NA_WIN_COLS)
    in_win = (kc >= col0) & (kc < col0 + NA_WIN_COLS)
    dc = np.clip(kc - qc + NA_WIN_COLS - 1, 0, 2 * NA_WIN_COLS - 2)
    row_hot = (dr[:, :, None] == np.arange(2 * NA_WIN_ROWS - 1)).astype(np.float32)
    col_hot = (dc[:, :, None] == np.arange(2 * NA_WIN_COLS - 1)).astype(np.float32)
    b = jnp.einsum('hac,dka,qxc->dhqkx', rpb.astype(F32), row_hot, col_hot, precision=lax.Precision.HIGHEST)
    b = jnp.where(in_win[None, None, :, None, :], b, NEG_INF)
    return b.reshape(NA_WIN_ROWS, NA_HEADS, GRID_W, NA_KEYS)


CONV_TB = 1024
CONV_HALO = 8
CONV_HALO_BLOCKS = CONV_TB // CONV_HALO


def _conv_kernel(prev_ref, x_ref, next_ref, w_ref, b_ref, o_ref):
    i = pl.program_id(0)
    seq = jnp.where(i < N_CTX_TOK // CONV_TB, SEQ, DEC_SEQ)
    x = x_ref[...]
    ext = jnp.concatenate([prev_ref[...], x, next_ref[...]], axis=0)
    n_ext = CONV_TB + 2 * CONV_HALO
    pos = (lax.broadcasted_iota(I32, (CONV_TB, 1), 0) + i * CONV_TB) & (seq - 1)
    half = CONV_W // 2
    acc = x * w_ref[half:half + 1, :]
    for s in range(-half, half + 1):
        if s == 0:
            continue
        shifted = pltpu.roll(ext, (-s) % n_ext, 0)[CONV_HALO:CONV_HALO + CONV_TB]
        valid = (pos + s >= 0) & (pos + s < seq)
        acc = acc + jnp.where(valid, shifted, 0.0) * w_ref[half + s:half + s + 1, :]
    acc = acc + b_ref[...]
    o_ref[...] = acc * _sigmoid(acc)


def _conv_act(xbc, conv_w, conv_b):
    return pl.pallas_call(
        _conv_kernel,
        grid=(N_TOK // CONV_TB,),
        in_specs=[pl.BlockSpec((CONV_HALO, CONV_DIM),
                               lambda i: (jnp.maximum(i * CONV_HALO_BLOCKS - 1, 0), 0)),
                  pl.BlockSpec((CONV_TB, CONV_DIM), lambda i: (i, 0)),
                  pl.BlockSpec((CONV_HALO, CONV_DIM),
                               lambda i: (jnp.minimum((i + 1) * CONV_HALO_BLOCKS, N_TOK // CONV_HALO - 1), 0)),
                  pl.BlockSpec((CONV_W, CONV_DIM), lambda i: (0, 0)),
                  pl.BlockSpec((1, CONV_DIM), lambda i: (0, 0))],
        out_specs=pl.BlockSpec((CONV_TB, CONV_DIM), lambda i: (i, 0)),
        out_shape=jax.ShapeDtypeStruct((N_TOK, CONV_DIM), F32),
        compiler_params=_cparams("parallel"),
        name="conv_act",
    )(xbc, xbc, xbc, conv_w, conv_b)


def _chunk_seq(g):
    ctx_n = SEQ // CHUNK
    lat_n = DEC_SEQ // CHUNK
    is_ctx = g < N_CTX_CHUNKS
    gl = g - N_CTX_CHUNKS
    sid = jnp.where(is_ctx, g // ctx_n, BATCH + gl // lat_n)
    cin = jnp.where(is_ctx, g % ctx_n, gl % lat_n)
    n = jnp.where(is_ctx, ctx_n, lat_n)
    return sid, cin, n


def _ssd_init(h0_ref, st_ref, gg, reverse):
    sid, cin, n = _chunk_seq(gg)

    @pl.when(cin == (n - 1 if reverse else 0))
    def _():
        st_ref[0] = jnp.where(sid >= BATCH, h0_ref[0, 0, 0], 0.0)


def _ssd_chunk(xa_ref, dt_ref, dtb, alog, y_ref, st_ref, reverse):
    x = dt_ref[...] + dtb
    dt = jnp.maximum(x, 0.0) + jnp.log1p(jnp.exp(-jnp.abs(x)))
    dta = dt * -jnp.exp(alog)
    ii = lax.broadcasted_iota(I32, (CHUNK, CHUNK), 0)
    jj = lax.broadcasted_iota(I32, (CHUNK, CHUNK), 1)
    tri = (jj >= ii) if reverse else (jj <= ii)
    cum = _dot_exact(tri.astype(F32), dta)
    cum_t = cum.T
    edge = 0 if reverse else CHUNK - 1
    tot = cum[edge:edge + 1, :]
    first = jj < SSM_HEAD_DIM
    first_rows = lax.broadcasted_iota(I32, (2 * SSM_HEAD_DIM, SSM_STATE), 0) < SSM_HEAD_DIM
    rep = SSM_HEADS // SSM_GROUPS
    for grp in range(SSM_GROUPS):
        bg = xa_ref[:, SSM_INNER + grp * SSM_STATE:SSM_INNER + (grp + 1) * SSM_STATE].astype(BF16)
        c0 = SSM_INNER + SSM_BC_DIM + grp * SSM_STATE
        cg = xa_ref[:, c0:c0 + SSM_STATE].astype(BF16)
        cb = _dot_nt(cg, bg)
        for ha in range(grp * rep, (grp + 1) * rep, 2):
            hb = ha + 1
            slab = slice(ha * SSM_HEAD_DIM, (hb + 1) * SSM_HEAD_DIM)
            col_a, col_b = cum[:, ha:ha + 1], cum[:, hb:hb + 1]
            tot_a, tot_b = tot[:, ha:ha + 1], tot[:, hb:hb + 1]
            decay_a = jnp.where(tri, jnp.exp(jnp.minimum(col_a - cum_t[ha:ha + 1, :], 0.0)), 0.0)
            decay_b = jnp.where(tri, jnp.exp(jnp.minimum(col_b - cum_t[hb:hb + 1, :], 0.0)), 0.0)
            xdt = xa_ref[:, slab] * jnp.where(first, dt[:, ha:ha + 1], dt[:, hb:hb + 1])
            xdt_b = xdt.astype(BF16)
            y = jnp.where(first, _dot((cb * decay_a).astype(BF16), xdt_b), _dot((cb * decay_b).astype(BF16), xdt_b))
            state = st_ref[0, slab, :]
            y = y + _dot_nt(cg, state.astype(BF16)) * jnp.where(first, jnp.exp(col_a), jnp.exp(col_b))
            y_ref[:, slab] = y
            to_end = jnp.where(first, jnp.exp(tot_a - col_a), jnp.exp(tot_b - col_b))
            upd = lax.dot_general((xdt * to_end).astype(BF16), bg, (((0,), (0,)), ((), ())),
                                  preferred_element_type=F32)
            st_ref[0, slab, :] = state * jnp.where(first_rows, jnp.exp(tot_a), jnp.exp(tot_b)) + upd


def _ssd_kernel(xaf_ref, dtf_ref, h0f_ref, xab_ref, dtb_ref, h0b_ref, bias_ref, alog_ref,
                yf_ref, stf_ref, yb_ref, stb_ref):
    g = pl.program_id(0)
    _ssd_init(h0f_ref, stf_ref, g, False)
    _ssd_chunk(xaf_ref, dtf_ref, bias_ref[0:1, :], alog_ref[0:1, :], yf_ref, stf_ref, False)
    _ssd_init(h0b_ref, stb_ref, N_CHUNKS - 1 - g, True)
    _ssd_chunk(xab_ref, dtb_ref, bias_ref[1:2, :], alog_ref[1:2, :], yb_ref, stb_ref, True)


def _ssd(layer, xa, dt_raw, h0, dt_bias, a_log):
    rev = lambda g: N_CHUNKS - 1 - g
    chunk = lambda w, order: pl.BlockSpec((CHUNK, w), lambda g: (order(g), 0))
    state = lambda order: pl.BlockSpec((1, SSM_INNER, SSM_STATE), lambda g: (_chunk_seq(order(g))[0], 0, 0))
    start = lambda order, d: pl.BlockSpec(
        (1, 1, 1, SSM_INNER, SSM_STATE),
        lambda g: (jnp.maximum(_chunk_seq(order(g))[0] - BATCH, 0), layer, d, 0, 0))
    same = lambda g: g
    y_shape = jax.ShapeDtypeStruct((N_TOK, SSM_INNER), F32)
    st_shape = jax.ShapeDtypeStruct((N_SEQ, SSM_INNER, SSM_STATE), F32)
    h0_fwd = h0_bwd = h0
    return pl.pallas_call(
        _ssd_kernel,
        grid=(N_CHUNKS,),
        in_specs=[chunk(CONV_DIM, same), chunk(LANES, same), start(same, 0),
                  chunk(CONV_DIM, rev), chunk(LANES, rev), start(rev, 1),
                  pl.BlockSpec((2, LANES), lambda g: (0, 0)),
                  pl.BlockSpec((2, LANES), lambda g: (0, 0))],
        out_specs=[chunk(SSM_INNER, same), state(same), chunk(SSM_INNER, rev), state(rev)],
        out_shape=[y_shape, st_shape, y_shape, st_shape],
        compiler_params=_cparams("arbitrary"),
        name="ssd",
    )(xa, dt_raw, h0_fwd, xa, dt_raw, h0_bwd, dt_bias, a_log)


def _outproj_kernel(x_ref, nao_c_ref, nao_l_ref, go_c_ref, go_l_ref, yf_ref, yb_ref, xs_ref, z_ref, dsk_ref, sn_ref,
                    wo_ref, m_ref, g2_ref, wrh_ref, wrl_ref, br_ref, xo_ref, h_ref, idx_ref, gate_ref, sel_ref, wo_s):
    @pl.when(pl.program_id(0) == 0)
    def _():
        wo_s[...] = wo_ref[0].astype(BF16)

    is_ctx = pl.program_id(0) < N_CTX_TOK // ROW_TILE
    nao = jnp.where(is_ctx, nao_c_ref[...], nao_l_ref[...])
    go = jnp.where(is_ctx, go_c_ref[...], go_l_ref[...])
    m = m_ref[0]
    z = z_ref[...]
    y = (yf_ref[...] + yb_ref[...] + xs_ref[...] * dsk_ref[...]) * (z * _sigmoid(z))
    s_o = _rms(y, sn_ref[...]).astype(BF16)
    mix = (_dot(nao, wo_s[0:NA_DIM, :])
           + _dot(go, wo_s[NA_DIM:NA_DIM + GQA_Q_DIM, :])
           + _dot(s_o, wo_s[NA_DIM + GQA_Q_DIM:, :]))
    x = x_ref[...] + m[2:3] * mix
    xo_ref[...] = x
    h = _rms(x, g2_ref[...]) * (1.0 + m[4:5]) + m[3:4]
    h_hi = h.astype(BF16)
    h_ref[...] = h_hi

    h_lo = (h - h_hi.astype(F32)).astype(BF16)
    w_hi = wrh_ref[...]
    logits = _dot(h_hi, w_hi) + _dot(h_lo, w_hi) + _dot(h_hi, wrl_ref[...]) + br_ref[...]
    lane = lax.broadcasted_iota(I32, logits.shape, 1).astype(F32)
    vals, idxs = [], []
    for _ in range(TOP_K):
        v = logits.max(axis=-1, keepdims=True)
        i = jnp.where(logits == v, lane, float(LANES)).min(axis=-1, keepdims=True)
        vals.append(v)
        idxs.append(i)
        logits = jnp.where(lane == i, -jnp.inf, logits)
    es = [jnp.exp(v - vals[0]) for v in vals]
    den = es[0] + es[1] + es[2] + es[3]
    idx_out = jnp.zeros(lane.shape, F32)
    gate_out = jnp.zeros(lane.shape, F32)
    sel = jnp.zeros(lane.shape, F32)
    for k in range(TOP_K):
        idx_out = jnp.where(lane == float(k), idxs[k], idx_out)
        gate_out = jnp.where(lane == float(k), es[k] / den, gate_out)
        sel = jnp.where(lane == idxs[k], 1.0, sel)
    idx_ref[...] = idx_out.astype(I32)
    gate_ref[...] = gate_out
    sel_ref[...] = sel.astype(BF16)


def _outproj(layer, x, nao_c, nao_l, go_c, go_l, yf, yb, xa, z, d_skip, ssm_norm, w_out, mods, gain2, w_router,
             b_router):
    row = lambda w: pl.BlockSpec((ROW_TILE, w), lambda i: (i, 0))
    full = lambda a, b: pl.BlockSpec((a, b), lambda i: (0, 0))
    ctx_tiles = N_CTX_TOK // ROW_TILE
    ctx_row = lambda w: pl.BlockSpec((ROW_TILE, w), lambda i: (jnp.minimum(i, ctx_tiles - 1), 0))
    lat_row = lambda w: pl.BlockSpec((ROW_TILE, w), lambda i: (jnp.maximum(i - ctx_tiles, 0), 0))
    wr_hi = w_router.astype(BF16)
    wr_lo = (w_router - wr_hi.astype(F32)).astype(BF16)
    return pl.pallas_call(
        _outproj_kernel,
        grid=(N_ROW_TILES,),
        in_specs=[row(D_MODEL), ctx_row(NA_DIM), lat_row(NA_DIM), ctx_row(GQA_Q_DIM), lat_row(GQA_Q_DIM),
                  row(SSM_INNER), row(SSM_INNER),
                  row(SSM_INNER), row(SSM_INNER), full(1, SSM_INNER), full(1, SSM_INNER),
                  pl.BlockSpec((1, D_MIX, D_MODEL), lambda i: (layer, 0, 0)),
                  pl.BlockSpec((1, 6, D_MODEL), lambda i: (_cond_of_tile(i), 0, 0)),
                  full(1, D_MODEL), full(D_MODEL, LANES), full(D_MODEL, LANES), full(1, LANES)],
        out_specs=[row(D_MODEL), row(D_MODEL), row(LANES), row(LANES), row(LANES)],
        out_shape=[jax.ShapeDtypeStruct((N_TOK, D_MODEL), F32),
                   jax.ShapeDtypeStruct((N_TOK, D_MODEL), BF16),
                   jax.ShapeDtypeStruct((N_TOK, LANES), I32),
                   jax.ShapeDtypeStruct((N_TOK, LANES), F32),
                   jax.ShapeDtypeStruct((N_TOK, LANES), BF16)],
        scratch_shapes=[pltpu.VMEM((D_MIX, D_MODEL), BF16)],
        compiler_params=_cparams("arbitrary"),
        name="outproj_router",
    )(x, nao_c, nao_l, go_c, go_l, yf, yb, xa, z, d_skip, ssm_norm, w_out, mods, gain2, wr_hi, wr_lo, b_router)


RANK_TB = 512


def _rank_kernel(sel_ref, rank_ref, cnt_ref, carry):
    @pl.when(pl.program_id(0) == 0)
    def _():
        carry[...] = jnp.zeros_like(carry)

    sel = sel_ref[...]
    ii = lax.broadcasted_iota(I32, (RANK_TB, RANK_TB), 0)
    jj = lax.broadcasted_iota(I32, (RANK_TB, RANK_TB), 1)
    before = (jj < ii).astype(BF16)
    rank_ref[...] = _dot(before, sel) + carry[0:1, :]
    carry[...] = carry[...] + _dot(jnp.ones((8, RANK_TB), BF16), sel)
    cnt_ref[...] = carry[...]


def _ranks(sel):
    return pl.pallas_call(
        _rank_kernel,
        grid=(N_TOK // RANK_TB,),
        in_specs=[pl.BlockSpec((RANK_TB, LANES), lambda i: (i, 0))],
        out_specs=[pl.BlockSpec((RANK_TB, LANES), lambda i: (i, 0)),
                   pl.BlockSpec((8, LANES), lambda i: (0, 0))],
        out_shape=[jax.ShapeDtypeStruct((N_TOK, LANES), F32),
                   jax.ShapeDtypeStruct((8, LANES), F32)],
        scratch_shapes=[pltpu.VMEM((8, LANES), F32)],
        compiler_params=_cparams("arbitrary"),
        name="moe_ranks",
    )(sel)


_RUN_PIECES = (512, 256, 128, 64, 32, 16, 8)


def _run_dma(src, s0, dst, d0, n, sem, *, wait, fixed_src=False, pieces=_RUN_PIECES):
    for size in pieces:
        @pl.when((n & size) != 0)
        def _(size=size):
            done = n & ~(2 * size - 1)
            s = 0 if fixed_src else pl.multiple_of(s0 + done, RUN_ALIGN)
            d = pl.multiple_of(d0 + done, RUN_ALIGN)
            copy = pltpu.make_async_copy(src.at[pl.ds(s, size), :], dst.at[pl.ds(d, size), :], sem)
            if wait:
                copy.wait()
            else:
                copy.start()


_TILE_PIECES = (2048, 1024, 512, 256, 128, 64, 32, 16, 8)


def _wait_rows(buf, n, sem):
    for size in _TILE_PIECES:
        @pl.when((n & size) != 0)
        def _(size=size):
            pltpu.make_async_copy(buf.at[pl.ds(0, size), :], buf.at[pl.ds(0, size), :], sem).wait()


def _local_rows(idx_ref, rank_ref, base_ref):
    pos = rank_ref[...] + base_ref[0]
    lane = lax.broadcasted_iota(I32, pos.shape, 1)
    idx = idx_ref[...]
    return [jnp.sum(jnp.where(lane == idx[:, k:k + 1], pos, 0.0), axis=-1, keepdims=True).astype(I32)
            for k in range(TOP_K)]


def _dispatch_kernel(run_ref, loc_ref, glb_ref, tot_ref, fs_ref, fl_ref, nu_ref,
                     h_ref, idx_ref, rank_ref, base_ref, out_ref, xl_ref, zbuf, sems):
    i = pl.program_id(0)
    slot = i % 2
    xl = xl_ref.at[slot]
    sem = sems.at[slot]

    @pl.when(i == 0)
    def _():
        zbuf[...] = jnp.zeros_like(zbuf)
        fill_pieces = tuple(s for s in _RUN_PIECES if s < MOE_BM)
        for wait in (False, True):
            for e in range(N_EXPERTS):
                _run_dma(zbuf, 0, out_ref, fs_ref[e], fl_ref[e], sem, wait=wait, fixed_src=True, pieces=fill_pieces)

        def tail(b, carry):
            copy = pltpu.make_async_copy(zbuf, out_ref.at[pl.ds(pl.multiple_of(b * MOE_BM, MOE_BM), MOE_BM), :], sem)
            copy.start()
            copy.wait()
            return carry

        lax.fori_loop(nu_ref[0], MOE_NB, tail, 0)

    @pl.when(i >= 2)
    def _():
        _wait_rows(xl, tot_ref[i - 2], sem)

    rows = _local_rows(idx_ref, rank_ref, base_ref)
    p = lax.broadcasted_iota(I32, (MOE_TILE, LOCAL_ROWS), 1)
    hot = (p == rows[0])
    for k in range(1, TOP_K):
        hot = hot | (p == rows[k])
    xl[...] = lax.dot_general(hot.astype(BF16), h_ref[...].astype(BF16), (((0,), (0,)), ((), ())),
                              preferred_element_type=F32)
    for e in range(N_EXPERTS):
        j = i * N_EXPERTS + e
        _run_dma(xl, loc_ref[j], out_ref, glb_ref[j], run_ref[j], sem, wait=False)

    @pl.when(i == N_MOE_TILES - 1)
    def _():
        _wait_rows(xl, tot_ref[i], sem)
        _wait_rows(xl_ref.at[1 - slot], tot_ref[i - 1], sems.at[1 - slot])


def _dispatch(plan, h, idx, rank):
    tile = lambda w: pl.BlockSpec((MOE_TILE, w), lambda i, *_: (i, 0))
    return pl.pallas_call(
        _dispatch_kernel,
        grid_spec=pltpu.PrefetchScalarGridSpec(
            num_scalar_prefetch=7,
            grid=(N_MOE_TILES,),
            in_specs=[tile(D_MODEL), tile(LANES), tile(LANES),
                      pl.BlockSpec((1, 1, LANES), lambda i, *_: (i, 0, 0))],
            out_specs=pl.BlockSpec(memory_space=pl.ANY),
            scratch_shapes=[pltpu.VMEM((2, LOCAL_ROWS, D_MODEL), F32),
                            pltpu.VMEM((MOE_BM, D_MODEL), F32),
                            pltpu.SemaphoreType.DMA((2,))]),
        out_shape=jax.ShapeDtypeStruct((MOE_NB * MOE_BM, D_MODEL), F32),
        compiler_params=_cparams("arbitrary"),
        name="moe_dispatch",
    )(plan["run"], plan["local"], plan["global"], plan["total"], plan["fill_start"], plan["fill_len"],
      plan["n_used"], h, idx, rank, plan["base"])


def _expert_kernel(be_ref, nu_ref, new_ref, fe_ref, x_ref, wu_ref, bu_ref, wd_ref, bd_ref, y_ref, wu_s, wd_s):
    del fe_ref
    b = pl.program_id(0)
    used = b < nu_ref[0]

    @pl.when(jnp.logical_not(used))
    def _():
        y_ref[...] = jnp.zeros_like(y_ref)

    @pl.when(jnp.logical_and(used, new_ref[b] == 1))
    def _():
        r = lax.broadcasted_iota(I32, (UP_GROUP, UP_GROUP), 0)
        c = lax.broadcasted_iota(I32, (UP_GROUP, UP_GROUP), 1)
        src = jnp.where(c < UP_GROUP // 2, 2 * c, 2 * (c - UP_GROUP // 2) + 1)
        perm = (r == src).astype(BF16)
        for g in range(2 * D_FF // UP_GROUP):
            cols = slice(g * UP_GROUP, (g + 1) * UP_GROUP)
            wu_s[:, cols] = _dot(wu_ref[0, 0, :, cols].astype(BF16), perm).astype(BF16)
        wd_s[...] = wd_ref[0, 0].astype(BF16)

    @pl.when(used)
    def _():
        up = _dot(x_ref[...].astype(BF16), wu_s[...]) + bu_ref[0]
        half = UP_GROUP // 2
        acts = []
        for g in range(2 * D_FF // UP_GROUP):
            gate = jnp.minimum(up[:, g * UP_GROUP:g * UP_GROUP + half], SWIGLU_LIMIT)
            lin = jnp.clip(up[:, g * UP_GROUP + half:(g + 1) * UP_GROUP], -SWIGLU_LIMIT, SWIGLU_LIMIT)
            acts.append((gate * _sigmoid(SWIGLU_ALPHA * gate) * (lin + 1.0)).astype(BF16))
        y_ref[...] = _dot(jnp.concatenate(acts, axis=-1), wd_s[...]) + bd_ref[0]


def _experts(layer, plan, xs, w_up, b_up, w_down, b_down):
    blk = lambda b, nu: jnp.maximum(jnp.minimum(b, nu[0] - 1), 0)
    return pl.pallas_call(
        _expert_kernel,
        grid_spec=pltpu.PrefetchScalarGridSpec(
            num_scalar_prefetch=4,
            grid=(MOE_NB,),
            in_specs=[pl.BlockSpec((MOE_BM, D_MODEL), lambda b, be, nu, nw, fe: (blk(b, nu), 0)),
                      pl.BlockSpec((1, 1, D_MODEL, 2 * D_FF), lambda b, be, nu, nw, fe: (layer, fe[blk(b, nu)], 0, 0)),
                      pl.BlockSpec((1, 1, 2 * D_FF), lambda b, be, nu, nw, fe: (be[blk(b, nu)], 0, 0)),
                      pl.BlockSpec((1, 1, D_FF, D_MODEL), lambda b, be, nu, nw, fe: (layer, fe[blk(b, nu)], 0, 0)),
                      pl.BlockSpec((1, 1, D_MODEL), lambda b, be, nu, nw, fe: (be[blk(b, nu)], 0, 0))],
            out_specs=pl.BlockSpec((MOE_BM, D_MODEL), lambda b, be, nu, nw, fe: (b, 0)),
            scratch_shapes=[pltpu.VMEM((D_MODEL, 2 * D_FF), BF16),
                            pltpu.VMEM((D_FF, D_MODEL), BF16)]),
        out_shape=jax.ShapeDtypeStruct((MOE_NB * MOE_BM, D_MODEL), F32),
        compiler_params=_cparams("arbitrary"),
        name="moe_experts",
    )(plan["blk_expert"], plan["n_used"], plan["blk_new"], plan["blk_fetch"], xs, w_up, b_up, w_down, b_down)


def _combine_kernel(run_ref, loc_ref, glb_ref, tot_ref, x_ref, gate_ref, idx_ref, rank_ref, base_ref, m_ref, fn_ref,
                    ys_ref, *rest, final):
    n_out = 2 if final else 1
    o_refs, (yl_ref, sems) = rest[:n_out], rest[n_out:]
    i = pl.program_id(0)
    slot = i % 2

    def fetch(tile, buf):
        yl = yl_ref.at[buf]
        for e in range(N_EXPERTS):
            j = tile * N_EXPERTS + e
            _run_dma(ys_ref, glb_ref[j], yl, loc_ref[j], run_ref[j], sems.at[buf], wait=False)

        def clear(r, carry):
            yl[pl.ds(pl.multiple_of(r * RUN_ALIGN, RUN_ALIGN), RUN_ALIGN), :] = jnp.zeros((RUN_ALIGN, D_MODEL), F32)
            return carry

        lax.fori_loop(tot_ref[tile] // RUN_ALIGN, LOCAL_ROWS // RUN_ALIGN, clear, 0)

    @pl.when(i == 0)
    def _():
        fetch(0, 0)

    @pl.when(i + 1 < N_MOE_TILES)
    def _():
        fetch(i + 1, 1 - slot)

    rows = _local_rows(idx_ref, rank_ref, base_ref)
    gate = gate_ref[...]
    p = lax.broadcasted_iota(I32, (MOE_TILE, LOCAL_ROWS), 1)
    w = jnp.zeros((MOE_TILE, LOCAL_ROWS), F32)
    for k in range(TOP_K):
        w = jnp.where(p == rows[k], gate[:, k:k + 1], w)

    _wait_rows(yl_ref.at[slot], tot_ref[i], sems.at[slot])
    acc = _dot(w.astype(BF16), yl_ref[slot].astype(BF16))
    x = x_ref[...] + m_ref[0][5:6] * acc
    if not final:
        o_refs[0][...] = x
        return
    y = _rms(x, fn_ref[...])
    ctx_ref, lat_ref = o_refs

    @pl.when(i < N_CTX_TOK // MOE_TILE)
    def _():
        ctx_ref[...] = y

    lat_ref[...] = y


def _combine(plan, x, gates, idx, rank, mods, final_norm, ys, final):
    tile = lambda w: pl.BlockSpec((MOE_TILE, w), lambda i, *_: (i, 0))
    ctx_tiles = N_CTX_TOK // MOE_TILE
    if final:
        out_specs = [pl.BlockSpec((MOE_TILE, D_MODEL), lambda i, *_: (jnp.minimum(i, ctx_tiles - 1), 0)),
                     pl.BlockSpec((MOE_TILE, D_MODEL), lambda i, *_: (jnp.maximum(i - ctx_tiles, 0), 0))]
        out_shape = [jax.ShapeDtypeStruct((N_CTX_TOK, D_MODEL), F32), jax.ShapeDtypeStruct((N_LAT_TOK, D_MODEL), F32)]
    else:
        out_specs = [tile(D_MODEL)]
        out_shape = [jax.ShapeDtypeStruct((N_TOK, D_MODEL), F32)]
    return pl.pallas_call(
        functools.partial(_combine_kernel, final=final),
        grid_spec=pltpu.PrefetchScalarGridSpec(
            num_scalar_prefetch=4,
            grid=(N_MOE_TILES,),
            in_specs=[tile(D_MODEL), tile(LANES), tile(LANES), tile(LANES),
                      pl.BlockSpec((1, 1, LANES), lambda i, *_: (i, 0, 0)),
                      pl.BlockSpec((1, 6, D_MODEL), lambda i, *_: (_cond_of_tile(i, MOE_TILE), 0, 0)),
                      pl.BlockSpec((1, D_MODEL), lambda i, *_: (0, 0)),
                      pl.BlockSpec(memory_space=pl.ANY)],
            out_specs=out_specs,
            scratch_shapes=[pltpu.VMEM((2, LOCAL_ROWS, D_MODEL), F32),
                            pltpu.SemaphoreType.DMA((2,))]),
        out_shape=out_shape,
        compiler_params=_cparams("arbitrary"),
        name="moe_combine",
    )(plan["run"], plan["local"], plan["global"], plan["total"], x, gates, idx, rank, plan["base"], mods,
      final_norm, ys)


def _moe_plan(rank, cnt):
    first = rank[::MOE_TILE, :N_EXPERTS].astype(I32)
    total = cnt[0:1, :N_EXPERTS].astype(I32)
    run = jnp.concatenate([first[1:], total], axis=0) - first
    run = (run + RUN_ALIGN - 1) // RUN_ALIGN * RUN_ALIGN
    local = jnp.cumsum(run, axis=1) - run
    sizes = run.sum(axis=0)
    padded = (sizes + MOE_BM - 1) // MOE_BM * MOE_BM
    ends = jnp.cumsum(padded)
    starts = ends - padded
    glob = starts[None, :] + jnp.cumsum(run, axis=0) - run
    blk_start = jnp.arange(MOE_NB, dtype=I32) * MOE_BM
    blk_expert = jnp.minimum(jnp.sum(ends[None, :] <= blk_start[:, None], axis=1), N_EXPERTS - 1).astype(I32)
    blk_new = jnp.concatenate([jnp.ones((1,), I32), (blk_expert[1:] != blk_expert[:-1]).astype(I32)])
    blk = jnp.arange(MOE_NB, dtype=I32)
    later_start = (blk_new[None, :] == 1) & (blk[None, :] > blk[:, None])
    next_start = jnp.min(jnp.where(later_start, blk[None, :], MOE_NB), axis=1)
    next_hot = (blk[None, :] == next_start[:, None]).astype(I32)
    next_expert = jnp.where(next_start < MOE_NB, jnp.sum(next_hot * blk_expert[None, :], axis=1), blk_expert)
    blk_fetch = jnp.where(blk_new == 1, blk_expert, next_expert)
    base = _pad_lanes((local - first).astype(F32)).reshape(N_MOE_TILES, 1, LANES)
    return {"run": run.reshape(-1), "local": local.reshape(-1), "global": glob.reshape(-1), "total": run.sum(axis=1),
            "fill_start": starts + sizes, "fill_len": padded - sizes, "n_used": ends[-1:] // MOE_BM,
            "blk_expert": blk_expert, "blk_new": blk_new, "blk_fetch": blk_fetch, "base": base}


def _pad_lanes(v, fill=0.0):
    return jnp.pad(v, ((0, 0), (0, LANES - v.shape[-1])), constant_values=fill)


def kernel(x_prompt, x_sample, cache_na_k, cache_na_v, cache_gqa_k, cache_gqa_v, state_ssm, c, c_ctx, w_ada, b_ada, norm_mix, norm_ffn, w_in, na_rpb, gqa_q_norm, gqa_k_norm, ssm_conv_w, ssm_conv_b, ssm_dt_bias, ssm_a_log, ssm_d, ssm_norm, w_out, w_router, b_router, w_up, b_up, w_down, b_down, final_norm):
    x = jnp.concatenate([x_prompt.reshape(N_CTX_TOK, D_MODEL), x_sample.reshape(N_LAT_TOK, D_MODEL)], axis=0)
    conds = jnp.concatenate([c_ctx[None], c, jnp.zeros((COND_ROWS - N_COND, D_MODEL), F32)], axis=0)
    mods = _adaln(conds, w_ada, b_ada).reshape(DEPTH, COND_ROWS, 6, D_MODEL)
    cos, sin = _rope_tables()

    b_up_s = b_up.reshape(DEPTH, N_EXPERTS, 2 * D_FF // UP_GROUP, UP_GROUP // 2, 2)
    b_up_s = jnp.swapaxes(b_up_s, -1, -2).reshape(DEPTH, N_EXPERTS, 1, 2 * D_FF)
    b_down_s = b_down.reshape(DEPTH, N_EXPERTS, 1, D_MODEL)
    na_k_ctx = cache_na_k.reshape(DEC_BATCH, DEPTH, PAST_LEN, NA_DIM)
    na_v_ctx = cache_na_v.reshape(DEC_BATCH, DEPTH, PAST_LEN, NA_DIM)
    gqa_k_ctx = cache_gqa_k.reshape(DEC_BATCH, DEPTH, PAST_LEN, GQA_KV_DIM)
    gqa_v_ctx = cache_gqa_v.reshape(DEC_BATCH, DEPTH, PAST_LEN, GQA_KV_DIM)
    ssm_ctx = state_ssm.reshape(DEC_BATCH, DEPTH, 2, SSM_INNER, SSM_STATE)

    ctx_out = []
    for l in range(DEPTH):
        qkv, gqa, z, xbc, dt_raw = _inproj(l, x, mods[l], norm_mix[l][None], w_in)
        qn, kn = gqa_q_norm[l][None], gqa_k_norm[l][None]

        nao_c, go_c, gk_c = _ctx_attn(qkv, gqa, qn, kn)
        go_l = _lat_gqa(l, gqa, gqa_k_ctx, gqa_v_ctx, cos, sin, qn, kn)
        nao_l = _lat_na(l, qkv, na_k_ctx, na_v_ctx, _na_bias_tables(na_rpb[l]))

        xa = _conv_act(xbc, ssm_conv_w[l], ssm_conv_b[l][None])
        y_f, st_f, y_b, st_b = _ssd(l, xa, dt_raw, ssm_ctx, _pad_lanes(ssm_dt_bias[l]), _pad_lanes(ssm_a_log[l]))
        sts = [st[:BATCH].reshape(BATCH, SSM_HEADS, SSM_HEAD_DIM, SSM_STATE) for st in (st_f, st_b)]

        x, h, top_idx, gates, sel = _outproj(
            l, x, nao_c, nao_l, go_c, go_l, y_f, y_b, xa, z, jnp.repeat(ssm_d[l], SSM_HEAD_DIM)[None],
            ssm_norm[l][None], w_out, mods[l], norm_ffn[l][None], _pad_lanes(w_router[l]),
            _pad_lanes(b_router[l][None], NEG_INF))

        rank, cnt = _ranks(sel)
        plan = _moe_plan(rank, cnt)
        y_sorted = _experts(l, plan, _dispatch(plan, h, top_idx, rank), w_up, b_up_s[l], w_down, b_down_s[l])
        outs = _combine(plan, x, gates, top_idx, rank, mods[l], final_norm[None], y_sorted, final=(l == DEPTH - 1))
        x = outs[0]

        ctx_out.append((
            qkv[:N_CTX_TOK, NA_DIM:2 * NA_DIM].reshape(BATCH, SEQ, NA_HEADS, HEAD_DIM),
            qkv[:N_CTX_TOK, 2 * NA_DIM:].reshape(BATCH, SEQ, NA_HEADS, HEAD_DIM),
            gk_c.reshape(BATCH, SEQ, GQA_KV_HEADS, HEAD_DIM),
            gqa[:N_CTX_TOK, GQA_Q_DIM + GQA_KV_DIM:].reshape(BATCH, SEQ, GQA_KV_HEADS, HEAD_DIM),
            jnp.stack(sts, axis=1)))

    y_prompt = outs[0].reshape(BATCH, SEQ, D_MODEL)
    y_sample = outs[1].reshape(DEC_BATCH, DEC_SEQ, D_MODEL)
    return (y_prompt, y_sample) + tuple(jnp.stack([e[i] for e in ctx_out], axis=1) for i in range(5))
```

```python
import functools

import numpy as np
import jax
import jax.numpy as jnp
from jax import lax
from jax.experimental import pallas as pl
from jax.experimental.pallas import tpu as pltpu

F32 = jnp.float32
BF16 = jnp.bfloat16
I32 = jnp.int32

D_MODEL = 1024
BATCH = 16
SEQ = 256
DEPTH = 2
DEC_BATCH = 2
DEC_SEQ = 2048
PAST_LEN = 512
GRID_W = 64
HEAD_DIM = 64
NA_HEADS = 4
NA_WIN_ROWS = 8
NA_WIN_COLS = 16
GQA_HEADS = 4
GQA_KV_HEADS = 2
ROPE_THETA = 10000.0
SSM_HEADS = 8
SSM_HEAD_DIM = 64
SSM_STATE = 64
SSM_GROUPS = 2
SSM_INNER = SSM_HEADS * SSM_HEAD_DIM
SSM_BC_DIM = SSM_GROUPS * SSM_STATE
CONV_DIM = SSM_INNER + 2 * SSM_BC_DIM
CONV_W = 5
CHUNK = 128
NA_DIM = NA_HEADS * HEAD_DIM
GQA_Q_DIM = GQA_HEADS * HEAD_DIM
GQA_KV_DIM = GQA_KV_HEADS * HEAD_DIM
D_MIX = NA_DIM + GQA_Q_DIM + SSM_INNER
IN_DIM = 3 * NA_DIM + GQA_Q_DIM + 2 * GQA_KV_DIM + SSM_INNER + CONV_DIM + SSM_HEADS
N_EXPERTS = 32
TOP_K = 4
D_FF = D_MODEL
SWIGLU_LIMIT = 7.0
SWIGLU_ALPHA = 1.702
EPS = 1e-6
NEG_INF = -1e30

LANES = 128
N_CTX_TOK = BATCH * SEQ
N_LAT_TOK = DEC_BATCH * DEC_SEQ
N_TOK = N_CTX_TOK + N_LAT_TOK
N_COND = 1 + DEC_BATCH
COND_ROWS = 16
IN_PAD = 3 * NA_DIM + GQA_Q_DIM + 2 * GQA_KV_DIM + SSM_INNER + CONV_DIM + LANES
ROW_TILE = 256
N_ROW_TILES = N_TOK // ROW_TILE
MOE_BM = 256
N_SLOTS = N_TOK * TOP_K
RUN_ALIGN = 8
MOE_TILE = 512
N_MOE_TILES = N_TOK // MOE_TILE
N_RUNS = N_MOE_TILES * N_EXPERTS
LOCAL_ROWS = -(-(MOE_TILE * TOP_K + N_EXPERTS * (RUN_ALIGN - 1)) // MOE_BM) * MOE_BM
MOE_NB = -(-(N_SLOTS + N_RUNS * (RUN_ALIGN - 1)) // MOE_BM) + N_EXPERTS
UP_GROUP = 256
N_SEQ = BATCH + DEC_BATCH
N_CHUNKS = N_TOK // CHUNK
N_CTX_CHUNKS = N_CTX_TOK // CHUNK
VMEM_LIMIT = 56 * 1024 * 1024


def _cparams(*sem):
    return pltpu.CompilerParams(dimension_semantics=sem, vmem_limit_bytes=VMEM_LIMIT)


def _sigmoid(x):
    return 1.0 / (1.0 + jnp.exp(-x))


def _dot(a, b):
    return jnp.dot(a, b, preferred_element_type=F32)


def _dot_nt(a, b):
    return lax.dot_general(a, b, (((1,), (1,)), ((), ())), preferred_element_type=F32)


def _dot_exact(a, b):
    return jnp.dot(a, b, preferred_element_type=F32, precision=lax.Precision.HIGHEST)


def _rms(x, g):
    return x * lax.rsqrt(jnp.mean(x * x, axis=-1, keepdims=True) + EPS) * g


def _cond_of_tile(i, rows=ROW_TILE):
    ctx_tiles = N_CTX_TOK // rows
    return jnp.where(i < ctx_tiles, 0, 1 + (i - ctx_tiles) // (DEC_SEQ // rows))


def _adaln_kernel(c_ref, w_ref, b_ref, o_ref):
    c = c_ref[...]
    s = (c * _sigmoid(c)).astype(BF16)
    o_ref[0] = _dot(s, w_ref[0].astype(BF16)) + b_ref[0]


def _adaln(conds, w_ada, b_ada):
    tn = 1536
    return pl.pallas_call(
        _adaln_kernel,
        grid=(DEPTH, 6 * D_MODEL // tn),
        in_specs=[pl.BlockSpec((COND_ROWS, D_MODEL), lambda l, j: (0, 0)),
                  pl.BlockSpec((1, D_MODEL, tn), lambda l, j: (l, 0, j)),
                  pl.BlockSpec((1, 1, tn), lambda l, j: (l, 0, j))],
        out_specs=pl.BlockSpec((1, COND_ROWS, tn), lambda l, j: (l, 0, j)),
        out_shape=jax.ShapeDtypeStruct((DEPTH, COND_ROWS, 6 * D_MODEL), F32),
        compiler_params=_cparams("parallel", "parallel"),
        name="adaln",
    )(conds, w_ada, b_ada.reshape(DEPTH, 1, 6 * D_MODEL))


_IN_SPLITS = (3 * NA_DIM, GQA_Q_DIM + 2 * GQA_KV_DIM, SSM_INNER, CONV_DIM, LANES)


def _half_specs(rows, width):
    ctx_tiles = N_CTX_TOK // rows
    return [pl.BlockSpec((rows, width), lambda i, *_: (jnp.minimum(i, ctx_tiles - 1), 0)),
            pl.BlockSpec((rows, width), lambda i, *_: (jnp.maximum(i - ctx_tiles, 0), 0))]


def _read_halves(ctx_ref, lat_ref, rows):
    return jnp.where(pl.program_id(0) < N_CTX_TOK // rows, ctx_ref[...], lat_ref[...])


def _write_halves(ctx_ref, lat_ref, value, rows):
    @pl.when(pl.program_id(0) < N_CTX_TOK // rows)
    def _():
        ctx_ref[...] = value

    lat_ref[...] = value


_HALF_SHAPES = [jax.ShapeDtypeStruct((N_CTX_TOK, D_MODEL), F32), jax.ShapeDtypeStruct((N_LAT_TOK, D_MODEL), F32)]


def _inproj_kernel(xc_ref, xl_ref, m_ref, g_ref, w_ref, qkv_ref, gqa_ref, z_ref, xbc_ref, dt_ref, w_s):
    @pl.when(pl.program_id(0) == 0)
    def _():
        w_s[...] = jnp.zeros_like(w_s)
        w_s[:, 0:IN_DIM] = w_ref[0].astype(BF16)

    m = m_ref[0]
    h = _rms(_read_halves(xc_ref, xl_ref, ROW_TILE), g_ref[...]) * (1.0 + m[1:2]) + m[0:1]
    p = _dot(h.astype(BF16), w_s[...])
    off = 0
    for ref, width in zip((qkv_ref, gqa_ref, z_ref, xbc_ref, dt_ref), _IN_SPLITS):
        ref[...] = p[:, off:off + width]
        off += width


def _inproj(layer, x, mods, gain, w_in):
    row = lambda w: pl.BlockSpec((ROW_TILE, w), lambda i: (i, 0))
    return pl.pallas_call(
        _inproj_kernel,
        grid=(N_ROW_TILES,),
        in_specs=_half_specs(ROW_TILE, D_MODEL) + [
                  pl.BlockSpec((1, 6, D_MODEL), lambda i: (_cond_of_tile(i), 0, 0)),
                  pl.BlockSpec((1, D_MODEL), lambda i: (0, 0)),
                  pl.BlockSpec((1, D_MODEL, IN_DIM), lambda i: (layer, 0, 0))],
        out_specs=[row(w) for w in _IN_SPLITS],
        out_shape=[jax.ShapeDtypeStruct((N_TOK, w), F32) for w in _IN_SPLITS],
        scratch_shapes=[pltpu.VMEM((D_MODEL, IN_PAD), BF16)],
        compiler_params=_cparams("arbitrary"),
        name="inproj",
    )(*x, mods, gain, w_in)


def _softmax_pv(scores, values):
    m = scores[0].max(axis=-1, keepdims=True)
    for s in scores[1:]:
        m = jnp.maximum(m, s.max(axis=-1, keepdims=True))
    den = 0.0
    acc = 0.0
    for s, v in zip(scores, values):
        e = jnp.exp(s - m)
        den = den + e.sum(axis=-1, keepdims=True)
        acc = acc + _dot(e.astype(BF16), v)
    return acc / den


def _heads_rms(x, n_heads, g):
    return jnp.concatenate(
        [_rms(x[:, h * HEAD_DIM:(h + 1) * HEAD_DIM], g) for h in range(n_heads)], axis=-1)


def _rope(x, cos, sin_signed):
    w = x.shape[-1]
    lane = lax.broadcasted_iota(I32, x.shape, 1)
    partner = jnp.where((lane & 1) == 0, pltpu.roll(x, w - 1, 1), pltpu.roll(x, 1, 1))
    return x * cos + partner * sin_signed


_ATT_SCALE = HEAD_DIM ** -0.5


def _ctx_attn_kernel(qkv_ref, gqa_ref, qn_ref, kn_ref, nao_ref, go_ref, gk_ref):
    outs = []
    for h in range(NA_HEADS):
        sl = slice(h * HEAD_DIM, (h + 1) * HEAD_DIM)
        q = qkv_ref[:, sl].astype(BF16)
        k = qkv_ref[:, NA_DIM + h * HEAD_DIM:NA_DIM + (h + 1) * HEAD_DIM].astype(BF16)
        v = qkv_ref[:, 2 * NA_DIM + h * HEAD_DIM:2 * NA_DIM + (h + 1) * HEAD_DIM].astype(BF16)
        outs.append(_softmax_pv([_dot_nt(q, k) * _ATT_SCALE], [v]))
    nao_ref[...] = jnp.concatenate(outs, axis=-1).astype(BF16)

    gq = _heads_rms(gqa_ref[:, 0:GQA_Q_DIM], GQA_HEADS, qn_ref[...])
    gk = _heads_rms(gqa_ref[:, GQA_Q_DIM:GQA_Q_DIM + GQA_KV_DIM], GQA_KV_HEADS, kn_ref[...])
    gk_ref[...] = gk
    rep = GQA_HEADS // GQA_KV_HEADS
    outs = []
    for h in range(GQA_HEADS):
        g = h // rep
        q = gq[:, h * HEAD_DIM:(h + 1) * HEAD_DIM].astype(BF16)
        k = gk[:, g * HEAD_DIM:(g + 1) * HEAD_DIM].astype(BF16)
        v0 = GQA_Q_DIM + GQA_KV_DIM + g * HEAD_DIM
        v = gqa_ref[:, v0:v0 + HEAD_DIM].astype(BF16)
        outs.append(_softmax_pv([_dot_nt(q, k) * _ATT_SCALE], [v]))
    go_ref[...] = jnp.concatenate(outs, axis=-1).astype(BF16)


def _ctx_attn(qkv, gqa, q_norm, k_norm):
    return pl.pallas_call(
        _ctx_attn_kernel,
        grid=(BATCH,),
        in_specs=[pl.BlockSpec((SEQ, 3 * NA_DIM), lambda b: (b, 0)),
                  pl.BlockSpec((SEQ, GQA_Q_DIM + 2 * GQA_KV_DIM), lambda b: (b, 0)),
                  pl.BlockSpec((1, HEAD_DIM), lambda b: (0, 0)),
                  pl.BlockSpec((1, HEAD_DIM), lambda b: (0, 0))],
        out_specs=[pl.BlockSpec((SEQ, NA_DIM), lambda b: (b, 0)),
                   pl.BlockSpec((SEQ, GQA_Q_DIM), lambda b: (b, 0)),
                   pl.BlockSpec((SEQ, GQA_KV_DIM), lambda b: (b, 0))],
        out_shape=[jax.ShapeDtypeStruct((N_CTX_TOK, NA_DIM), BF16),
                   jax.ShapeDtypeStruct((N_CTX_TOK, GQA_Q_DIM), BF16),
                   jax.ShapeDtypeStruct((N_CTX_TOK, GQA_KV_DIM), F32)],
        compiler_params=_cparams("parallel"),
        name="ctx_attn",
    )(qkv, gqa, q_norm, k_norm)


GQA_TQ = 256
GQA_KEYS = PAST_LEN + DEC_SEQ


def _lat_gqa_kernel(gqa_ref, ck_ref, cv_ref, cos_ref, sin_ref, qn_ref, kn_ref, o_ref, kbuf, vbuf):
    qb = pl.program_id(1)

    @pl.when(qb == 0)
    def _():
        kbuf[0:PAST_LEN, :] = ck_ref[0, 0].astype(BF16)
        vbuf[0:PAST_LEN, :] = cv_ref[0, 0].astype(BF16)
        k = _heads_rms(gqa_ref[:, GQA_Q_DIM:GQA_Q_DIM + GQA_KV_DIM], GQA_KV_HEADS, kn_ref[...])
        k = _rope(k, cos_ref[:, 0:GQA_KV_DIM], sin_ref[:, 0:GQA_KV_DIM])
        kbuf[PAST_LEN:GQA_KEYS, :] = k.astype(BF16)
        vbuf[PAST_LEN:GQA_KEYS, :] = gqa_ref[:, GQA_Q_DIM + GQA_KV_DIM:].astype(BF16)

    r0 = pl.multiple_of(qb * GQA_TQ, GQA_TQ)
    q = _heads_rms(gqa_ref[pl.ds(r0, GQA_TQ), 0:GQA_Q_DIM], GQA_HEADS, qn_ref[...])
    q = _rope(q, cos_ref[pl.ds(r0, GQA_TQ), :], sin_ref[pl.ds(r0, GQA_TQ), :]).astype(BF16)
    rep = GQA_HEADS // GQA_KV_HEADS
    outs = []
    for h in range(GQA_HEADS):
        g = h // rep
        k = kbuf[:, g * HEAD_DIM:(g + 1) * HEAD_DIM]
        v = vbuf[:, g * HEAD_DIM:(g + 1) * HEAD_DIM]
        s = _dot_nt(q[:, h * HEAD_DIM:(h + 1) * HEAD_DIM], k) * _ATT_SCALE
        outs.append(_softmax_pv([s], [v]))
    o_ref[...] = jnp.concatenate(outs, axis=-1).astype(BF16)


def _lat_gqa(layer, gqa, cache_k, cache_v, cos, sin, q_norm, k_norm):
    lat_blk = N_CTX_TOK // DEC_SEQ
    return pl.pallas_call(
        _lat_gqa_kernel,
        grid=(DEC_BATCH, DEC_SEQ // GQA_TQ),
        in_specs=[pl.BlockSpec((DEC_SEQ, GQA_Q_DIM + 2 * GQA_KV_DIM), lambda b, q: (lat_blk + b, 0)),
                  pl.BlockSpec((1, 1, PAST_LEN, GQA_KV_DIM), lambda b, q: (b, layer, 0, 0)),
                  pl.BlockSpec((1, 1, PAST_LEN, GQA_KV_DIM), lambda b, q: (b, layer, 0, 0)),
                  pl.BlockSpec((DEC_SEQ, GQA_Q_DIM), lambda b, q: (0, 0)),
                  pl.BlockSpec((DEC_SEQ, GQA_Q_DIM), lambda b, q: (0, 0)),
                  pl.BlockSpec((1, HEAD_DIM), lambda b, q: (0, 0)),
                  pl.BlockSpec((1, HEAD_DIM), lambda b, q: (0, 0))],
        out_specs=pl.BlockSpec((GQA_TQ, GQA_Q_DIM), lambda b, q: (b * (DEC_SEQ // GQA_TQ) + q, 0)),
        out_shape=jax.ShapeDtypeStruct((N_LAT_TOK, GQA_Q_DIM), BF16),
        scratch_shapes=[pltpu.VMEM((GQA_KEYS, GQA_KV_DIM), BF16),
                        pltpu.VMEM((GQA_KEYS, GQA_KV_DIM), BF16)],
        compiler_params=_cparams("arbitrary", "arbitrary"),
        name="lat_gqa",
    )(gqa, cache_k, cache_v, cos, sin, q_norm, k_norm)


def _rope_tables():
    t = np.arange(DEC_SEQ)
    row = (t // GRID_W).astype(np.float64)
    col = (t % GRID_W).astype(np.float64)
    axis_dim = HEAD_DIM // 2
    inv_freq = ROPE_THETA ** (-np.arange(0, axis_dim, 2, dtype=np.float64) / axis_dim)
    ang = np.concatenate([row[:, None] * inv_freq, col[:, None] * inv_freq], axis=-1)
    cos = np.repeat(np.cos(ang), 2, axis=-1)
    sin = np.repeat(np.sin(ang), 2, axis=-1) * np.tile(np.array([-1.0, 1.0]), HEAD_DIM // 2)
    return (jnp.asarray(np.tile(cos, (1, GQA_HEADS)), F32), jnp.asarray(np.tile(sin, (1, GQA_HEADS)), F32))


NA_ROWS = DEC_SEQ // GRID_W
NA_KEYS = NA_WIN_ROWS * GRID_W


NA_ROWS_PER_STEP = 4


def _lat_na_kernel(qkv_ref, ck_ref, cv_ref, *rest):
    bias_refs, o_ref = rest[:NA_ROWS_PER_STEP], rest[NA_ROWS_PER_STEP]
    for j in range(NA_ROWS_PER_STEP):
        r = pl.program_id(1) * NA_ROWS_PER_STEP + j
        r0 = jnp.clip(r - NA_WIN_ROWS // 2, 0, NA_ROWS - NA_WIN_ROWS)
        q0 = pl.multiple_of(r * GRID_W, GRID_W)
        k0 = pl.multiple_of(r0 * GRID_W, GRID_W)
        first = lax.broadcasted_iota(I32, (GRID_W, 2 * HEAD_DIM), 1) < HEAD_DIM
        outs = []
        for ha in range(0, NA_HEADS, 2):
            c0 = ha * HEAD_DIM
            slab = slice(c0, c0 + 2 * HEAD_DIM)
            q = qkv_ref[pl.ds(q0, GRID_W), slab]
            k = qkv_ref[pl.ds(k0, NA_KEYS), NA_DIM + c0:NA_DIM + c0 + 2 * HEAD_DIM].astype(BF16)
            v = qkv_ref[pl.ds(k0, NA_KEYS), 2 * NA_DIM + c0:2 * NA_DIM + c0 + 2 * HEAD_DIM].astype(BF16)
            kc = ck_ref[0, 0, :, slab].astype(BF16)
            vc = cv_ref[0, 0, :, slab].astype(BF16)
            pair = []
            for h, own in ((ha, first), (ha + 1, jnp.logical_not(first))):
                qh = jnp.where(own, q, 0.0).astype(BF16)
                s_nb = _dot_nt(qh, k) * _ATT_SCALE + bias_refs[j][0, h]
                s_ctx = _dot_nt(qh, kc) * _ATT_SCALE
                pair.append(_softmax_pv([s_nb, s_ctx], [v, vc]))
            outs.append(jnp.where(first, pair[0], pair[1]))
        o_ref[j * GRID_W:(j + 1) * GRID_W, :] = jnp.concatenate(outs, axis=-1).astype(BF16)


def _na_row_offset(r):
    return r - jnp.clip(r - NA_WIN_ROWS // 2, 0, NA_ROWS - NA_WIN_ROWS)


def _lat_na(layer, qkv, cache_k, cache_v, bias):
    lat_blk = N_CTX_TOK // DEC_SEQ
    steps = NA_ROWS // NA_ROWS_PER_STEP
    bias_spec = lambda j: pl.BlockSpec((1, NA_HEADS, GRID_W, NA_KEYS),
                                       lambda b, s: (_na_row_offset(s * NA_ROWS_PER_STEP + j), 0, 0, 0))
    return pl.pallas_call(
        _lat_na_kernel,
        grid=(DEC_BATCH, steps),
        in_specs=[pl.BlockSpec((DEC_SEQ, 3 * NA_DIM), lambda b, s: (lat_blk + b, 0)),
                  pl.BlockSpec((1, 1, PAST_LEN, NA_DIM), lambda b, s: (b, layer, 0, 0)),
                  pl.BlockSpec((1, 1, PAST_LEN, NA_DIM), lambda b, s: (b, layer, 0, 0))]
                 + [bias_spec(j) for j in range(NA_ROWS_PER_STEP)],
        out_specs=pl.BlockSpec((NA_ROWS_PER_STEP * GRID_W, NA_DIM), lambda b, s: (b * steps + s, 0)),
        out_shape=jax.ShapeDtypeStruct((N_LAT_TOK, NA_DIM), BF16),
        compiler_params=_cparams("parallel", "arbitrary"),
        name="lat_na",
    )(qkv, cache_k, cache_v, *([bias] * NA_ROWS_PER_STEP))


def _na_bias_tables(rpb):
    d = np.arange(NA_WIN_ROWS)[:, None]
    kr = np.arange(NA_WIN_ROWS)[None, :]
    dr = kr - d + NA_WIN_ROWS - 1
    qc = np.arange(GRID_W)[:, None]
    kc = np.arange(GRID_W)[None, :]
    col0 = np.clip(qc - NA_WIN_COLS // 2, 0, GRID_W - NA_WIN_COLS)
    in_win = (kc >= col0) & (kc < col0 + NA_WIN_COLS)
    dc = np.clip(kc - qc + NA_WIN_COLS - 1, 0, 2 * NA_WIN_COLS - 2)
    row_hot = (dr[:, :, None] == np.arange(2 * NA_WIN_ROWS - 1)).astype(np.float32)
    col_hot = (dc[:, :, None] == np.arange(2 * NA_WIN_COLS - 1)).astype(np.float32)
    b = jnp.einsum('hac,dka,qxc->dhqkx', rpb.astype(F32), row_hot, col_hot, precision=lax.Precision.HIGHEST)
    b = jnp.where(in_win[None, None, :, None, :], b, NEG_INF)
    return b.reshape(NA_WIN_ROWS, NA_HEADS, GRID_W, NA_KEYS)


CONV_TB = 1024
CONV_HALO = 8
CONV_HALO_BLOCKS = CONV_TB // CONV_HALO


def _conv_kernel(prev_ref, x_ref, next_ref, w_ref, b_ref, o_ref):
    i = pl.program_id(0)
    seq = jnp.where(i < N_CTX_TOK // CONV_TB, SEQ, DEC_SEQ)
    x = x_ref[...]
    ext = jnp.concatenate([prev_ref[...], x, next_ref[...]], axis=0)
    n_ext = CONV_TB + 2 * CONV_HALO
    pos = (lax.broadcasted_iota(I32, (CONV_TB, 1), 0) + i * CONV_TB) & (seq - 1)
    half = CONV_W // 2
    acc = x * w_ref[half:half + 1, :]
    for s in range(-half, half + 1):
        if s == 0:
            continue
        shifted = pltpu.roll(ext, (-s) % n_ext, 0)[CONV_HALO:CONV_HALO + CONV_TB]
        valid = (pos + s >= 0) & (pos + s < seq)
        acc = acc + jnp.where(valid, shifted, 0.0) * w_ref[half + s:half + s + 1, :]
    acc = acc + b_ref[...]
    o_ref[...] = acc * _sigmoid(acc)


def _conv_act(xbc, conv_w, conv_b):
    return pl.pallas_call(
        _conv_kernel,
        grid=(N_TOK // CONV_TB,),
        in_specs=[pl.BlockSpec((CONV_HALO, CONV_DIM),
                               lambda i: (jnp.maximum(i * CONV_HALO_BLOCKS - 1, 0), 0)),
                  pl.BlockSpec((CONV_TB, CONV_DIM), lambda i: (i, 0)),
                  pl.BlockSpec((CONV_HALO, CONV_DIM),
                               lambda i: (jnp.minimum((i + 1) * CONV_HALO_BLOCKS, N_TOK // CONV_HALO - 1), 0)),
                  pl.BlockSpec((CONV_W, CONV_DIM), lambda i: (0, 0)),
                  pl.BlockSpec((1, CONV_DIM), lambda i: (0, 0))],
        out_specs=pl.BlockSpec((CONV_TB, CONV_DIM), lambda i: (i, 0)),
        out_shape=jax.ShapeDtypeStruct((N_TOK, CONV_DIM), F32),
        compiler_params=_cparams("parallel"),
        name="conv_act",
    )(xbc, xbc, xbc, conv_w, conv_b)


def _chunk_seq(g):
    ctx_n = SEQ // CHUNK
    lat_n = DEC_SEQ // CHUNK
    is_ctx = g < N_CTX_CHUNKS
    gl = g - N_CTX_CHUNKS
    sid = jnp.where(is_ctx, g // ctx_n, BATCH + gl // lat_n)
    cin = jnp.where(is_ctx, g % ctx_n, gl % lat_n)
    n = jnp.where(is_ctx, ctx_n, lat_n)
    return sid, cin, n


def _ssd_init(h0_ref, st_ref, gg, reverse):
    sid, cin, n = _chunk_seq(gg)

    @pl.when(cin == (n - 1 if reverse else 0))
    def _():
        st_ref[0] = jnp.where(sid >= BATCH, h0_ref[0, 0, 0], 0.0)


def _ssd_chunk(xa_ref, dt_ref, dtb, alog, y_ref, st_ref, reverse):
    x = dt_ref[...] + dtb
    dt = jnp.maximum(x, 0.0) + jnp.log1p(jnp.exp(-jnp.abs(x)))
    dta = dt * -jnp.exp(alog)
    ii = lax.broadcasted_iota(I32, (CHUNK, CHUNK), 0)
    jj = lax.broadcasted_iota(I32, (CHUNK, CHUNK), 1)
    tri = (jj >= ii) if reverse else (jj <= ii)
    cum = _dot_exact(tri.astype(F32), dta)
    cum_t = cum.T
    edge = 0 if reverse else CHUNK - 1
    tot = cum[edge:edge + 1, :]
    first = jj < SSM_HEAD_DIM
    first_rows = lax.broadcasted_iota(I32, (2 * SSM_HEAD_DIM, SSM_STATE), 0) < SSM_HEAD_DIM
    rep = SSM_HEADS // SSM_GROUPS
    for grp in range(SSM_GROUPS):
        bg = xa_ref[:, SSM_INNER + grp * SSM_STATE:SSM_INNER + (grp + 1) * SSM_STATE].astype(BF16)
        c0 = SSM_INNER + SSM_BC_DIM + grp * SSM_STATE
        cg = xa_ref[:, c0:c0 + SSM_STATE].astype(BF16)
        cb = _dot_nt(cg, bg)
        for ha in range(grp * rep, (grp + 1) * rep, 2):
            hb = ha + 1
            slab = slice(ha * SSM_HEAD_DIM, (hb + 1) * SSM_HEAD_DIM)
            col_a, col_b = cum[:, ha:ha + 1], cum[:, hb:hb + 1]
            tot_a, tot_b = tot[:, ha:ha + 1], tot[:, hb:hb + 1]
            decay_a = jnp.where(tri, jnp.exp(jnp.minimum(col_a - cum_t[ha:ha + 1, :], 0.0)), 0.0)
            decay_b = jnp.where(tri, jnp.exp(jnp.minimum(col_b - cum_t[hb:hb + 1, :], 0.0)), 0.0)
            xdt = xa_ref[:, slab] * jnp.where(first, dt[:, ha:ha + 1], dt[:, hb:hb + 1])
            xdt_b = xdt.astype(BF16)
            y = jnp.where(first, _dot((cb * decay_a).astype(BF16), xdt_b), _dot((cb * decay_b).astype(BF16), xdt_b))
            state = st_ref[0, slab, :]
            y = y + _dot_nt(cg, state.astype(BF16)) * jnp.where(first, jnp.exp(col_a), jnp.exp(col_b))
            y_ref[:, slab] = y
            to_end = jnp.where(first, jnp.exp(tot_a - col_a), jnp.exp(tot_b - col_b))
            upd = lax.dot_general((xdt * to_end).astype(BF16), bg, (((0,), (0,)), ((), ())),
                                  preferred_element_type=F32)
            st_ref[0, slab, :] = state * jnp.where(first_rows, jnp.exp(tot_a), jnp.exp(tot_b)) + upd


def _ssd_kernel(xaf_ref, dtf_ref, h0f_ref, xab_ref, dtb_ref, h0b_ref, bias_ref, alog_ref,
                yf_ref, stf_ref, yb_ref, stb_ref):
    g = pl.program_id(0)
    _ssd_init(h0f_ref, stf_ref, g, False)
    _ssd_chunk(xaf_ref, dtf_ref, bias_ref[0:1, :], alog_ref[0:1, :], yf_ref, stf_ref, False)
    _ssd_init(h0b_ref, stb_ref, N_CHUNKS - 1 - g, True)
    _ssd_chunk(xab_ref, dtb_ref, bias_ref[1:2, :], alog_ref[1:2, :], yb_ref, stb_ref, True)


def _ssd(layer, xa, dt_raw, h0, dt_bias, a_log):
    rev = lambda g: N_CHUNKS - 1 - g
    chunk = lambda w, order: pl.BlockSpec((CHUNK, w), lambda g: (order(g), 0))
    state = lambda order: pl.BlockSpec((1, SSM_INNER, SSM_STATE), lambda g: (_chunk_seq(order(g))[0], 0, 0))
    start = lambda order, d: pl.BlockSpec(
        (1, 1, 1, SSM_INNER, SSM_STATE),
        lambda g: (jnp.maximum(_chunk_seq(order(g))[0] - BATCH, 0), layer, d, 0, 0))
    same = lambda g: g
    y_shape = jax.ShapeDtypeStruct((N_TOK, SSM_INNER), F32)
    st_shape = jax.ShapeDtypeStruct((N_SEQ, SSM_INNER, SSM_STATE), F32)
    h0_fwd = h0_bwd = h0
    return pl.pallas_call(
        _ssd_kernel,
        grid=(N_CHUNKS,),
        in_specs=[chunk(CONV_DIM, same), chunk(LANES, same), start(same, 0),
                  chunk(CONV_DIM, rev), chunk(LANES, rev), start(rev, 1),
                  pl.BlockSpec((2, LANES), lambda g: (0, 0)),
                  pl.BlockSpec((2, LANES), lambda g: (0, 0))],
        out_specs=[chunk(SSM_INNER, same), state(same), chunk(SSM_INNER, rev), state(rev)],
        out_shape=[y_shape, st_shape, y_shape, st_shape],
        compiler_params=_cparams("arbitrary"),
        name="ssd",
    )(xa, dt_raw, h0_fwd, xa, dt_raw, h0_bwd, dt_bias, a_log)


def _outproj_kernel(xc_ref, xl_ref, nao_c_ref, nao_l_ref, go_c_ref, go_l_ref, yf_ref, yb_ref, xs_ref, z_ref, dsk_ref,
                    sn_ref, wo_ref, m_ref, g2_ref, wrh_ref, wrl_ref, br_ref, xoc_ref, xol_ref, h_ref, idx_ref,
                    gate_ref, sel_ref, wo_s):
    @pl.when(pl.program_id(0) == 0)
    def _():
        wo_s[...] = wo_ref[0].astype(BF16)

    nao = _read_halves(nao_c_ref, nao_l_ref, ROW_TILE)
    go = _read_halves(go_c_ref, go_l_ref, ROW_TILE)
    m = m_ref[0]
    z = z_ref[...]
    y = (yf_ref[...] + yb_ref[...] + xs_ref[...] * dsk_ref[...]) * (z * _sigmoid(z))
    s_o = _rms(y, sn_ref[...]).astype(BF16)
    mix = (_dot(nao, wo_s[0:NA_DIM, :])
           + _dot(go, wo_s[NA_DIM:NA_DIM + GQA_Q_DIM, :])
           + _dot(s_o, wo_s[NA_DIM + GQA_Q_DIM:, :]))
    x = _read_halves(xc_ref, xl_ref, ROW_TILE) + m[2:3] * mix
    _write_halves(xoc_ref, xol_ref, x, ROW_TILE)
    h = _rms(x, g2_ref[...]) * (1.0 + m[4:5]) + m[3:4]
    h_hi = h.astype(BF16)
    h_ref[...] = h_hi

    h_lo = (h - h_hi.astype(F32)).astype(BF16)
    w_hi = wrh_ref[...]
    logits = _dot(h_hi, w_hi) + _dot(h_lo, w_hi) + _dot(h_hi, wrl_ref[...]) + br_ref[...]
    lane = lax.broadcasted_iota(I32, logits.shape, 1).astype(F32)
    vals, idxs = [], []
    for _ in range(TOP_K):
        v = logits.max(axis=-1, keepdims=True)
        i = jnp.where(logits == v, lane, float(LANES)).min(axis=-1, keepdims=True)
        vals.append(v)
        idxs.append(i)
        logits = jnp.where(lane == i, -jnp.inf, logits)
    es = [jnp.exp(v - vals[0]) for v in vals]
    den = es[0] + es[1] + es[2] + es[3]
    idx_out = jnp.zeros(lane.shape, F32)
    gate_out = jnp.zeros(lane.shape, F32)
    sel = jnp.zeros(lane.shape, F32)
    for k in range(TOP_K):
        idx_out = jnp.where(lane == float(k), idxs[k], idx_out)
        gate_out = jnp.where(lane == float(k), es[k] / den, gate_out)
        sel = jnp.where(lane == idxs[k], 1.0, sel)
    idx_ref[...] = idx_out.astype(I32)
    gate_ref[...] = gate_out
    sel_ref[...] = sel.astype(BF16)


def _outproj(layer, x, nao_c, nao_l, go_c, go_l, yf, yb, xa, z, d_skip, ssm_norm, w_out, mods, gain2, w_router,
             b_router):
    row = lambda w: pl.BlockSpec((ROW_TILE, w), lambda i: (i, 0))
    full = lambda a, b: pl.BlockSpec((a, b), lambda i: (0, 0))
    wr_hi = w_router.astype(BF16)
    wr_lo = (w_router - wr_hi.astype(F32)).astype(BF16)
    return pl.pallas_call(
        _outproj_kernel,
        grid=(N_ROW_TILES,),
        in_specs=_half_specs(ROW_TILE, D_MODEL) + _half_specs(ROW_TILE, NA_DIM) + _half_specs(ROW_TILE, GQA_Q_DIM) + [
                  row(SSM_INNER), row(SSM_INNER),
                  row(SSM_INNER), row(SSM_INNER), full(1, SSM_INNER), full(1, SSM_INNER),
                  pl.BlockSpec((1, D_MIX, D_MODEL), lambda i: (layer, 0, 0)),
                  pl.BlockSpec((1, 6, D_MODEL), lambda i: (_cond_of_tile(i), 0, 0)),
                  full(1, D_MODEL), full(D_MODEL, LANES), full(D_MODEL, LANES), full(1, LANES)],
        out_specs=_half_specs(ROW_TILE, D_MODEL) + [row(D_MODEL), row(LANES), row(LANES), row(LANES)],
        out_shape=_HALF_SHAPES + [
                   jax.ShapeDtypeStruct((N_TOK, D_MODEL), BF16),
                   jax.ShapeDtypeStruct((N_TOK, LANES), I32),
                   jax.ShapeDtypeStruct((N_TOK, LANES), F32),
                   jax.ShapeDtypeStruct((N_TOK, LANES), BF16)],
        scratch_shapes=[pltpu.VMEM((D_MIX, D_MODEL), BF16)],
        compiler_params=_cparams("arbitrary"),
        name="outproj_router",
    )(*x, nao_c, nao_l, go_c, go_l, yf, yb, xa, z, d_skip, ssm_norm, w_out, mods, gain2, wr_hi, wr_lo, b_router)


RANK_TB = 512


def _rank_kernel(sel_ref, rank_ref, cnt_ref, carry):
    @pl.when(pl.program_id(0) == 0)
    def _():
        carry[...] = jnp.zeros_like(carry)

    sel = sel_ref[...]
    ii = lax.broadcasted_iota(I32, (RANK_TB, RANK_TB), 0)
    jj = lax.broadcasted_iota(I32, (RANK_TB, RANK_TB), 1)
    before = (jj < ii).astype(BF16)
    rank_ref[...] = _dot(before, sel) + carry[0:1, :]
    carry[...] = carry[...] + _dot(jnp.ones((8, RANK_TB), BF16), sel)
    cnt_ref[...] = carry[...]


def _ranks(sel):
    return pl.pallas_call(
        _rank_kernel,
        grid=(N_TOK // RANK_TB,),
        in_specs=[pl.BlockSpec((RANK_TB, LANES), lambda i: (i, 0))],
        out_specs=[pl.BlockSpec((RANK_TB, LANES), lambda i: (i, 0)),
                   pl.BlockSpec((8, LANES), lambda i: (0, 0))],
        out_shape=[jax.ShapeDtypeStruct((N_TOK, LANES), F32),
                   jax.ShapeDtypeStruct((8, LANES), F32)],
        scratch_shapes=[pltpu.VMEM((8, LANES), F32)],
        compiler_params=_cparams("arbitrary"),
        name="moe_ranks",
    )(sel)


_RUN_PIECES = (512, 256, 128, 64, 32, 16, 8)


def _run_dma(src, s0, dst, d0, n, sem, *, wait, fixed_src=False, pieces=_RUN_PIECES):
    for size in pieces:
        @pl.when((n & size) != 0)
        def _(size=size):
            done = n & ~(2 * size - 1)
            s = 0 if fixed_src else pl.multiple_of(s0 + done, RUN_ALIGN)
            d = pl.multiple_of(d0 + done, RUN_ALIGN)
            copy = pltpu.make_async_copy(src.at[pl.ds(s, size), :], dst.at[pl.ds(d, size), :], sem)
            if wait:
                copy.wait()
            else:
                copy.start()


_TILE_PIECES = (2048, 1024, 512, 256, 128, 64, 32, 16, 8)


def _wait_rows(buf, n, sem):
    for size in _TILE_PIECES:
        @pl.when((n & size) != 0)
        def _(size=size):
            pltpu.make_async_copy(buf.at[pl.ds(0, size), :], buf.at[pl.ds(0, size), :], sem).wait()


def _local_rows(idx_ref, rank_ref, base_ref):
    pos = rank_ref[...] + base_ref[0]
    lane = lax.broadcasted_iota(I32, pos.shape, 1)
    idx = idx_ref[...]
    return [jnp.sum(jnp.where(lane == idx[:, k:k + 1], pos, 0.0), axis=-1, keepdims=True).astype(I32)
            for k in range(TOP_K)]


def _dispatch_kernel(run_ref, loc_ref, glb_ref, tot_ref, fs_ref, fl_ref, nu_ref,
                     h_ref, idx_ref, rank_ref, base_ref, out_ref, xl_ref, zbuf, sems):
    i = pl.program_id(0)
    slot = i % 2
    xl = xl_ref.at[slot]
    sem = sems.at[slot]

    @pl.when(i == 0)
    def _():
        zbuf[...] = jnp.zeros_like(zbuf)
        fill_pieces = tuple(s for s in _RUN_PIECES if s < MOE_BM)
        for wait in (False, True):
            for e in range(N_EXPERTS):
                _run_dma(zbuf, 0, out_ref, fs_ref[e], fl_ref[e], sem, wait=wait, fixed_src=True, pieces=fill_pieces)

        def tail(b, carry):
            copy = pltpu.make_async_copy(zbuf, out_ref.at[pl.ds(pl.multiple_of(b * MOE_BM, MOE_BM), MOE_BM), :], sem)
            copy.start()
            copy.wait()
            return carry

        lax.fori_loop(nu_ref[0], MOE_NB, tail, 0)

    @pl.when(i >= 2)
    def _():
        _wait_rows(xl, tot_ref[i - 2], sem)

    rows = _local_rows(idx_ref, rank_ref, base_ref)
    p = lax.broadcasted_iota(I32, (MOE_TILE, LOCAL_ROWS), 1)
    hot = (p == rows[0])
    for k in range(1, TOP_K):
        hot = hot | (p == rows[k])
    xl[...] = lax.dot_general(hot.astype(BF16), h_ref[...].astype(BF16), (((0,), (0,)), ((), ())),
                              preferred_element_type=F32)
    for e in range(N_EXPERTS):
        j = i * N_EXPERTS + e
        _run_dma(xl, loc_ref[j], out_ref, glb_ref[j], run_ref[j], sem, wait=False)

    @pl.when(i == N_MOE_TILES - 1)
    def _():
        _wait_rows(xl, tot_ref[i], sem)
        _wait_rows(xl_ref.at[1 - slot], tot_ref[i - 1], sems.at[1 - slot])


def _dispatch(plan, h, idx, rank):
    tile = lambda w: pl.BlockSpec((MOE_TILE, w), lambda i, *_: (i, 0))
    return pl.pallas_call(
        _dispatch_kernel,
        grid_spec=pltpu.PrefetchScalarGridSpec(
            num_scalar_prefetch=7,
            grid=(N_MOE_TILES,),
            in_specs=[tile(D_MODEL), tile(LANES), tile(LANES),
                      pl.BlockSpec((1, 1, LANES), lambda i, *_: (i, 0, 0))],
            out_specs=pl.BlockSpec(memory_space=pl.ANY),
            scratch_shapes=[pltpu.VMEM((2, LOCAL_ROWS, D_MODEL), F32),
                            pltpu.VMEM((MOE_BM, D_MODEL), F32),
                            pltpu.SemaphoreType.DMA((2,))]),
        out_shape=jax.ShapeDtypeStruct((MOE_NB * MOE_BM, D_MODEL), F32),
        compiler_params=_cparams("arbitrary"),
        name="moe_dispatch",
    )(plan["run"], plan["local"], plan["global"], plan["total"], plan["fill_start"], plan["fill_len"],
      plan["n_used"], h, idx, rank, plan["base"])


def _expert_kernel(be_ref, nu_ref, new_ref, fe_ref, x_ref, wu_ref, bu_ref, wd_ref, bd_ref, y_ref, wu_s, wd_s):
    del fe_ref
    b = pl.program_id(0)
    used = b < nu_ref[0]

    @pl.when(jnp.logical_not(used))
    def _():
        y_ref[...] = jnp.zeros_like(y_ref)

    @pl.when(jnp.logical_and(used, new_ref[b] == 1))
    def _():
        r = lax.broadcasted_iota(I32, (UP_GROUP, UP_GROUP), 0)
        c = lax.broadcasted_iota(I32, (UP_GROUP, UP_GROUP), 1)
        src = jnp.where(c < UP_GROUP // 2, 2 * c, 2 * (c - UP_GROUP // 2) + 1)
        perm = (r == src).astype(BF16)
        for g in range(2 * D_FF // UP_GROUP):
            cols = slice(g * UP_GROUP, (g + 1) * UP_GROUP)
            wu_s[:, cols] = _dot(wu_ref[0, 0, :, cols].astype(BF16), perm).astype(BF16)
        wd_s[...] = wd_ref[0, 0].astype(BF16)

    @pl.when(used)
    def _():
        up = _dot(x_ref[...].astype(BF16), wu_s[...]) + bu_ref[0]
        half = UP_GROUP // 2
        acts = []
        for g in range(2 * D_FF // UP_GROUP):
            gate = jnp.minimum(up[:, g * UP_GROUP:g * UP_GROUP + half], SWIGLU_LIMIT)
            lin = jnp.clip(up[:, g * UP_GROUP + half:(g + 1) * UP_GROUP], -SWIGLU_LIMIT, SWIGLU_LIMIT)
            acts.append((gate * _sigmoid(SWIGLU_ALPHA * gate) * (lin + 1.0)).astype(BF16))
        y_ref[...] = _dot(jnp.concatenate(acts, axis=-1), wd_s[...]) + bd_ref[0]


def _experts(layer, plan, xs, w_up, b_up, w_down, b_down):
    blk = lambda b, nu: jnp.maximum(jnp.minimum(b, nu[0] - 1), 0)
    return pl.pallas_call(
        _expert_kernel,
        grid_spec=pltpu.PrefetchScalarGridSpec(
            num_scalar_prefetch=4,
            grid=(MOE_NB,),
            in_specs=[pl.BlockSpec((MOE_BM, D_MODEL), lambda b, be, nu, nw, fe: (blk(b, nu), 0)),
                      pl.BlockSpec((1, 1, D_MODEL, 2 * D_FF), lambda b, be, nu, nw, fe: (layer, fe[blk(b, nu)], 0, 0)),
                      pl.BlockSpec((1, 1, 2 * D_FF), lambda b, be, nu, nw, fe: (be[blk(b, nu)], 0, 0)),
                      pl.BlockSpec((1, 1, D_FF, D_MODEL), lambda b, be, nu, nw, fe: (layer, fe[blk(b, nu)], 0, 0)),
                      pl.BlockSpec((1, 1, D_MODEL), lambda b, be, nu, nw, fe: (be[blk(b, nu)], 0, 0))],
            out_specs=pl.BlockSpec((MOE_BM, D_MODEL), lambda b, be, nu, nw, fe: (b, 0)),
            scratch_shapes=[pltpu.VMEM((D_MODEL, 2 * D_FF), BF16),
                            pltpu.VMEM((D_FF, D_MODEL), BF16)]),
        out_shape=jax.ShapeDtypeStruct((MOE_NB * MOE_BM, D_MODEL), F32),
        compiler_params=_cparams("arbitrary"),
        name="moe_experts",
    )(plan["blk_expert"], plan["n_used"], plan["blk_new"], plan["blk_fetch"], xs, w_up, b_up, w_down, b_down)


def _combine_kernel(run_ref, loc_ref, glb_ref, tot_ref, xc_ref, xl_ref, gate_ref, idx_ref, rank_ref, base_ref, m_ref,
                    fn_ref, ys_ref, oc_ref, ol_ref, yl_ref, sems, *, final):
    i = pl.program_id(0)
    slot = i % 2

    def fetch(tile, buf):
        yl = yl_ref.at[buf]
        for e in range(N_EXPERTS):
            j = tile * N_EXPERTS + e
            _run_dma(ys_ref, glb_ref[j], yl, loc_ref[j], run_ref[j], sems.at[buf], wait=False)

        def clear(r, carry):
            yl[pl.ds(pl.multiple_of(r * RUN_ALIGN, RUN_ALIGN), RUN_ALIGN), :] = jnp.zeros((RUN_ALIGN, D_MODEL), F32)
            return carry

        lax.fori_loop(tot_ref[tile] // RUN_ALIGN, LOCAL_ROWS // RUN_ALIGN, clear, 0)

    @pl.when(i == 0)
    def _():
        fetch(0, 0)

    @pl.when(i + 1 < N_MOE_TILES)
    def _():
        fetch(i + 1, 1 - slot)

    rows = _local_rows(idx_ref, rank_ref, base_ref)
    gate = gate_ref[...]
    p = lax.broadcasted_iota(I32, (MOE_TILE, LOCAL_ROWS), 1)
    w = jnp.zeros((MOE_TILE, LOCAL_ROWS), F32)
    for k in range(TOP_K):
        w = jnp.where(p == rows[k], gate[:, k:k + 1], w)

    _wait_rows(yl_ref.at[slot], tot_ref[i], sems.at[slot])
    acc = _dot(w.astype(BF16), yl_ref[slot].astype(BF16))
    x = _read_halves(xc_ref, xl_ref, MOE_TILE) + m_ref[0][5:6] * acc
    _write_halves(oc_ref, ol_ref, _rms(x, fn_ref[...]) if final else x, MOE_TILE)


def _combine(plan, x, gates, idx, rank, mods, final_norm, ys, final):
    tile = lambda w: pl.BlockSpec((MOE_TILE, w), lambda i, *_: (i, 0))
    return pl.pallas_call(
        functools.partial(_combine_kernel, final=final),
        grid_spec=pltpu.PrefetchScalarGridSpec(
            num_scalar_prefetch=4,
            grid=(N_MOE_TILES,),
            in_specs=_half_specs(MOE_TILE, D_MODEL) + [tile(LANES), tile(LANES), tile(LANES),
                      pl.BlockSpec((1, 1, LANES), lambda i, *_: (i, 0, 0)),
                      pl.BlockSpec((1, 6, D_MODEL), lambda i, *_: (_cond_of_tile(i, MOE_TILE), 0, 0)),
                      pl.BlockSpec((1, D_MODEL), lambda i, *_: (0, 0)),
                      pl.BlockSpec(memory_space=pl.ANY)],
            out_specs=_half_specs(MOE_TILE, D_MODEL),
            scratch_shapes=[pltpu.VMEM((2, LOCAL_ROWS, D_MODEL), F32),
                            pltpu.SemaphoreType.DMA((2,))]),
        out_shape=_HALF_SHAPES,
        compiler_params=_cparams("arbitrary"),
        name="moe_combine",
    )(plan["run"], plan["local"], plan["global"], plan["total"], *x, gates, idx, rank, plan["base"], mods,
      final_norm, ys)


def _moe_plan(rank, cnt):
    first = rank[::MOE_TILE, :N_EXPERTS].astype(I32)
    total = cnt[0:1, :N_EXPERTS].astype(I32)
    run = jnp.concatenate([first[1:], total], axis=0) - first
    run = (run + RUN_ALIGN - 1) // RUN_ALIGN * RUN_ALIGN
    local = jnp.cumsum(run, axis=1) - run
    sizes = run.sum(axis=0)
    padded = (sizes + MOE_BM - 1) // MOE_BM * MOE_BM
    ends = jnp.cumsum(padded)
    starts = ends - padded
    glob = starts[None, :] + jnp.cumsum(run, axis=0) - run
    blk_start = jnp.arange(MOE_NB, dtype=I32) * MOE_BM
    blk_expert = jnp.minimum(jnp.sum(ends[None, :] <= blk_start[:, None], axis=1), N_EXPERTS - 1).astype(I32)
    blk_new = jnp.concatenate([jnp.ones((1,), I32), (blk_expert[1:] != blk_expert[:-1]).astype(I32)])
    blk = jnp.arange(MOE_NB, dtype=I32)
    later_start = (blk_new[None, :] == 1) & (blk[None, :] > blk[:, None])
    next_start = jnp.min(jnp.where(later_start, blk[None, :], MOE_NB), axis=1)
    next_hot = (blk[None, :] == next_start[:, None]).astype(I32)
    next_expert = jnp.where(next_start < MOE_NB, jnp.sum(next_hot * blk_expert[None, :], axis=1), blk_expert)
    blk_fetch = jnp.where(blk_new == 1, blk_expert, next_expert)
    base = _pad_lanes((local - first).astype(F32)).reshape(N_MOE_TILES, 1, LANES)
    return {"run": run.reshape(-1), "local": local.reshape(-1), "global": glob.reshape(-1), "total": run.sum(axis=1),
            "fill_start": starts + sizes, "fill_len": padded - sizes, "n_used": ends[-1:] // MOE_BM,
            "blk_expert": blk_expert, "blk_new": blk_new, "blk_fetch": blk_fetch, "base": base}


def _pad_lanes(v, fill=0.0):
    return jnp.pad(v, ((0, 0), (0, LANES - v.shape[-1])), constant_values=fill)


def kernel(x_prompt, x_sample, cache_na_k, cache_na_v, cache_gqa_k, cache_gqa_v, state_ssm, c, c_ctx, w_ada, b_ada, norm_mix, norm_ffn, w_in, na_rpb, gqa_q_norm, gqa_k_norm, ssm_conv_w, ssm_conv_b, ssm_dt_bias, ssm_a_log, ssm_d, ssm_norm, w_out, w_router, b_router, w_up, b_up, w_down, b_down, final_norm):
    x = (x_prompt.reshape(N_CTX_TOK, D_MODEL), x_sample.reshape(N_LAT_TOK, D_MODEL))
    conds = jnp.concatenate([c_ctx[None], c, jnp.zeros((COND_ROWS - N_COND, D_MODEL), F32)], axis=0)
    mods = _adaln(conds, w_ada, b_ada).reshape(DEPTH, COND_ROWS, 6, D_MODEL)
    cos, sin = _rope_tables()

    b_up_s = b_up.reshape(DEPTH, N_EXPERTS, 2 * D_FF // UP_GROUP, UP_GROUP // 2, 2)
    b_up_s = jnp.swapaxes(b_up_s, -1, -2).reshape(DEPTH, N_EXPERTS, 1, 2 * D_FF)
    b_down_s = b_down.reshape(DEPTH, N_EXPERTS, 1, D_MODEL)
    na_k_ctx = cache_na_k.reshape(DEC_BATCH, DEPTH, PAST_LEN, NA_DIM)
    na_v_ctx = cache_na_v.reshape(DEC_BATCH, DEPTH, PAST_LEN, NA_DIM)
    gqa_k_ctx = cache_gqa_k.reshape(DEC_BATCH, DEPTH, PAST_LEN, GQA_KV_DIM)
    gqa_v_ctx = cache_gqa_v.reshape(DEC_BATCH, DEPTH, PAST_LEN, GQA_KV_DIM)
    ssm_ctx = state_ssm.reshape(DEC_BATCH, DEPTH, 2, SSM_INNER, SSM_STATE)

    ctx_out = []
    for l in range(DEPTH):
        qkv, gqa, z, xbc, dt_raw = _inproj(l, x, mods[l], norm_mix[l][None], w_in)
        qn, kn = gqa_q_norm[l][None], gqa_k_norm[l][None]

        nao_c, go_c, gk_c = _ctx_attn(qkv, gqa, qn, kn)
        go_l = _lat_gqa(l, gqa, gqa_k_ctx, gqa_v_ctx, cos, sin, qn, kn)
        nao_l = _lat_na(l, qkv, na_k_ctx, na_v_ctx, _na_bias_tables(na_rpb[l]))

        xa = _conv_act(xbc, ssm_conv_w[l], ssm_conv_b[l][None])
        y_f, st_f, y_b, st_b = _ssd(l, xa, dt_raw, ssm_ctx, _pad_lanes(ssm_dt_bias[l]), _pad_lanes(ssm_a_log[l]))
        sts = [st[:BATCH].reshape(BATCH, SSM_HEADS, SSM_HEAD_DIM, SSM_STATE) for st in (st_f, st_b)]

        x_c, x_l, h, top_idx, gates, sel = _outproj(
            l, x, nao_c, nao_l, go_c, go_l, y_f, y_b, xa, z, jnp.repeat(ssm_d[l], SSM_HEAD_DIM)[None],
            ssm_norm[l][None], w_out, mods[l], norm_ffn[l][None], _pad_lanes(w_router[l]),
            _pad_lanes(b_router[l][None], NEG_INF))

        rank, cnt = _ranks(sel)
        plan = _moe_plan(rank, cnt)
        y_sorted = _experts(l, plan, _dispatch(plan, h, top_idx, rank), w_up, b_up_s[l], w_down, b_down_s[l])
        x = _combine(plan, (x_c, x_l), gates, top_idx, rank, mods[l], final_norm[None], y_sorted,
                     final=(l == DEPTH - 1))

        ctx_out.append((
            qkv[:N_CTX_TOK, NA_DIM:2 * NA_DIM].reshape(BATCH, SEQ, NA_HEADS, HEAD_DIM),
            qkv[:N_CTX_TOK, 2 * NA_DIM:].reshape(BATCH, SEQ, NA_HEADS, HEAD_DIM),
            gk_c.reshape(BATCH, SEQ, GQA_KV_HEADS, HEAD_DIM),
            gqa[:N_CTX_TOK, GQA_Q_DIM + GQA_KV_DIM:].reshape(BATCH, SEQ, GQA_KV_HEADS, HEAD_DIM),
            jnp.stack(sts, axis=1)))

    y_prompt = x[0].reshape(BATCH, SEQ, D_MODEL)
    y_sample = x[1].reshape(DEC_BATCH, DEC_SEQ, D_MODEL)
    return (y_prompt, y_sample) + tuple(jnp.stack([e[i] for e in ctx_out], axis=1) for i in range(5))
```

```python
import functools

import numpy as np
import jax
import jax.numpy as jnp
from jax import lax
from jax.experimental import pallas as pl
from jax.experimental.pallas import tpu as pltpu

F32 = jnp.float32
BF16 = jnp.bfloat16
I32 = jnp.int32

D_MODEL = 1024
BATCH = 16
SEQ = 256
DEPTH = 2
DEC_BATCH = 2
DEC_SEQ = 2048
PAST_LEN = 512
GRID_W = 64
HEAD_DIM = 64
NA_HEADS = 4
NA_WIN_ROWS = 8
NA_WIN_COLS = 16
GQA_HEADS = 4
GQA_KV_HEADS = 2
ROPE_THETA = 10000.0
SSM_HEADS = 8
SSM_HEAD_DIM = 64
SSM_STATE = 64
SSM_GROUPS = 2
SSM_INNER = SSM_HEADS * SSM_HEAD_DIM
SSM_BC_DIM = SSM_GROUPS * SSM_STATE
CONV_DIM = SSM_INNER + 2 * SSM_BC_DIM
CONV_W = 5
CHUNK = 128
NA_DIM = NA_HEADS * HEAD_DIM
GQA_Q_DIM = GQA_HEADS * HEAD_DIM
GQA_KV_DIM = GQA_KV_HEADS * HEAD_DIM
D_MIX = NA_DIM + GQA_Q_DIM + SSM_INNER
IN_DIM = 3 * NA_DIM + GQA_Q_DIM + 2 * GQA_KV_DIM + SSM_INNER + CONV_DIM + SSM_HEADS
N_EXPERTS = 32
TOP_K = 4
D_FF = D_MODEL
SWIGLU_LIMIT = 7.0
SWIGLU_ALPHA = 1.702
EPS = 1e-6
NEG_INF = -1e30

LANES = 128
N_CTX_TOK = BATCH * SEQ
N_LAT_TOK = DEC_BATCH * DEC_SEQ
N_TOK = N_CTX_TOK + N_LAT_TOK
N_COND = 1 + DEC_BATCH
COND_ROWS = 16
IN_PAD = 3 * NA_DIM + GQA_Q_DIM + 2 * GQA_KV_DIM + SSM_INNER + CONV_DIM + LANES
ROW_TILE = 256
N_ROW_TILES = N_TOK // ROW_TILE
MOE_BM = 256
N_SLOTS = N_TOK * TOP_K
RUN_ALIGN = 8
MOE_TILE = 512
N_MOE_TILES = N_TOK // MOE_TILE
N_RUNS = N_MOE_TILES * N_EXPERTS
LOCAL_ROWS = -(-(MOE_TILE * TOP_K + N_EXPERTS * (RUN_ALIGN - 1)) // MOE_BM) * MOE_BM
MOE_NB = -(-(N_SLOTS + N_RUNS * (RUN_ALIGN - 1)) // MOE_BM) + N_EXPERTS
UP_GROUP = 256
N_SEQ = BATCH + DEC_BATCH
N_CHUNKS = N_TOK // CHUNK
N_CTX_CHUNKS = N_CTX_TOK // CHUNK
VMEM_LIMIT = 56 * 1024 * 1024


def _cparams(*sem):
    return pltpu.CompilerParams(dimension_semantics=sem, vmem_limit_bytes=VMEM_LIMIT)


def _sigmoid(x):
    return 1.0 / (1.0 + jnp.exp(-x))


def _dot(a, b):
    return jnp.dot(a, b, preferred_element_type=F32)


def _dot_nt(a, b):
    return lax.dot_general(a, b, (((1,), (1,)), ((), ())), preferred_element_type=F32)


def _dot_exact(a, b):
    return jnp.dot(a, b, preferred_element_type=F32, precision=lax.Precision.HIGHEST)


def _rms(x, g):
    return x * lax.rsqrt(jnp.mean(x * x, axis=-1, keepdims=True) + EPS) * g


def _cond_of_tile(i, rows=ROW_TILE):
    ctx_tiles = N_CTX_TOK // rows
    return jnp.where(i < ctx_tiles, 0, 1 + (i - ctx_tiles) // (DEC_SEQ // rows))


def _adaln_kernel(c_ref, w_ref, b_ref, o_ref):
    c = c_ref[...]
    s = (c * _sigmoid(c)).astype(BF16)
    o_ref[0] = _dot(s, w_ref[0].astype(BF16)) + b_ref[0]


def _adaln(conds, w_ada, b_ada):
    tn = 1536
    return pl.pallas_call(
        _adaln_kernel,
        grid=(DEPTH, 6 * D_MODEL // tn),
        in_specs=[pl.BlockSpec((COND_ROWS, D_MODEL), lambda l, j: (0, 0)),
                  pl.BlockSpec((1, D_MODEL, tn), lambda l, j: (l, 0, j)),
                  pl.BlockSpec((1, 1, tn), lambda l, j: (l, 0, j))],
        out_specs=pl.BlockSpec((1, COND_ROWS, tn), lambda l, j: (l, 0, j)),
        out_shape=jax.ShapeDtypeStruct((DEPTH, COND_ROWS, 6 * D_MODEL), F32),
        compiler_params=_cparams("parallel", "parallel"),
        name="adaln",
    )(conds, w_ada, b_ada.reshape(DEPTH, 1, 6 * D_MODEL))


_IN_SPLITS = (3 * NA_DIM, GQA_Q_DIM + 2 * GQA_KV_DIM, SSM_INNER, CONV_DIM, LANES)


def _half_specs(rows, width):
    ctx_tiles = N_CTX_TOK // rows
    return [pl.BlockSpec((rows, width), lambda i, *_: (jnp.minimum(i, ctx_tiles - 1), 0)),
            pl.BlockSpec((rows, width), lambda i, *_: (jnp.maximum(i - ctx_tiles, 0), 0))]


def _read_halves(ctx_ref, lat_ref, rows):
    return jnp.where(pl.program_id(0) < N_CTX_TOK // rows, ctx_ref[...], lat_ref[...])


def _write_halves(ctx_ref, lat_ref, value, rows):
    @pl.when(pl.program_id(0) < N_CTX_TOK // rows)
    def _():
        ctx_ref[...] = value

    lat_ref[...] = value


_HALF_SHAPES = [jax.ShapeDtypeStruct((N_CTX_TOK, D_MODEL), F32), jax.ShapeDtypeStruct((N_LAT_TOK, D_MODEL), F32)]


def _inproj_kernel(xc_ref, xl_ref, m_ref, g_ref, w_ref, qkv_ref, gqa_ref, z_ref, xbc_ref, dt_ref, w_s):
    @pl.when(pl.program_id(0) == 0)
    def _():
        w_s[...] = jnp.zeros_like(w_s)
        w_s[:, 0:IN_DIM] = w_ref[0].astype(BF16)

    m = m_ref[0]
    h = _rms(_read_halves(xc_ref, xl_ref, ROW_TILE), g_ref[...]) * (1.0 + m[1:2]) + m[0:1]
    p = _dot(h.astype(BF16), w_s[...])
    off = 0
    for ref, width in zip((qkv_ref, gqa_ref, z_ref, xbc_ref, dt_ref), _IN_SPLITS):
        ref[...] = p[:, off:off + width]
        off += width


def _inproj(layer, x, mods, gain, w_in):
    row = lambda w: pl.BlockSpec((ROW_TILE, w), lambda i: (i, 0))
    return pl.pallas_call(
        _inproj_kernel,
        grid=(N_ROW_TILES,),
        in_specs=_half_specs(ROW_TILE, D_MODEL) + [
                  pl.BlockSpec((1, 6, D_MODEL), lambda i: (_cond_of_tile(i), 0, 0)),
                  pl.BlockSpec((1, D_MODEL), lambda i: (0, 0)),
                  pl.BlockSpec((1, D_MODEL, IN_DIM), lambda i: (layer, 0, 0))],
        out_specs=[row(w) for w in _IN_SPLITS],
        out_shape=[jax.ShapeDtypeStruct((N_TOK, w), F32) for w in _IN_SPLITS],
        scratch_shapes=[pltpu.VMEM((D_MODEL, IN_PAD), BF16)],
        compiler_params=_cparams("arbitrary"),
        name="inproj",
    )(*x, mods, gain, w_in)


def _softmax_pv(scores, values):
    m = scores[0].max(axis=-1, keepdims=True)
    for s in scores[1:]:
        m = jnp.maximum(m, s.max(axis=-1, keepdims=True))
    den = 0.0
    acc = 0.0
    for s, v in zip(scores, values):
        e = jnp.exp(s - m)
        den = den + e.sum(axis=-1, keepdims=True)
        acc = acc + _dot(e.astype(BF16), v)
    return acc / den


def _heads_rms(x, n_heads, g):
    return jnp.concatenate(
        [_rms(x[:, h * HEAD_DIM:(h + 1) * HEAD_DIM], g) for h in range(n_heads)], axis=-1)


def _rope(x, cos, sin_signed):
    w = x.shape[-1]
    lane = lax.broadcasted_iota(I32, x.shape, 1)
    partner = jnp.where((lane & 1) == 0, pltpu.roll(x, w - 1, 1), pltpu.roll(x, 1, 1))
    return x * cos + partner * sin_signed


_ATT_SCALE = HEAD_DIM ** -0.5


def _ctx_attn_kernel(qkv_ref, gqa_ref, qn_ref, kn_ref, nao_ref, go_ref, gk_ref):
    first = lax.broadcasted_iota(I32, (SEQ, 2 * HEAD_DIM), 1) < HEAD_DIM
    outs = []
    for ha in range(0, NA_HEADS, 2):
        c0 = ha * HEAD_DIM
        q = qkv_ref[:, c0:c0 + 2 * HEAD_DIM]
        k = qkv_ref[:, NA_DIM + c0:NA_DIM + c0 + 2 * HEAD_DIM].astype(BF16)
        v = qkv_ref[:, 2 * NA_DIM + c0:2 * NA_DIM + c0 + 2 * HEAD_DIM].astype(BF16)
        pair = [_softmax_pv([_dot_nt(jnp.where(own, q, 0.0).astype(BF16), k) * _ATT_SCALE], [v])
                for own in (first, jnp.logical_not(first))]
        outs.append(jnp.where(first, pair[0], pair[1]))
    nao_ref[...] = jnp.concatenate(outs, axis=-1).astype(BF16)

    gq = _heads_rms(gqa_ref[:, 0:GQA_Q_DIM], GQA_HEADS, qn_ref[...])
    gk = _heads_rms(gqa_ref[:, GQA_Q_DIM:GQA_Q_DIM + GQA_KV_DIM], GQA_KV_HEADS, kn_ref[...])
    gk_ref[...] = gk
    rep = GQA_HEADS // GQA_KV_HEADS
    outs = []
    for h in range(GQA_HEADS):
        g = h // rep
        q = gq[:, h * HEAD_DIM:(h + 1) * HEAD_DIM].astype(BF16)
        k = gk[:, g * HEAD_DIM:(g + 1) * HEAD_DIM].astype(BF16)
        v0 = GQA_Q_DIM + GQA_KV_DIM + g * HEAD_DIM
        v = gqa_ref[:, v0:v0 + HEAD_DIM].astype(BF16)
        outs.append(_softmax_pv([_dot_nt(q, k) * _ATT_SCALE], [v]))
    go_ref[...] = jnp.concatenate(outs, axis=-1).astype(BF16)


def _ctx_attn(qkv, gqa, q_norm, k_norm):
    return pl.pallas_call(
        _ctx_attn_kernel,
        grid=(BATCH,),
        in_specs=[pl.BlockSpec((SEQ, 3 * NA_DIM), lambda b: (b, 0)),
                  pl.BlockSpec((SEQ, GQA_Q_DIM + 2 * GQA_KV_DIM), lambda b: (b, 0)),
                  pl.BlockSpec((1, HEAD_DIM), lambda b: (0, 0)),
                  pl.BlockSpec((1, HEAD_DIM), lambda b: (0, 0))],
        out_specs=[pl.BlockSpec((SEQ, NA_DIM), lambda b: (b, 0)),
                   pl.BlockSpec((SEQ, GQA_Q_DIM), lambda b: (b, 0)),
                   pl.BlockSpec((SEQ, GQA_KV_DIM), lambda b: (b, 0))],
        out_shape=[jax.ShapeDtypeStruct((N_CTX_TOK, NA_DIM), BF16),
                   jax.ShapeDtypeStruct((N_CTX_TOK, GQA_Q_DIM), BF16),
                   jax.ShapeDtypeStruct((N_CTX_TOK, GQA_KV_DIM), F32)],
        compiler_params=_cparams("parallel"),
        name="ctx_attn",
    )(qkv, gqa, q_norm, k_norm)


GQA_TQ = 256
GQA_KEYS = PAST_LEN + DEC_SEQ


def _lat_gqa_kernel(gqa_ref, ck_ref, cv_ref, cos_ref, sin_ref, qn_ref, kn_ref, o_ref, kbuf, vbuf):
    qb = pl.program_id(1)

    @pl.when(qb == 0)
    def _():
        kbuf[0:PAST_LEN, :] = ck_ref[0, 0].astype(BF16)
        vbuf[0:PAST_LEN, :] = cv_ref[0, 0].astype(BF16)
        k = _heads_rms(gqa_ref[:, GQA_Q_DIM:GQA_Q_DIM + GQA_KV_DIM], GQA_KV_HEADS, kn_ref[...])
        k = _rope(k, cos_ref[:, 0:GQA_KV_DIM], sin_ref[:, 0:GQA_KV_DIM])
        kbuf[PAST_LEN:GQA_KEYS, :] = k.astype(BF16)
        vbuf[PAST_LEN:GQA_KEYS, :] = gqa_ref[:, GQA_Q_DIM + GQA_KV_DIM:].astype(BF16)

    r0 = pl.multiple_of(qb * GQA_TQ, GQA_TQ)
    q = _heads_rms(gqa_ref[pl.ds(r0, GQA_TQ), 0:GQA_Q_DIM], GQA_HEADS, qn_ref[...])
    q = _rope(q, cos_ref[pl.ds(r0, GQA_TQ), :], sin_ref[pl.ds(r0, GQA_TQ), :]).astype(BF16)
    rep = GQA_HEADS // GQA_KV_HEADS
    outs = []
    for h in range(GQA_HEADS):
        g = h // rep
        k = kbuf[:, g * HEAD_DIM:(g + 1) * HEAD_DIM]
        v = vbuf[:, g * HEAD_DIM:(g + 1) * HEAD_DIM]
        s = _dot_nt(q[:, h * HEAD_DIM:(h + 1) * HEAD_DIM], k) * _ATT_SCALE
        outs.append(_softmax_pv([s], [v]))
    o_ref[...] = jnp.concatenate(outs, axis=-1).astype(BF16)


def _lat_gqa(layer, gqa, cache_k, cache_v, cos, sin, q_norm, k_norm):
    lat_blk = N_CTX_TOK // DEC_SEQ
    return pl.pallas_call(
        _lat_gqa_kernel,
        grid=(DEC_BATCH, DEC_SEQ // GQA_TQ),
        in_specs=[pl.BlockSpec((DEC_SEQ, GQA_Q_DIM + 2 * GQA_KV_DIM), lambda b, q: (lat_blk + b, 0)),
                  pl.BlockSpec((1, 1, PAST_LEN, GQA_KV_DIM), lambda b, q: (b, layer, 0, 0)),
                  pl.BlockSpec((1, 1, PAST_LEN, GQA_KV_DIM), lambda b, q: (b, layer, 0, 0)),
                  pl.BlockSpec((DEC_SEQ, GQA_Q_DIM), lambda b, q: (0, 0)),
                  pl.BlockSpec((DEC_SEQ, GQA_Q_DIM), lambda b, q: (0, 0)),
                  pl.BlockSpec((1, HEAD_DIM), lambda b, q: (0, 0)),
                  pl.BlockSpec((1, HEAD_DIM), lambda b, q: (0, 0))],
        out_specs=pl.BlockSpec((GQA_TQ, GQA_Q_DIM), lambda b, q: (b * (DEC_SEQ // GQA_TQ) + q, 0)),
        out_shape=jax.ShapeDtypeStruct((N_LAT_TOK, GQA_Q_DIM), BF16),
        scratch_shapes=[pltpu.VMEM((GQA_KEYS, GQA_KV_DIM), BF16),
                        pltpu.VMEM((GQA_KEYS, GQA_KV_DIM), BF16)],
        compiler_params=_cparams("arbitrary", "arbitrary"),
        name="lat_gqa",
    )(gqa, cache_k, cache_v, cos, sin, q_norm, k_norm)


def _rope_tables():
    t = np.arange(DEC_SEQ)
    row = (t // GRID_W).astype(np.float64)
    col = (t % GRID_W).astype(np.float64)
    axis_dim = HEAD_DIM // 2
    inv_freq = ROPE_THETA ** (-np.arange(0, axis_dim, 2, dtype=np.float64) / axis_dim)
    ang = np.concatenate([row[:, None] * inv_freq, col[:, None] * inv_freq], axis=-1)
    cos = np.repeat(np.cos(ang), 2, axis=-1)
    sin = np.repeat(np.sin(ang), 2, axis=-1) * np.tile(np.array([-1.0, 1.0]), HEAD_DIM // 2)
    return (jnp.asarray(np.tile(cos, (1, GQA_HEADS)), F32), jnp.asarray(np.tile(sin, (1, GQA_HEADS)), F32))


NA_ROWS = DEC_SEQ // GRID_W
NA_KEYS = NA_WIN_ROWS * GRID_W


NA_ROWS_PER_STEP = 4


def _lat_na_kernel(qkv_ref, ck_ref, cv_ref, *rest):
    bias_refs, o_ref = rest[:NA_ROWS_PER_STEP], rest[NA_ROWS_PER_STEP]
    for j in range(NA_ROWS_PER_STEP):
        r = pl.program_id(1) * NA_ROWS_PER_STEP + j
        r0 = jnp.clip(r - NA_WIN_ROWS // 2, 0, NA_ROWS - NA_WIN_ROWS)
        q0 = pl.multiple_of(r * GRID_W, GRID_W)
        k0 = pl.multiple_of(r0 * GRID_W, GRID_W)
        first = lax.broadcasted_iota(I32, (GRID_W, 2 * HEAD_DIM), 1) < HEAD_DIM
        outs = []
        for ha in range(0, NA_HEADS, 2):
            c0 = ha * HEAD_DIM
            slab = slice(c0, c0 + 2 * HEAD_DIM)
            q = qkv_ref[pl.ds(q0, GRID_W), slab]
            k = qkv_ref[pl.ds(k0, NA_KEYS), NA_DIM + c0:NA_DIM + c0 + 2 * HEAD_DIM].astype(BF16)
            v = qkv_ref[pl.ds(k0, NA_KEYS), 2 * NA_DIM + c0:2 * NA_DIM + c0 + 2 * HEAD_DIM].astype(BF16)
            kc = ck_ref[0, 0, :, slab].astype(BF16)
            vc = cv_ref[0, 0, :, slab].astype(BF16)
            pair = []
            for h, own in ((ha, first), (ha + 1, jnp.logical_not(first))):
                qh = jnp.where(own, q, 0.0).astype(BF16)
                s_nb = _dot_nt(qh, k) * _ATT_SCALE + bias_refs[j][0, h]
                s_ctx = _dot_nt(qh, kc) * _ATT_SCALE
                pair.append(_softmax_pv([s_nb, s_ctx], [v, vc]))
            outs.append(jnp.where(first, pair[0], pair[1]))
        o_ref[j * GRID_W:(j + 1) * GRID_W, :] = jnp.concatenate(outs, axis=-1).astype(BF16)


def _na_row_offset(r):
    return r - jnp.clip(r - NA_WIN_ROWS // 2, 0, NA_ROWS - NA_WIN_ROWS)


def _lat_na(layer, qkv, cache_k, cache_v, bias):
    lat_blk = N_CTX_TOK // DEC_SEQ
    steps = NA_ROWS // NA_ROWS_PER_STEP
    bias_spec = lambda j: pl.BlockSpec((1, NA_HEADS, GRID_W, NA_KEYS),
                                       lambda b, s: (_na_row_offset(s * NA_ROWS_PER_STEP + j), 0, 0, 0))
    return pl.pallas_call(
        _lat_na_kernel,
        grid=(DEC_BATCH, steps),
        in_specs=[pl.BlockSpec((DEC_SEQ, 3 * NA_DIM), lambda b, s: (lat_blk + b, 0)),
                  pl.BlockSpec((1, 1, PAST_LEN, NA_DIM), lambda b, s: (b, layer, 0, 0)),
                  pl.BlockSpec((1, 1, PAST_LEN, NA_DIM), lambda b, s: (b, layer, 0, 0))]
                 + [bias_spec(j) for j in range(NA_ROWS_PER_STEP)],
        out_specs=pl.BlockSpec((NA_ROWS_PER_STEP * GRID_W, NA_DIM), lambda b, s: (b * steps + s, 0)),
        out_shape=jax.ShapeDtypeStruct((N_LAT_TOK, NA_DIM), BF16),
        compiler_params=_cparams("parallel", "arbitrary"),
        name="lat_na",
    )(qkv, cache_k, cache_v, *([bias] * NA_ROWS_PER_STEP))


def _na_bias_tables(rpb):
    d = np.arange(NA_WIN_ROWS)[:, None]
    kr = np.arange(NA_WIN_ROWS)[None, :]
    dr = kr - d + NA_WIN_ROWS - 1
    qc = np.arange(GRID_W)[:, None]
    kc = np.arange(GRID_W)[None, :]
    col0 = np.clip(qc - NA_WIN_COLS // 2, 0, GRID_W - NA_WIN_COLS)
    in_win = (kc >= col0) & (kc < col0 + NA_WIN_COLS)
    dc = np.clip(kc - qc + NA_WIN_COLS - 1, 0, 2 * NA_WIN_COLS - 2)
    row_hot = (dr[:, :, None] == np.arange(2 * NA_WIN_ROWS - 1)).astype(np.float32)
    col_hot = (dc[:, :, None] == np.arange(2 * NA_WIN_COLS - 1)).astype(np.float32)
    b = jnp.einsum('hac,dka,qxc->dhqkx', rpb.astype(F32), row_hot, col_hot, precision=lax.Precision.HIGHEST)
    b = jnp.where(in_win[None, None, :, None, :], b, NEG_INF)
    return b.reshape(NA_WIN_ROWS, NA_HEADS, GRID_W, NA_KEYS)


CONV_TB = 1024
CONV_HALO = 8
CONV_HALO_BLOCKS = CONV_TB // CONV_HALO


def _conv_kernel(prev_ref, x_ref, next_ref, w_ref, b_ref, o_ref):
    i = pl.program_id(0)
    seq = jnp.where(i < N_CTX_TOK // CONV_TB, SEQ, DEC_SEQ)
    x = x_ref[...]
    ext = jnp.concatenate([prev_ref[...], x, next_ref[...]], axis=0)
    n_ext = CONV_TB + 2 * CONV_HALO
    pos = (lax.broadcasted_iota(I32, (CONV_TB, 1), 0) + i * CONV_TB) & (seq - 1)
    half = CONV_W // 2
    acc = x * w_ref[half:half + 1, :]
    for s in range(-half, half + 1):
        if s == 0:
            continue
        shifted = pltpu.roll(ext, (-s) % n_ext, 0)[CONV_HALO:CONV_HALO + CONV_TB]
        valid = (pos + s >= 0) & (pos + s < seq)
        acc = acc + jnp.where(valid, shifted, 0.0) * w_ref[half + s:half + s + 1, :]
    acc = acc + b_ref[...]
    o_ref[...] = acc * _sigmoid(acc)


def _conv_act(xbc, conv_w, conv_b):
    return pl.pallas_call(
        _conv_kernel,
        grid=(N_TOK // CONV_TB,),
        in_specs=[pl.BlockSpec((CONV_HALO, CONV_DIM),
                               lambda i: (jnp.maximum(i * CONV_HALO_BLOCKS - 1, 0), 0)),
                  pl.BlockSpec((CONV_TB, CONV_DIM), lambda i: (i, 0)),
                  pl.BlockSpec((CONV_HALO, CONV_DIM),
                               lambda i: (jnp.minimum((i + 1) * CONV_HALO_BLOCKS, N_TOK // CONV_HALO - 1), 0)),
                  pl.BlockSpec((CONV_W, CONV_DIM), lambda i: (0, 0)),
                  pl.BlockSpec((1, CONV_DIM), lambda i: (0, 0))],
        out_specs=pl.BlockSpec((CONV_TB, CONV_DIM), lambda i: (i, 0)),
        out_shape=jax.ShapeDtypeStruct((N_TOK, CONV_DIM), F32),
        compiler_params=_cparams("parallel"),
        name="conv_act",
    )(xbc, xbc, xbc, conv_w, conv_b)


def _chunk_seq(g):
    ctx_n = SEQ // CHUNK
    lat_n = DEC_SEQ // CHUNK
    is_ctx = g < N_CTX_CHUNKS
    gl = g - N_CTX_CHUNKS
    sid = jnp.where(is_ctx, g // ctx_n, BATCH + gl // lat_n)
    cin = jnp.where(is_ctx, g % ctx_n, gl % lat_n)
    n = jnp.where(is_ctx, ctx_n, lat_n)
    return sid, cin, n


def _ssd_init(h0_ref, st_ref, gg, reverse):
    sid, cin, n = _chunk_seq(gg)

    @pl.when(cin == (n - 1 if reverse else 0))
    def _():
        st_ref[0] = jnp.where(sid >= BATCH, h0_ref[0, 0, 0], 0.0)


def _ssd_chunk(xa_ref, dt_ref, dtb, alog, y_ref, st_ref, reverse):
    x = dt_ref[...] + dtb
    dt = jnp.maximum(x, 0.0) + jnp.log1p(jnp.exp(-jnp.abs(x)))
    dta = dt * -jnp.exp(alog)
    ii = lax.broadcasted_iota(I32, (CHUNK, CHUNK), 0)
    jj = lax.broadcasted_iota(I32, (CHUNK, CHUNK), 1)
    tri = (jj >= ii) if reverse else (jj <= ii)
    cum = _dot_exact(tri.astype(F32), dta)
    cum_t = cum.T
    edge = 0 if reverse else CHUNK - 1
    tot = cum[edge:edge + 1, :]
    first = jj < SSM_HEAD_DIM
    first_rows = lax.broadcasted_iota(I32, (2 * SSM_HEAD_DIM, SSM_STATE), 0) < SSM_HEAD_DIM
    rep = SSM_HEADS // SSM_GROUPS
    for grp in range(SSM_GROUPS):
        bg = xa_ref[:, SSM_INNER + grp * SSM_STATE:SSM_INNER + (grp + 1) * SSM_STATE].astype(BF16)
        c0 = SSM_INNER + SSM_BC_DIM + grp * SSM_STATE
        cg = xa_ref[:, c0:c0 + SSM_STATE].astype(BF16)
        cb = _dot_nt(cg, bg)
        for ha in range(grp * rep, (grp + 1) * rep, 2):
            hb = ha + 1
            slab = slice(ha * SSM_HEAD_DIM, (hb + 1) * SSM_HEAD_DIM)
            col_a, col_b = cum[:, ha:ha + 1], cum[:, hb:hb + 1]
            tot_a, tot_b = tot[:, ha:ha + 1], tot[:, hb:hb + 1]
            decay_a = jnp.where(tri, jnp.exp(jnp.minimum(col_a - cum_t[ha:ha + 1, :], 0.0)), 0.0)
            decay_b = jnp.where(tri, jnp.exp(jnp.minimum(col_b - cum_t[hb:hb + 1, :], 0.0)), 0.0)
            xdt = xa_ref[:, slab] * jnp.where(first, dt[:, ha:ha + 1], dt[:, hb:hb + 1])
            xdt_b = xdt.astype(BF16)
            y = jnp.where(first, _dot((cb * decay_a).astype(BF16), xdt_b), _dot((cb * decay_b).astype(BF16), xdt_b))
            state = st_ref[0, slab, :]
            y = y + _dot_nt(cg, state.astype(BF16)) * jnp.where(first, jnp.exp(col_a), jnp.exp(col_b))
            y_ref[:, slab] = y
            to_end = jnp.where(first, jnp.exp(tot_a - col_a), jnp.exp(tot_b - col_b))
            upd = lax.dot_general((xdt * to_end).astype(BF16), bg, (((0,), (0,)), ((), ())),
                                  preferred_element_type=F32)
            st_ref[0, slab, :] = state * jnp.where(first_rows, jnp.exp(tot_a), jnp.exp(tot_b)) + upd


def _ssd_kernel(xaf_ref, dtf_ref, h0f_ref, xab_ref, dtb_ref, h0b_ref, bias_ref, alog_ref,
                yf_ref, stf_ref, yb_ref, stb_ref):
    g = pl.program_id(0)
    _ssd_init(h0f_ref, stf_ref, g, False)
    _ssd_chunk(xaf_ref, dtf_ref, bias_ref[0:1, :], alog_ref[0:1, :], yf_ref, stf_ref, False)
    _ssd_init(h0b_ref, stb_ref, N_CHUNKS - 1 - g, True)
    _ssd_chunk(xab_ref, dtb_ref, bias_ref[1:2, :], alog_ref[1:2, :], yb_ref, stb_ref, True)


def _ssd(layer, xa, dt_raw, h0, dt_bias, a_log):
    rev = lambda g: N_CHUNKS - 1 - g
    chunk = lambda w, order: pl.BlockSpec((CHUNK, w), lambda g: (order(g), 0))
    state = lambda order: pl.BlockSpec((1, SSM_INNER, SSM_STATE), lambda g: (_chunk_seq(order(g))[0], 0, 0))
    start = lambda order, d: pl.BlockSpec(
        (1, 1, 1, SSM_INNER, SSM_STATE),
        lambda g: (jnp.maximum(_chunk_seq(order(g))[0] - BATCH, 0), layer, d, 0, 0))
    same = lambda g: g
    y_shape = jax.ShapeDtypeStruct((N_TOK, SSM_INNER), F32)
    st_shape = jax.ShapeDtypeStruct((N_SEQ, SSM_INNER, SSM_STATE), F32)
    h0_fwd = h0_bwd = h0
    return pl.pallas_call(
        _ssd_kernel,
        grid=(N_CHUNKS,),
        in_specs=[chunk(CONV_DIM, same), chunk(LANES, same), start(same, 0),
                  chunk(CONV_DIM, rev), chunk(LANES, rev), start(rev, 1),
                  pl.BlockSpec((2, LANES), lambda g: (0, 0)),
                  pl.BlockSpec((2, LANES), lambda g: (0, 0))],
        out_specs=[chunk(SSM_INNER, same), state(same), chunk(SSM_INNER, rev), state(rev)],
        out_shape=[y_shape, st_shape, y_shape, st_shape],
        compiler_params=_cparams("arbitrary"),
        name="ssd",
    )(xa, dt_raw, h0_fwd, xa, dt_raw, h0_bwd, dt_bias, a_log)


def _outproj_kernel(xc_ref, xl_ref, nao_c_ref, nao_l_ref, go_c_ref, go_l_ref, yf_ref, yb_ref, xs_ref, z_ref, dsk_ref,
                    sn_ref, wo_ref, m_ref, g2_ref, wrh_ref, wrl_ref, br_ref, xoc_ref, xol_ref, h_ref, idx_ref,
                    gate_ref, sel_ref, wo_s):
    @pl.when(pl.program_id(0) == 0)
    def _():
        wo_s[...] = wo_ref[0].astype(BF16)

    nao = _read_halves(nao_c_ref, nao_l_ref, ROW_TILE)
    go = _read_halves(go_c_ref, go_l_ref, ROW_TILE)
    m = m_ref[0]
    z = z_ref[...]
    y = (yf_ref[...] + yb_ref[...] + xs_ref[...] * dsk_ref[...]) * (z * _sigmoid(z))
    s_o = _rms(y, sn_ref[...]).astype(BF16)
    mix = (_dot(nao, wo_s[0:NA_DIM, :])
           + _dot(go, wo_s[NA_DIM:NA_DIM + GQA_Q_DIM, :])
           + _dot(s_o, wo_s[NA_DIM + GQA_Q_DIM:, :]))
    x = _read_halves(xc_ref, xl_ref, ROW_TILE) + m[2:3] * mix
    _write_halves(xoc_ref, xol_ref, x, ROW_TILE)
    h = _rms(x, g2_ref[...]) * (1.0 + m[4:5]) + m[3:4]
    h_hi = h.astype(BF16)
    h_ref[...] = h_hi

    h_lo = (h - h_hi.astype(F32)).astype(BF16)
    w_hi = wrh_ref[...]
    logits = _dot(h_hi, w_hi) + _dot(h_lo, w_hi) + _dot(h_hi, wrl_ref[...]) + br_ref[...]
    lane = lax.broadcasted_iota(I32, logits.shape, 1).astype(F32)
    vals, idxs = [], []
    for _ in range(TOP_K):
        v = logits.max(axis=-1, keepdims=True)
        i = jnp.where(logits == v, lane, float(LANES)).min(axis=-1, keepdims=True)
        vals.append(v)
        idxs.append(i)
        logits = jnp.where(lane == i, -jnp.inf, logits)
    es = [jnp.exp(v - vals[0]) for v in vals]
    den = es[0] + es[1] + es[2] + es[3]
    idx_out = jnp.zeros(lane.shape, F32)
    gate_out = jnp.zeros(lane.shape, F32)
    sel = jnp.zeros(lane.shape, F32)
    for k in range(TOP_K):
        idx_out = jnp.where(lane == float(k), idxs[k], idx_out)
        gate_out = jnp.where(lane == float(k), es[k] / den, gate_out)
        sel = jnp.where(lane == idxs[k], 1.0, sel)
    idx_ref[...] = idx_out.astype(I32)
    gate_ref[...] = gate_out
    sel_ref[...] = sel.astype(BF16)


def _outproj(layer, x, nao_c, nao_l, go_c, go_l, yf, yb, xa, z, d_skip, ssm_norm, w_out, mods, gain2, w_router,
             b_router):
    row = lambda w: pl.BlockSpec((ROW_TILE, w), lambda i: (i, 0))
    full = lambda a, b: pl.BlockSpec((a, b), lambda i: (0, 0))
    wr_hi = w_router.astype(BF16)
    wr_lo = (w_router - wr_hi.astype(F32)).astype(BF16)
    return pl.pallas_call(
        _outproj_kernel,
        grid=(N_ROW_TILES,),
        in_specs=_half_specs(ROW_TILE, D_MODEL) + _half_specs(ROW_TILE, NA_DIM) + _half_specs(ROW_TILE, GQA_Q_DIM) + [
                  row(SSM_INNER), row(SSM_INNER),
                  row(SSM_INNER), row(SSM_INNER), full(1, SSM_INNER), full(1, SSM_INNER),
                  pl.BlockSpec((1, D_MIX, D_MODEL), lambda i: (layer, 0, 0)),
                  pl.BlockSpec((1, 6, D_MODEL), lambda i: (_cond_of_tile(i), 0, 0)),
                  full(1, D_MODEL), full(D_MODEL, LANES), full(D_MODEL, LANES), full(1, LANES)],
        out_specs=_half_specs(ROW_TILE, D_MODEL) + [row(D_MODEL), row(LANES), row(LANES), row(LANES)],
        out_shape=_HALF_SHAPES + [
                   jax.ShapeDtypeStruct((N_TOK, D_MODEL), BF16),
                   jax.ShapeDtypeStruct((N_TOK, LANES), I32),
                   jax.ShapeDtypeStruct((N_TOK, LANES), F32),
                   jax.ShapeDtypeStruct((N_TOK, LANES), BF16)],
        scratch_shapes=[pltpu.VMEM((D_MIX, D_MODEL), BF16)],
        compiler_params=_cparams("arbitrary"),
        name="outproj_router",
    )(*x, nao_c, nao_l, go_c, go_l, yf, yb, xa, z, d_skip, ssm_norm, w_out, mods, gain2, wr_hi, wr_lo, b_router)


RANK_TB = 512


def _rank_kernel(sel_ref, rank_ref, cnt_ref, carry):
    @pl.when(pl.program_id(0) == 0)
    def _():
        carry[...] = jnp.zeros_like(carry)

    sel = sel_ref[...]
    ii = lax.broadcasted_iota(I32, (RANK_TB, RANK_TB), 0)
    jj = lax.broadcasted_iota(I32, (RANK_TB, RANK_TB), 1)
    before = (jj < ii).astype(BF16)
    rank_ref[...] = _dot(before, sel) + carry[0:1, :]
    carry[...] = carry[...] + _dot(jnp.ones((8, RANK_TB), BF16), sel)
    cnt_ref[...] = carry[...]


def _ranks(sel):
    return pl.pallas_call(
        _rank_kernel,
        grid=(N_TOK // RANK_TB,),
        in_specs=[pl.BlockSpec((RANK_TB, LANES), lambda i: (i, 0))],
        out_specs=[pl.BlockSpec((RANK_TB, LANES), lambda i: (i, 0)),
                   pl.BlockSpec((8, LANES), lambda i: (0, 0))],
        out_shape=[jax.ShapeDtypeStruct((N_TOK, LANES), F32),
                   jax.ShapeDtypeStruct((8, LANES), F32)],
        scratch_shapes=[pltpu.VMEM((8, LANES), F32)],
        compiler_params=_cparams("arbitrary"),
        name="moe_ranks",
    )(sel)


_RUN_PIECES = (512, 256, 128, 64, 32, 16, 8)


def _run_dma(src, s0, dst, d0, n, sem, *, wait, fixed_src=False, pieces=_RUN_PIECES):
    for size in pieces:
        @pl.when((n & size) != 0)
        def _(size=size):
            done = n & ~(2 * size - 1)
            s = 0 if fixed_src else pl.multiple_of(s0 + done, RUN_ALIGN)
            d = pl.multiple_of(d0 + done, RUN_ALIGN)
            copy = pltpu.make_async_copy(src.at[pl.ds(s, size), :], dst.at[pl.ds(d, size), :], sem)
            if wait:
                copy.wait()
            else:
                copy.start()


_TILE_PIECES = (2048, 1024, 512, 256, 128, 64, 32, 16, 8)


def _wait_rows(buf, n, sem):
    for size in _TILE_PIECES:
        @pl.when((n & size) != 0)
        def _(size=size):
            pltpu.make_async_copy(buf.at[pl.ds(0, size), :], buf.at[pl.ds(0, size), :], sem).wait()


def _local_rows(idx_ref, rank_ref, base_ref):
    pos = rank_ref[...] + base_ref[0]
    lane = lax.broadcasted_iota(I32, pos.shape, 1)
    idx = idx_ref[...]
    return [jnp.sum(jnp.where(lane == idx[:, k:k + 1], pos, 0.0), axis=-1, keepdims=True).astype(I32)
            for k in range(TOP_K)]


def _dispatch_kernel(run_ref, loc_ref, glb_ref, tot_ref, fs_ref, fl_ref, nu_ref,
                     h_ref, idx_ref, rank_ref, base_ref, out_ref, xl_ref, zbuf, sems):
    i = pl.program_id(0)
    slot = i % 2
    xl = xl_ref.at[slot]
    sem = sems.at[slot]

    @pl.when(i == 0)
    def _():
        zbuf[...] = jnp.zeros_like(zbuf)
        fill_pieces = tuple(s for s in _RUN_PIECES if s < MOE_BM)
        for wait in (False, True):
            for e in range(N_EXPERTS):
                _run_dma(zbuf, 0, out_ref, fs_ref[e], fl_ref[e], sem, wait=wait, fixed_src=True, pieces=fill_pieces)

        def tail(b, carry):
            copy = pltpu.make_async_copy(zbuf, out_ref.at[pl.ds(pl.multiple_of(b * MOE_BM, MOE_BM), MOE_BM), :], sem)
            copy.start()
            copy.wait()
            return carry

        lax.fori_loop(nu_ref[0], MOE_NB, tail, 0)

    @pl.when(i >= 2)
    def _():
        _wait_rows(xl, tot_ref[i - 2], sem)

    rows = _local_rows(idx_ref, rank_ref, base_ref)
    p = lax.broadcasted_iota(I32, (MOE_TILE, LOCAL_ROWS), 1)
    hot = (p == rows[0])
    for k in range(1, TOP_K):
        hot = hot | (p == rows[k])
    xl[...] = lax.dot_general(hot.astype(BF16), h_ref[...].astype(BF16), (((0,), (0,)), ((), ())),
                              preferred_element_type=F32)
    for e in range(N_EXPERTS):
        j = i * N_EXPERTS + e
        _run_dma(xl, loc_ref[j], out_ref, glb_ref[j], run_ref[j], sem, wait=False)

    @pl.when(i == N_MOE_TILES - 1)
    def _():
        _wait_rows(xl, tot_ref[i], sem)
        _wait_rows(xl_ref.at[1 - slot], tot_ref[i - 1], sems.at[1 - slot])


def _dispatch(plan, h, idx, rank):
    tile = lambda w: pl.BlockSpec((MOE_TILE, w), lambda i, *_: (i, 0))
    return pl.pallas_call(
        _dispatch_kernel,
        grid_spec=pltpu.PrefetchScalarGridSpec(
            num_scalar_prefetch=7,
            grid=(N_MOE_TILES,),
            in_specs=[tile(D_MODEL), tile(LANES), tile(LANES),
                      pl.BlockSpec((1, 1, LANES), lambda i, *_: (i, 0, 0))],
            out_specs=pl.BlockSpec(memory_space=pl.ANY),
            scratch_shapes=[pltpu.VMEM((2, LOCAL_ROWS, D_MODEL), F32),
                            pltpu.VMEM((MOE_BM, D_MODEL), F32),
                            pltpu.SemaphoreType.DMA((2,))]),
        out_shape=jax.ShapeDtypeStruct((MOE_NB * MOE_BM, D_MODEL), F32),
        compiler_params=_cparams("arbitrary"),
        name="moe_dispatch",
    )(plan["run"], plan["local"], plan["global"], plan["total"], plan["fill_start"], plan["fill_len"],
      plan["n_used"], h, idx, rank, plan["base"])


def _expert_kernel(be_ref, nu_ref, new_ref, fe_ref, x_ref, wu_ref, bu_ref, wd_ref, bd_ref, y_ref, wu_s, wd_s):
    del fe_ref
    b = pl.program_id(0)
    used = b < nu_ref[0]

    @pl.when(jnp.logical_not(used))
    def _():
        y_ref[...] = jnp.zeros_like(y_ref)

    @pl.when(jnp.logical_and(used, new_ref[b] == 1))
    def _():
        r = lax.broadcasted_iota(I32, (UP_GROUP, UP_GROUP), 0)
        c = lax.broadcasted_iota(I32, (UP_GROUP, UP_GROUP), 1)
        src = jnp.where(c < UP_GROUP // 2, 2 * c, 2 * (c - UP_GROUP // 2) + 1)
        perm = (r == src).astype(BF16)
        for g in range(2 * D_FF // UP_GROUP):
            cols = slice(g * UP_GROUP, (g + 1) * UP_GROUP)
            wu_s[:, cols] = _dot(wu_ref[0, 0, :, cols].astype(BF16), perm).astype(BF16)
        wd_s[...] = wd_ref[0, 0].astype(BF16)

    @pl.when(used)
    def _():
        up = _dot(x_ref[...].astype(BF16), wu_s[...]) + bu_ref[0]
        half = UP_GROUP // 2
        acts = []
        for g in range(2 * D_FF // UP_GROUP):
            gate = jnp.minimum(up[:, g * UP_GROUP:g * UP_GROUP + half], SWIGLU_LIMIT)
            lin = jnp.clip(up[:, g * UP_GROUP + half:(g + 1) * UP_GROUP], -SWIGLU_LIMIT, SWIGLU_LIMIT)
            acts.append((gate * _sigmoid(SWIGLU_ALPHA * gate) * (lin + 1.0)).astype(BF16))
        y_ref[...] = _dot(jnp.concatenate(acts, axis=-1), wd_s[...]) + bd_ref[0]


def _experts(layer, plan, xs, w_up, b_up, w_down, b_down):
    blk = lambda b, nu: jnp.maximum(jnp.minimum(b, nu[0] - 1), 0)
    return pl.pallas_call(
        _expert_kernel,
        grid_spec=pltpu.PrefetchScalarGridSpec(
            num_scalar_prefetch=4,
            grid=(MOE_NB,),
            in_specs=[pl.BlockSpec((MOE_BM, D_MODEL), lambda b, be, nu, nw, fe: (blk(b, nu), 0)),
                      pl.BlockSpec((1, 1, D_MODEL, 2 * D_FF), lambda b, be, nu, nw, fe: (layer, fe[blk(b, nu)], 0, 0)),
                      pl.BlockSpec((1, 1, 2 * D_FF), lambda b, be, nu, nw, fe: (be[blk(b, nu)], 0, 0)),
                      pl.BlockSpec((1, 1, D_FF, D_MODEL), lambda b, be, nu, nw, fe: (layer, fe[blk(b, nu)], 0, 0)),
                      pl.BlockSpec((1, 1, D_MODEL), lambda b, be, nu, nw, fe: (be[blk(b, nu)], 0, 0))],
            out_specs=pl.BlockSpec((MOE_BM, D_MODEL), lambda b, be, nu, nw, fe: (b, 0)),
            scratch_shapes=[pltpu.VMEM((D_MODEL, 2 * D_FF), BF16),
                            pltpu.VMEM((D_FF, D_MODEL), BF16)]),
        out_shape=jax.ShapeDtypeStruct((MOE_NB * MOE_BM, D_MODEL), F32),
        compiler_params=_cparams("arbitrary"),
        name="moe_experts",
    )(plan["blk_expert"], plan["n_used"], plan["blk_new"], plan["blk_fetch"], xs, w_up, b_up, w_down, b_down)


def _combine_kernel(run_ref, loc_ref, glb_ref, tot_ref, xc_ref, xl_ref, gate_ref, idx_ref, rank_ref, base_ref, m_ref,
                    fn_ref, ys_ref, oc_ref, ol_ref, yl_ref, sems, *, final):
    i = pl.program_id(0)
    slot = i % 2

    def fetch(tile, buf):
        yl = yl_ref.at[buf]
        for e in range(N_EXPERTS):
            j = tile * N_EXPERTS + e
            _run_dma(ys_ref, glb_ref[j], yl, loc_ref[j], run_ref[j], sems.at[buf], wait=False)

        def clear(r, carry):
            yl[pl.ds(pl.multiple_of(r * RUN_ALIGN, RUN_ALIGN), RUN_ALIGN), :] = jnp.zeros((RUN_ALIGN, D_MODEL), F32)
            return carry

        lax.fori_loop(tot_ref[tile] // RUN_ALIGN, LOCAL_ROWS // RUN_ALIGN, clear, 0)

    @pl.when(i == 0)
    def _():
        fetch(0, 0)

    @pl.when(i + 1 < N_MOE_TILES)
    def _():
        fetch(i + 1, 1 - slot)

    rows = _local_rows(idx_ref, rank_ref, base_ref)
    gate = gate_ref[...]
    p = lax.broadcasted_iota(I32, (MOE_TILE, LOCAL_ROWS), 1)
    w = jnp.zeros((MOE_TILE, LOCAL_ROWS), F32)
    for k in range(TOP_K):
        w = jnp.where(p == rows[k], gate[:, k:k + 1], w)

    _wait_rows(yl_ref.at[slot], tot_ref[i], sems.at[slot])
    acc = _dot(w.astype(BF16), yl_ref[slot].astype(BF16))
    x = _read_halves(xc_ref, xl_ref, MOE_TILE) + m_ref[0][5:6] * acc
    _write_halves(oc_ref, ol_ref, _rms(x, fn_ref[...]) if final else x, MOE_TILE)


def _combine(plan, x, gates, idx, rank, mods, final_norm, ys, final):
    tile = lambda w: pl.BlockSpec((MOE_TILE, w), lambda i, *_: (i, 0))
    return pl.pallas_call(
        functools.partial(_combine_kernel, final=final),
        grid_spec=pltpu.PrefetchScalarGridSpec(
            num_scalar_prefetch=4,
            grid=(N_MOE_TILES,),
            in_specs=_half_specs(MOE_TILE, D_MODEL) + [tile(LANES), tile(LANES), tile(LANES),
                      pl.BlockSpec((1, 1, LANES), lambda i, *_: (i, 0, 0)),
                      pl.BlockSpec((1, 6, D_MODEL), lambda i, *_: (_cond_of_tile(i, MOE_TILE), 0, 0)),
                      pl.BlockSpec((1, D_MODEL), lambda i, *_: (0, 0)),
                      pl.BlockSpec(memory_space=pl.ANY)],
            out_specs=_half_specs(MOE_TILE, D_MODEL),
            scratch_shapes=[pltpu.VMEM((2, LOCAL_ROWS, D_MODEL), F32),
                            pltpu.SemaphoreType.DMA((2,))]),
        out_shape=_HALF_SHAPES,
        compiler_params=_cparams("arbitrary"),
        name="moe_combine",
    )(plan["run"], plan["local"], plan["global"], plan["total"], *x, gates, idx, rank, plan["base"], mods,
      final_norm, ys)


def _moe_plan(rank, cnt):
    first = rank[::MOE_TILE, :N_EXPERTS].astype(I32)
    total = cnt[0:1, :N_EXPERTS].astype(I32)
    run = jnp.concatenate([first[1:], total], axis=0) - first
    run = (run + RUN_ALIGN - 1) // RUN_ALIGN * RUN_ALIGN
    local = jnp.cumsum(run, axis=1) - run
    sizes = run.sum(axis=0)
    padded = (sizes + MOE_BM - 1) // MOE_BM * MOE_BM
    ends = jnp.cumsum(padded)
    starts = ends - padded
    glob = starts[None, :] + jnp.cumsum(run, axis=0) - run
    blk_start = jnp.arange(MOE_NB, dtype=I32) * MOE_BM
    blk_expert = jnp.minimum(jnp.sum(ends[None, :] <= blk_start[:, None], axis=1), N_EXPERTS - 1).astype(I32)
    blk_new = jnp.concatenate([jnp.ones((1,), I32), (blk_expert[1:] != blk_expert[:-1]).astype(I32)])
    blk = jnp.arange(MOE_NB, dtype=I32)
    later_start = (blk_new[None, :] == 1) & (blk[None, :] > blk[:, None])
    next_start = jnp.min(jnp.where(later_start, blk[None, :], MOE_NB), axis=1)
    next_hot = (blk[None, :] == next_start[:, None]).astype(I32)
    next_expert = jnp.where(next_start < MOE_NB, jnp.sum(next_hot * blk_expert[None, :], axis=1), blk_expert)
    blk_fetch = jnp.where(blk_new == 1, blk_expert, next_expert)
    base = _pad_lanes((local - first).astype(F32)).reshape(N_MOE_TILES, 1, LANES)
    return {"run": run.reshape(-1), "local": local.reshape(-1), "global": glob.reshape(-1), "total": run.sum(axis=1),
            "fill_start": starts + sizes, "fill_len": padded - sizes, "n_used": ends[-1:] // MOE_BM,
            "blk_expert": blk_expert, "blk_new": blk_new, "blk_fetch": blk_fetch, "base": base}


def _pad_lanes(v, fill=0.0):
    return jnp.pad(v, ((0, 0), (0, LANES - v.shape[-1])), constant_values=fill)


def kernel(x_prompt, x_sample, cache_na_k, cache_na_v, cache_gqa_k, cache_gqa_v, state_ssm, c, c_ctx, w_ada, b_ada, norm_mix, norm_ffn, w_in, na_rpb, gqa_q_norm, gqa_k_norm, ssm_conv_w, ssm_conv_b, ssm_dt_bias, ssm_a_log, ssm_d, ssm_norm, w_out, w_router, b_router, w_up, b_up, w_down, b_down, final_norm):
    x = (x_prompt.reshape(N_CTX_TOK, D_MODEL), x_sample.reshape(N_LAT_TOK, D_MODEL))
    conds = jnp.concatenate([c_ctx[None], c, jnp.zeros((COND_ROWS - N_COND, D_MODEL), F32)], axis=0)
    mods = _adaln(conds, w_ada, b_ada).reshape(DEPTH, COND_ROWS, 6, D_MODEL)
    cos, sin = _rope_tables()

    b_up_s = b_up.reshape(DEPTH, N_EXPERTS, 2 * D_FF // UP_GROUP, UP_GROUP // 2, 2)
    b_up_s = jnp.swapaxes(b_up_s, -1, -2).reshape(DEPTH, N_EXPERTS, 1, 2 * D_FF)
    b_down_s = b_down.reshape(DEPTH, N_EXPERTS, 1, D_MODEL)
    na_k_ctx = cache_na_k.reshape(DEC_BATCH, DEPTH, PAST_LEN, NA_DIM)
    na_v_ctx = cache_na_v.reshape(DEC_BATCH, DEPTH, PAST_LEN, NA_DIM)
    gqa_k_ctx = cache_gqa_k.reshape(DEC_BATCH, DEPTH, PAST_LEN, GQA_KV_DIM)
    gqa_v_ctx = cache_gqa_v.reshape(DEC_BATCH, DEPTH, PAST_LEN, GQA_KV_DIM)
    ssm_ctx = state_ssm.reshape(DEC_BATCH, DEPTH, 2, SSM_INNER, SSM_STATE)

    ctx_out = []
    for l in range(DEPTH):
        qkv, gqa, z, xbc, dt_raw = _inproj(l, x, mods[l], norm_mix[l][None], w_in)
        qn, kn = gqa_q_norm[l][None], gqa_k_norm[l][None]

        nao_c, go_c, gk_c = _ctx_attn(qkv, gqa, qn, kn)
        go_l = _lat_gqa(l, gqa, gqa_k_ctx, gqa_v_ctx, cos, sin, qn, kn)
        nao_l = _lat_na(l, qkv, na_k_ctx, na_v_ctx, _na_bias_tables(na_rpb[l]))

        xa = _conv_act(xbc, ssm_conv_w[l], ssm_conv_b[l][None])
        y_f, st_f, y_b, st_b = _ssd(l, xa, dt_raw, ssm_ctx, _pad_lanes(ssm_dt_bias[l]), _pad_lanes(ssm_a_log[l]))
        sts = [st[:BATCH].reshape(BATCH, SSM_HEADS, SSM_HEAD_DIM, SSM_STATE) for st in (st_f, st_b)]

        x_c, x_l, h, top_idx, gates, sel = _outproj(
            l, x, nao_c, nao_l, go_c, go_l, y_f, y_b, xa, z, jnp.repeat(ssm_d[l], SSM_HEAD_DIM)[None],
            ssm_norm[l][None], w_out, mods[l], norm_ffn[l][None], _pad_lanes(w_router[l]),
            _pad_lanes(b_router[l][None], NEG_INF))

        rank, cnt = _ranks(sel)
        plan = _moe_plan(rank, cnt)
        y_sorted = _experts(l, plan, _dispatch(plan, h, top_idx, rank), w_up, b_up_s[l], w_down, b_down_s[l])
        x = _combine(plan, (x_c, x_l), gates, top_idx, rank, mods[l], final_norm[None], y_sorted,
                     final=(l == DEPTH - 1))

        ctx_out.append((
            qkv[:N_CTX_TOK, NA_DIM:2 * NA_DIM].reshape(BATCH, SEQ, NA_HEADS, HEAD_DIM),
            qkv[:N_CTX_TOK, 2 * NA_DIM:].reshape(BATCH, SEQ, NA_HEADS, HEAD_DIM),
            gk_c.reshape(BATCH, SEQ, GQA_KV_HEADS, HEAD_DIM),
            gqa[:N_CTX_TOK, GQA_Q_DIM + GQA_KV_DIM:].reshape(BATCH, SEQ, GQA_KV_HEADS, HEAD_DIM),
            jnp.stack(sts, axis=1)))

    y_prompt = x[0].reshape(BATCH, SEQ, D_MODEL)
    y_sample = x[1].reshape(DEC_BATCH, DEC_SEQ, D_MODEL)
    return (y_prompt, y_sample) + tuple(jnp.stack([e[i] for e in ctx_out], axis=1) for i in range(5))
```

```python
import functools

import numpy as np
import jax
import jax.numpy as jnp
from jax import lax
from jax.experimental import pallas as pl
from jax.experimental.pallas import tpu as pltpu

F32 = jnp.float32
BF16 = jnp.bfloat16
I32 = jnp.int32

D_MODEL = 1024
BATCH = 16
SEQ = 256
DEPTH = 2
DEC_BATCH = 2
DEC_SEQ = 2048
PAST_LEN = 512
GRID_W = 64
HEAD_DIM = 64
NA_HEADS = 4
NA_WIN_ROWS = 8
NA_WIN_COLS = 16
GQA_HEADS = 4
GQA_KV_HEADS = 2
ROPE_THETA = 10000.0
SSM_HEADS = 8
SSM_HEAD_DIM = 64
SSM_STATE = 64
SSM_GROUPS = 2
SSM_INNER = SSM_HEADS * SSM_HEAD_DIM
SSM_BC_DIM = SSM_GROUPS * SSM_STATE
CONV_DIM = SSM_INNER + 2 * SSM_BC_DIM
CONV_W = 5
CHUNK = 128
NA_DIM = NA_HEADS * HEAD_DIM
GQA_Q_DIM = GQA_HEADS * HEAD_DIM
GQA_KV_DIM = GQA_KV_HEADS * HEAD_DIM
D_MIX = NA_DIM + GQA_Q_DIM + SSM_INNER
IN_DIM = 3 * NA_DIM + GQA_Q_DIM + 2 * GQA_KV_DIM + SSM_INNER + CONV_DIM + SSM_HEADS
N_EXPERTS = 32
TOP_K = 4
D_FF = D_MODEL
SWIGLU_LIMIT = 7.0
SWIGLU_ALPHA = 1.702
EPS = 1e-6
NEG_INF = -1e30

LANES = 128
N_CTX_TOK = BATCH * SEQ
N_LAT_TOK = DEC_BATCH * DEC_SEQ
N_TOK = N_CTX_TOK + N_LAT_TOK
N_COND = 1 + DEC_BATCH
COND_ROWS = 16
IN_PAD = 3 * NA_DIM + GQA_Q_DIM + 2 * GQA_KV_DIM + SSM_INNER + CONV_DIM + LANES
ROW_TILE = 256
N_ROW_TILES = N_TOK // ROW_TILE
MOE_BM = 256
N_SLOTS = N_TOK * TOP_K
RUN_ALIGN = 8
MOE_TILE = 512
N_MOE_TILES = N_TOK // MOE_TILE
N_RUNS = N_MOE_TILES * N_EXPERTS
LOCAL_ROWS = -(-(MOE_TILE * TOP_K + N_EXPERTS * (RUN_ALIGN - 1)) // MOE_BM) * MOE_BM
MOE_NB = -(-(N_SLOTS + N_RUNS * (RUN_ALIGN - 1)) // MOE_BM) + N_EXPERTS
UP_GROUP = 256
N_SEQ = BATCH + DEC_BATCH
N_CHUNKS = N_TOK // CHUNK
N_CTX_CHUNKS = N_CTX_TOK // CHUNK
VMEM_LIMIT = 56 * 1024 * 1024


def _cparams(*sem):
    return pltpu.CompilerParams(dimension_semantics=sem, vmem_limit_bytes=VMEM_LIMIT)


def _sigmoid(x):
    return 1.0 / (1.0 + jnp.exp(-x))


def _dot(a, b):
    return jnp.dot(a, b, preferred_element_type=F32)


def _dot_nt(a, b):
    return lax.dot_general(a, b, (((1,), (1,)), ((), ())), preferred_element_type=F32)


def _dot_exact(a, b):
    return jnp.dot(a, b, preferred_element_type=F32, precision=lax.Precision.HIGHEST)


def _rms(x, g):
    return x * lax.rsqrt(jnp.mean(x * x, axis=-1, keepdims=True) + EPS) * g


def _cond_of_tile(i, rows=ROW_TILE):
    ctx_tiles = N_CTX_TOK // rows
    return jnp.where(i < ctx_tiles, 0, 1 + (i - ctx_tiles) // (DEC_SEQ // rows))


def _adaln_kernel(c_ref, w_ref, b_ref, o_ref):
    c = c_ref[...]
    s = (c * _sigmoid(c)).astype(BF16)
    o_ref[0] = _dot(s, w_ref[0].astype(BF16)) + b_ref[0]


def _adaln(conds, w_ada, b_ada):
    tn = 1536
    return pl.pallas_call(
        _adaln_kernel,
        grid=(DEPTH, 6 * D_MODEL // tn),
        in_specs=[pl.BlockSpec((COND_ROWS, D_MODEL), lambda l, j: (0, 0)),
                  pl.BlockSpec((1, D_MODEL, tn), lambda l, j: (l, 0, j)),
                  pl.BlockSpec((1, 1, tn), lambda l, j: (l, 0, j))],
        out_specs=pl.BlockSpec((1, COND_ROWS, tn), lambda l, j: (l, 0, j)),
        out_shape=jax.ShapeDtypeStruct((DEPTH, COND_ROWS, 6 * D_MODEL), F32),
        compiler_params=_cparams("parallel", "parallel"),
        name="adaln",
    )(conds, w_ada, b_ada.reshape(DEPTH, 1, 6 * D_MODEL))


_IN_SPLITS = (3 * NA_DIM, GQA_Q_DIM + 2 * GQA_KV_DIM, SSM_INNER, CONV_DIM, LANES)


def _half_specs(rows, width):
    ctx_tiles = N_CTX_TOK // rows
    return [pl.BlockSpec((rows, width), lambda i, *_: (jnp.minimum(i, ctx_tiles - 1), 0)),
            pl.BlockSpec((rows, width), lambda i, *_: (jnp.maximum(i - ctx_tiles, 0), 0))]


def _read_halves(ctx_ref, lat_ref, rows):
    return jnp.where(pl.program_id(0) < N_CTX_TOK // rows, ctx_ref[...], lat_ref[...])


def _write_halves(ctx_ref, lat_ref, value, rows):
    @pl.when(pl.program_id(0) < N_CTX_TOK // rows)
    def _():
        ctx_ref[...] = value

    lat_ref[...] = value


_HALF_SHAPES = [jax.ShapeDtypeStruct((N_CTX_TOK, D_MODEL), F32), jax.ShapeDtypeStruct((N_LAT_TOK, D_MODEL), F32)]


def _inproj_kernel(xc_ref, xl_ref, m_ref, g_ref, w_ref, qkv_ref, gqa_ref, z_ref, xbc_ref, dt_ref, w_s):
    @pl.when(pl.program_id(0) == 0)
    def _():
        w_s[...] = jnp.zeros_like(w_s)
        w_s[:, 0:IN_DIM] = w_ref[0].astype(BF16)

    m = m_ref[0]
    h = _rms(_read_halves(xc_ref, xl_ref, ROW_TILE), g_ref[...]) * (1.0 + m[1:2]) + m[0:1]
    p = _dot(h.astype(BF16), w_s[...])
    off = 0
    for ref, width in zip((qkv_ref, gqa_ref, z_ref, xbc_ref, dt_ref), _IN_SPLITS):
        ref[...] = p[:, off:off + width]
        off += width


def _inproj(layer, x, mods, gain, w_in):
    row = lambda w: pl.BlockSpec((ROW_TILE, w), lambda i: (i, 0))
    return pl.pallas_call(
        _inproj_kernel,
        grid=(N_ROW_TILES,),
        in_specs=_half_specs(ROW_TILE, D_MODEL) + [
                  pl.BlockSpec((1, 6, D_MODEL), lambda i: (_cond_of_tile(i), 0, 0)),
                  pl.BlockSpec((1, D_MODEL), lambda i: (0, 0)),
                  pl.BlockSpec((1, D_MODEL, IN_DIM), lambda i: (layer, 0, 0))],
        out_specs=[row(w) for w in _IN_SPLITS],
        out_shape=[jax.ShapeDtypeStruct((N_TOK, w), F32) for w in _IN_SPLITS],
        scratch_shapes=[pltpu.VMEM((D_MODEL, IN_PAD), BF16)],
        compiler_params=_cparams("arbitrary"),
        name="inproj",
    )(*x, mods, gain, w_in)


def _softmax_pv(scores, values):
    m = scores[0].max(axis=-1, keepdims=True)
    for s in scores[1:]:
        m = jnp.maximum(m, s.max(axis=-1, keepdims=True))
    den = 0.0
    acc = 0.0
    for s, v in zip(scores, values):
        e = jnp.exp(s - m)
        den = den + e.sum(axis=-1, keepdims=True)
        acc = acc + _dot(e.astype(BF16), v)
    return acc / den


def _heads_rms(x, n_heads, g):
    return jnp.concatenate(
        [_rms(x[:, h * HEAD_DIM:(h + 1) * HEAD_DIM], g) for h in range(n_heads)], axis=-1)


def _rope(x, cos, sin_signed):
    w = x.shape[-1]
    lane = lax.broadcasted_iota(I32, x.shape, 1)
    partner = jnp.where((lane & 1) == 0, pltpu.roll(x, w - 1, 1), pltpu.roll(x, 1, 1))
    return x * cos + partner * sin_signed


_ATT_SCALE = HEAD_DIM ** -0.5


def _ctx_attn_kernel(qkv_ref, gqa_ref, qn_ref, kn_ref, nao_ref, go_ref, gk_ref):
    first = lax.broadcasted_iota(I32, (SEQ, 2 * HEAD_DIM), 1) < HEAD_DIM
    outs = []
    for ha in range(0, NA_HEADS, 2):
        c0 = ha * HEAD_DIM
        q = qkv_ref[:, c0:c0 + 2 * HEAD_DIM] * _ATT_SCALE
        k = qkv_ref[:, NA_DIM + c0:NA_DIM + c0 + 2 * HEAD_DIM].astype(BF16)
        v = qkv_ref[:, 2 * NA_DIM + c0:2 * NA_DIM + c0 + 2 * HEAD_DIM].astype(BF16)
        pair = [_softmax_pv([_dot_nt(jnp.where(own, q, 0.0).astype(BF16), k)], [v])
                for own in (first, jnp.logical_not(first))]
        outs.append(jnp.where(first, pair[0], pair[1]))
    nao_ref[...] = jnp.concatenate(outs, axis=-1).astype(BF16)

    gq = _heads_rms(gqa_ref[:, 0:GQA_Q_DIM], GQA_HEADS, qn_ref[...])
    gk = _heads_rms(gqa_ref[:, GQA_Q_DIM:GQA_Q_DIM + GQA_KV_DIM], GQA_KV_HEADS, kn_ref[...])
    gk_ref[...] = gk
    rep = GQA_HEADS // GQA_KV_HEADS
    outs = []
    for h in range(GQA_HEADS):
        g = h // rep
        q = (gq[:, h * HEAD_DIM:(h + 1) * HEAD_DIM] * _ATT_SCALE).astype(BF16)
        k = gk[:, g * HEAD_DIM:(g + 1) * HEAD_DIM].astype(BF16)
        v0 = GQA_Q_DIM + GQA_KV_DIM + g * HEAD_DIM
        v = gqa_ref[:, v0:v0 + HEAD_DIM].astype(BF16)
        outs.append(_softmax_pv([_dot_nt(q, k)], [v]))
    go_ref[...] = jnp.concatenate(outs, axis=-1).astype(BF16)


def _ctx_attn(qkv, gqa, q_norm, k_norm):
    return pl.pallas_call(
        _ctx_attn_kernel,
        grid=(BATCH,),
        in_specs=[pl.BlockSpec((SEQ, 3 * NA_DIM), lambda b: (b, 0)),
                  pl.BlockSpec((SEQ, GQA_Q_DIM + 2 * GQA_KV_DIM), lambda b: (b, 0)),
                  pl.BlockSpec((1, HEAD_DIM), lambda b: (0, 0)),
                  pl.BlockSpec((1, HEAD_DIM), lambda b: (0, 0))],
        out_specs=[pl.BlockSpec((SEQ, NA_DIM), lambda b: (b, 0)),
                   pl.BlockSpec((SEQ, GQA_Q_DIM), lambda b: (b, 0)),
                   pl.BlockSpec((SEQ, GQA_KV_DIM), lambda b: (b, 0))],
        out_shape=[jax.ShapeDtypeStruct((N_CTX_TOK, NA_DIM), BF16),
                   jax.ShapeDtypeStruct((N_CTX_TOK, GQA_Q_DIM), BF16),
                   jax.ShapeDtypeStruct((N_CTX_TOK, GQA_KV_DIM), F32)],
        compiler_params=_cparams("parallel"),
        name="ctx_attn",
    )(qkv, gqa, q_norm, k_norm)


GQA_TQ = 256
GQA_KEYS = PAST_LEN + DEC_SEQ


def _lat_gqa_kernel(gqa_ref, ck_ref, cv_ref, cos_ref, sin_ref, qn_ref, kn_ref, o_ref, kbuf, vbuf):
    qb = pl.program_id(1)

    @pl.when(qb == 0)
    def _():
        kbuf[0:PAST_LEN, :] = ck_ref[0, 0].astype(BF16)
        vbuf[0:PAST_LEN, :] = cv_ref[0, 0].astype(BF16)
        k = _heads_rms(gqa_ref[:, GQA_Q_DIM:GQA_Q_DIM + GQA_KV_DIM], GQA_KV_HEADS, kn_ref[...])
        k = _rope(k, cos_ref[:, 0:GQA_KV_DIM], sin_ref[:, 0:GQA_KV_DIM])
        kbuf[PAST_LEN:GQA_KEYS, :] = k.astype(BF16)
        vbuf[PAST_LEN:GQA_KEYS, :] = gqa_ref[:, GQA_Q_DIM + GQA_KV_DIM:].astype(BF16)

    r0 = pl.multiple_of(qb * GQA_TQ, GQA_TQ)
    q = _heads_rms(gqa_ref[pl.ds(r0, GQA_TQ), 0:GQA_Q_DIM], GQA_HEADS, qn_ref[...])
    q = (_rope(q, cos_ref[pl.ds(r0, GQA_TQ), :], sin_ref[pl.ds(r0, GQA_TQ), :]) * _ATT_SCALE).astype(BF16)
    rep = GQA_HEADS // GQA_KV_HEADS
    outs = []
    for h in range(GQA_HEADS):
        g = h // rep
        k = kbuf[:, g * HEAD_DIM:(g + 1) * HEAD_DIM]
        v = vbuf[:, g * HEAD_DIM:(g + 1) * HEAD_DIM]
        s = _dot_nt(q[:, h * HEAD_DIM:(h + 1) * HEAD_DIM], k)
        outs.append(_softmax_pv([s], [v]))
    o_ref[...] = jnp.concatenate(outs, axis=-1).astype(BF16)


def _lat_gqa(layer, gqa, cache_k, cache_v, cos, sin, q_norm, k_norm):
    lat_blk = N_CTX_TOK // DEC_SEQ
    return pl.pallas_call(
        _lat_gqa_kernel,
        grid=(DEC_BATCH, DEC_SEQ // GQA_TQ),
        in_specs=[pl.BlockSpec((DEC_SEQ, GQA_Q_DIM + 2 * GQA_KV_DIM), lambda b, q: (lat_blk + b, 0)),
                  pl.BlockSpec((1, 1, PAST_LEN, GQA_KV_DIM), lambda b, q: (b, layer, 0, 0)),
                  pl.BlockSpec((1, 1, PAST_LEN, GQA_KV_DIM), lambda b, q: (b, layer, 0, 0)),
                  pl.BlockSpec((DEC_SEQ, GQA_Q_DIM), lambda b, q: (0, 0)),
                  pl.BlockSpec((DEC_SEQ, GQA_Q_DIM), lambda b, q: (0, 0)),
                  pl.BlockSpec((1, HEAD_DIM), lambda b, q: (0, 0)),
                  pl.BlockSpec((1, HEAD_DIM), lambda b, q: (0, 0))],
        out_specs=pl.BlockSpec((GQA_TQ, GQA_Q_DIM), lambda b, q: (b * (DEC_SEQ // GQA_TQ) + q, 0)),
        out_shape=jax.ShapeDtypeStruct((N_LAT_TOK, GQA_Q_DIM), BF16),
        scratch_shapes=[pltpu.VMEM((GQA_KEYS, GQA_KV_DIM), BF16),
                        pltpu.VMEM((GQA_KEYS, GQA_KV_DIM), BF16)],
        compiler_params=_cparams("arbitrary", "arbitrary"),
        name="lat_gqa",
    )(gqa, cache_k, cache_v, cos, sin, q_norm, k_norm)


def _rope_tables():
    t = np.arange(DEC_SEQ)
    row = (t // GRID_W).astype(np.float64)
    col = (t % GRID_W).astype(np.float64)
    axis_dim = HEAD_DIM // 2
    inv_freq = ROPE_THETA ** (-np.arange(0, axis_dim, 2, dtype=np.float64) / axis_dim)
    ang = np.concatenate([row[:, None] * inv_freq, col[:, None] * inv_freq], axis=-1)
    cos = np.repeat(np.cos(ang), 2, axis=-1)
    sin = np.repeat(np.sin(ang), 2, axis=-1) * np.tile(np.array([-1.0, 1.0]), HEAD_DIM // 2)
    return (jnp.asarray(np.tile(cos, (1, GQA_HEADS)), F32), jnp.asarray(np.tile(sin, (1, GQA_HEADS)), F32))


NA_ROWS = DEC_SEQ // GRID_W
NA_KEYS = NA_WIN_ROWS * GRID_W


NA_ROWS_PER_STEP = 4


def _lat_na_kernel(qkv_ref, ck_ref, cv_ref, *rest):
    bias_refs, o_ref = rest[:NA_ROWS_PER_STEP], rest[NA_ROWS_PER_STEP]
    for j in range(NA_ROWS_PER_STEP):
        r = pl.program_id(1) * NA_ROWS_PER_STEP + j
        r0 = jnp.clip(r - NA_WIN_ROWS // 2, 0, NA_ROWS - NA_WIN_ROWS)
        q0 = pl.multiple_of(r * GRID_W, GRID_W)
        k0 = pl.multiple_of(r0 * GRID_W, GRID_W)
        first = lax.broadcasted_iota(I32, (GRID_W, 2 * HEAD_DIM), 1) < HEAD_DIM
        outs = []
        for ha in range(0, NA_HEADS, 2):
            c0 = ha * HEAD_DIM
            slab = slice(c0, c0 + 2 * HEAD_DIM)
            q = qkv_ref[pl.ds(q0, GRID_W), slab] * _ATT_SCALE
            k = qkv_ref[pl.ds(k0, NA_KEYS), NA_DIM + c0:NA_DIM + c0 + 2 * HEAD_DIM].astype(BF16)
            v = qkv_ref[pl.ds(k0, NA_KEYS), 2 * NA_DIM + c0:2 * NA_DIM + c0 + 2 * HEAD_DIM].astype(BF16)
            kc = ck_ref[0, 0, :, slab].astype(BF16)
            vc = cv_ref[0, 0, :, slab].astype(BF16)
            pair = []
            for h, own in ((ha, first), (ha + 1, jnp.logical_not(first))):
                qh = jnp.where(own, q, 0.0).astype(BF16)
                s_nb = _dot_nt(qh, k) + bias_refs[j][0, h]
                s_ctx = _dot_nt(qh, kc)
                pair.append(_softmax_pv([s_nb, s_ctx], [v, vc]))
            outs.append(jnp.where(first, pair[0], pair[1]))
        o_ref[j * GRID_W:(j + 1) * GRID_W, :] = jnp.concatenate(outs, axis=-1).astype(BF16)


def _na_row_offset(r):
    return r - jnp.clip(r - NA_WIN_ROWS // 2, 0, NA_ROWS - NA_WIN_ROWS)


def _lat_na(layer, qkv, cache_k, cache_v, bias):
    lat_blk = N_CTX_TOK // DEC_SEQ
    steps = NA_ROWS // NA_ROWS_PER_STEP
    bias_spec = lambda j: pl.BlockSpec((1, NA_HEADS, GRID_W, NA_KEYS),
                                       lambda b, s: (_na_row_offset(s * NA_ROWS_PER_STEP + j), 0, 0, 0))
    return pl.pallas_call(
        _lat_na_kernel,
        grid=(DEC_BATCH, steps),
        in_specs=[pl.BlockSpec((DEC_SEQ, 3 * NA_DIM), lambda b, s: (lat_blk + b, 0)),
                  pl.BlockSpec((1, 1, PAST_LEN, NA_DIM), lambda b, s: (b, layer, 0, 0)),
                  pl.BlockSpec((1, 1, PAST_LEN, NA_DIM), lambda b, s: (b, layer, 0, 0))]
                 + [bias_spec(j) for j in range(NA_ROWS_PER_STEP)],
        out_specs=pl.BlockSpec((NA_ROWS_PER_STEP * GRID_W, NA_DIM), lambda b, s: (b * steps + s, 0)),
        out_shape=jax.ShapeDtypeStruct((N_LAT_TOK, NA_DIM), BF16),
        compiler_params=_cparams("parallel", "arbitrary"),
        name="lat_na",
    )(qkv, cache_k, cache_v, *([bias] * NA_ROWS_PER_STEP))


def _na_bias_tables(rpb):
    d = np.arange(NA_WIN_ROWS)[:, None]
    kr = np.arange(NA_WIN_ROWS)[None, :]
    dr = kr - d + NA_WIN_ROWS - 1
    qc = np.arange(GRID_W)[:, None]
    kc = np.arange(GRID_W)[None, :]
    col0 = np.clip(qc - NA_WIN_COLS // 2, 0, GRID_W - NA_WIN_COLS)
    in_win = (kc >= col0) & (kc < col0 + NA_WIN_COLS)
    dc = np.clip(kc - qc + NA_WIN_COLS - 1, 0, 2 * NA_WIN_COLS - 2)
    row_hot = (dr[:, :, None] == np.arange(2 * NA_WIN_ROWS - 1)).astype(np.float32)
    col_hot = (dc[:, :, None] == np.arange(2 * NA_WIN_COLS - 1)).astype(np.float32)
    b = jnp.einsum('hac,dka,qxc->dhqkx', rpb.astype(F32), row_hot, col_hot, precision=lax.Precision.HIGHEST)
    b = jnp.where(in_win[None, None, :, None, :], b, NEG_INF)
    return b.reshape(NA_WIN_ROWS, NA_HEADS, GRID_W, NA_KEYS)


CONV_TB = 1024
CONV_HALO = 8
CONV_HALO_BLOCKS = CONV_TB // CONV_HALO


def _conv_kernel(prev_ref, x_ref, next_ref, w_ref, b_ref, o_ref):
    i = pl.program_id(0)
    seq = jnp.where(i < N_CTX_TOK // CONV_TB, SEQ, DEC_SEQ)
    x = x_ref[...]
    ext = jnp.concatenate([prev_ref[...], x, next_ref[...]], axis=0)
    n_ext = CONV_TB + 2 * CONV_HALO
    pos = (lax.broadcasted_iota(I32, (CONV_TB, 1), 0) + i * CONV_TB) & (seq - 1)
    half = CONV_W // 2
    acc = x * w_ref[half:half + 1, :]
    for s in range(-half, half + 1):
        if s == 0:
            continue
        shifted = pltpu.roll(ext, (-s) % n_ext, 0)[CONV_HALO:CONV_HALO + CONV_TB]
        valid = (pos + s >= 0) & (pos + s < seq)
        acc = acc + jnp.where(valid, shifted, 0.0) * w_ref[half + s:half + s + 1, :]
    acc = acc + b_ref[...]
    o_ref[...] = acc * _sigmoid(acc)


def _conv_act(xbc, conv_w, conv_b):
    return pl.pallas_call(
        _conv_kernel,
        grid=(N_TOK // CONV_TB,),
        in_specs=[pl.BlockSpec((CONV_HALO, CONV_DIM),
                               lambda i: (jnp.maximum(i * CONV_HALO_BLOCKS - 1, 0), 0)),
                  pl.BlockSpec((CONV_TB, CONV_DIM), lambda i: (i, 0)),
                  pl.BlockSpec((CONV_HALO, CONV_DIM),
                               lambda i: (jnp.minimum((i + 1) * CONV_HALO_BLOCKS, N_TOK // CONV_HALO - 1), 0)),
                  pl.BlockSpec((CONV_W, CONV_DIM), lambda i: (0, 0)),
                  pl.BlockSpec((1, CONV_DIM), lambda i: (0, 0))],
        out_specs=pl.BlockSpec((CONV_TB, CONV_DIM), lambda i: (i, 0)),
        out_shape=jax.ShapeDtypeStruct((N_TOK, CONV_DIM), F32),
        compiler_params=_cparams("parallel"),
        name="conv_act",
    )(xbc, xbc, xbc, conv_w, conv_b)


def _chunk_seq(g):
    ctx_n = SEQ // CHUNK
    lat_n = DEC_SEQ // CHUNK
    is_ctx = g < N_CTX_CHUNKS
    gl = g - N_CTX_CHUNKS
    sid = jnp.where(is_ctx, g // ctx_n, BATCH + gl // lat_n)
    cin = jnp.where(is_ctx, g % ctx_n, gl % lat_n)
    n = jnp.where(is_ctx, ctx_n, lat_n)
    return sid, cin, n


def _ssd_init(h0_ref, st_ref, gg, reverse):
    sid, cin, n = _chunk_seq(gg)

    @pl.when(cin == (n - 1 if reverse else 0))
    def _():
        st_ref[0] = jnp.where(sid >= BATCH, h0_ref[0, 0, 0], 0.0)


def _ssd_chunk(xa_ref, dt_ref, dtb, alog, y_ref, st_ref, reverse):
    x = dt_ref[...] + dtb
    dt = jnp.maximum(x, 0.0) + jnp.log1p(jnp.exp(-jnp.abs(x)))
    dta = dt * -jnp.exp(alog)
    ii = lax.broadcasted_iota(I32, (CHUNK, CHUNK), 0)
    jj = lax.broadcasted_iota(I32, (CHUNK, CHUNK), 1)
    tri = (jj >= ii) if reverse else (jj <= ii)
    cum = _dot_exact(tri.astype(F32), dta)
    cum_t = cum.T
    edge = 0 if reverse else CHUNK - 1
    tot = cum[edge:edge + 1, :]
    first = jj < SSM_HEAD_DIM
    first_rows = lax.broadcasted_iota(I32, (2 * SSM_HEAD_DIM, SSM_STATE), 0) < SSM_HEAD_DIM
    rep = SSM_HEADS // SSM_GROUPS
    for grp in range(SSM_GROUPS):
        bg = xa_ref[:, SSM_INNER + grp * SSM_STATE:SSM_INNER + (grp + 1) * SSM_STATE].astype(BF16)
        c0 = SSM_INNER + SSM_BC_DIM + grp * SSM_STATE
        cg = xa_ref[:, c0:c0 + SSM_STATE].astype(BF16)
        cb = _dot_nt(cg, bg)
        for ha in range(grp * rep, (grp + 1) * rep, 2):
            hb = ha + 1
            slab = slice(ha * SSM_HEAD_DIM, (hb + 1) * SSM_HEAD_DIM)
            col_a, col_b = cum[:, ha:ha + 1], cum[:, hb:hb + 1]
            tot_a, tot_b = tot[:, ha:ha + 1], tot[:, hb:hb + 1]
            decay_a = jnp.where(tri, jnp.exp(jnp.minimum(col_a - cum_t[ha:ha + 1, :], 0.0)), 0.0)
            decay_b = jnp.where(tri, jnp.exp(jnp.minimum(col_b - cum_t[hb:hb + 1, :], 0.0)), 0.0)
            xdt = xa_ref[:, slab] * jnp.where(first, dt[:, ha:ha + 1], dt[:, hb:hb + 1])
            xdt_b = xdt.astype(BF16)
            y = jnp.where(first, _dot((cb * decay_a).astype(BF16), xdt_b), _dot((cb * decay_b).astype(BF16), xdt_b))
            state = st_ref[0, slab, :]
            y = y + _dot_nt(cg, state.astype(BF16)) * jnp.where(first, jnp.exp(col_a), jnp.exp(col_b))
            y_ref[:, slab] = y
            to_end = jnp.where(first, jnp.exp(tot_a - col_a), jnp.exp(tot_b - col_b))
            upd = lax.dot_general((xdt * to_end).astype(BF16), bg, (((0,), (0,)), ((), ())),
                                  preferred_element_type=F32)
            st_ref[0, slab, :] = state * jnp.where(first_rows, jnp.exp(tot_a), jnp.exp(tot_b)) + upd


def _ssd_kernel(xaf_ref, dtf_ref, h0f_ref, xab_ref, dtb_ref, h0b_ref, bias_ref, alog_ref,
                yf_ref, stf_ref, yb_ref, stb_ref):
    g = pl.program_id(0)
    _ssd_init(h0f_ref, stf_ref, g, False)
    _ssd_chunk(xaf_ref, dtf_ref, bias_ref[0:1, :], alog_ref[0:1, :], yf_ref, stf_ref, False)
    _ssd_init(h0b_ref, stb_ref, N_CHUNKS - 1 - g, True)
    _ssd_chunk(xab_ref, dtb_ref, bias_ref[1:2, :], alog_ref[1:2, :], yb_ref, stb_ref, True)


def _ssd(layer, xa, dt_raw, h0, dt_bias, a_log):
    rev = lambda g: N_CHUNKS - 1 - g
    chunk = lambda w, order: pl.BlockSpec((CHUNK, w), lambda g: (order(g), 0))
    state = lambda order: pl.BlockSpec((1, SSM_INNER, SSM_STATE), lambda g: (_chunk_seq(order(g))[0], 0, 0))
    start = lambda order, d: pl.BlockSpec(
        (1, 1, 1, SSM_INNER, SSM_STATE),
        lambda g: (jnp.maximum(_chunk_seq(order(g))[0] - BATCH, 0), layer, d, 0, 0))
    same = lambda g: g
    y_shape = jax.ShapeDtypeStruct((N_TOK, SSM_INNER), F32)
    st_shape = jax.ShapeDtypeStruct((N_SEQ, SSM_INNER, SSM_STATE), F32)
    h0_fwd = h0_bwd = h0
    return pl.pallas_call(
        _ssd_kernel,
        grid=(N_CHUNKS,),
        in_specs=[chunk(CONV_DIM, same), chunk(LANES, same), start(same, 0),
                  chunk(CONV_DIM, rev), chunk(LANES, rev), start(rev, 1),
                  pl.BlockSpec((2, LANES), lambda g: (0, 0)),
                  pl.BlockSpec((2, LANES), lambda g: (0, 0))],
        out_specs=[chunk(SSM_INNER, same), state(same), chunk(SSM_INNER, rev), state(rev)],
        out_shape=[y_shape, st_shape, y_shape, st_shape],
        compiler_params=_cparams("arbitrary"),
        name="ssd",
    )(xa, dt_raw, h0_fwd, xa, dt_raw, h0_bwd, dt_bias, a_log)


def _outproj_kernel(xc_ref, xl_ref, nao_c_ref, nao_l_ref, go_c_ref, go_l_ref, yf_ref, yb_ref, xs_ref, z_ref, dsk_ref,
                    sn_ref, wo_ref, m_ref, g2_ref, wrh_ref, wrl_ref, br_ref, xoc_ref, xol_ref, h_ref, idx_ref,
                    gate_ref, sel_ref, wo_s):
    @pl.when(pl.program_id(0) == 0)
    def _():
        wo_s[...] = wo_ref[0].astype(BF16)

    nao = _read_halves(nao_c_ref, nao_l_ref, ROW_TILE)
    go = _read_halves(go_c_ref, go_l_ref, ROW_TILE)
    m = m_ref[0]
    z = z_ref[...]
    y = (yf_ref[...] + yb_ref[...] + xs_ref[...] * dsk_ref[...]) * (z * _sigmoid(z))
    s_o = _rms(y, sn_ref[...]).astype(BF16)
    mix = (_dot(nao, wo_s[0:NA_DIM, :])
           + _dot(go, wo_s[NA_DIM:NA_DIM + GQA_Q_DIM, :])
           + _dot(s_o, wo_s[NA_DIM + GQA_Q_DIM:, :]))
    x = _read_halves(xc_ref, xl_ref, ROW_TILE) + m[2:3] * mix
    _write_halves(xoc_ref, xol_ref, x, ROW_TILE)
    h = _rms(x, g2_ref[...]) * (1.0 + m[4:5]) + m[3:4]
    h_hi = h.astype(BF16)
    h_ref[...] = h_hi

    h_lo = (h - h_hi.astype(F32)).astype(BF16)
    w_hi = wrh_ref[...]
    logits = _dot(h_hi, w_hi) + _dot(h_lo, w_hi) + _dot(h_hi, wrl_ref[...]) + br_ref[...]
    lane = lax.broadcasted_iota(I32, logits.shape, 1).astype(F32)
    vals, idxs = [], []
    for _ in range(TOP_K):
        v = logits.max(axis=-1, keepdims=True)
        i = jnp.where(logits == v, lane, float(LANES)).min(axis=-1, keepdims=True)
        vals.append(v)
        idxs.append(i)
        logits = jnp.where(lane == i, -jnp.inf, logits)
    es = [jnp.exp(v - vals[0]) for v in vals]
    den = es[0] + es[1] + es[2] + es[3]
    idx_out = jnp.zeros(lane.shape, F32)
    gate_out = jnp.zeros(lane.shape, F32)
    sel = jnp.zeros(lane.shape, F32)
    for k in range(TOP_K):
        idx_out = jnp.where(lane == float(k), idxs[k], idx_out)
        gate_out = jnp.where(lane == float(k), es[k] / den, gate_out)
        sel = jnp.where(lane == idxs[k], 1.0, sel)
    idx_ref[...] = idx_out.astype(I32)
    gate_ref[...] = gate_out
    sel_ref[...] = sel.astype(BF16)


def _outproj(layer, x, nao_c, nao_l, go_c, go_l, yf, yb, xa, z, d_skip, ssm_norm, w_out, mods, gain2, w_router,
             b_router):
    row = lambda w: pl.BlockSpec((ROW_TILE, w), lambda i: (i, 0))
    full = lambda a, b: pl.BlockSpec((a, b), lambda i: (0, 0))
    wr_hi = w_router.astype(BF16)
    wr_lo = (w_router - wr_hi.astype(F32)).astype(BF16)
    return pl.pallas_call(
        _outproj_kernel,
        grid=(N_ROW_TILES,),
        in_specs=_half_specs(ROW_TILE, D_MODEL) + _half_specs(ROW_TILE, NA_DIM) + _half_specs(ROW_TILE, GQA_Q_DIM) + [
                  row(SSM_INNER), row(SSM_INNER),
                  row(SSM_INNER), row(SSM_INNER), full(1, SSM_INNER), full(1, SSM_INNER),
                  pl.BlockSpec((1, D_MIX, D_MODEL), lambda i: (layer, 0, 0)),
                  pl.BlockSpec((1, 6, D_MODEL), lambda i: (_cond_of_tile(i), 0, 0)),
                  full(1, D_MODEL), full(D_MODEL, LANES), full(D_MODEL, LANES), full(1, LANES)],
        out_specs=_half_specs(ROW_TILE, D_MODEL) + [row(D_MODEL), row(LANES), row(LANES), row(LANES)],
        out_shape=_HALF_SHAPES + [
                   jax.ShapeDtypeStruct((N_TOK, D_MODEL), BF16),
                   jax.ShapeDtypeStruct((N_TOK, LANES), I32),
                   jax.ShapeDtypeStruct((N_TOK, LANES), F32),
                   jax.ShapeDtypeStruct((N_TOK, LANES), BF16)],
        scratch_shapes=[pltpu.VMEM((D_MIX, D_MODEL), BF16)],
        compiler_params=_cparams("arbitrary"),
        name="outproj_router",
    )(*x, nao_c, nao_l, go_c, go_l, yf, yb, xa, z, d_skip, ssm_norm, w_out, mods, gain2, wr_hi, wr_lo, b_router)


RANK_TB = 512


def _rank_kernel(sel_ref, rank_ref, cnt_ref, carry):
    @pl.when(pl.program_id(0) == 0)
    def _():
        carry[...] = jnp.zeros_like(carry)

    sel = sel_ref[...]
    ii = lax.broadcasted_iota(I32, (RANK_TB, RANK_TB), 0)
    jj = lax.broadcasted_iota(I32, (RANK_TB, RANK_TB), 1)
    before = (jj < ii).astype(BF16)
    rank_ref[...] = _dot(before, sel) + carry[0:1, :]
    carry[...] = carry[...] + _dot(jnp.ones((8, RANK_TB), BF16), sel)
    cnt_ref[...] = carry[...]


def _ranks(sel):
    return pl.pallas_call(
        _rank_kernel,
        grid=(N_TOK // RANK_TB,),
        in_specs=[pl.BlockSpec((RANK_TB, LANES), lambda i: (i, 0))],
        out_specs=[pl.BlockSpec((RANK_TB, LANES), lambda i: (i, 0)),
                   pl.BlockSpec((8, LANES), lambda i: (0, 0))],
        out_shape=[jax.ShapeDtypeStruct((N_TOK, LANES), F32),
                   jax.ShapeDtypeStruct((8, LANES), F32)],
        scratch_shapes=[pltpu.VMEM((8, LANES), F32)],
        compiler_params=_cparams("arbitrary"),
        name="moe_ranks",
    )(sel)


_RUN_PIECES = (512, 256, 128, 64, 32, 16, 8)


def _run_dma(src, s0, dst, d0, n, sem, *, wait, fixed_src=False, pieces=_RUN_PIECES):
    for size in pieces:
        @pl.when((n & size) != 0)
        def _(size=size):
            done = n & ~(2 * size - 1)
            s = 0 if fixed_src else pl.multiple_of(s0 + done, RUN_ALIGN)
            d = pl.multiple_of(d0 + done, RUN_ALIGN)
            copy = pltpu.make_async_copy(src.at[pl.ds(s, size), :], dst.at[pl.ds(d, size), :], sem)
            if wait:
                copy.wait()
            else:
                copy.start()


_TILE_PIECES = (2048, 1024, 512, 256, 128, 64, 32, 16, 8)


def _wait_rows(buf, n, sem):
    for size in _TILE_PIECES:
        @pl.when((n & size) != 0)
        def _(size=size):
            pltpu.make_async_copy(buf.at[pl.ds(0, size), :], buf.at[pl.ds(0, size), :], sem).wait()


def _local_rows(idx_ref, rank_ref, base_ref):
    pos = rank_ref[...] + base_ref[0]
    lane = lax.broadcasted_iota(I32, pos.shape, 1)
    idx = idx_ref[...]
    return [jnp.sum(jnp.where(lane == idx[:, k:k + 1], pos, 0.0), axis=-1, keepdims=True).astype(I32)
            for k in range(TOP_K)]


def _dispatch_kernel(run_ref, loc_ref, glb_ref, tot_ref, fs_ref, fl_ref, nu_ref,
                     h_ref, idx_ref, rank_ref, base_ref, out_ref, xl_ref, zbuf, sems):
    i = pl.program_id(0)
    slot = i % 2
    xl = xl_ref.at[slot]
    sem = sems.at[slot]

    @pl.when(i == 0)
    def _():
        zbuf[...] = jnp.zeros_like(zbuf)
        fill_pieces = tuple(s for s in _RUN_PIECES if s < MOE_BM)
        for wait in (False, True):
            for e in range(N_EXPERTS):
                _run_dma(zbuf, 0, out_ref, fs_ref[e], fl_ref[e], sem, wait=wait, fixed_src=True, pieces=fill_pieces)

        def tail(b, carry):
            copy = pltpu.make_async_copy(zbuf, out_ref.at[pl.ds(pl.multiple_of(b * MOE_BM, MOE_BM), MOE_BM), :], sem)
            copy.start()
            copy.wait()
            return carry

        lax.fori_loop(nu_ref[0], MOE_NB, tail, 0)

    @pl.when(i >= 2)
    def _():
        _wait_rows(xl, tot_ref[i - 2], sem)

    rows = _local_rows(idx_ref, rank_ref, base_ref)
    p = lax.broadcasted_iota(I32, (MOE_TILE, LOCAL_ROWS), 1)
    hot = (p == rows[0])
    for k in range(1, TOP_K):
        hot = hot | (p == rows[k])
    xl[...] = lax.dot_general(hot.astype(BF16), h_ref[...].astype(BF16), (((0,), (0,)), ((), ())),
                              preferred_element_type=F32)
    for e in range(N_EXPERTS):
        j = i * N_EXPERTS + e
        _run_dma(xl, loc_ref[j], out_ref, glb_ref[j], run_ref[j], sem, wait=False)

    @pl.when(i == N_MOE_TILES - 1)
    def _():
        _wait_rows(xl, tot_ref[i], sem)
        _wait_rows(xl_ref.at[1 - slot], tot_ref[i - 1], sems.at[1 - slot])


def _dispatch(plan, h, idx, rank):
    tile = lambda w: pl.BlockSpec((MOE_TILE, w), lambda i, *_: (i, 0))
    return pl.pallas_call(
        _dispatch_kernel,
        grid_spec=pltpu.PrefetchScalarGridSpec(
            num_scalar_prefetch=7,
            grid=(N_MOE_TILES,),
            in_specs=[tile(D_MODEL), tile(LANES), tile(LANES),
                      pl.BlockSpec((1, 1, LANES), lambda i, *_: (i, 0, 0))],
            out_specs=pl.BlockSpec(memory_space=pl.ANY),
            scratch_shapes=[pltpu.VMEM((2, LOCAL_ROWS, D_MODEL), F32),
                            pltpu.VMEM((MOE_BM, D_MODEL), F32),
                            pltpu.SemaphoreType.DMA((2,))]),
        out_shape=jax.ShapeDtypeStruct((MOE_NB * MOE_BM, D_MODEL), F32),
        compiler_params=_cparams("arbitrary"),
        name="moe_dispatch",
    )(plan["run"], plan["local"], plan["global"], plan["total"], plan["fill_start"], plan["fill_len"],
      plan["n_used"], h, idx, rank, plan["base"])


def _expert_kernel(be_ref, nu_ref, new_ref, fe_ref, x_ref, wu_ref, bu_ref, wd_ref, bd_ref, y_ref, wu_s, wd_s):
    del fe_ref
    b = pl.program_id(0)
    used = b < nu_ref[0]

    @pl.when(jnp.logical_not(used))
    def _():
        y_ref[...] = jnp.zeros_like(y_ref)

    @pl.when(jnp.logical_and(used, new_ref[b] == 1))
    def _():
        r = lax.broadcasted_iota(I32, (UP_GROUP, UP_GROUP), 0)
        c = lax.broadcasted_iota(I32, (UP_GROUP, UP_GROUP), 1)
        src = jnp.where(c < UP_GROUP // 2, 2 * c, 2 * (c - UP_GROUP // 2) + 1)
        perm = (r == src).astype(BF16)
        for g in range(2 * D_FF // UP_GROUP):
            cols = slice(g * UP_GROUP, (g + 1) * UP_GROUP)
            wu_s[:, cols] = _dot(wu_ref[0, 0, :, cols].astype(BF16), perm).astype(BF16)
        wd_s[...] = wd_ref[0, 0].astype(BF16)

    @pl.when(used)
    def _():
        up = _dot(x_ref[...].astype(BF16), wu_s[...]) + bu_ref[0]
        half = UP_GROUP // 2
        acts = []
        for g in range(2 * D_FF // UP_GROUP):
            gate = jnp.minimum(up[:, g * UP_GROUP:g * UP_GROUP + half], SWIGLU_LIMIT)
            lin = jnp.clip(up[:, g * UP_GROUP + half:(g + 1) * UP_GROUP], -SWIGLU_LIMIT, SWIGLU_LIMIT)
            acts.append((gate * _sigmoid(SWIGLU_ALPHA * gate) * (lin + 1.0)).astype(BF16))
        y_ref[...] = _dot(jnp.concatenate(acts, axis=-1), wd_s[...]) + bd_ref[0]


def _experts(layer, plan, xs, w_up, b_up, w_down, b_down):
    blk = lambda b, nu: jnp.maximum(jnp.minimum(b, nu[0] - 1), 0)
    return pl.pallas_call(
        _expert_kernel,
        grid_spec=pltpu.PrefetchScalarGridSpec(
            num_scalar_prefetch=4,
            grid=(MOE_NB,),
            in_specs=[pl.BlockSpec((MOE_BM, D_MODEL), lambda b, be, nu, nw, fe: (blk(b, nu), 0)),
                      pl.BlockSpec((1, 1, D_MODEL, 2 * D_FF), lambda b, be, nu, nw, fe: (layer, fe[blk(b, nu)], 0, 0)),
                      pl.BlockSpec((1, 1, 2 * D_FF), lambda b, be, nu, nw, fe: (be[blk(b, nu)], 0, 0)),
                      pl.BlockSpec((1, 1, D_FF, D_MODEL), lambda b, be, nu, nw, fe: (layer, fe[blk(b, nu)], 0, 0)),
                      pl.BlockSpec((1, 1, D_MODEL), lambda b, be, nu, nw, fe: (be[blk(b, nu)], 0, 0))],
            out_specs=pl.BlockSpec((MOE_BM, D_MODEL), lambda b, be, nu, nw, fe: (b, 0)),
            scratch_shapes=[pltpu.VMEM((D_MODEL, 2 * D_FF), BF16),
                            pltpu.VMEM((D_FF, D_MODEL), BF16)]),
        out_shape=jax.ShapeDtypeStruct((MOE_NB * MOE_BM, D_MODEL), F32),
        compiler_params=_cparams("arbitrary"),
        name="moe_experts",
    )(plan["blk_expert"], plan["n_used"], plan["blk_new"], plan["blk_fetch"], xs, w_up, b_up, w_down, b_down)


def _combine_kernel(run_ref, loc_ref, glb_ref, tot_ref, xc_ref, xl_ref, gate_ref, idx_ref, rank_ref, base_ref, m_ref,
                    fn_ref, ys_ref, oc_ref, ol_ref, yl_ref, sems, *, final):
    i = pl.program_id(0)
    slot = i % 2

    def fetch(tile, buf):
        yl = yl_ref.at[buf]
        for e in range(N_EXPERTS):
            j = tile * N_EXPERTS + e
            _run_dma(ys_ref, glb_ref[j], yl, loc_ref[j], run_ref[j], sems.at[buf], wait=False)

        def clear(r, carry):
            yl[pl.ds(pl.multiple_of(r * RUN_ALIGN, RUN_ALIGN), RUN_ALIGN), :] = jnp.zeros((RUN_ALIGN, D_MODEL), F32)
            return carry

        lax.fori_loop(tot_ref[tile] // RUN_ALIGN, LOCAL_ROWS // RUN_ALIGN, clear, 0)

    @pl.when(i == 0)
    def _():
        fetch(0, 0)

    @pl.when(i + 1 < N_MOE_TILES)
    def _():
        fetch(i + 1, 1 - slot)

    rows = _local_rows(idx_ref, rank_ref, base_ref)
    gate = gate_ref[...]
    p = lax.broadcasted_iota(I32, (MOE_TILE, LOCAL_ROWS), 1)
    w = jnp.zeros((MOE_TILE, LOCAL_ROWS), F32)
    for k in range(TOP_K):
        w = jnp.where(p == rows[k], gate[:, k:k + 1], w)

    _wait_rows(yl_ref.at[slot], tot_ref[i], sems.at[slot])
    acc = _dot(w.astype(BF16), yl_ref[slot].astype(BF16))
    x = _read_halves(xc_ref, xl_ref, MOE_TILE) + m_ref[0][5:6] * acc
    _write_halves(oc_ref, ol_ref, _rms(x, fn_ref[...]) if final else x, MOE_TILE)


def _combine(plan, x, gates, idx, rank, mods, final_norm, ys, final):
    tile = lambda w: pl.BlockSpec((MOE_TILE, w), lambda i, *_: (i, 0))
    return pl.pallas_call(
        functools.partial(_combine_kernel, final=final),
        grid_spec=pltpu.PrefetchScalarGridSpec(
            num_scalar_prefetch=4,
            grid=(N_MOE_TILES,),
            in_specs=_half_specs(MOE_TILE, D_MODEL) + [tile(LANES), tile(LANES), tile(LANES),
                      pl.BlockSpec((1, 1, LANES), lambda i, *_: (i, 0, 0)),
                      pl.BlockSpec((1, 6, D_MODEL), lambda i, *_: (_cond_of_tile(i, MOE_TILE), 0, 0)),
                      pl.BlockSpec((1, D_MODEL), lambda i, *_: (0, 0)),
                      pl.BlockSpec(memory_space=pl.ANY)],
            out_specs=_half_specs(MOE_TILE, D_MODEL),
            scratch_shapes=[pltpu.VMEM((2, LOCAL_ROWS, D_MODEL), F32),
                            pltpu.SemaphoreType.DMA((2,))]),
        out_shape=_HALF_SHAPES,
        compiler_params=_cparams("arbitrary"),
        name="moe_combine",
    )(plan["run"], plan["local"], plan["global"], plan["total"], *x, gates, idx, rank, plan["base"], mods,
      final_norm, ys)


def _moe_plan(rank, cnt):
    first = rank[::MOE_TILE, :N_EXPERTS].astype(I32)
    total = cnt[0:1, :N_EXPERTS].astype(I32)
    run = jnp.concatenate([first[1:], total], axis=0) - first
    run = (run + RUN_ALIGN - 1) // RUN_ALIGN * RUN_ALIGN
    local = jnp.cumsum(run, axis=1) - run
    sizes = run.sum(axis=0)
    padded = (sizes + MOE_BM - 1) // MOE_BM * MOE_BM
    ends = jnp.cumsum(padded)
    starts = ends - padded
    glob = starts[None, :] + jnp.cumsum(run, axis=0) - run
    blk_start = jnp.arange(MOE_NB, dtype=I32) * MOE_BM
    blk_expert = jnp.minimum(jnp.sum(ends[None, :] <= blk_start[:, None], axis=1), N_EXPERTS - 1).astype(I32)
    blk_new = jnp.concatenate([jnp.ones((1,), I32), (blk_expert[1:] != blk_expert[:-1]).astype(I32)])
    blk = jnp.arange(MOE_NB, dtype=I32)
    later_start = (blk_new[None, :] == 1) & (blk[None, :] > blk[:, None])
    next_start = jnp.min(jnp.where(later_start, blk[None, :], MOE_NB), axis=1)
    next_hot = (blk[None, :] == next_start[:, None]).astype(I32)
    next_expert = jnp.where(next_start < MOE_NB, jnp.sum(next_hot * blk_expert[None, :], axis=1), blk_expert)
    blk_fetch = jnp.where(blk_new == 1, blk_expert, next_expert)
    base = _pad_lanes((local - first).astype(F32)).reshape(N_MOE_TILES, 1, LANES)
    return {"run": run.reshape(-1), "local": local.reshape(-1), "global": glob.reshape(-1), "total": run.sum(axis=1),
            "fill_start": starts + sizes, "fill_len": padded - sizes, "n_used": ends[-1:] // MOE_BM,
            "blk_expert": blk_expert, "blk_new": blk_new, "blk_fetch": blk_fetch, "base": base}


def _pad_lanes(v, fill=0.0):
    return jnp.pad(v, ((0, 0), (0, LANES - v.shape[-1])), constant_values=fill)


def kernel(x_prompt, x_sample, cache_na_k, cache_na_v, cache_gqa_k, cache_gqa_v, state_ssm, c, c_ctx, w_ada, b_ada, norm_mix, norm_ffn, w_in, na_rpb, gqa_q_norm, gqa_k_norm, ssm_conv_w, ssm_conv_b, ssm_dt_bias, ssm_a_log, ssm_d, ssm_norm, w_out, w_router, b_router, w_up, b_up, w_down, b_down, final_norm):
    x = (x_prompt.reshape(N_CTX_TOK, D_MODEL), x_sample.reshape(N_LAT_TOK, D_MODEL))
    conds = jnp.concatenate([c_ctx[None], c, jnp.zeros((COND_ROWS - N_COND, D_MODEL), F32)], axis=0)
    mods = _adaln(conds, w_ada, b_ada).reshape(DEPTH, COND_ROWS, 6, D_MODEL)
    cos, sin = _rope_tables()

    b_up_s = b_up.reshape(DEPTH, N_EXPERTS, 2 * D_FF // UP_GROUP, UP_GROUP // 2, 2)
    b_up_s = jnp.swapaxes(b_up_s, -1, -2).reshape(DEPTH, N_EXPERTS, 1, 2 * D_FF)
    b_down_s = b_down.reshape(DEPTH, N_EXPERTS, 1, D_MODEL)
    na_k_ctx = cache_na_k.reshape(DEC_BATCH, DEPTH, PAST_LEN, NA_DIM)
    na_v_ctx = cache_na_v.reshape(DEC_BATCH, DEPTH, PAST_LEN, NA_DIM)
    gqa_k_ctx = cache_gqa_k.reshape(DEC_BATCH, DEPTH, PAST_LEN, GQA_KV_DIM)
    gqa_v_ctx = cache_gqa_v.reshape(DEC_BATCH, DEPTH, PAST_LEN, GQA_KV_DIM)
    ssm_ctx = state_ssm.reshape(DEC_BATCH, DEPTH, 2, SSM_INNER, SSM_STATE)

    ctx_out = []
    for l in range(DEPTH):
        qkv, gqa, z, xbc, dt_raw = _inproj(l, x, mods[l], norm_mix[l][None], w_in)
        qn, kn = gqa_q_norm[l][None], gqa_k_norm[l][None]

        nao_c, go_c, gk_c = _ctx_attn(qkv, gqa, qn, kn)
        go_l = _lat_gqa(l, gqa, gqa_k_ctx, gqa_v_ctx, cos, sin, qn, kn)
        nao_l = _lat_na(l, qkv, na_k_ctx, na_v_ctx, _na_bias_tables(na_rpb[l]))

        xa = _conv_act(xbc, ssm_conv_w[l], ssm_conv_b[l][None])
        y_f, st_f, y_b, st_b = _ssd(l, xa, dt_raw, ssm_ctx, _pad_lanes(ssm_dt_bias[l]), _pad_lanes(ssm_a_log[l]))
        sts = [st[:BATCH].reshape(BATCH, SSM_HEADS, SSM_HEAD_DIM, SSM_STATE) for st in (st_f, st_b)]

        x_c, x_l, h, top_idx, gates, sel = _outproj(
            l, x, nao_c, nao_l, go_c, go_l, y_f, y_b, xa, z, jnp.repeat(ssm_d[l], SSM_HEAD_DIM)[None],
            ssm_norm[l][None], w_out, mods[l], norm_ffn[l][None], _pad_lanes(w_router[l]),
            _pad_lanes(b_router[l][None], NEG_INF))

        rank, cnt = _ranks(sel)
        plan = _moe_plan(rank, cnt)
        y_sorted = _experts(l, plan, _dispatch(plan, h, top_idx, rank), w_up, b_up_s[l], w_down, b_down_s[l])
        x = _combine(plan, (x_c, x_l), gates, top_idx, rank, mods[l], final_norm[None], y_sorted,
                     final=(l == DEPTH - 1))

        ctx_out.append((
            qkv[:N_CTX_TOK, NA_DIM:2 * NA_DIM].reshape(BATCH, SEQ, NA_HEADS, HEAD_DIM),
            qkv[:N_CTX_TOK, 2 * NA_DIM:].reshape(BATCH, SEQ, NA_HEADS, HEAD_DIM),
            gk_c.reshape(BATCH, SEQ, GQA_KV_HEADS, HEAD_DIM),
            gqa[:N_CTX_TOK, GQA_Q_DIM + GQA_KV_DIM:].reshape(BATCH, SEQ, GQA_KV_HEADS, HEAD_DIM),
            jnp.stack(sts, axis=1)))

    y_prompt = x[0].reshape(BATCH, SEQ, D_MODEL)
    y_sample = x[1].reshape(DEC_BATCH, DEC_SEQ, D_MODEL)
    return (y_prompt, y_sample) + tuple(jnp.stack([e[i] for e in ctx_out], axis=1) for i in range(5))
```
